```python
import math
import jax, jax.numpy as jnp
from jax import lax
import numpy as np

D_MODEL = 1024
BATCH = 8
SEQ = 8192
DEPTH = 2

CONV_DIM = D_MODEL
CONV_KERNEL = 31
SSM_EXPAND = 2
SSM_DIM = SSM_EXPAND * D_MODEL
SSM_HEAD_DIM = 64
SSM_HEADS = SSM_DIM // SSM_HEAD_DIM
SSM_GROUPS = 4
SSM_STATE = 128
SSM_CONV = 4
SSM_CHUNK = 128
SSM_BC = 2 * SSM_GROUPS * SSM_STATE
FFN_DIM = 2816
FFN_CONV = 3
DN_ALPHA = (2 * DEPTH) ** 0.25
DN_BETA = (8 * DEPTH) ** -0.25
LN_EPS = 1e-5
RMS_EPS = 1e-5
IN_SIZES = (2 * CONV_DIM, SSM_DIM, SSM_DIM + SSM_BC, SSM_HEADS, D_MODEL, D_MODEL)
IN_DIM = sum(IN_SIZES)

kernel_name = "hybrid_conformer_ssd_deepnorm"


def layer_norm(x, g, b):
    xf = x.astype(jnp.float32)
    mu = jnp.mean(xf, axis=-1, keepdims=True)
    var = jnp.mean(jnp.square(xf - mu), axis=-1, keepdims=True)
    return ((xf - mu) * lax.rsqrt(var + LN_EPS) * g.astype(jnp.float32)
            + b.astype(jnp.float32)).astype(x.dtype)


def causal_dwconv(x, w, b):
    k = w.shape[0]
    y = lax.conv_general_dilated(
        x, w[:, None, :].astype(x.dtype), window_strides=(1,),
        padding=[(k - 1, 0)], dimension_numbers=("NWC", "WIO", "NWC"),
        feature_group_count=x.shape[-1])
    return y + b.astype(x.dtype)


def split_in(u):
    idx = [int(v) for v in np.cumsum(IN_SIZES)[:-1]]
    return jnp.split(u, idx, axis=-1)


def conformer_branch(u_glu, dw_w, dw_b, ln_g, ln_b, w_out):
    a, g = jnp.split(u_glu, 2, axis=-1)
    v = a * jax.nn.sigmoid(g)
    v = causal_dwconv(v, dw_w, dw_b)
    v = jax.nn.silu(layer_norm(v, ln_g, ln_b))
    return v @ w_out


def ssd_chunked(x, dt, a_head, bm, cm):
    b, t, h, p = x.shape
    g, n = bm.shape[2], bm.shape[3]
    r = h // g
    L = SSM_CHUNK
    c = t // L
    xs = (x * dt[..., None]).reshape(b, c, L, g, r, p)
    a = (dt * a_head).reshape(b, c, L, g, r).transpose(0, 1, 3, 4, 2)
    bc = bm.reshape(b, c, L, g, n)
    cc = cm.reshape(b, c, L, g, n)
    a_cs = jnp.cumsum(a, axis=-1)
    seg = a_cs[..., :, None] - a_cs[..., None, :]
    causal = jnp.tril(jnp.ones((L, L), dtype=bool))
    decay = jnp.where(causal, jnp.exp(jnp.where(causal, seg, 0.0)), 0.0)
    cb = jnp.einsum("bclgn,bcsgn->bcgls", cc, bc)
    y_diag = jnp.einsum("bcgrls,bcsgrp->bclgrp", cb[:, :, :, None] * decay, xs)
    decay_states = jnp.exp(a_cs[..., -1:] - a_cs)
    states = jnp.einsum("bclgn,bcgrl,bclgrp->bcgrpn", bc, decay_states, xs)
    chunk_decay = jnp.exp(a_cs[..., -1])

    def step(carry, inp):
        s_c, d_c = inp
        return d_c[..., None, None] * carry + s_c, carry

    init = jnp.zeros((b, g, r, p, n), states.dtype)
    _, prev = lax.scan(step, init, (jnp.moveaxis(states, 1, 0),
                                    jnp.moveaxis(chunk_decay, 1, 0)))
    prev = jnp.moveaxis(prev, 0, 1)
    y_off = jnp.einsum("bclgn,bcgrpn,bcgrl->bclgrp", cc, prev, jnp.exp(a_cs))
    return (y_diag + y_off).reshape(b, t, h, p)


def ssd_branch(z, xbc, dt_raw, conv_w, conv_b, dt_bias, a_log, d_skip, norm_w, w_out):
    bsz, t, _ = z.shape
    xbc = jax.nn.silu(causal_dwconv(xbc, conv_w, conv_b))
    gn = SSM_GROUPS * SSM_STATE
    xs = xbc[..., :SSM_DIM]
    bm = xbc[..., SSM_DIM:SSM_DIM + gn].reshape(bsz, t, SSM_GROUPS, SSM_STATE)
    cm = xbc[..., SSM_DIM + gn:].reshape(bsz, t, SSM_GROUPS, SSM_STATE)
    dt = jax.nn.softplus(dt_raw.astype(jnp.float32) + dt_bias.astype(jnp.float32))
    a_head = -jnp.exp(a_log.astype(jnp.float32))
    xh = xs.reshape(bsz, t, SSM_HEADS, SSM_HEAD_DIM).astype(jnp.float32)
    y = ssd_chunked(xh, dt, a_head, bm.astype(jnp.float32), cm.astype(jnp.float32))
    y = y + xh * d_skip.astype(jnp.float32)[:, None]
    yg = (y.reshape(bsz, t, SSM_DIM) * jax.nn.silu(z.astype(jnp.float32)))
    yg = yg.reshape(bsz, t, SSM_GROUPS, SSM_DIM // SSM_GROUPS)
    yg = yg * lax.rsqrt(jnp.mean(jnp.square(yg), axis=-1, keepdims=True) + RMS_EPS)
    yn = (yg.reshape(bsz, t, SSM_DIM) * norm_w.astype(jnp.float32)).astype(z.dtype)
    return yn @ w_out


def conv_ffn(h, w_up, dw_w, dw_b, w_down):
    u = causal_dwconv(h @ w_up, dw_w, dw_b)
    gate, val = jnp.split(u, 2, axis=-1)
    return (jax.nn.silu(gate) * val) @ w_down


def _fwd_setup_inputs(seed: int = 0) -> dict:
    key = jax.random.key(seed)
    ks = jax.random.split(key, 32)
    f32 = jnp.float32
    nrm = lambda k, shape, s: (jax.random.normal(k, shape, f32) * s).astype(f32)
    gain = lambda k, shape: 1.0 + 0.05 * jax.random.normal(k, shape, f32)
    small = lambda k, shape: 0.02 * jax.random.normal(k, shape, f32)
    dt0 = jnp.exp(jax.random.uniform(ks[12], (DEPTH, SSM_HEADS), f32)
                  * (math.log(0.1) - math.log(0.001)) + math.log(0.001))
    return {
        "x": jax.random.normal(ks[0], (BATCH, SEQ, D_MODEL), f32),
        "ln_in_g": gain(ks[1], (D_MODEL,)),
        "ln_in_b": small(ks[2], (D_MODEL,)),
        "w_in": nrm(ks[3], (DEPTH, D_MODEL, IN_DIM), D_MODEL ** -0.5),
        "conv_dw_w": nrm(ks[4], (DEPTH, CONV_KERNEL, CONV_DIM), CONV_KERNEL ** -0.5),
        "conv_dw_b": small(ks[5], (DEPTH, CONV_DIM)),
        "conv_ln_g": gain(ks[6], (DEPTH, CONV_DIM)),
        "conv_ln_b": small(ks[7], (DEPTH, CONV_DIM)),
        "w_conv_out": nrm(ks[8], (DEPTH, CONV_DIM, D_MODEL), DN_BETA * CONV_DIM ** -0.5),
        "ssm_conv_w": nrm(ks[9], (DEPTH, SSM_CONV, SSM_DIM + SSM_BC), SSM_CONV ** -0.5),
        "ssm_conv_b": small(ks[10], (DEPTH, SSM_DIM + SSM_BC)),
        "ssm_dt_bias": dt0 + jnp.log(-jnp.expm1(-dt0)),
        "ssm_a_log": jnp.log(jax.random.uniform(ks[13], (DEPTH, SSM_HEADS), f32, 1.0, 16.0)),
        "ssm_d": gain(ks[14], (DEPTH, SSM_HEADS)),
        "ssm_norm_w": gain(ks[15], (DEPTH, SSM_DIM)),
        "w_ssm_out": nrm(ks[16], (DEPTH, SSM_DIM, D_MODEL), DN_BETA * SSM_DIM ** -0.5),
        "w_o": nrm(ks[17], (DEPTH, D_MODEL, D_MODEL), DN_BETA * D_MODEL ** -0.5),
        "ln1_g": gain(ks[18], (DEPTH, D_MODEL)),
        "ln1_b": small(ks[19], (DEPTH, D_MODEL)),
        "w_ffn_up": nrm(ks[20], (DEPTH, D_MODEL, 2 * FFN_DIM), DN_BETA * D_MODEL ** -0.5),
        "ffn_dw_w": nrm(ks[21], (DEPTH, FFN_CONV, 2 * FFN_DIM), FFN_CONV ** -0.5),
        "ffn_dw_b": small(ks[22], (DEPTH, 2 * FFN_DIM)),
        "w_ffn_down": nrm(ks[23], (DEPTH, FFN_DIM, D_MODEL), DN_BETA * FFN_DIM ** -0.5),
        "ln2_g": gain(ks[24], (DEPTH, D_MODEL)),
        "ln2_b": small(ks[25], (DEPTH, D_MODEL)),
    }


def _fwd_reference(x, ln_in_g, ln_in_b, w_in, conv_dw_w, conv_dw_b, conv_ln_g, conv_ln_b,
              w_conv_out, ssm_conv_w, ssm_conv_b, ssm_dt_bias, ssm_a_log, ssm_d,
              ssm_norm_w, w_ssm_out, w_o, ln1_g, ln1_b, w_ffn_up, ffn_dw_w, ffn_dw_b,
              w_ffn_down, ln2_g, ln2_b):
    h = layer_norm(x, ln_in_g, ln_in_b)
    for l in range(DEPTH):
        u = h @ w_in[l]
        u_glu, z, xbc, dt_raw, gate_a, gate_b = split_in(u)
        y_a = conformer_branch(u_glu, conv_dw_w[l], conv_dw_b[l], conv_ln_g[l],
                               conv_ln_b[l], w_conv_out[l])
        y_b = ssd_branch(z, xbc, dt_raw, ssm_conv_w[l], ssm_conv_b[l], ssm_dt_bias[l],
                         ssm_a_log[l], ssm_d[l], ssm_norm_w[l], w_ssm_out[l])
        mix = (jax.nn.sigmoid(gate_a) * y_a + jax.nn.sigmoid(gate_b) * y_b) @ w_o[l]
        h = layer_norm(DN_ALPHA * h + mix, ln1_g[l], ln1_b[l])
        ffn = conv_ffn(h, w_ffn_up[l], ffn_dw_w[l], ffn_dw_b[l], w_ffn_down[l])
        h = layer_norm(DN_ALPHA * h + ffn, ln2_g[l], ln2_b[l])
    return h


import jax as _jax
import jax.numpy as _jnp

TWIN_FORMAT = 'train_step'
FWD_PARAMS = ['x', 'ln_in_g', 'ln_in_b', 'w_in', 'conv_dw_w', 'conv_dw_b', 'conv_ln_g', 'conv_ln_b', 'w_conv_out', 'ssm_conv_w', 'ssm_conv_b', 'ssm_dt_bias', 'ssm_a_log', 'ssm_d', 'ssm_norm_w', 'w_ssm_out', 'w_o', 'ln1_g', 'ln1_b', 'w_ffn_up', 'ffn_dw_w', 'ffn_dw_b', 'w_ffn_down', 'ln2_g', 'ln2_b']
TWIN_WEIGHTS = ['ln_in_g', 'ln_in_b', 'w_in', 'conv_dw_w', 'conv_dw_b', 'conv_ln_g', 'conv_ln_b', 'w_conv_out', 'ssm_conv_w', 'ssm_conv_b', 'ssm_dt_bias', 'ssm_a_log', 'ssm_d', 'ssm_norm_w', 'w_ssm_out', 'w_o', 'ln1_g', 'ln1_b', 'w_ffn_up', 'ffn_dw_w', 'ffn_dw_b', 'w_ffn_down', 'ln2_g', 'ln2_b']
TWIN_DIFF_INPUT = 'x'
TWIN_INPUTS = ['x', 'ln_in_g', 'ln_in_b', 'w_in', 'conv_dw_w', 'conv_dw_b', 'conv_ln_g', 'conv_ln_b', 'w_conv_out', 'ssm_conv_w', 'ssm_conv_b', 'ssm_dt_bias', 'ssm_a_log', 'ssm_d', 'ssm_norm_w', 'w_ssm_out', 'w_o', 'ln1_g', 'ln1_b', 'w_ffn_up', 'ffn_dw_w', 'ffn_dw_b', 'w_ffn_down', 'ln2_g', 'ln2_b', 'loss_target', 'm_ln_in_g', 'm_ln_in_b', 'm_w_in', 'm_conv_dw_w', 'm_conv_dw_b', 'm_conv_ln_g', 'm_conv_ln_b', 'm_w_conv_out', 'm_ssm_conv_w', 'm_ssm_conv_b', 'm_ssm_dt_bias', 'm_ssm_a_log', 'm_ssm_d', 'm_ssm_norm_w', 'm_w_ssm_out', 'm_w_o', 'm_ln1_g', 'm_ln1_b', 'm_w_ffn_up', 'm_ffn_dw_w', 'm_ffn_dw_b', 'm_w_ffn_down', 'm_ln2_g', 'm_ln2_b', 'v_ln_in_g', 'v_ln_in_b', 'v_w_in', 'v_conv_dw_w', 'v_conv_dw_b', 'v_conv_ln_g', 'v_conv_ln_b', 'v_w_conv_out', 'v_ssm_conv_w', 'v_ssm_conv_b', 'v_ssm_dt_bias', 'v_ssm_a_log', 'v_ssm_d', 'v_ssm_norm_w', 'v_w_ssm_out', 'v_w_o', 'v_ln1_g', 'v_ln1_b', 'v_w_ffn_up', 'v_ffn_dw_w', 'v_ffn_dw_b', 'v_w_ffn_down', 'v_ln2_g', 'v_ln2_b']
TWIN_OUTPUTS = ['loss', 'grad_x', 'grad_ln_in_g', 'grad_ln_in_b', 'grad_w_in', 'grad_conv_dw_w', 'grad_conv_dw_b', 'grad_conv_ln_g', 'grad_conv_ln_b', 'grad_w_conv_out', 'grad_ssm_conv_w', 'grad_ssm_conv_b', 'grad_ssm_dt_bias', 'grad_ssm_a_log', 'grad_ssm_d', 'grad_ssm_norm_w', 'grad_w_ssm_out', 'grad_w_o', 'grad_ln1_g', 'grad_ln1_b', 'grad_w_ffn_up', 'grad_ffn_dw_w', 'grad_ffn_dw_b', 'grad_w_ffn_down', 'grad_ln2_g', 'grad_ln2_b', 'delta_ln_in_g', 'delta_ln_in_b', 'delta_w_in', 'delta_conv_dw_w', 'delta_conv_dw_b', 'delta_conv_ln_g', 'delta_conv_ln_b', 'delta_w_conv_out', 'delta_ssm_conv_w', 'delta_ssm_conv_b', 'delta_ssm_dt_bias', 'delta_ssm_a_log', 'delta_ssm_d', 'delta_ssm_norm_w', 'delta_w_ssm_out', 'delta_w_o', 'delta_ln1_g', 'delta_ln1_b', 'delta_w_ffn_up', 'delta_ffn_dw_w', 'delta_ffn_dw_b', 'delta_w_ffn_down', 'delta_ln2_g', 'delta_ln2_b', 'new_m_ln_in_g', 'new_m_ln_in_b', 'new_m_w_in', 'new_m_conv_dw_w', 'new_m_conv_dw_b', 'new_m_conv_ln_g', 'new_m_conv_ln_b', 'new_m_w_conv_out', 'new_m_ssm_conv_w', 'new_m_ssm_conv_b', 'new_m_ssm_dt_bias', 'new_m_ssm_a_log', 'new_m_ssm_d', 'new_m_ssm_norm_w', 'new_m_w_ssm_out', 'new_m_w_o', 'new_m_ln1_g', 'new_m_ln1_b', 'new_m_w_ffn_up', 'new_m_ffn_dw_w', 'new_m_ffn_dw_b', 'new_m_w_ffn_down', 'new_m_ln2_g', 'new_m_ln2_b', 'new_v_ln_in_g', 'new_v_ln_in_b', 'new_v_w_in', 'new_v_conv_dw_w', 'new_v_conv_dw_b', 'new_v_conv_ln_g', 'new_v_conv_ln_b', 'new_v_w_conv_out', 'new_v_ssm_conv_w', 'new_v_ssm_conv_b', 'new_v_ssm_dt_bias', 'new_v_ssm_a_log', 'new_v_ssm_d', 'new_v_ssm_norm_w', 'new_v_w_ssm_out', 'new_v_w_o', 'new_v_ln1_g', 'new_v_ln1_b', 'new_v_w_ffn_up', 'new_v_ffn_dw_w', 'new_v_ffn_dw_b', 'new_v_w_ffn_down', 'new_v_ln2_g', 'new_v_ln2_b']
TWIN_LEAF_KINDS = {'loss': 'loss', 'grad_x': 'grad_x', 'grad_ln_in_g': 'grad_w', 'grad_ln_in_b': 'grad_w', 'grad_w_in': 'grad_w', 'grad_conv_dw_w': 'grad_w', 'grad_conv_dw_b': 'grad_w', 'grad_conv_ln_g': 'grad_w', 'grad_conv_ln_b': 'grad_w', 'grad_w_conv_out': 'grad_w', 'grad_ssm_conv_w': 'grad_w', 'grad_ssm_conv_b': 'grad_w', 'grad_ssm_dt_bias': 'grad_w', 'grad_ssm_a_log': 'grad_w', 'grad_ssm_d': 'grad_w', 'grad_ssm_norm_w': 'grad_w', 'grad_w_ssm_out': 'grad_w', 'grad_w_o': 'grad_w', 'grad_ln1_g': 'grad_w', 'grad_ln1_b': 'grad_w', 'grad_w_ffn_up': 'grad_w', 'grad_ffn_dw_w': 'grad_w', 'grad_ffn_dw_b': 'grad_w', 'grad_w_ffn_down': 'grad_w', 'grad_ln2_g': 'grad_w', 'grad_ln2_b': 'grad_w', 'delta_ln_in_g': 'delta_w', 'delta_ln_in_b': 'delta_w', 'delta_w_in': 'delta_w', 'delta_conv_dw_w': 'delta_w', 'delta_conv_dw_b': 'delta_w', 'delta_conv_ln_g': 'delta_w', 'delta_conv_ln_b': 'delta_w', 'delta_w_conv_out': 'delta_w', 'delta_ssm_conv_w': 'delta_w', 'delta_ssm_conv_b': 'delta_w', 'delta_ssm_dt_bias': 'delta_w', 'delta_ssm_a_log': 'delta_w', 'delta_ssm_d': 'delta_w', 'delta_ssm_norm_w': 'delta_w', 'delta_w_ssm_out': 'delta_w', 'delta_w_o': 'delta_w', 'delta_ln1_g': 'delta_w', 'delta_ln1_b': 'delta_w', 'delta_w_ffn_up': 'delta_w', 'delta_ffn_dw_w': 'delta_w', 'delta_ffn_dw_b': 'delta_w', 'delta_w_ffn_down': 'delta_w', 'delta_ln2_g': 'delta_w', 'delta_ln2_b': 'delta_w', 'new_m_ln_in_g': 'new_m', 'new_m_ln_in_b': 'new_m', 'new_m_w_in': 'new_m', 'new_m_conv_dw_w': 'new_m', 'new_m_conv_dw_b': 'new_m', 'new_m_conv_ln_g': 'new_m', 'new_m_conv_ln_b': 'new_m', 'new_m_w_conv_out': 'new_m', 'new_m_ssm_conv_w': 'new_m', 'new_m_ssm_conv_b': 'new_m', 'new_m_ssm_dt_bias': 'new_m', 'new_m_ssm_a_log': 'new_m', 'new_m_ssm_d': 'new_m', 'new_m_ssm_norm_w': 'new_m', 'new_m_w_ssm_out': 'new_m', 'new_m_w_o': 'new_m', 'new_m_ln1_g': 'new_m', 'new_m_ln1_b': 'new_m', 'new_m_w_ffn_up': 'new_m', 'new_m_ffn_dw_w': 'new_m', 'new_m_ffn_dw_b': 'new_m', 'new_m_w_ffn_down': 'new_m', 'new_m_ln2_g': 'new_m', 'new_m_ln2_b': 'new_m', 'new_v_ln_in_g': 'new_v', 'new_v_ln_in_b': 'new_v', 'new_v_w_in': 'new_v', 'new_v_conv_dw_w': 'new_v', 'new_v_conv_dw_b': 'new_v', 'new_v_conv_ln_g': 'new_v', 'new_v_conv_ln_b': 'new_v', 'new_v_w_conv_out': 'new_v', 'new_v_ssm_conv_w': 'new_v', 'new_v_ssm_conv_b': 'new_v', 'new_v_ssm_dt_bias': 'new_v', 'new_v_ssm_a_log': 'new_v', 'new_v_ssm_d': 'new_v', 'new_v_ssm_norm_w': 'new_v', 'new_v_w_ssm_out': 'new_v', 'new_v_w_o': 'new_v', 'new_v_ln1_g': 'new_v', 'new_v_ln1_b': 'new_v', 'new_v_w_ffn_up': 'new_v', 'new_v_ffn_dw_w': 'new_v', 'new_v_ffn_dw_b': 'new_v', 'new_v_w_ffn_down': 'new_v', 'new_v_ln2_g': 'new_v', 'new_v_ln2_b': 'new_v'}


def _forward(args):
    return _fwd_reference(*[args[k] for k in FWD_PARAMS])


def _output_shape():
    def fwd():
        inp = _fwd_setup_inputs(0)
        return _fwd_reference(*[inp[k] for k in FWD_PARAMS])
    out = _jax.eval_shape(fwd)
    return out.shape, out.dtype

N_MICROBATCH = 1
ADAM_LR = 0.001
ADAM_B1 = 0.9
ADAM_B2 = 0.999
ADAM_EPS = 1e-08
ADAM_WD = 0.01
ADAM_STEP = 10
PER_EXAMPLE_BATCH_AXIS = {'x': 0, 'loss_target': 0}
SHARED_INPUTS = []
_WEIGHT_DTYPES = {'ln_in_g': _jnp.float32, 'ln_in_b': _jnp.float32, 'w_in': _jnp.float32, 'conv_dw_w': _jnp.float32, 'conv_dw_b': _jnp.float32, 'conv_ln_g': _jnp.float32, 'conv_ln_b': _jnp.float32, 'w_conv_out': _jnp.float32, 'ssm_conv_w': _jnp.float32, 'ssm_conv_b': _jnp.float32, 'ssm_dt_bias': _jnp.float32, 'ssm_a_log': _jnp.float32, 'ssm_d': _jnp.float32, 'ssm_norm_w': _jnp.float32, 'w_ssm_out': _jnp.float32, 'w_o': _jnp.float32, 'ln1_g': _jnp.float32, 'ln1_b': _jnp.float32, 'w_ffn_up': _jnp.float32, 'ffn_dw_w': _jnp.float32, 'ffn_dw_b': _jnp.float32, 'w_ffn_down': _jnp.float32, 'ln2_g': _jnp.float32, 'ln2_b': _jnp.float32}
MOMENT_SCALE = {'ln_in_g': 6.205893e+00, 'ln_in_b': 1.065118e+00, 'w_in': 1.374761e-02, 'conv_dw_w': 1.536615e-02, 'conv_dw_b': 7.189495e-02, 'conv_ln_g': 3.148351e-02, 'conv_ln_b': 4.417596e-02, 'w_conv_out': 4.111781e-02, 'ssm_conv_w': 1.582673e-02, 'ssm_conv_b': 2.632782e-02, 'ssm_dt_bias': 4.988232e-02, 'ssm_a_log': 7.818357e-02, 'ssm_d': 1.204980e-01, 'ssm_norm_w': 2.091870e-02, 'w_ssm_out': 5.450116e-02, 'w_o': 6.791568e-02, 'ln1_g': 6.317061e+00, 'ln1_b': 1.068734e+00, 'w_ffn_up': 1.530978e-02, 'ffn_dw_w': 7.585560e-03, 'ffn_dw_b': 1.745065e-02, 'w_ffn_down': 2.508492e-02, 'ln2_g': 4.651115e+01, 'ln2_b': 2.334619e+00}


def _to_microbatches(a, axis):
    t = _jnp.moveaxis(a, axis, 0)
    t = t.reshape((N_MICROBATCH, t.shape[0] // N_MICROBATCH) + t.shape[1:])
    return _jnp.moveaxis(t, 1, axis + 1)


def setup_inputs(seed: int = 0) -> dict:
    inp = _fwd_setup_inputs(seed)
    key = _jax.random.fold_in(_jax.random.key(seed), 7919)
    shape, _ = _output_shape()
    out = dict(inp)
    out["loss_target"] = _jax.random.normal(_jax.random.fold_in(key, 0), shape, _jnp.float32)
    for i, name in enumerate(TWIN_WEIGHTS):
        w = inp[name].astype(_jnp.float32)
        if MOMENT_SCALE is None:
            s = _jnp.sqrt(_jnp.mean(_jnp.square(w)) + 1e-30)
        else:
            s = MOMENT_SCALE[name]
        km, kv = _jax.random.split(_jax.random.fold_in(key, i + 1))
        out[name] = w
        out["m_" + name] = s * _jax.random.normal(km, w.shape, _jnp.float32)
        out["v_" + name] = (s * s) * _jax.random.uniform(kv, w.shape, _jnp.float32, 0.5, 1.5)
    if N_MICROBATCH > 1:
        for name, axis in PER_EXAMPLE_BATCH_AXIS.items():
            out[name] = _to_microbatches(out[name], axis)
    return {'x': out['x'], 'ln_in_g': out['ln_in_g'], 'ln_in_b': out['ln_in_b'], 'w_in': out['w_in'], 'conv_dw_w': out['conv_dw_w'], 'conv_dw_b': out['conv_dw_b'], 'conv_ln_g': out['conv_ln_g'], 'conv_ln_b': out['conv_ln_b'], 'w_conv_out': out['w_conv_out'], 'ssm_conv_w': out['ssm_conv_w'], 'ssm_conv_b': out['ssm_conv_b'], 'ssm_dt_bias': out['ssm_dt_bias'], 'ssm_a_log': out['ssm_a_log'], 'ssm_d': out['ssm_d'], 'ssm_norm_w': out['ssm_norm_w'], 'w_ssm_out': out['w_ssm_out'], 'w_o': out['w_o'], 'ln1_g': out['ln1_g'], 'ln1_b': out['ln1_b'], 'w_ffn_up': out['w_ffn_up'], 'ffn_dw_w': out['ffn_dw_w'], 'ffn_dw_b': out['ffn_dw_b'], 'w_ffn_down': out['w_ffn_down'], 'ln2_g': out['ln2_g'], 'ln2_b': out['ln2_b'], 'loss_target': out['loss_target'], 'm_ln_in_g': out['m_ln_in_g'], 'm_ln_in_b': out['m_ln_in_b'], 'm_w_in': out['m_w_in'], 'm_conv_dw_w': out['m_conv_dw_w'], 'm_conv_dw_b': out['m_conv_dw_b'], 'm_conv_ln_g': out['m_conv_ln_g'], 'm_conv_ln_b': out['m_conv_ln_b'], 'm_w_conv_out': out['m_w_conv_out'], 'm_ssm_conv_w': out['m_ssm_conv_w'], 'm_ssm_conv_b': out['m_ssm_conv_b'], 'm_ssm_dt_bias': out['m_ssm_dt_bias'], 'm_ssm_a_log': out['m_ssm_a_log'], 'm_ssm_d': out['m_ssm_d'], 'm_ssm_norm_w': out['m_ssm_norm_w'], 'm_w_ssm_out': out['m_w_ssm_out'], 'm_w_o': out['m_w_o'], 'm_ln1_g': out['m_ln1_g'], 'm_ln1_b': out['m_ln1_b'], 'm_w_ffn_up': out['m_w_ffn_up'], 'm_ffn_dw_w': out['m_ffn_dw_w'], 'm_ffn_dw_b': out['m_ffn_dw_b'], 'm_w_ffn_down': out['m_w_ffn_down'], 'm_ln2_g': out['m_ln2_g'], 'm_ln2_b': out['m_ln2_b'], 'v_ln_in_g': out['v_ln_in_g'], 'v_ln_in_b': out['v_ln_in_b'], 'v_w_in': out['v_w_in'], 'v_conv_dw_w': out['v_conv_dw_w'], 'v_conv_dw_b': out['v_conv_dw_b'], 'v_conv_ln_g': out['v_conv_ln_g'], 'v_conv_ln_b': out['v_conv_ln_b'], 'v_w_conv_out': out['v_w_conv_out'], 'v_ssm_conv_w': out['v_ssm_conv_w'], 'v_ssm_conv_b': out['v_ssm_conv_b'], 'v_ssm_dt_bias': out['v_ssm_dt_bias'], 'v_ssm_a_log': out['v_ssm_a_log'], 'v_ssm_d': out['v_ssm_d'], 'v_ssm_norm_w': out['v_ssm_norm_w'], 'v_w_ssm_out': out['v_w_ssm_out'], 'v_w_o': out['v_w_o'], 'v_ln1_g': out['v_ln1_g'], 'v_ln1_b': out['v_ln1_b'], 'v_w_ffn_up': out['v_w_ffn_up'], 'v_ffn_dw_w': out['v_ffn_dw_w'], 'v_ffn_dw_b': out['v_ffn_dw_b'], 'v_w_ffn_down': out['v_w_ffn_down'], 'v_ln2_g': out['v_ln2_g'], 'v_ln2_b': out['v_ln2_b']}


def _loss(weights, diff, rest, loss_target):
    with _jax.named_scope("forward"):
        args = {**rest, TWIN_DIFF_INPUT: diff, **{k: w.astype(_WEIGHT_DTYPES[k]) for k, w in weights.items()}}
        y = _forward(args)
    with _jax.named_scope("loss_head"):
        err = _jnp.square(y.astype(_jnp.float32) - loss_target)
        return 0.5 * _jnp.sum(_jnp.mean(err, axis=-1)) if err.ndim else 0.5 * err


def _adamw(w, g, m, v):
    m = ADAM_B1 * m + (1.0 - ADAM_B1) * g
    v = ADAM_B2 * v + (1.0 - ADAM_B2) * _jnp.square(g)
    m_hat = m / (1.0 - ADAM_B1 ** ADAM_STEP)
    v_hat = v / (1.0 - ADAM_B2 ** ADAM_STEP)
    delta = -ADAM_LR * (m_hat / (_jnp.sqrt(v_hat) + ADAM_EPS) + ADAM_WD * w)
    return delta, m, v


def reference(x, ln_in_g, ln_in_b, w_in, conv_dw_w, conv_dw_b, conv_ln_g, conv_ln_b, w_conv_out, ssm_conv_w, ssm_conv_b, ssm_dt_bias, ssm_a_log, ssm_d, ssm_norm_w, w_ssm_out, w_o, ln1_g, ln1_b, w_ffn_up, ffn_dw_w, ffn_dw_b, w_ffn_down, ln2_g, ln2_b, loss_target, m_ln_in_g, m_ln_in_b, m_w_in, m_conv_dw_w, m_conv_dw_b, m_conv_ln_g, m_conv_ln_b, m_w_conv_out, m_ssm_conv_w, m_ssm_conv_b, m_ssm_dt_bias, m_ssm_a_log, m_ssm_d, m_ssm_norm_w, m_w_ssm_out, m_w_o, m_ln1_g, m_ln1_b, m_w_ffn_up, m_ffn_dw_w, m_ffn_dw_b, m_w_ffn_down, m_ln2_g, m_ln2_b, v_ln_in_g, v_ln_in_b, v_w_in, v_conv_dw_w, v_conv_dw_b, v_conv_ln_g, v_conv_ln_b, v_w_conv_out, v_ssm_conv_w, v_ssm_conv_b, v_ssm_dt_bias, v_ssm_a_log, v_ssm_d, v_ssm_norm_w, v_w_ssm_out, v_w_o, v_ln1_g, v_ln1_b, v_w_ffn_up, v_ffn_dw_w, v_ffn_dw_b, v_w_ffn_down, v_ln2_g, v_ln2_b):
    given = dict(x=x, ln_in_g=ln_in_g, ln_in_b=ln_in_b, w_in=w_in, conv_dw_w=conv_dw_w, conv_dw_b=conv_dw_b, conv_ln_g=conv_ln_g, conv_ln_b=conv_ln_b, w_conv_out=w_conv_out, ssm_conv_w=ssm_conv_w, ssm_conv_b=ssm_conv_b, ssm_dt_bias=ssm_dt_bias, ssm_a_log=ssm_a_log, ssm_d=ssm_d, ssm_norm_w=ssm_norm_w, w_ssm_out=w_ssm_out, w_o=w_o, ln1_g=ln1_g, ln1_b=ln1_b, w_ffn_up=w_ffn_up, ffn_dw_w=ffn_dw_w, ffn_dw_b=ffn_dw_b, w_ffn_down=w_ffn_down, ln2_g=ln2_g, ln2_b=ln2_b, loss_target=loss_target, m_ln_in_g=m_ln_in_g, m_ln_in_b=m_ln_in_b, m_w_in=m_w_in, m_conv_dw_w=m_conv_dw_w, m_conv_dw_b=m_conv_dw_b, m_conv_ln_g=m_conv_ln_g, m_conv_ln_b=m_conv_ln_b, m_w_conv_out=m_w_conv_out, m_ssm_conv_w=m_ssm_conv_w, m_ssm_conv_b=m_ssm_conv_b, m_ssm_dt_bias=m_ssm_dt_bias, m_ssm_a_log=m_ssm_a_log, m_ssm_d=m_ssm_d, m_ssm_norm_w=m_ssm_norm_w, m_w_ssm_out=m_w_ssm_out, m_w_o=m_w_o, m_ln1_g=m_ln1_g, m_ln1_b=m_ln1_b, m_w_ffn_up=m_w_ffn_up, m_ffn_dw_w=m_ffn_dw_w, m_ffn_dw_b=m_ffn_dw_b, m_w_ffn_down=m_w_ffn_down, m_ln2_g=m_ln2_g, m_ln2_b=m_ln2_b, v_ln_in_g=v_ln_in_g, v_ln_in_b=v_ln_in_b, v_w_in=v_w_in, v_conv_dw_w=v_conv_dw_w, v_conv_dw_b=v_conv_dw_b, v_conv_ln_g=v_conv_ln_g, v_conv_ln_b=v_conv_ln_b, v_w_conv_out=v_w_conv_out, v_ssm_conv_w=v_ssm_conv_w, v_ssm_conv_b=v_ssm_conv_b, v_ssm_dt_bias=v_ssm_dt_bias, v_ssm_a_log=v_ssm_a_log, v_ssm_d=v_ssm_d, v_ssm_norm_w=v_ssm_norm_w, v_w_ssm_out=v_w_ssm_out, v_w_o=v_w_o, v_ln1_g=v_ln1_g, v_ln1_b=v_ln1_b, v_w_ffn_up=v_w_ffn_up, v_ffn_dw_w=v_ffn_dw_w, v_ffn_dw_b=v_ffn_dw_b, v_w_ffn_down=v_w_ffn_down, v_ln2_g=v_ln2_g, v_ln2_b=v_ln2_b)
    weights = {n: given[n] for n in TWIN_WEIGHTS}
    shared = {n: given[n] for n in SHARED_INPUTS}
    per_example = {n: given[n] for n in ['x']}
    grad_fn = _jax.value_and_grad(_loss, argnums=(0, 1))

    def one_microbatch(ex, loss_target):
        ex = dict(ex)
        diff = ex.pop(TWIN_DIFF_INPUT)
        return grad_fn(weights, diff, {**shared, **ex}, loss_target)

    if N_MICROBATCH == 1:
        loss, (grad_w, grad_x) = one_microbatch(per_example, given["loss_target"])
    else:
        def body(carry, xs):
            loss_sum, grad_sum = carry
            l_k, (gw_k, gx_k) = one_microbatch(xs[0], xs[1])
            with _jax.named_scope("update"):
                return (loss_sum + l_k, _jax.tree.map(_jnp.add, grad_sum, gw_k)), gx_k

        init = (_jnp.zeros((), _jnp.float32), _jax.tree.map(_jnp.zeros_like, weights))
        (loss, grad_w), grad_x = _jax.lax.scan(body, init, (per_example, given["loss_target"]))
    with _jax.named_scope("update"):
        delta_w, new_m, new_v = {}, {}, {}
        for n in TWIN_WEIGHTS:
            delta_w[n], new_m[n], new_v[n] = _adamw(weights[n], grad_w[n], given["m_" + n], given["v_" + n])
    return (loss, grad_x, *[grad_w[n] for n in TWIN_WEIGHTS], *[delta_w[n] for n in TWIN_WEIGHTS],
            *[new_m[n] for n in TWIN_WEIGHTS], *[new_v[n] for n in TWIN_WEIGHTS])
```

```python
import functools
import math

import jax
import jax.numpy as jnp
from jax import lax
from jax.experimental import pallas as pl
from jax.experimental.pallas import tpu as pltpu

F32 = jnp.float32
BF16 = jnp.bfloat16
HI = lax.Precision.HIGHEST

D = 1024
DEPTH = 2
CONV_K = 31
SD = 2 * D
P = 64
H = SD // P
G = 4
R = H // G
N = 128
RP = R * P
SSM_K = 4
L = 128
XBC = SD + 2 * G * N
FFN = 2816
FFN_K = 3
IN_DIM = 2 * D + SD + XBC + H + 2 * D
ALPHA = (2 * DEPTH) ** 0.25
LN_EPS = 1e-5
RMS_EPS = 1e-5
ADAM_LR, ADAM_B1, ADAM_B2, ADAM_EPS, ADAM_WD, ADAM_STEP = 0.001, 0.9, 0.999, 1e-08, 0.01, 10

HP = 128
NCHIP = 4
PACK_W = 1024
VMEM_LIMIT = 56 * 1024 * 1024
TM = 512
TM_X = 256
TM_FFN = 128
TK = 1024
EW_ROWS = 512

assert D == 2 * RP and 2 * G * N == D and XBC == 3 * D and H <= HP


def _pcall(body, *, name, grid=(), in_specs, out_specs, out_shape, scratch_shapes=()):
    params = pltpu.CompilerParams(vmem_limit_bytes=VMEM_LIMIT, dimension_semantics=("arbitrary",) * len(grid))
    return pl.pallas_call(body, name=name, grid=grid, in_specs=in_specs, out_specs=out_specs, out_shape=out_shape,
                          scratch_shapes=list(scratch_shapes), compiler_params=params)


def _ccall(body, *, name, in_specs, out_specs, out_shape, scratch_shapes):
    return pl.pallas_call(body, name=name, in_specs=in_specs, out_specs=out_specs, out_shape=out_shape,
                          scratch_shapes=list(scratch_shapes))


def _full_spec(a):
    nd = a.ndim
    return pl.BlockSpec(a.shape, lambda *_: (0,) * nd)


def _sds(shape, dtype):
    return jax.ShapeDtypeStruct(tuple(shape), dtype)


def _mm(a, b, *, name, grid, a_spec, b_spec, o_spec, out_shape, acc_shape, trans_a=False, add=None, add_spec=None):
    nk = grid[2]
    dn = (((0,), (0,)), ((), ())) if trans_a else (((1,), (0,)), ((), ()))
    has_add = add is not None

    def body(*refs):
        a_ref, b_ref = refs[0], refs[1]
        add_ref = refs[2] if has_add else None
        o_ref = refs[3] if has_add else refs[2]
        part = lax.dot_general(a_ref[...].astype(BF16), b_ref[...].astype(BF16), dn, preferred_element_type=F32)

        def finish(res):
            if has_add:
                res = res + add_ref[...]
            o_ref[...] = res.astype(o_ref.dtype)

        if nk == 1:
            finish(part)
        else:
            acc = refs[-1]
            k = pl.program_id(2)

            @pl.when(k == 0)
            def _():
                acc[...] = part

            @pl.when(k > 0)
            def _():
                acc[...] += part

            @pl.when(k == nk - 1)
            def _():
                finish(acc[...])

    ins = [a, b] + ([add] if has_add else [])
    specs = [a_spec, b_spec] + ([add_spec] if has_add else [])
    scratch = [pltpu.VMEM(acc_shape, F32)] if nk > 1 else []
    return _pcall(body, name=name, grid=grid, in_specs=specs, out_specs=o_spec, out_shape=out_shape,
                  scratch_shapes=scratch)(*ins)


def _mm_nn(a, b, *, name, out_dtype=F32, tn=None, tk=None, add=None):
    M, K = a.shape
    Nn = b.shape[1]
    tm = min(TM, M)
    tn = Nn if tn is None else tn
    tk = K if tk is None else tk
    grid = (M // tm, Nn // tn, K // tk)
    return _mm(a, b, name=name, grid=grid,
               a_spec=pl.BlockSpec((tm, tk), lambda i, j, k: (i, k)),
               b_spec=pl.BlockSpec((tk, tn), lambda i, j, k: (k, j)),
               o_spec=pl.BlockSpec((tm, tn), lambda i, j, k: (i, j)),
               out_shape=_sds((M, Nn), out_dtype), acc_shape=(tm, tn), add=add,
               add_spec=pl.BlockSpec((tm, tn), lambda i, j, k: (i, j)))


def _mm_nn_slab_out(a, b, *, name, width):
    M, K = a.shape
    S = b.shape[1] // width
    tm = min(TM, M)
    return _mm(a, b, name=name, grid=(M // tm, S, 1),
               a_spec=pl.BlockSpec((tm, K), lambda i, j, k: (i, 0)),
               b_spec=pl.BlockSpec((K, width), lambda i, j, k: (0, j)),
               o_spec=pl.BlockSpec((None, tm, width), lambda i, j, k: (j, i, 0)),
               out_shape=_sds((S, M, width), F32), acc_shape=(tm, width))


def _mm_slab_in(a3, b, *, name, add):
    S, M, width = a3.shape
    Nn = b.shape[1]
    tm = min(TM, M)
    return _mm(a3, b, name=name, grid=(M // tm, 1, S),
               a_spec=pl.BlockSpec((None, tm, width), lambda i, j, k: (k, i, 0)),
               b_spec=pl.BlockSpec((width, Nn), lambda i, j, k: (k, 0)),
               o_spec=pl.BlockSpec((tm, Nn), lambda i, j, k: (i, 0)),
               out_shape=_sds((M, Nn), F32), acc_shape=(tm, Nn), add=add,
               add_spec=pl.BlockSpec((tm, Nn), lambda i, j, k: (i, 0)))


def _mm_tn(a, b, *, name, tmo, tn=None):
    T, M = a.shape
    Nn = b.shape[1]
    tn = Nn if tn is None else tn
    tk = min(TK, T)
    return _mm(a, b, name=name, grid=(M // tmo, Nn // tn, T // tk), trans_a=True,
               a_spec=pl.BlockSpec((tk, tmo), lambda i, j, k: (k, i)),
               b_spec=pl.BlockSpec((tk, tn), lambda i, j, k: (k, j)),
               o_spec=pl.BlockSpec((tmo, tn), lambda i, j, k: (i, j)),
               out_shape=_sds((M, Nn), F32), acc_shape=(tmo, tn))


def _mm_tn_slab(a, b3, *, name, tmo):
    T, M = a.shape
    S, _, width = b3.shape
    tk = min(TK, T)
    return _mm(a, b3, name=name, grid=(M // tmo, S, T // tk), trans_a=True,
               a_spec=pl.BlockSpec((tk, tmo), lambda i, j, k: (k, i)),
               b_spec=pl.BlockSpec((None, tk, width), lambda i, j, k: (j, k, 0)),
               o_spec=pl.BlockSpec((tmo, width), lambda i, j, k: (i, j)),
               out_shape=_sds((M, S * width), F32), acc_shape=(tmo, width))


def _r2(a, tm):
    return (a, (tm, a.shape[1]), lambda i: (i, 0))


def _slab(a3, s, tm):
    return (a3, (None, tm, a3.shape[2]), lambda i: (s, i, 0))


def _o2(T, C, dtype, tm):
    return ((T, C), dtype, (tm, C), lambda i: (i, 0))


def _rows_fwd(fn, row_ins, par_ins, outs, *, name, nt):
    nr, npar = len(row_ins), len(par_ins)

    def body(*refs):
        vals = [r[...] for r in refs[:nr + npar]]
        res = fn(*vals)
        for o_ref, v in zip(refs[nr + npar:], res):
            o_ref[...] = v.astype(o_ref.dtype)

    return _pcall(body, name=name, grid=(nt,),
                  in_specs=[pl.BlockSpec(bs, im) for (_, bs, im) in row_ins] + [_full_spec(p) for p in par_ins],
                  out_specs=[pl.BlockSpec(bs, im) for (_, _, bs, im) in outs],
                  out_shape=[_sds(s, d) for (s, d, _, _) in outs])(*[r[0] for r in row_ins], *par_ins)


def _rows_bwd(fn, row_ins, par_ins, cot_ins, drow_outs, *, name, nt):
    nr, npar, nc = len(row_ins), len(par_ins), len(cot_ins)
    keep = [k for k, o in enumerate(drow_outs) if o is not None]

    def body(*refs):
        vals = [r[...].astype(F32) for r in refs[:nr + npar]]
        cots = [r[...].astype(F32) for r in refs[nr + npar:nr + npar + nc]]
        orefs = refs[nr + npar + nc:]
        _, vjp = jax.vjp(fn, *vals)
        grads = vjp(tuple(cots))
        for o_ref, k in zip(orefs[:len(keep)], keep):
            o_ref[...] = grads[k].astype(o_ref.dtype)
        prefs = orefs[len(keep):]

        @pl.when(pl.program_id(0) == 0)
        def _():
            for p_ref in prefs:
                p_ref[...] = jnp.zeros_like(p_ref)

        for p_ref, g in zip(prefs, grads[nr:]):
            p_ref[...] += g

    outs = [drow_outs[k] for k in keep]
    res = _pcall(body, name=name, grid=(nt,),
                 in_specs=[pl.BlockSpec(bs, im) for (_, bs, im) in row_ins] + [_full_spec(p) for p in par_ins]
                 + [pl.BlockSpec(bs, im) for (_, bs, im) in cot_ins],
                 out_specs=[pl.BlockSpec(bs, im) for (_, _, bs, im) in outs] + [_full_spec(p) for p in par_ins],
                 out_shape=[_sds(s, d) for (s, d, _, _) in outs] + [_sds(p.shape, F32) for p in par_ins],
                 )(*[r[0] for r in row_ins], *par_ins, *[c[0] for c in cot_ins])
    return list(res[:len(keep)]), list(res[len(keep):])


def _layer_norm(v, g, b):
    mu = jnp.mean(v, axis=-1, keepdims=True)
    var = jnp.mean(jnp.square(v - mu), axis=-1, keepdims=True)
    return (v - mu) * lax.rsqrt(var + LN_EPS) * g + b


def _silu(v):
    return v * jax.nn.sigmoid(v)


def _softplus(v):
    return jnp.maximum(v, 0.0) + jnp.log1p(jnp.exp(-jnp.abs(v)))


def _halo_of(K):
    return 8 * ((K - 1 + 7) // 8)


def _conv_taps(buf, p, w_ref, K, halo, tm):
    acc = None
    for k in range(K):
        term = buf[p, pl.ds(halo - (K - 1) + k, tm), :] * w_ref[p, k:k + 1, :]
        acc = term if acc is None else acc + term
    return acc


def _conv_fwd(pre, post, row_ins, w, par_ins, outs, *, K, C, name, tm, nt):
    nparts = w.shape[0]
    halo = _halo_of(K)
    nr, npar = len(row_ins), len(par_ins)

    def body(*refs):
        rows = [r[...] for r in refs[:nr]]
        w_ref = refs[nr]
        pars = [r[...] for r in refs[nr + 1:nr + 1 + npar]]
        orefs = refs[nr + 1 + npar:-1]
        buf = refs[-1]
        i = pl.program_id(0)
        xin = pre(*rows)
        cs = []
        for p in range(nparts):
            @pl.when(i == 0)
            def _():
                buf[p, pl.ds(0, halo), :] = jnp.zeros((halo, C), F32)

            @pl.when(i > 0)
            def _():
                buf[p, pl.ds(0, halo), :] = buf[p, pl.ds(tm, halo), :]

            buf[p, pl.ds(halo, tm), :] = xin[p]
            cs.append(_conv_taps(buf, p, w_ref, K, halo, tm))
        res = post(cs, *pars)
        for o_ref, v in zip(orefs, res):
            o_ref[...] = v.astype(o_ref.dtype)

    return _pcall(body, name=name, grid=(nt,),
                  in_specs=[pl.BlockSpec(bs, im) for (_, bs, im) in row_ins] + [_full_spec(w)] + [_full_spec(p) for p in par_ins],
                  out_specs=[pl.BlockSpec(bs, im) for (_, _, bs, im) in outs],
                  out_shape=[_sds(s, d) for (s, d, _, _) in outs],
                  scratch_shapes=[pltpu.VMEM((nparts, halo + tm, C), F32)])(*[r[0] for r in row_ins], w, *par_ins)


def _conv_bwd(pre, post, row_ins, halo_ins, w, par_ins, cot_ins, drow_outs, *, K, C, name, tm, nt):
    nparts = w.shape[0]
    halo = _halo_of(K)
    nr, npar, nc = len(row_ins), len(par_ins), len(cot_ins)

    def body(*refs):
        rows = [r[...].astype(F32) for r in refs[:nr]]
        halos = [r[...].astype(F32) for r in refs[nr:2 * nr]]
        w_ref = refs[2 * nr]
        pars = [r[...] for r in refs[2 * nr + 1:2 * nr + 1 + npar]]
        cots = [r[...].astype(F32) for r in refs[2 * nr + 1 + npar:2 * nr + 1 + npar + nc]]
        rest = refs[2 * nr + 1 + npar + nc:]
        drow_refs, dw_ref, dpar_refs, bufx, bufd = rest[:nr], rest[nr], rest[nr + 1:nr + 1 + npar], rest[-2], rest[-1]
        s = pl.program_id(0)
        first_tile = s == nt - 1

        @pl.when(s == 0)
        def _():
            dw_ref[...] = jnp.zeros_like(dw_ref)
            for p_ref in dpar_refs:
                p_ref[...] = jnp.zeros_like(p_ref)
            for p in range(nparts):
                bufd[p, pl.ds(tm, halo), :] = jnp.zeros((halo, C), F32)

        xin, pre_vjp = jax.vjp(pre, *rows)
        xh = pre(*halos)
        cs = []
        for p in range(nparts):
            bufx[p, pl.ds(0, halo), :] = jnp.where(first_tile, 0.0, xh[p])
            bufx[p, pl.ds(halo, tm), :] = xin[p]
            cs.append(_conv_taps(bufx, p, w_ref, K, halo, tm))
        _, post_vjp = jax.vjp(lambda c, q: post(c, *q), cs, pars)
        dcs, dpars = post_vjp(tuple(cots))
        dxin = []
        for p in range(nparts):
            dc = dcs[p]
            bufd[p, pl.ds(0, tm), :] = dc
            acc = None
            for k in range(K):
                term = bufd[p, pl.ds(K - 1 - k, tm), :] * w_ref[p, k:k + 1, :]
                acc = term if acc is None else acc + term
                dw_ref[p, k:k + 1, :] += jnp.sum(dc * bufx[p, pl.ds(halo - (K - 1) + k, tm), :], axis=0, keepdims=True)
            dxin.append(acc)
            bufd[p, pl.ds(tm, halo), :] = dc[0:halo, :]
        drows = pre_vjp(dxin)
        for o_ref, g in zip(drow_refs, drows):
            o_ref[...] = g.astype(o_ref.dtype)
        for p_ref, g in zip(dpar_refs, dpars):
            p_ref[...] += g

    rev = lambda im: (lambda s: im(nt - 1 - s))
    res = _pcall(body, name=name, grid=(nt,),
                 in_specs=[pl.BlockSpec(bs, rev(im)) for (_, bs, im) in row_ins]
                 + [pl.BlockSpec(bs, rev(im)) for (_, bs, im) in halo_ins]
                 + [_full_spec(w)] + [_full_spec(p) for p in par_ins]
                 + [pl.BlockSpec(bs, rev(im)) for (_, bs, im) in cot_ins],
                 out_specs=[pl.BlockSpec(bs, rev(im)) for (_, _, bs, im) in drow_outs] + [_full_spec(w)]
                 + [_full_spec(p) for p in par_ins],
                 out_shape=[_sds(sh, d) for (sh, d, _, _) in drow_outs] + [_sds(w.shape, F32)]
                 + [_sds(p.shape, F32) for p in par_ins],
                 scratch_shapes=[pltpu.VMEM((nparts, halo + tm, C), F32), pltpu.VMEM((nparts, tm + halo, C), F32)],
                 )(*[r[0] for r in row_ins], *[r[0] for r in halo_ins], w, *par_ins, *[c[0] for c in cot_ins])
    return list(res[:nr]), res[nr], list(res[nr + 1:])


def _halo_r2(a, tm, halo):
    q = tm // halo
    return (a, (halo, a.shape[1]), lambda i: (jnp.maximum(i * q - 1, 0), 0))


def _halo_slab(a3, s, tm, halo):
    q = tm // halo
    return (a3, (None, halo, a3.shape[2]), lambda i: (s, jnp.maximum(i * q - 1, 0), 0))


def _ssd_group(xs, dtr, Bg, Cg, zg, sp, alog, dtb, dsk, nwg, *, g):
    li = lax.broadcasted_iota(jnp.int32, (L, L), 0)
    si = lax.broadcasted_iota(jnp.int32, (L, L), 1)
    causal = li >= si
    tri = causal.astype(F32)
    hi = lax.broadcasted_iota(jnp.int32, (HP, RP), 0)
    ci = lax.broadcasted_iota(jnp.int32, (HP, RP), 1)
    lo = (hi - g * R) * P
    E = ((ci >= lo) & (ci < lo + P)).astype(F32)

    dt = _softplus(dtr + dtb)
    a = dt * (-jnp.exp(alog))
    a_cs = jnp.dot(tri, a, precision=HI, preferred_element_type=F32)
    acs_e = jnp.dot(a_cs, E, precision=HI, preferred_element_type=F32)
    dt_e = jnp.dot(dt, E, precision=HI, preferred_element_type=F32)
    alast_e = acs_e[L - 1:L, :]
    xdt = xs * dt_e
    a_csT = a_cs.T
    cb = lax.dot_general(Cg.astype(BF16), Bg.astype(BF16), (((1,), (1,)), ((), ())), preferred_element_type=F32)
    y_off = jnp.dot(Cg.astype(BF16), sp.astype(BF16), preferred_element_type=F32) * jnp.exp(acs_e)
    yd = []
    for r in range(R):
        h = g * R + r
        seg = a_cs[:, h:h + 1] - a_csT[h:h + 1, :]
        dec = jnp.where(causal, jnp.exp(jnp.where(causal, seg, 0.0)), 0.0)
        m = (cb * dec).astype(BF16)
        yd.append(jnp.dot(m, xdt[:, r * P:(r + 1) * P].astype(BF16), preferred_element_type=F32))
    y = jnp.concatenate(yd, axis=1) + y_off + xs * jnp.dot(dsk, E, precision=HI, preferred_element_type=F32)
    yg = y * _silu(zg)
    yn = yg * lax.rsqrt(jnp.mean(jnp.square(yg), axis=-1, keepdims=True) + RMS_EPS) * nwg
    sc = lax.dot_general(Bg.astype(BF16), (xdt * jnp.exp(alast_e - acs_e)).astype(BF16), (((0,), (0,)), ((), ())),
                         preferred_element_type=F32)
    return yn, jnp.exp(alast_e) * sp + sc


def _group_cols(g):
    return g // 2, (g % 2) * RP


def _ssd_fwd(x0, x1, bc, dtr, u3, alog, dtb, dsk, nw, *, name):
    T = x0.shape[0]
    nc = T // L

    def body(x0_ref, x1_ref, bc_ref, dtr_ref, z0_ref, z1_ref, alog_ref, dtb_ref, dsk_ref, nw_ref, yn_ref, sp_ref, S):
        @pl.when(pl.program_id(0) == 0)
        def _():
            S[...] = jnp.zeros_like(S)

        xr, zr = (x0_ref, x1_ref), (z0_ref, z1_ref)
        for g in range(G):
            s, off = _group_cols(g)
            sp = S[g]
            sp_ref[0, g] = sp
            yn, s_next = _ssd_group(xr[s][:, off:off + RP], dtr_ref[...], bc_ref[:, g * N:(g + 1) * N],
                                    bc_ref[:, G * N + g * N:G * N + (g + 1) * N], zr[s][:, off:off + RP], sp,
                                    alog_ref[...], dtb_ref[...], dsk_ref[...], nw_ref[:, g * RP:(g + 1) * RP], g=g)
            yn_ref[:, g * RP:(g + 1) * RP] = yn.astype(yn_ref.dtype)
            S[g] = s_next

    row = lambda C: pl.BlockSpec((L, C), lambda c: (c, 0))
    zspec = lambda s: pl.BlockSpec((None, L, D), lambda c: (s, c, 0))
    pars = [alog, dtb, dsk, nw]
    return _pcall(body, name=name, grid=(nc,),
                  in_specs=[row(D), row(D), row(D), row(HP), zspec(2), zspec(3)] + [_full_spec(p) for p in pars],
                  out_specs=[row(SD), pl.BlockSpec((1, G, N, RP), lambda c: (c, 0, 0, 0))],
                  out_shape=[_sds((T, SD), BF16), _sds((nc, G, N, RP), F32)],
                  scratch_shapes=[pltpu.VMEM((G, N, RP), F32)])(x0, x1, bc, dtr, u3, u3, *pars)


def _ssd_bwd(x0, x1, bc, dtr, u3, sprev, dyn, alog, dtb, dsk, nw, *, name):
    T = x0.shape[0]
    nc = T // L

    def body(x0_ref, x1_ref, bc_ref, dtr_ref, z0_ref, z1_ref, sp_ref, dyn_ref, alog_ref, dtb_ref, dsk_ref, nw_ref,
             dx0_ref, dx1_ref, dbc_ref, ddtr_ref, dz0_ref, dz1_ref, dalog_ref, ddtb_ref, ddsk_ref, dnw_ref, dS):
        @pl.when(pl.program_id(0) == 0)
        def _():
            dS[...] = jnp.zeros_like(dS)
            for r in (dalog_ref, ddtb_ref, ddsk_ref, dnw_ref):
                r[...] = jnp.zeros_like(r)

        xr, zr = (x0_ref, x1_ref), (z0_ref, z1_ref)
        dxr, dzr = (dx0_ref, dx1_ref), (dz0_ref, dz1_ref)
        ddtr = jnp.zeros((L, HP), F32)
        for g in range(G):
            s, off = _group_cols(g)
            _, vjp = jax.vjp(functools.partial(_ssd_group, g=g), xr[s][:, off:off + RP], dtr_ref[...],
                             bc_ref[:, g * N:(g + 1) * N], bc_ref[:, G * N + g * N:G * N + (g + 1) * N],
                             zr[s][:, off:off + RP], sp_ref[0, g], alog_ref[...], dtb_ref[...], dsk_ref[...],
                             nw_ref[:, g * RP:(g + 1) * RP])
            dxs, ddt_g, dB, dC, dz, dsp, dal, ddb, dds, dnwg = vjp((dyn_ref[:, g * RP:(g + 1) * RP], dS[g]))
            dxr[s][:, off:off + RP] = dxs
            dzr[s][:, off:off + RP] = dz.astype(dz0_ref.dtype)
            dbc_ref[:, g * N:(g + 1) * N] = dB
            dbc_ref[:, G * N + g * N:G * N + (g + 1) * N] = dC
            dS[g] = dsp
            ddtr = ddtr + ddt_g
            dalog_ref[...] += dal
            ddtb_ref[...] += ddb
            ddsk_ref[...] += dds
            dnw_ref[:, g * RP:(g + 1) * RP] += dnwg
        ddtr_ref[...] = ddtr.astype(ddtr_ref.dtype)

    row = lambda C: pl.BlockSpec((L, C), lambda c: (nc - 1 - c, 0))
    zspec = lambda s: pl.BlockSpec((None, L, D), lambda c: (s, nc - 1 - c, 0))
    pars = [alog, dtb, dsk, nw]
    return _pcall(body, name=name, grid=(nc,),
                  in_specs=[row(D), row(D), row(D), row(HP), zspec(2), zspec(3),
                            pl.BlockSpec((1, G, N, RP), lambda c: (nc - 1 - c, 0, 0, 0)), row(SD)]
                  + [_full_spec(p) for p in pars],
                  out_specs=[row(D), row(D), row(D), row(HP), row(D), row(D)] + [_full_spec(p) for p in pars],
                  out_shape=[_sds((T, D), F32)] * 3 + [_sds((T, HP), BF16), _sds((T, D), BF16), _sds((T, D), BF16)]
                  + [_sds(p.shape, F32) for p in pars],
                  scratch_shapes=[pltpu.VMEM((G, N, RP), F32)])(x0, x1, bc, dtr, u3, u3, sprev, dyn, *pars)


def _loss_head(y, target, *, name):
    T = y.shape[0]
    tm = min(TM, T)

    def body(y_ref, t_ref, loss_ref, dy_ref):
        e = y_ref[...] - t_ref[...]
        dy_ref[...] = e * (1.0 / D)

        @pl.when(pl.program_id(0) == 0)
        def _():
            loss_ref[...] = jnp.zeros_like(loss_ref)

        loss_ref[...] += 0.5 * jnp.sum(jnp.mean(jnp.square(e), axis=-1, keepdims=True), axis=0, keepdims=True)

    row = pl.BlockSpec((tm, D), lambda i: (i, 0))
    return _pcall(body, name=name, grid=(T // tm,), in_specs=[row, row],
                  out_specs=[pl.BlockSpec((1, 128), lambda i: (0, 0)), row],
                  out_shape=[_sds((1, 128), F32), _sds((T, D), F32)])(y, target)


_HBM = pl.BlockSpec(memory_space=pltpu.HBM)
_MESH = pl.DeviceIdType.MESH


def _chip_exchange(bufs, *, scatter, name):
    nb = len(bufs)

    def body(*refs):
        in_refs, out_refs = refs[:nb], refs[nb:2 * nb]
        send_sems, recv_sems, local_sems = refs[2 * nb:]
        x, y, c = lax.axis_index("x"), lax.axis_index("y"), lax.axis_index("c")
        me = 2 * x + y
        peers = [(1 - x, y), (x, 1 - y), (1 - x, 1 - y)]
        copies = []
        for b in range(nb):
            src_own = in_refs[b].at[me] if scatter else in_refs[b]
            own = pltpu.make_async_copy(src_own, out_refs[b].at[me], local_sems.at[b])
            own.start()
            copies.append(own)
        sends = []
        for b in range(nb):
            for k, (px, py) in enumerate(peers):
                src = in_refs[b].at[2 * px + py] if scatter else in_refs[b]
                cp = pltpu.make_async_remote_copy(src_ref=src, dst_ref=out_refs[b].at[me],
                                                  send_sem=send_sems.at[b, k], recv_sem=recv_sems.at[b, k],
                                                  device_id=(px, py, c), device_id_type=_MESH)
                cp.start()
                sends.append(cp)
        for b in range(nb):
            for k, (px, py) in enumerate(peers):
                slot = out_refs[b].at[2 * px + py]
                pltpu.make_async_remote_copy(src_ref=slot, dst_ref=slot, send_sem=send_sems.at[b, k],
                                             recv_sem=recv_sems.at[b, k], device_id=(px, py, c),
                                             device_id_type=_MESH).wait_recv()
        for cp in sends:
            cp.wait_send()
        for cp in copies:
            cp.wait()

    outs = [_sds(b.shape if scatter else (NCHIP,) + b.shape, b.dtype) for b in bufs]
    return _ccall(
        body, name=name, in_specs=[_HBM] * nb, out_specs=[_HBM] * nb, out_shape=outs,
        scratch_shapes=[pltpu.SemaphoreType.DMA((nb, 3)), pltpu.SemaphoreType.DMA((nb, 3)), pltpu.SemaphoreType.DMA((nb,))],
    )(*bufs)


def _core_swap(buf, *, name):
    def body(in_ref, out_ref, send_sem, recv_sem):
        x, y, c = lax.axis_index("x"), lax.axis_index("y"), lax.axis_index("c")
        cp = pltpu.make_async_remote_copy(src_ref=in_ref, dst_ref=out_ref, send_sem=send_sem, recv_sem=recv_sem,
                                          device_id=(x, y, 1 - c), device_id_type=_MESH)
        cp.start()
        cp.wait()

    return _ccall(body, name=name, in_specs=[_HBM], out_specs=_HBM, out_shape=_sds(buf.shape, buf.dtype),
                  scratch_shapes=[pltpu.SemaphoreType.DMA, pltpu.SemaphoreType.DMA])(buf)


def _all_gather8(buf, *, name):
    def body(in_ref, out_ref, send_sems, recv_sems, local_sem):
        x, y, c = lax.axis_index("x"), lax.axis_index("y"), lax.axis_index("c")
        me = 4 * x + 2 * y + c
        own = pltpu.make_async_copy(in_ref, out_ref.at[me], local_sem)
        own.start()
        flips = [(fx, fy, fc) for fx in (0, 1) for fy in (0, 1) for fc in (0, 1)][1:]
        peers = [(x ^ fx, y ^ fy, c ^ fc) for fx, fy, fc in flips]
        sends = []
        for k, peer in enumerate(peers):
            cp = pltpu.make_async_remote_copy(src_ref=in_ref, dst_ref=out_ref.at[me], send_sem=send_sems.at[k],
                                              recv_sem=recv_sems.at[k], device_id=peer, device_id_type=_MESH)
            cp.start()
            sends.append(cp)
        for k, (px, py, pc) in enumerate(peers):
            slot = out_ref.at[4 * px + 2 * py + pc]
            pltpu.make_async_remote_copy(src_ref=slot, dst_ref=slot, send_sem=send_sems.at[k], recv_sem=recv_sems.at[k],
                                         device_id=(px, py, pc), device_id_type=_MESH).wait_recv()
        for cp in sends:
            cp.wait_send()
        own.wait()

    return _ccall(body, name=name, in_specs=[_HBM], out_specs=_HBM, out_shape=_sds((8,) + buf.shape, buf.dtype),
                  scratch_shapes=[pltpu.SemaphoreType.DMA((7,)), pltpu.SemaphoreType.DMA((7,)), pltpu.SemaphoreType.DMA])(buf)


def _sum_slots(stack, *, name):
    S, Rr, C = stack.shape
    tr = min(EW_ROWS, Rr)

    def body(s_ref, o_ref):
        acc = s_ref[0].astype(F32)
        for j in range(1, S):
            acc = acc + s_ref[j].astype(F32)
        o_ref[...] = acc

    return _pcall(body, name=name, grid=(Rr // tr,), in_specs=[pl.BlockSpec((S, tr, C), lambda i: (0, i, 0))],
                  out_specs=pl.BlockSpec((tr, C), lambda i: (i, 0)), out_shape=_sds((Rr, C), F32))(stack)


def _adamw(g_parts, w, m, v, *, name):
    Rr, C = w.shape
    tr = min(EW_ROWS, Rr)
    ng = len(g_parts)
    c1 = 1.0 / (1.0 - ADAM_B1 ** ADAM_STEP)
    c2 = 1.0 / (1.0 - ADAM_B2 ** ADAM_STEP)

    def body(*refs):
        g = refs[0][...]
        for r in refs[1:ng]:
            g = g + r[...]
        w_ref, m_ref, v_ref, g_out, d_out, m_out, v_out = refs[ng:]
        mn = ADAM_B1 * m_ref[...] + (1.0 - ADAM_B1) * g
        vn = ADAM_B2 * v_ref[...] + (1.0 - ADAM_B2) * jnp.square(g)
        g_out[...] = g
        m_out[...] = mn
        v_out[...] = vn
        d_out[...] = -ADAM_LR * ((mn * c1) / (jnp.sqrt(vn * c2) + ADAM_EPS) + ADAM_WD * w_ref[...])

    spec = pl.BlockSpec((tr, C), lambda i: (i, 0))
    return _pcall(body, name=name, grid=(Rr // tr,), in_specs=[spec] * (ng + 3), out_specs=[spec] * 4,
                  out_shape=[_sds((Rr, C), F32)] * 4)(*g_parts, w, m, v)


def _pack(arrs, dtype, row_mult):
    flat = jnp.concatenate([a.reshape(-1).astype(dtype) for a in arrs])
    n = flat.shape[0]
    unit = row_mult * PACK_W
    total = unit * ((n + unit - 1) // unit)
    if total > n:
        flat = jnp.concatenate([flat, jnp.zeros((total - n,), dtype)])
    return flat.reshape(-1, PACK_W)


def _unpack(buf, shapes):
    flat = buf.reshape(-1)
    out, off = [], 0
    for s in shapes:
        n = math.prod(s)
        out.append(flat[off:off + n].reshape(s))
        off += n
    return out


def _conf_pre(a, g):
    return [a * jax.nn.sigmoid(g)]


def _conf_post(cs, cb, lg, lb):
    return (_silu(_layer_norm(cs[0] + cb, lg, lb)),)


def _xbc_pre(x0, x1, x2):
    return [x0, x1, x2]


def _xbc_post(cs, b0, b1, b2):
    return (_silu(cs[0] + b0), _silu(cs[1] + b1), _silu(cs[2] + b2))


def _ffn_pre(gate, val):
    return [gate, val]


def _ffn_post(cs, bg, bv):
    return (_silu(cs[0] + bg) * (cs[1] + bv),)


def _mix_fn(ga, gb, ya, yb):
    return (jax.nn.sigmoid(ga) * ya + jax.nn.sigmoid(gb) * yb,)


def _res_ln_fn(h, r, g, b):
    return (_layer_norm(ALPHA * h + r, g, b),)


def _ln_fn(x, g, b):
    return (_layer_norm(x, g, b),)


def _layer_fwd(h, hb, W, l):
    T = h.shape[0]
    tm = min(TM, T)
    nt = T // tm
    tmf = min(TM_FFN, T)
    ntf = T // tmf
    nm = lambda s: f"l{l}_{s}"
    u3 = _mm_nn_slab_out(hb, W["w_p"], name=nm("u"), width=D)
    dtr = _mm_nn(hb, W["w_dt"], name=nm("dt"))
    (v3,) = _conv_fwd(_conf_pre, _conf_post, [_slab(u3, 0, tm), _slab(u3, 1, tm)], W["conv_w"],
                      [W["conv_b"], W["conv_ln_g"], W["conv_ln_b"]], [_o2(T, D, BF16, tm)],
                      K=CONV_K, C=D, name=nm("conf"), tm=tm, nt=nt)
    ya = _mm_nn(v3, W["w_co"], name=nm("ya"))
    tmx = min(TM_X, T)
    x0, x1, bc = _conv_fwd(_xbc_pre, _xbc_post, [_slab(u3, 6, tmx), _slab(u3, 7, tmx), _slab(u3, 8, tmx)], W["ssm_w"],
                           W["ssm_b"], [_o2(T, D, F32, tmx)] * 3, K=SSM_K, C=D, name=nm("xbc"), tm=tmx, nt=T // tmx)
    yn, sprev = _ssd_fwd(x0, x1, bc, dtr, u3, W["a_log"], W["dt_bias"], W["d_skip"], W["norm_w"], name=nm("ssd"))
    yb = _mm_nn(yn, W["w_so"], name=nm("yb"), tk=min(SD, 1024))
    (m,) = _rows_fwd(_mix_fn, [_slab(u3, 4, tm), _slab(u3, 5, tm), _r2(ya, tm), _r2(yb, tm)], [],
                     [_o2(T, D, BF16, tm)], name=nm("mix"), nt=nt)
    mix = _mm_nn(m, W["w_o"], name=nm("wo"))
    h1, h1b = _rows_fwd(lambda a, r, g, b: _res_ln_fn(a, r, g, b) * 2, [_r2(h, tm), _r2(mix, tm)],
                        [W["ln1_g"], W["ln1_b"]], [_o2(T, D, F32, tm), _o2(T, D, BF16, tm)], name=nm("ln1"), nt=nt)
    up3 = _mm_nn_slab_out(h1b, W["w_up"], name=nm("up"), width=FFN)
    (f,) = _conv_fwd(_ffn_pre, _ffn_post, [_slab(up3, 0, tmf), _slab(up3, 1, tmf)], W["ffn_w"], W["ffn_b"],
                     [_o2(T, FFN, BF16, tmf)], K=FFN_K, C=FFN, name=nm("ffnact"), tm=tmf, nt=ntf)
    ffn = _mm_nn(f, W["w_dn"], name=nm("dn"))
    h2, h2b = _rows_fwd(lambda a, r, g, b: _res_ln_fn(a, r, g, b) * 2, [_r2(h1, tm), _r2(ffn, tm)],
                        [W["ln2_g"], W["ln2_b"]], [_o2(T, D, F32, tm), _o2(T, D, BF16, tm)], name=nm("ln2"), nt=nt)
    saved = dict(h=h, hb=hb, u3=u3, dtr=dtr, v3=v3, ya=ya, x0=x0, x1=x1, bc=bc, sprev=sprev, yn=yn, yb=yb, m=m,
                 mix=mix, h1=h1, h1b=h1b, up3=up3, f=f, ffn=ffn)
    return h2, h2b, saved


def _layer_bwd(dh2, W, sv, l):
    T = dh2.shape[0]
    tm = min(TM, T)
    nt = T // tm
    tmf = min(TM_FFN, T)
    ntf = T // tmf
    nm = lambda s: f"l{l}_{s}"
    gr = {}
    (dres2, dffn), (gr["ln2_g"], gr["ln2_b"]) = _rows_bwd(
        _res_ln_fn, [_r2(sv["h1"], tm), _r2(sv["ffn"], tm)], [W["ln2_g"], W["ln2_b"]], [_r2(dh2, tm)],
        [_o2(T, D, F32, tm), _o2(T, D, BF16, tm)], name=nm("ln2_b"), nt=nt)
    df = _mm_nn(dffn, W["w_dn_t"], name=nm("dn_dx"))
    gr["w_dn"] = _mm_tn(sv["f"], dffn, name=nm("dn_dw"), tmo=FFN // 2)
    up3 = sv["up3"]
    h8 = _halo_of(FFN_K)
    (dgate, dval), gr["ffn_w"], gr["ffn_b"] = _conv_bwd(
        _ffn_pre, _ffn_post, [_slab(up3, 0, tmf), _slab(up3, 1, tmf)],
        [_halo_slab(up3, 0, tmf, h8), _halo_slab(up3, 1, tmf, h8)], W["ffn_w"], W["ffn_b"], [_r2(df, tmf)],
        [_o2(T, FFN, BF16, tmf)] * 2, K=FFN_K, C=FFN, name=nm("ffnact_b"), tm=tmf, nt=ntf)
    dup3 = jnp.stack([dgate, dval])
    dh1 = _mm_slab_in(dup3, W["w_up_t"], name=nm("up_dx"), add=dres2)
    gr["w_up"] = _mm_tn_slab(sv["h1b"], dup3, name=nm("up_dw"), tmo=min(512, D))
    (dres1, dmix), (gr["ln1_g"], gr["ln1_b"]) = _rows_bwd(
        _res_ln_fn, [_r2(sv["h"], tm), _r2(sv["mix"], tm)], [W["ln1_g"], W["ln1_b"]], [_r2(dh1, tm)],
        [_o2(T, D, F32, tm), _o2(T, D, BF16, tm)], name=nm("ln1_b"), nt=nt)
    dm = _mm_nn(dmix, W["w_o_t"], name=nm("wo_dx"))
    gr["w_o"] = _mm_tn(sv["m"], dmix, name=nm("wo_dw"), tmo=min(512, D))
    u3 = sv["u3"]
    (dga, dgb, dya, dyb), _ = _rows_bwd(
        _mix_fn, [_slab(u3, 4, tm), _slab(u3, 5, tm), _r2(sv["ya"], tm), _r2(sv["yb"], tm)], [], [_r2(dm, tm)],
        [_o2(T, D, BF16, tm)] * 4, name=nm("mix_b"), nt=nt)
    dv3 = _mm_nn(dya, W["w_co_t"], name=nm("ya_dx"))
    gr["w_co"] = _mm_tn(sv["v3"], dya, name=nm("ya_dw"), tmo=min(512, D))
    h32 = _halo_of(CONV_K)
    (da, dg), gr["conv_w"], (gr["conv_b"], gr["conv_ln_g"], gr["conv_ln_b"]) = _conv_bwd(
        _conf_pre, _conf_post, [_slab(u3, 0, tm), _slab(u3, 1, tm)],
        [_halo_slab(u3, 0, tm, h32), _halo_slab(u3, 1, tm, h32)], W["conv_w"],
        [W["conv_b"], W["conv_ln_g"], W["conv_ln_b"]], [_r2(dv3, tm)], [_o2(T, D, BF16, tm)] * 2,
        K=CONV_K, C=D, name=nm("conf_b"), tm=tm, nt=nt)
    dyn = _mm_nn(dyb, W["w_so_t"], name=nm("yb_dx"))
    gr["w_so"] = _mm_tn(sv["yn"], dyb, name=nm("yb_dw"), tmo=min(512, SD))
    (dx0, dx1, dbc, ddtr, dz0, dz1, gr["a_log"], gr["dt_bias"], gr["d_skip"], gr["norm_w"]) = _ssd_bwd(
        sv["x0"], sv["x1"], sv["bc"], sv["dtr"], u3, sv["sprev"], dyn, W["a_log"], W["dt_bias"], W["d_skip"],
        W["norm_w"], name=nm("ssd_b"))
    h8s = _halo_of(SSM_K)
    tmx = min(TM_X, T)
    (du6, du7, du8), gr["ssm_w"], gr["ssm_b"] = _conv_bwd(
        _xbc_pre, _xbc_post, [_slab(u3, 6, tmx), _slab(u3, 7, tmx), _slab(u3, 8, tmx)],
        [_halo_slab(u3, 6, tmx, h8s), _halo_slab(u3, 7, tmx, h8s), _halo_slab(u3, 8, tmx, h8s)], W["ssm_w"], W["ssm_b"],
        [_r2(dx0, tmx), _r2(dx1, tmx), _r2(dbc, tmx)], [_o2(T, D, BF16, tmx)] * 3, K=SSM_K, C=D, name=nm("xbc_b"),
        tm=tmx, nt=T // tmx)
    du3 = jnp.stack([da, dg, dz0, dz1, dga, dgb, du6, du7, du8])
    dh_a = _mm_nn(ddtr, W["w_dt_t"], name=nm("dt_dx"), add=dres1)
    dh = _mm_slab_in(du3, W["w_p_t"], name=nm("u_dx"), add=dh_a)
    gr["w_p"] = _mm_tn_slab(sv["hb"], du3, name=nm("u_dw"), tmo=min(512, D))
    gr["w_dt"] = _mm_tn(sv["hb"], ddtr, name=nm("dt_dw"), tmo=min(512, D))
    return dh, gr


_U_SPLIT = (2 * D + SD, 2 * D + SD + XBC, 2 * D + SD + XBC + H)


def _pad_rows(a, rows):
    return jnp.concatenate([a, jnp.zeros((rows - a.shape[0],) + a.shape[1:], a.dtype)], axis=0)


def _pad_lanes(a, lanes):
    return jnp.concatenate([a, jnp.zeros(a.shape[:-1] + (lanes - a.shape[-1],), a.dtype)], axis=-1)


def _layer_weights(full, l):
    e0, e1, e2 = _U_SPLIT
    w_in = full["w_in"][l]
    w_p = jnp.concatenate([w_in[:, :e0], w_in[:, e2:], w_in[:, e0:e1]], axis=1)
    w_dt = _pad_lanes(w_in[:, e1:e2], HP)
    row = lambda a: a.reshape(1, -1)
    ssm_w = full["ssm_conv_w"][l]
    ffn_w = full["ffn_dw_w"][l]
    ssm_b = full["ssm_conv_b"][l]
    ffn_b = full["ffn_dw_b"][l]
    W = dict(
        w_p=w_p, w_p_t=w_p.T, w_dt=w_dt, w_dt_t=w_dt.T,
        w_co=full["w_conv_out"][l], w_co_t=full["w_conv_out"][l].T,
        w_so=full["w_ssm_out"][l], w_so_t=full["w_ssm_out"][l].T,
        w_o=full["w_o"][l], w_o_t=full["w_o"][l].T,
        w_up=full["w_ffn_up"][l], w_up_t=full["w_ffn_up"][l].T,
        w_dn=full["w_ffn_down"][l], w_dn_t=full["w_ffn_down"][l].T,
        conv_w=_pad_rows(full["conv_dw_w"][l], 32)[None],
        conv_b=row(full["conv_dw_b"][l]), conv_ln_g=row(full["conv_ln_g"][l]), conv_ln_b=row(full["conv_ln_b"][l]),
        ssm_w=jnp.stack([_pad_rows(ssm_w[:, p * D:(p + 1) * D], 8) for p in range(3)]),
        ssm_b=[row(ssm_b[p * D:(p + 1) * D]) for p in range(3)],
        a_log=_pad_lanes(row(full["ssm_a_log"][l]), HP), dt_bias=_pad_lanes(row(full["ssm_dt_bias"][l]), HP),
        d_skip=_pad_lanes(row(full["ssm_d"][l]), HP), norm_w=row(full["ssm_norm_w"][l]),
        ln1_g=row(full["ln1_g"][l]), ln1_b=row(full["ln1_b"][l]),
        ffn_w=jnp.stack([_pad_rows(ffn_w[:, p * FFN:(p + 1) * FFN], 8) for p in range(2)]),
        ffn_b=[row(ffn_b[p * FFN:(p + 1) * FFN]) for p in range(2)],
        ln2_g=row(full["ln2_g"][l]), ln2_b=row(full["ln2_b"][l]),
    )
    return W


def _layer_grads_to_reference_layout(gr):
    e0, e1, e2 = _U_SPLIT
    nx = XBC
    wp = gr["w_p"]
    w_in = jnp.concatenate([wp[:, :e0], wp[:, e0 + 2 * D:e0 + 2 * D + nx], gr["w_dt"][:, :H], wp[:, e0:e0 + 2 * D]], axis=1)
    return dict(
        w_in=w_in, conv_dw_w=gr["conv_w"][0, :CONV_K], conv_dw_b=gr["conv_b"][0], conv_ln_g=gr["conv_ln_g"][0],
        conv_ln_b=gr["conv_ln_b"][0], w_conv_out=gr["w_co"],
        ssm_conv_w=jnp.concatenate([gr["ssm_w"][p, :SSM_K] for p in range(3)], axis=1),
        ssm_conv_b=jnp.concatenate([b[0] for b in gr["ssm_b"]]),
        ssm_dt_bias=gr["dt_bias"][0, :H], ssm_a_log=gr["a_log"][0, :H], ssm_d=gr["d_skip"][0, :H],
        ssm_norm_w=gr["norm_w"][0], w_ssm_out=gr["w_so"], w_o=gr["w_o"], ln1_g=gr["ln1_g"][0], ln1_b=gr["ln1_b"][0],
        w_ffn_up=gr["w_up"], ffn_dw_w=jnp.concatenate([gr["ffn_w"][p, :FFN_K] for p in range(2)], axis=1),
        ffn_dw_b=jnp.concatenate([b[0] for b in gr["ffn_b"]]), w_ffn_down=gr["w_dn"], ln2_g=gr["ln2_g"][0],
        ln2_b=gr["ln2_b"][0],
    )


_BIG = dict(w_in=2, w_conv_out=1, w_ssm_out=1, w_o=1, w_ffn_up=2, w_ffn_down=1)
_SMALL_SHARDED = dict(conv_dw_w=2, ssm_conv_w=2, ffn_dw_w=2)
_REPLICATED = ("ln_in_g", "ln_in_b", "conv_dw_b", "conv_ln_g", "conv_ln_b", "ssm_conv_b", "ssm_dt_bias", "ssm_a_log",
               "ssm_d", "ssm_norm_w", "ln1_g", "ln1_b", "ffn_dw_b", "ln2_g", "ln2_b")
_WEIGHTS = ("ln_in_g", "ln_in_b", "w_in", "conv_dw_w", "conv_dw_b", "conv_ln_g", "conv_ln_b", "w_conv_out", "ssm_conv_w",
            "ssm_conv_b", "ssm_dt_bias", "ssm_a_log", "ssm_d", "ssm_norm_w", "w_ssm_out", "w_o", "ln1_g", "ln1_b",
            "w_ffn_up", "ffn_dw_w", "ffn_dw_b", "w_ffn_down", "ln2_g", "ln2_b")


def _chip_shard(a, axis, j):
    n = a.shape[axis] // NCHIP
    return lax.slice_in_dim(a, j * n, (j + 1) * n, axis=axis)


def kernel(x, ln_in_g, ln_in_b, w_in, conv_dw_w, conv_dw_b, conv_ln_g, conv_ln_b, w_conv_out, ssm_conv_w, ssm_conv_b, ssm_dt_bias, ssm_a_log, ssm_d, ssm_norm_w, w_ssm_out, w_o, ln1_g, ln1_b, w_ffn_up, ffn_dw_w, ffn_dw_b, w_ffn_down, ln2_g, ln2_b, loss_target, m_ln_in_g, m_ln_in_b, m_w_in, m_conv_dw_w, m_conv_dw_b, m_conv_ln_g, m_conv_ln_b, m_w_conv_out, m_ssm_conv_w, m_ssm_conv_b, m_ssm_dt_bias, m_ssm_a_log, m_ssm_d, m_ssm_norm_w, m_w_ssm_out, m_w_o, m_ln1_g, m_ln1_b, m_w_ffn_up, m_ffn_dw_w, m_ffn_dw_b, m_w_ffn_down, m_ln2_g, m_ln2_b, v_ln_in_g, v_ln_in_b, v_w_in, v_conv_dw_w, v_conv_dw_b, v_conv_ln_g, v_conv_ln_b, v_w_conv_out, v_ssm_conv_w, v_ssm_conv_b, v_ssm_dt_bias, v_ssm_a_log, v_ssm_d, v_ssm_norm_w, v_w_ssm_out, v_w_o, v_ln1_g, v_ln1_b, v_w_ffn_up, v_ffn_dw_w, v_ffn_dw_b, v_w_ffn_down, v_ln2_g, v_ln2_b):
    args = locals()
    w = {n: args[n] for n in _WEIGHTS}
    mom = {n: args["m_" + n] for n in _WEIGHTS}
    vel = {n: args["v_" + n] for n in _WEIGHTS}
    T = x.shape[1]
    tm = min(TM, T)
    nt = T // tm
    chip = 2 * lax.axis_index("x") + lax.axis_index("y")

    big_names, small_names = list(_BIG), list(_SMALL_SHARDED)
    send_big = _pack([w[n] for n in big_names], BF16, 16)
    send_small = _pack([w[n] for n in small_names], F32, 8)
    got_big, got_small = _chip_exchange([send_big, send_small], scatter=False, name="gather_weights")
    full = {}
    for names, got, axes in ((big_names, got_big, _BIG), (small_names, got_small, _SMALL_SHARDED)):
        per_chip = [_unpack(got[j], [w[n].shape for n in names]) for j in range(NCHIP)]
        for k, n in enumerate(names):
            full[n] = jnp.concatenate([per_chip[j][k] for j in range(NCHIP)], axis=axes[n])
    for n in _REPLICATED:
        full[n] = w[n]
    Ws = [_layer_weights(full, l) for l in range(DEPTH)]

    x2 = x.reshape(T, D)
    g_in, b_in = ln_in_g.reshape(1, D), ln_in_b.reshape(1, D)
    h, hb = _rows_fwd(lambda a, g, b: _ln_fn(a, g, b) * 2, [_r2(x2, tm)], [g_in, b_in],
                      [_o2(T, D, F32, tm), _o2(T, D, BF16, tm)], name="ln_in", nt=nt)
    saved = []
    for l in range(DEPTH):
        h, hb, sv = _layer_fwd(h, hb, Ws[l], l)
        saved.append(sv)
    loss_row, dh = _loss_head(h, loss_target.reshape(T, D), name="loss")

    layer_grads = [None] * DEPTH
    for l in reversed(range(DEPTH)):
        dh, gr = _layer_bwd(dh, Ws[l], saved[l], l)
        layer_grads[l] = _layer_grads_to_reference_layout(gr)
    (grad_x2,), (d_g_in, d_b_in) = _rows_bwd(_ln_fn, [_r2(x2, tm)], [g_in, b_in], [_r2(dh, tm)], [_o2(T, D, F32, tm)],
                                             name="ln_in_b", nt=nt)
    local = {n: jnp.stack([layer_grads[l][n] for l in range(DEPTH)]) for n in _WEIGHTS[2:]}
    local["ln_in_g"], local["ln_in_b"] = d_g_in[0], d_b_in[0]

    send = jnp.stack([_pack([_chip_shard(local[n], _BIG[n], j) for n in big_names], BF16, EW_ROWS) for j in range(NCHIP)])
    (got,) = _chip_exchange([send], scatter=True, name="exchange_grads")
    mine = _sum_slots(got, name="sum_chips")
    other = _core_swap(mine, name="swap_cores")
    pk = lambda d: _pack([d[n] for n in big_names], F32, EW_ROWS)
    big_out = _adamw([mine, other], pk(w), pk(mom), pk(vel), name="adamw_big")
    big_out = [_unpack(o, [w[n].shape for n in big_names]) for o in big_out]

    rest_names = list(_REPLICATED) + small_names
    part = _pack([loss_row] + [local[n] for n in rest_names], F32, 8)
    parts = _all_gather8(part, name="gather_small")
    total = _sum_slots(parts, name="sum_devices")
    tot = _unpack(total, [loss_row.shape] + [local[n].shape for n in rest_names])
    loss = tot[0][0, 0]
    g_rest = {}
    for n, t in zip(rest_names, tot[1:]):
        if n in _SMALL_SHARDED:
            ax = _SMALL_SHARDED[n]
            t = lax.dynamic_slice_in_dim(t, chip * w[n].shape[ax], w[n].shape[ax], axis=ax)
        g_rest[n] = t
    pk = lambda d: _pack([d[n] for n in rest_names], F32, 8)
    rest_out = _adamw([pk(g_rest)], pk(w), pk(mom), pk(vel), name="adamw_rest")
    rest_out = [_unpack(o, [w[n].shape for n in rest_names]) for o in rest_out]

    res = [{}, {}, {}, {}]
    for names, outs in ((big_names, big_out), (rest_names, rest_out)):
        for q in range(4):
            for k, n in enumerate(names):
                res[q][n] = outs[q][k]
    grad_x = grad_x2.reshape(x.shape)
    return (loss, grad_x, *[res[0][n] for n in _WEIGHTS], *[res[1][n] for n in _WEIGHTS],
            *[res[2][n] for n in _WEIGHTS], *[res[3][n] for n in _WEIGHTS])
```

```python
import functools
import math

import jax
import jax.numpy as jnp
from jax import lax
from jax.experimental import pallas as pl
from jax.experimental.pallas import tpu as pltpu

F32 = jnp.float32
BF16 = jnp.bfloat16
HI = lax.Precision.HIGHEST

D = 1024
DEPTH = 2
CONV_K = 31
SD = 2 * D
P = 64
H = SD // P
G = 4
R = H // G
N = 128
RP = R * P
SSM_K = 4
L = 128
XBC = SD + 2 * G * N
FFN = 2816
FFN_K = 3
IN_DIM = 2 * D + SD + XBC + H + 2 * D
ALPHA = (2 * DEPTH) ** 0.25
LN_EPS = 1e-5
RMS_EPS = 1e-5
ADAM_LR, ADAM_B1, ADAM_B2, ADAM_EPS, ADAM_WD, ADAM_STEP = 0.001, 0.9, 0.999, 1e-08, 0.01, 10

HP = 128
NCHIP = 4
PACK_W = 1024
VMEM_LIMIT = 56 * 1024 * 1024
TM = 512
TM_X = 256
TM_FFN = 128
TK = 1024
EW_ROWS = 512

assert D == 2 * RP and 2 * G * N == D and XBC == 3 * D and H <= HP


def _pcall(body, *, name, grid=(), in_specs, out_specs, out_shape, scratch_shapes=()):
    params = pltpu.CompilerParams(vmem_limit_bytes=VMEM_LIMIT, dimension_semantics=("arbitrary",) * len(grid))
    return pl.pallas_call(body, name=name, grid=grid, in_specs=in_specs, out_specs=out_specs, out_shape=out_shape,
                          scratch_shapes=list(scratch_shapes), compiler_params=params)


def _ccall(body, *, name, in_specs, out_specs, out_shape, scratch_shapes):
    return pl.pallas_call(body, name=name, in_specs=in_specs, out_specs=out_specs, out_shape=out_shape,
                          scratch_shapes=list(scratch_shapes))


def _full_spec(a):
    nd = a.ndim
    return pl.BlockSpec(a.shape, lambda *_: (0,) * nd)


def _sds(shape, dtype):
    return jax.ShapeDtypeStruct(tuple(shape), dtype)


def _mm(a, b, *, name, grid, a_spec, b_spec, o_spec, out_shape, acc_shape, trans_a=False, trans_b=False, add=None,
        add_spec=None):
    nk = grid[2]
    dn = (((0 if trans_a else 1,), (1 if trans_b else 0,)), ((), ()))
    has_add = add is not None

    def body(*refs):
        a_ref, b_ref = refs[0], refs[1]
        add_ref = refs[2] if has_add else None
        o_ref = refs[3] if has_add else refs[2]
        part = lax.dot_general(a_ref[...].astype(BF16), b_ref[...].astype(BF16), dn, preferred_element_type=F32)

        def finish(res):
            if has_add:
                res = res + add_ref[...]
            o_ref[...] = res.astype(o_ref.dtype)

        if nk == 1:
            finish(part)
        else:
            acc = refs[-1]
            k = pl.program_id(2)

            @pl.when(k == 0)
            def _():
                acc[...] = part

            @pl.when(k > 0)
            def _():
                acc[...] += part

            @pl.when(k == nk - 1)
            def _():
                finish(acc[...])

    ins = [a, b] + ([add] if has_add else [])
    specs = [a_spec, b_spec] + ([add_spec] if has_add else [])
    scratch = [pltpu.VMEM(acc_shape, F32)] if nk > 1 else []
    return _pcall(body, name=name, grid=grid, in_specs=specs, out_specs=o_spec, out_shape=out_shape,
                  scratch_shapes=scratch)(*ins)


def _mm_nn(a, b, *, name, out_dtype=F32, tn=None, tk=None, add=None):
    M, K = a.shape
    Nn = b.shape[1]
    tm = min(TM, M)
    tn = Nn if tn is None else tn
    tk = K if tk is None else tk
    grid = (M // tm, Nn // tn, K // tk)
    return _mm(a, b, name=name, grid=grid,
               a_spec=pl.BlockSpec((tm, tk), lambda i, j, k: (i, k)),
               b_spec=pl.BlockSpec((tk, tn), lambda i, j, k: (k, j)),
               o_spec=pl.BlockSpec((tm, tn), lambda i, j, k: (i, j)),
               out_shape=_sds((M, Nn), out_dtype), acc_shape=(tm, tn), add=add,
               add_spec=pl.BlockSpec((tm, tn), lambda i, j, k: (i, j)))


def _mm_nn_slab_out(a, b, *, name, width):
    M, K = a.shape
    S = b.shape[1] // width
    tm = min(TM, M)
    return _mm(a, b, name=name, grid=(M // tm, S, 1),
               a_spec=pl.BlockSpec((tm, K), lambda i, j, k: (i, 0)),
               b_spec=pl.BlockSpec((K, width), lambda i, j, k: (0, j)),
               o_spec=pl.BlockSpec((None, tm, width), lambda i, j, k: (j, i, 0)),
               out_shape=_sds((S, M, width), F32), acc_shape=(tm, width))


def _mm_nt(a, b, *, name, add=None):
    M, K = a.shape
    Nn = b.shape[0]
    tm = min(TM, M)
    return _mm(a, b, name=name, grid=(M // tm, 1, 1), trans_b=True,
               a_spec=pl.BlockSpec((tm, K), lambda i, j, k: (i, 0)),
               b_spec=pl.BlockSpec((Nn, K), lambda i, j, k: (0, 0)),
               o_spec=pl.BlockSpec((tm, Nn), lambda i, j, k: (i, 0)),
               out_shape=_sds((M, Nn), F32), acc_shape=(tm, Nn), add=add,
               add_spec=pl.BlockSpec((tm, Nn), lambda i, j, k: (i, 0)))


def _mm_slab_in_nt(a3, b, *, name, add):
    S, M, width = a3.shape
    Nn = b.shape[0]
    tm = min(TM, M)
    return _mm(a3, b, name=name, grid=(M // tm, 1, S), trans_b=True,
               a_spec=pl.BlockSpec((None, tm, width), lambda i, j, k: (k, i, 0)),
               b_spec=pl.BlockSpec((Nn, width), lambda i, j, k: (0, k)),
               o_spec=pl.BlockSpec((tm, Nn), lambda i, j, k: (i, 0)),
               out_shape=_sds((M, Nn), F32), acc_shape=(tm, Nn), add=add,
               add_spec=pl.BlockSpec((tm, Nn), lambda i, j, k: (i, 0)))


def _mm_tn(a, b, *, name, tmo, tn=None):
    T, M = a.shape
    Nn = b.shape[1]
    tn = Nn if tn is None else tn
    tk = min(TK, T)
    return _mm(a, b, name=name, grid=(M // tmo, Nn // tn, T // tk), trans_a=True,
               a_spec=pl.BlockSpec((tk, tmo), lambda i, j, k: (k, i)),
               b_spec=pl.BlockSpec((tk, tn), lambda i, j, k: (k, j)),
               o_spec=pl.BlockSpec((tmo, tn), lambda i, j, k: (i, j)),
               out_shape=_sds((M, Nn), BF16), acc_shape=(tmo, tn))


def _mm_tn_slab(a, b3, *, name, tmo):
    T, M = a.shape
    S, _, width = b3.shape
    tk = min(TK, T)
    return _mm(a, b3, name=name, grid=(M // tmo, S, T // tk), trans_a=True,
               a_spec=pl.BlockSpec((tk, tmo), lambda i, j, k: (k, i)),
               b_spec=pl.BlockSpec((None, tk, width), lambda i, j, k: (j, k, 0)),
               o_spec=pl.BlockSpec((tmo, width), lambda i, j, k: (i, j)),
               out_shape=_sds((M, S * width), BF16), acc_shape=(tmo, width))


def _r2(a, tm):
    return (a, (tm, a.shape[1]), lambda i: (i, 0))


def _slab(a3, s, tm):
    return (a3, (None, tm, a3.shape[2]), lambda i: (s, i, 0))


def _o2(T, C, dtype, tm):
    return ((T, C), dtype, (tm, C), lambda i: (i, 0))


def _rows_fwd(fn, row_ins, par_ins, outs, *, name, nt):
    nr, npar = len(row_ins), len(par_ins)

    def body(*refs):
        vals = [r[...] for r in refs[:nr + npar]]
        res = fn(*vals)
        for o_ref, v in zip(refs[nr + npar:], res):
            o_ref[...] = v.astype(o_ref.dtype)

    return _pcall(body, name=name, grid=(nt,),
                  in_specs=[pl.BlockSpec(bs, im) for (_, bs, im) in row_ins] + [_full_spec(p) for p in par_ins],
                  out_specs=[pl.BlockSpec(bs, im) for (_, _, bs, im) in outs],
                  out_shape=[_sds(s, d) for (s, d, _, _) in outs])(*[r[0] for r in row_ins], *par_ins)


def _rows_bwd(fn, row_ins, par_ins, cot_ins, drow_outs, *, name, nt):
    nr, npar, nc = len(row_ins), len(par_ins), len(cot_ins)
    keep = [k for k, o in enumerate(drow_outs) if o is not None]

    def body(*refs):
        vals = [r[...].astype(F32) for r in refs[:nr + npar]]
        cots = [r[...].astype(F32) for r in refs[nr + npar:nr + npar + nc]]
        orefs = refs[nr + npar + nc:]
        _, vjp = jax.vjp(fn, *vals)
        grads = vjp(tuple(cots))
        for o_ref, k in zip(orefs[:len(keep)], keep):
            o_ref[...] = grads[k].astype(o_ref.dtype)
        prefs = orefs[len(keep):]

        @pl.when(pl.program_id(0) == 0)
        def _():
            for p_ref in prefs:
                p_ref[...] = jnp.zeros_like(p_ref)

        for p_ref, g in zip(prefs, grads[nr:]):
            p_ref[...] += g

    outs = [drow_outs[k] for k in keep]
    res = _pcall(body, name=name, grid=(nt,),
                 in_specs=[pl.BlockSpec(bs, im) for (_, bs, im) in row_ins] + [_full_spec(p) for p in par_ins]
                 + [pl.BlockSpec(bs, im) for (_, bs, im) in cot_ins],
                 out_specs=[pl.BlockSpec(bs, im) for (_, _, bs, im) in outs] + [_full_spec(p) for p in par_ins],
                 out_shape=[_sds(s, d) for (s, d, _, _) in outs] + [_sds(p.shape, F32) for p in par_ins],
                 )(*[r[0] for r in row_ins], *par_ins, *[c[0] for c in cot_ins])
    return list(res[:len(keep)]), list(res[len(keep):])


def _layer_norm(v, g, b):
    mu = jnp.mean(v, axis=-1, keepdims=True)
    var = jnp.mean(jnp.square(v - mu), axis=-1, keepdims=True)
    return (v - mu) * lax.rsqrt(var + LN_EPS) * g + b


def _silu(v):
    return v * jax.nn.sigmoid(v)


def _softplus(v):
    return jnp.maximum(v, 0.0) + jnp.log1p(jnp.exp(-jnp.abs(v)))


def _halo_of(K):
    return 8 * ((K - 1 + 7) // 8)


def _conv_taps(buf, p, w_ref, K, halo, tm):
    acc = None
    for k in range(K):
        term = buf[p, pl.ds(halo - (K - 1) + k, tm), :] * w_ref[p, k:k + 1, :]
        acc = term if acc is None else acc + term
    return acc


def _conv_fwd(pre, post, row_ins, w, par_ins, outs, *, K, C, name, tm, nt):
    nparts = w.shape[0]
    halo = _halo_of(K)
    nr, npar = len(row_ins), len(par_ins)

    def body(*refs):
        rows = [r[...] for r in refs[:nr]]
        w_ref = refs[nr]
        pars = [r[...] for r in refs[nr + 1:nr + 1 + npar]]
        orefs = refs[nr + 1 + npar:-1]
        buf = refs[-1]
        i = pl.program_id(0)
        xin = pre(*rows)
        cs = []
        for p in range(nparts):
            @pl.when(i == 0)
            def _():
                buf[p, pl.ds(0, halo), :] = jnp.zeros((halo, C), F32)

            @pl.when(i > 0)
            def _():
                buf[p, pl.ds(0, halo), :] = buf[p, pl.ds(tm, halo), :]

            buf[p, pl.ds(halo, tm), :] = xin[p]
            cs.append(_conv_taps(buf, p, w_ref, K, halo, tm))
        res = post(cs, *pars)
        for o_ref, v in zip(orefs, res):
            o_ref[...] = v.astype(o_ref.dtype)

    return _pcall(body, name=name, grid=(nt,),
                  in_specs=[pl.BlockSpec(bs, im) for (_, bs, im) in row_ins] + [_full_spec(w)] + [_full_spec(p) for p in par_ins],
                  out_specs=[pl.BlockSpec(bs, im) for (_, _, bs, im) in outs],
                  out_shape=[_sds(s, d) for (s, d, _, _) in outs],
                  scratch_shapes=[pltpu.VMEM((nparts, halo + tm, C), F32)])(*[r[0] for r in row_ins], w, *par_ins)


def _conv_bwd(pre, post, row_ins, halo_ins, w, par_ins, cot_ins, drow_outs, *, K, C, name, tm, nt):
    nparts = w.shape[0]
    halo = _halo_of(K)
    nr, npar, nc = len(row_ins), len(par_ins), len(cot_ins)

    def body(*refs):
        rows = [r[...].astype(F32) for r in refs[:nr]]
        halos = [r[...].astype(F32) for r in refs[nr:2 * nr]]
        w_ref = refs[2 * nr]
        pars = [r[...] for r in refs[2 * nr + 1:2 * nr + 1 + npar]]
        cots = [r[...].astype(F32) for r in refs[2 * nr + 1 + npar:2 * nr + 1 + npar + nc]]
        rest = refs[2 * nr + 1 + npar + nc:]
        drow_refs, dw_ref, dpar_refs, bufx, bufd = rest[:nr], rest[nr], rest[nr + 1:nr + 1 + npar], rest[-2], rest[-1]
        s = pl.program_id(0)
        first_tile = s == nt - 1

        @pl.when(s == 0)
        def _():
            dw_ref[...] = jnp.zeros_like(dw_ref)
            for p_ref in dpar_refs:
                p_ref[...] = jnp.zeros_like(p_ref)
            for p in range(nparts):
                bufd[p, pl.ds(tm, halo), :] = jnp.zeros((halo, C), F32)

        xin, pre_vjp = jax.vjp(pre, *rows)
        xh = pre(*halos)
        cs = []
        for p in range(nparts):
            bufx[p, pl.ds(0, halo), :] = jnp.where(first_tile, 0.0, xh[p])
            bufx[p, pl.ds(halo, tm), :] = xin[p]
            cs.append(_conv_taps(bufx, p, w_ref, K, halo, tm))
        _, post_vjp = jax.vjp(lambda c, q: post(c, *q), cs, pars)
        dcs, dpars = post_vjp(tuple(cots))
        dxin = []
        for p in range(nparts):
            dc = dcs[p]
            bufd[p, pl.ds(0, tm), :] = dc
            acc = None
            for k in range(K):
                term = bufd[p, pl.ds(K - 1 - k, tm), :] * w_ref[p, k:k + 1, :]
                acc = term if acc is None else acc + term
                dw_ref[p, k:k + 1, :] += jnp.sum(dc * bufx[p, pl.ds(halo - (K - 1) + k, tm), :], axis=0, keepdims=True)
            dxin.append(acc)
            bufd[p, pl.ds(tm, halo), :] = dc[0:halo, :]
        drows = pre_vjp(dxin)
        for o_ref, g in zip(drow_refs, drows):
            o_ref[...] = g.astype(o_ref.dtype)
        for p_ref, g in zip(dpar_refs, dpars):
            p_ref[...] += g

    rev = lambda im: (lambda s: im(nt - 1 - s))
    res = _pcall(body, name=name, grid=(nt,),
                 in_specs=[pl.BlockSpec(bs, rev(im)) for (_, bs, im) in row_ins]
                 + [pl.BlockSpec(bs, rev(im)) for (_, bs, im) in halo_ins]
                 + [_full_spec(w)] + [_full_spec(p) for p in par_ins]
                 + [pl.BlockSpec(bs, rev(im)) for (_, bs, im) in cot_ins],
                 out_specs=[pl.BlockSpec(bs, rev(im)) for (_, _, bs, im) in drow_outs] + [_full_spec(w)]
                 + [_full_spec(p) for p in par_ins],
                 out_shape=[_sds(sh, d) for (sh, d, _, _) in drow_outs] + [_sds(w.shape, F32)]
                 + [_sds(p.shape, F32) for p in par_ins],
                 scratch_shapes=[pltpu.VMEM((nparts, halo + tm, C), F32), pltpu.VMEM((nparts, tm + halo, C), F32)],
                 )(*[r[0] for r in row_ins], *[r[0] for r in halo_ins], w, *par_ins, *[c[0] for c in cot_ins])
    return list(res[:nr]), res[nr], list(res[nr + 1:])


def _halo_r2(a, tm, halo):
    q = tm // halo
    return (a, (halo, a.shape[1]), lambda i: (jnp.maximum(i * q - 1, 0), 0))


def _halo_slab(a3, s, tm, halo):
    q = tm // halo
    return (a3, (None, halo, a3.shape[2]), lambda i: (s, jnp.maximum(i * q - 1, 0), 0))


def _dg(a, b, ca, cb):
    return lax.dot_general(a.astype(BF16), b.astype(BF16), (((ca,), (cb,)), ((), ())), preferred_element_type=F32)


@jax.custom_vjp
def _dot_nn(a, b):
    return _dg(a, b, 1, 0)


_dot_nn.defvjp(lambda a, b: (_dg(a, b, 1, 0), (a, b)),
               lambda res, g: (_dg(g, res[1], 1, 1), _dg(res[0], g, 0, 0)))


@jax.custom_vjp
def _dot_nt(a, b):
    return _dg(a, b, 1, 1)


_dot_nt.defvjp(lambda a, b: (_dg(a, b, 1, 1), (a, b)),
               lambda res, g: (_dg(g, res[1], 1, 0), _dg(g, res[0], 0, 0)))


@jax.custom_vjp
def _dot_tn(a, b):
    return _dg(a, b, 0, 0)


_dot_tn.defvjp(lambda a, b: (_dg(a, b, 0, 0), (a, b)),
               lambda res, g: (_dg(res[1], g, 1, 1), _dg(res[0], g, 1, 0)))


def _split3(v):
    hi = v.astype(BF16)
    r = v - hi.astype(F32)
    mid = r.astype(BF16)
    return hi, mid, (r - mid.astype(F32)).astype(BF16)


def _x01(v, m, cv, cm, m_left=False):
    acc = None
    for piece in _split3(v):
        t = _dg(m, piece, cm, cv) if m_left else _dg(piece, m, cv, cm)
        acc = t if acc is None else acc + t
    return acc


@jax.custom_vjp
def _expand01(v, m):
    return _x01(v, m, 1, 0)


_expand01.defvjp(lambda v, m: (_x01(v, m, 1, 0), m),
                 lambda m, g: (_x01(g, m, 1, 1), jnp.zeros_like(m)))


@jax.custom_vjp
def _mix01(m, v):
    return _x01(v, m, 0, 1, m_left=True)


_mix01.defvjp(lambda m, v: (_x01(v, m, 0, 1, m_left=True), m),
              lambda m, g: (jnp.zeros_like(m), _x01(g, m, 0, 0, m_left=True)))


def _ssd_group(xs, dtr, Bg, Cg, zg, sp, alog, dtb, dsk, nwg, *, g):
    li = lax.broadcasted_iota(jnp.int32, (L, L), 0)
    si = lax.broadcasted_iota(jnp.int32, (L, L), 1)
    causal = li >= si
    tri = causal.astype(F32)
    hi = lax.broadcasted_iota(jnp.int32, (HP, RP), 0)
    ci = lax.broadcasted_iota(jnp.int32, (HP, RP), 1)
    lo = (hi - g * R) * P
    E = ((ci >= lo) & (ci < lo + P)).astype(F32)

    dt = _softplus(dtr + dtb)
    a = dt * (-jnp.exp(alog))
    a_cs = _mix01(tri, a)
    acs_e = _expand01(a_cs, E)
    dt_e = _expand01(dt, E)
    alast_e = acs_e[L - 1:L, :]
    xdt = xs * dt_e
    a_csT = a_cs.T
    cb = _dot_nt(Cg, Bg)
    y_off = _dot_nn(Cg, sp) * jnp.exp(acs_e)
    yd = []
    for r in range(R):
        h = g * R + r
        seg = a_cs[:, h:h + 1] - a_csT[h:h + 1, :]
        dec = jnp.where(causal, jnp.exp(jnp.where(causal, seg, 0.0)), 0.0)
        yd.append(_dot_nn(cb * dec, xdt[:, r * P:(r + 1) * P]))
    y = jnp.concatenate(yd, axis=1) + y_off + xs * _expand01(jnp.broadcast_to(dsk, (8, HP)), E)[0:1, :]
    yg = y * _silu(zg)
    yn = yg * lax.rsqrt(jnp.mean(jnp.square(yg), axis=-1, keepdims=True) + RMS_EPS) * nwg
    sc = _dot_tn(Bg, xdt * jnp.exp(alast_e - acs_e))
    return yn, jnp.exp(alast_e) * sp + sc


def _group_cols(g):
    return g // 2, (g % 2) * RP


def _ssd_fwd(x0, x1, bc, dtr, u3, alog, dtb, dsk, nw, *, name):
    T = x0.shape[0]
    nc = T // L

    def body(x0_ref, x1_ref, bc_ref, dtr_ref, z0_ref, z1_ref, alog_ref, dtb_ref, dsk_ref, nw_ref, yn_ref, sp_ref, S):
        @pl.when(pl.program_id(0) == 0)
        def _():
            S[...] = jnp.zeros_like(S)

        xr, zr = (x0_ref, x1_ref), (z0_ref, z1_ref)
        for g in range(G):
            s, off = _group_cols(g)
            sp = S[g]
            sp_ref[0, g] = sp
            yn, s_next = _ssd_group(xr[s][:, off:off + RP], dtr_ref[...], bc_ref[:, g * N:(g + 1) * N],
                                    bc_ref[:, G * N + g * N:G * N + (g + 1) * N], zr[s][:, off:off + RP], sp,
                                    alog_ref[...], dtb_ref[...], dsk_ref[...], nw_ref[:, g * RP:(g + 1) * RP], g=g)
            yn_ref[:, g * RP:(g + 1) * RP] = yn.astype(yn_ref.dtype)
            S[g] = s_next

    row = lambda C: pl.BlockSpec((L, C), lambda c: (c, 0))
    zspec = lambda s: pl.BlockSpec((None, L, D), lambda c: (s, c, 0))
    pars = [alog, dtb, dsk, nw]
    return _pcall(body, name=name, grid=(nc,),
                  in_specs=[row(D), row(D), row(D), row(HP), zspec(2), zspec(3)] + [_full_spec(p) for p in pars],
                  out_specs=[row(SD), pl.BlockSpec((1, G, N, RP), lambda c: (c, 0, 0, 0))],
                  out_shape=[_sds((T, SD), BF16), _sds((nc, G, N, RP), F32)],
                  scratch_shapes=[pltpu.VMEM((G, N, RP), F32)])(x0, x1, bc, dtr, u3, u3, *pars)


def _ssd_bwd(x0, x1, bc, dtr, u3, sprev, dyn, alog, dtb, dsk, nw, *, name):
    T = x0.shape[0]
    nc = T // L

    def body(x0_ref, x1_ref, bc_ref, dtr_ref, z0_ref, z1_ref, sp_ref, dyn_ref, alog_ref, dtb_ref, dsk_ref, nw_ref,
             dx0_ref, dx1_ref, dbc_ref, ddtr_ref, dz0_ref, dz1_ref, dalog_ref, ddtb_ref, ddsk_ref, dnw_ref, dS):
        @pl.when(pl.program_id(0) == 0)
        def _():
            dS[...] = jnp.zeros_like(dS)
            for r in (dalog_ref, ddtb_ref, ddsk_ref, dnw_ref):
                r[...] = jnp.zeros_like(r)

        xr, zr = (x0_ref, x1_ref), (z0_ref, z1_ref)
        dxr, dzr = (dx0_ref, dx1_ref), (dz0_ref, dz1_ref)
        ddtr = jnp.zeros((L, HP), F32)
        for g in range(G):
            s, off = _group_cols(g)
            _, vjp = jax.vjp(functools.partial(_ssd_group, g=g), xr[s][:, off:off + RP], dtr_ref[...],
                             bc_ref[:, g * N:(g + 1) * N], bc_ref[:, G * N + g * N:G * N + (g + 1) * N],
                             zr[s][:, off:off + RP], sp_ref[0, g], alog_ref[...], dtb_ref[...], dsk_ref[...],
                             nw_ref[:, g * RP:(g + 1) * RP])
            dxs, ddt_g, dB, dC, dz, dsp, dal, ddb, dds, dnwg = vjp((dyn_ref[:, g * RP:(g + 1) * RP], dS[g]))
            dxr[s][:, off:off + RP] = dxs
            dzr[s][:, off:off + RP] = dz.astype(dz0_ref.dtype)
            dbc_ref[:, g * N:(g + 1) * N] = dB
            dbc_ref[:, G * N + g * N:G * N + (g + 1) * N] = dC
            dS[g] = dsp
            ddtr = ddtr + ddt_g
            dalog_ref[...] += dal
            ddtb_ref[...] += ddb
            ddsk_ref[...] += dds
            dnw_ref[:, g * RP:(g + 1) * RP] += dnwg
        ddtr_ref[...] = ddtr.astype(ddtr_ref.dtype)

    row = lambda C: pl.BlockSpec((L, C), lambda c: (nc - 1 - c, 0))
    zspec = lambda s: pl.BlockSpec((None, L, D), lambda c: (s, nc - 1 - c, 0))
    pars = [alog, dtb, dsk, nw]
    return _pcall(body, name=name, grid=(nc,),
                  in_specs=[row(D), row(D), row(D), row(HP), zspec(2), zspec(3),
                            pl.BlockSpec((1, G, N, RP), lambda c: (nc - 1 - c, 0, 0, 0)), row(SD)]
                  + [_full_spec(p) for p in pars],
                  out_specs=[row(D), row(D), row(D), row(HP), row(D), row(D)] + [_full_spec(p) for p in pars],
                  out_shape=[_sds((T, D), F32)] * 3 + [_sds((T, HP), BF16), _sds((T, D), BF16), _sds((T, D), BF16)]
                  + [_sds(p.shape, F32) for p in pars],
                  scratch_shapes=[pltpu.VMEM((G, N, RP), F32)])(x0, x1, bc, dtr, u3, u3, sprev, dyn, *pars)


def _loss_head(y, target, *, name):
    T = y.shape[0]
    tm = min(TM, T)

    def body(y_ref, t_ref, loss_ref, dy_ref):
        e = y_ref[...] - t_ref[...]
        dy_ref[...] = e * (1.0 / D)

        @pl.when(pl.program_id(0) == 0)
        def _():
            loss_ref[...] = jnp.zeros_like(loss_ref)

        loss_ref[...] += 0.5 * jnp.sum(jnp.mean(jnp.square(e), axis=-1, keepdims=True), axis=0, keepdims=True)

    row = pl.BlockSpec((tm, D), lambda i: (i, 0))
    return _pcall(body, name=name, grid=(T // tm,), in_specs=[row, row],
                  out_specs=[pl.BlockSpec((1, 128), lambda i: (0, 0)), row],
                  out_shape=[_sds((1, 128), F32), _sds((T, D), F32)])(y, target)


_HBM = pl.BlockSpec(memory_space=pltpu.HBM)
_MESH = pl.DeviceIdType.MESH


def _chip_exchange(bufs, *, scatter, name):
    nb = len(bufs)

    def body(*refs):
        in_refs, out_refs = refs[:nb], refs[nb:2 * nb]
        send_sems, recv_sems, local_sems = refs[2 * nb:]
        x, y, c = lax.axis_index("x"), lax.axis_index("y"), lax.axis_index("c")
        me = 2 * x + y
        peers = [(1 - x, y), (x, 1 - y), (1 - x, 1 - y)]
        copies = []
        for b in range(nb):
            src_own = in_refs[b].at[me] if scatter else in_refs[b]
            own = pltpu.make_async_copy(src_own, out_refs[b].at[me], local_sems.at[b])
            own.start()
            copies.append(own)
        sends = []
        for b in range(nb):
            for k, (px, py) in enumerate(peers):
                src = in_refs[b].at[2 * px + py] if scatter else in_refs[b]
                cp = pltpu.make_async_remote_copy(src_ref=src, dst_ref=out_refs[b].at[me],
                                                  send_sem=send_sems.at[b, k], recv_sem=recv_sems.at[b, k],
                                                  device_id=(px, py, c), device_id_type=_MESH)
                cp.start()
                sends.append(cp)
        for b in range(nb):
            for k, (px, py) in enumerate(peers):
                slot = out_refs[b].at[2 * px + py]
                pltpu.make_async_remote_copy(src_ref=slot, dst_ref=slot, send_sem=send_sems.at[b, k],
                                             recv_sem=recv_sems.at[b, k], device_id=(px, py, c),
                                             device_id_type=_MESH).wait_recv()
        for cp in sends:
            cp.wait_send()
        for cp in copies:
            cp.wait()

    outs = [_sds(b.shape if scatter else (NCHIP,) + b.shape, b.dtype) for b in bufs]
    return _ccall(
        body, name=name, in_specs=[_HBM] * nb, out_specs=[_HBM] * nb, out_shape=outs,
        scratch_shapes=[pltpu.SemaphoreType.DMA((nb, 3)), pltpu.SemaphoreType.DMA((nb, 3)), pltpu.SemaphoreType.DMA((nb,))],
    )(*bufs)


def _core_swap(bufs, *, name):
    nb = len(bufs)

    def body(*refs):
        in_refs, out_refs, send_sems, recv_sems = refs[:nb], refs[nb:2 * nb], refs[2 * nb], refs[2 * nb + 1]
        x, y, c = lax.axis_index("x"), lax.axis_index("y"), lax.axis_index("c")
        cps = [pltpu.make_async_remote_copy(src_ref=in_refs[b], dst_ref=out_refs[b], send_sem=send_sems.at[b],
                                            recv_sem=recv_sems.at[b], device_id=(x, y, 1 - c), device_id_type=_MESH)
               for b in range(nb)]
        for cp in cps:
            cp.start()
        for cp in cps:
            cp.wait()

    return _ccall(body, name=name, in_specs=[_HBM] * nb, out_specs=[_HBM] * nb,
                  out_shape=[_sds(b.shape, b.dtype) for b in bufs],
                  scratch_shapes=[pltpu.SemaphoreType.DMA((nb,)), pltpu.SemaphoreType.DMA((nb,))])(*bufs)


def _all_gather8(buf, *, name):
    def body(in_ref, out_ref, send_sems, recv_sems, local_sem):
        x, y, c = lax.axis_index("x"), lax.axis_index("y"), lax.axis_index("c")
        me = 4 * x + 2 * y + c
        own = pltpu.make_async_copy(in_ref, out_ref.at[me], local_sem)
        own.start()
        flips = [(fx, fy, fc) for fx in (0, 1) for fy in (0, 1) for fc in (0, 1)][1:]
        peers = [(x ^ fx, y ^ fy, c ^ fc) for fx, fy, fc in flips]
        sends = []
        for k, peer in enumerate(peers):
            cp = pltpu.make_async_remote_copy(src_ref=in_ref, dst_ref=out_ref.at[me], send_sem=send_sems.at[k],
                                              recv_sem=recv_sems.at[k], device_id=peer, device_id_type=_MESH)
            cp.start()
            sends.append(cp)
        for k, (px, py, pc) in enumerate(peers):
            slot = out_ref.at[4 * px + 2 * py + pc]
            pltpu.make_async_remote_copy(src_ref=slot, dst_ref=slot, send_sem=send_sems.at[k], recv_sem=recv_sems.at[k],
                                         device_id=(px, py, pc), device_id_type=_MESH).wait_recv()
        for cp in sends:
            cp.wait_send()
        own.wait()

    return _ccall(body, name=name, in_specs=[_HBM], out_specs=_HBM, out_shape=_sds((8,) + buf.shape, buf.dtype),
                  scratch_shapes=[pltpu.SemaphoreType.DMA((7,)), pltpu.SemaphoreType.DMA((7,)), pltpu.SemaphoreType.DMA])(buf)


def _row_tile(rows, cap):
    if rows <= cap:
        return rows
    return max(t for t in range(16, cap + 1, 16) if rows % t == 0)


def _sum_slots(stack, *, name, cap=256):
    S, Rr, C = stack.shape
    tr = _row_tile(Rr, cap)

    def body(s_ref, o_ref):
        acc = s_ref[0].astype(F32)
        for j in range(1, S):
            acc = acc + s_ref[j].astype(F32)
        o_ref[...] = acc

    return _pcall(body, name=name, grid=(Rr // tr,), in_specs=[pl.BlockSpec((S, tr, C), lambda i: (0, i, 0))],
                  out_specs=pl.BlockSpec((tr, C), lambda i: (i, 0)), out_shape=_sds((Rr, C), F32))(stack)


def _adamw(g_parts, w, m, v, *, name, cap=128):
    Rr, C = w.shape
    tr = _row_tile(Rr, cap)
    ng = len(g_parts)
    c1 = 1.0 / (1.0 - ADAM_B1 ** ADAM_STEP)
    c2 = 1.0 / (1.0 - ADAM_B2 ** ADAM_STEP)

    def body(*refs):
        g = refs[0][...]
        for r in refs[1:ng]:
            g = g + r[...]
        w_ref, m_ref, v_ref, g_out, d_out, m_out, v_out = refs[ng:]
        mn = ADAM_B1 * m_ref[...] + (1.0 - ADAM_B1) * g
        vn = ADAM_B2 * v_ref[...] + (1.0 - ADAM_B2) * jnp.square(g)
        g_out[...] = g
        m_out[...] = mn
        v_out[...] = vn
        d_out[...] = -ADAM_LR * ((mn * c1) / (jnp.sqrt(vn * c2) + ADAM_EPS) + ADAM_WD * w_ref[...])

    spec = pl.BlockSpec((tr, C), lambda i: (i, 0))
    return _pcall(body, name=name, grid=(Rr // tr,), in_specs=[spec] * (ng + 3), out_specs=[spec] * 4,
                  out_shape=[_sds((Rr, C), F32)] * 4)(*g_parts, w, m, v)


def _pack(arrs, dtype, row_mult):
    flat = jnp.concatenate([a.reshape(-1).astype(dtype) for a in arrs])
    n = flat.shape[0]
    unit = row_mult * PACK_W
    total = unit * ((n + unit - 1) // unit)
    if total > n:
        flat = jnp.concatenate([flat, jnp.zeros((total - n,), dtype)])
    return flat.reshape(-1, PACK_W)


def _unpack(buf, shapes):
    flat = buf.reshape(-1)
    out, off = [], 0
    for s in shapes:
        n = math.prod(s)
        out.append(flat[off:off + n].reshape(s))
        off += n
    return out


def _conf_pre(a, g):
    return [a * jax.nn.sigmoid(g)]


def _conf_post(cs, cb, lg, lb):
    return (_silu(_layer_norm(cs[0] + cb, lg, lb)),)


def _xbc_pre(x0, x1, x2):
    return [x0, x1, x2]


def _xbc_post(cs, b0, b1, b2):
    return (_silu(cs[0] + b0), _silu(cs[1] + b1), _silu(cs[2] + b2))


def _ffn_pre(gate, val):
    return [gate, val]


def _ffn_post(cs, bg, bv):
    return (_silu(cs[0] + bg) * (cs[1] + bv),)


def _mix_fn(ga, gb, ya, yb):
    return (jax.nn.sigmoid(ga) * ya + jax.nn.sigmoid(gb) * yb,)


def _res_ln_fn(h, r, g, b):
    return (_layer_norm(ALPHA * h + r, g, b),)


def _ln_fn(x, g, b):
    return (_layer_norm(x, g, b),)


def _layer_fwd(h, hb, W, l):
    T = h.shape[0]
    tm = min(TM, T)
    nt = T // tm
    tmf = min(TM_FFN, T)
    ntf = T // tmf
    nm = lambda s: f"l{l}_{s}"
    u3 = _mm_nn_slab_out(hb, W["w_p"], name=nm("u"), width=D)
    dtr = _mm_nn(hb, W["w_dt"], name=nm("dt"))
    (v3,) = _conv_fwd(_conf_pre, _conf_post, [_slab(u3, 0, tm), _slab(u3, 1, tm)], W["conv_w"],
                      [W["conv_b"], W["conv_ln_g"], W["conv_ln_b"]], [_o2(T, D, BF16, tm)],
                      K=CONV_K, C=D, name=nm("conf"), tm=tm, nt=nt)
    ya = _mm_nn(v3, W["w_co"], name=nm("ya"))
    tmx = min(TM_X, T)
    x0, x1, bc = _conv_fwd(_xbc_pre, _xbc_post, [_slab(u3, 6, tmx), _slab(u3, 7, tmx), _slab(u3, 8, tmx)], W["ssm_w"],
                           W["ssm_b"], [_o2(T, D, F32, tmx)] * 3, K=SSM_K, C=D, name=nm("xbc"), tm=tmx, nt=T // tmx)
    yn, sprev = _ssd_fwd(x0, x1, bc, dtr, u3, W["a_log"], W["dt_bias"], W["d_skip"], W["norm_w"], name=nm("ssd"))
    yb = _mm_nn(yn, W["w_so"], name=nm("yb"), tk=min(SD, 1024))
    (m,) = _rows_fwd(_mix_fn, [_slab(u3, 4, tm), _slab(u3, 5, tm), _r2(ya, tm), _r2(yb, tm)], [],
                     [_o2(T, D, BF16, tm)], name=nm("mix"), nt=nt)
    mix = _mm_nn(m, W["w_o"], name=nm("wo"))
    h1, h1b = _rows_fwd(lambda a, r, g, b: _res_ln_fn(a, r, g, b) * 2, [_r2(h, tm), _r2(mix, tm)],
                        [W["ln1_g"], W["ln1_b"]], [_o2(T, D, F32, tm), _o2(T, D, BF16, tm)], name=nm("ln1"), nt=nt)
    up3 = _mm_nn_slab_out(h1b, W["w_up"], name=nm("up"), width=FFN)
    (f,) = _conv_fwd(_ffn_pre, _ffn_post, [_slab(up3, 0, tmf), _slab(up3, 1, tmf)], W["ffn_w"], W["ffn_b"],
                     [_o2(T, FFN, BF16, tmf)], K=FFN_K, C=FFN, name=nm("ffnact"), tm=tmf, nt=ntf)
    ffn = _mm_nn(f, W["w_dn"], name=nm("dn"))
    h2, h2b = _rows_fwd(lambda a, r, g, b: _res_ln_fn(a, r, g, b) * 2, [_r2(h1, tm), _r2(ffn, tm)],
                        [W["ln2_g"], W["ln2_b"]], [_o2(T, D, F32, tm), _o2(T, D, BF16, tm)], name=nm("ln2"), nt=nt)
    saved = dict(h=h, hb=hb, u3=u3, dtr=dtr, v3=v3, ya=ya, x0=x0, x1=x1, bc=bc, sprev=sprev, yn=yn, yb=yb, m=m,
                 mix=mix, h1=h1, h1b=h1b, up3=up3, f=f, ffn=ffn)
    return h2, h2b, saved


def _layer_bwd(dh2, W, sv, l):
    T = dh2.shape[0]
    tm = min(TM, T)
    nt = T // tm
    tmf = min(TM_FFN, T)
    ntf = T // tmf
    nm = lambda s: f"l{l}_{s}"
    gr = {}
    (dres2, dffn), (gr["ln2_g"], gr["ln2_b"]) = _rows_bwd(
        _res_ln_fn, [_r2(sv["h1"], tm), _r2(sv["ffn"], tm)], [W["ln2_g"], W["ln2_b"]], [_r2(dh2, tm)],
        [_o2(T, D, F32, tm), _o2(T, D, BF16, tm)], name=nm("ln2_b"), nt=nt)
    df = _mm_nt(dffn, W["w_dn"], name=nm("dn_dx"))
    gr["w_dn"] = _mm_tn(sv["f"], dffn, name=nm("dn_dw"), tmo=FFN // 2)
    up3 = sv["up3"]
    h8 = _halo_of(FFN_K)
    (dgate, dval), gr["ffn_w"], gr["ffn_b"] = _conv_bwd(
        _ffn_pre, _ffn_post, [_slab(up3, 0, tmf), _slab(up3, 1, tmf)],
        [_halo_slab(up3, 0, tmf, h8), _halo_slab(up3, 1, tmf, h8)], W["ffn_w"], W["ffn_b"], [_r2(df, tmf)],
        [_o2(T, FFN, BF16, tmf)] * 2, K=FFN_K, C=FFN, name=nm("ffnact_b"), tm=tmf, nt=ntf)
    dup3 = jnp.stack([dgate, dval])
    dh1 = _mm_slab_in_nt(dup3, W["w_up"], name=nm("up_dx"), add=dres2)
    gr["w_up"] = _mm_tn_slab(sv["h1b"], dup3, name=nm("up_dw"), tmo=min(512, D))
    (dres1, dmix), (gr["ln1_g"], gr["ln1_b"]) = _rows_bwd(
        _res_ln_fn, [_r2(sv["h"], tm), _r2(sv["mix"], tm)], [W["ln1_g"], W["ln1_b"]], [_r2(dh1, tm)],
        [_o2(T, D, F32, tm), _o2(T, D, BF16, tm)], name=nm("ln1_b"), nt=nt)
    dm = _mm_nt(dmix, W["w_o"], name=nm("wo_dx"))
    gr["w_o"] = _mm_tn(sv["m"], dmix, name=nm("wo_dw"), tmo=min(512, D))
    u3 = sv["u3"]
    (dga, dgb, dya, dyb), _ = _rows_bwd(
        _mix_fn, [_slab(u3, 4, tm), _slab(u3, 5, tm), _r2(sv["ya"], tm), _r2(sv["yb"], tm)], [], [_r2(dm, tm)],
        [_o2(T, D, BF16, tm)] * 4, name=nm("mix_b"), nt=nt)
    dv3 = _mm_nt(dya, W["w_co"], name=nm("ya_dx"))
    gr["w_co"] = _mm_tn(sv["v3"], dya, name=nm("ya_dw"), tmo=min(512, D))
    h32 = _halo_of(CONV_K)
    (da, dg), gr["conv_w"], (gr["conv_b"], gr["conv_ln_g"], gr["conv_ln_b"]) = _conv_bwd(
        _conf_pre, _conf_post, [_slab(u3, 0, tm), _slab(u3, 1, tm)],
        [_halo_slab(u3, 0, tm, h32), _halo_slab(u3, 1, tm, h32)], W["conv_w"],
        [W["conv_b"], W["conv_ln_g"], W["conv_ln_b"]], [_r2(dv3, tm)], [_o2(T, D, BF16, tm)] * 2,
        K=CONV_K, C=D, name=nm("conf_b"), tm=tm, nt=nt)
    dyn = _mm_nt(dyb, W["w_so"], name=nm("yb_dx"))
    gr["w_so"] = _mm_tn(sv["yn"], dyb, name=nm("yb_dw"), tmo=min(512, SD))
    (dx0, dx1, dbc, ddtr, dz0, dz1, gr["a_log"], gr["dt_bias"], gr["d_skip"], gr["norm_w"]) = _ssd_bwd(
        sv["x0"], sv["x1"], sv["bc"], sv["dtr"], u3, sv["sprev"], dyn, W["a_log"], W["dt_bias"], W["d_skip"],
        W["norm_w"], name=nm("ssd_b"))
    h8s = _halo_of(SSM_K)
    tmx = min(TM_X, T)
    (du6, du7, du8), gr["ssm_w"], gr["ssm_b"] = _conv_bwd(
        _xbc_pre, _xbc_post, [_slab(u3, 6, tmx), _slab(u3, 7, tmx), _slab(u3, 8, tmx)],
        [_halo_slab(u3, 6, tmx, h8s), _halo_slab(u3, 7, tmx, h8s), _halo_slab(u3, 8, tmx, h8s)], W["ssm_w"], W["ssm_b"],
        [_r2(dx0, tmx), _r2(dx1, tmx), _r2(dbc, tmx)], [_o2(T, D, BF16, tmx)] * 3, K=SSM_K, C=D, name=nm("xbc_b"),
        tm=tmx, nt=T // tmx)
    du3 = jnp.stack([da, dg, dz0, dz1, dga, dgb, du6, du7, du8])
    dh_a = _mm_nt(ddtr, W["w_dt"], name=nm("dt_dx"), add=dres1)
    dh = _mm_slab_in_nt(du3, W["w_p"], name=nm("u_dx"), add=dh_a)
    gr["w_p"] = _mm_tn_slab(sv["hb"], du3, name=nm("u_dw"), tmo=min(512, D))
    gr["w_dt"] = _mm_tn(sv["hb"], ddtr, name=nm("dt_dw"), tmo=min(512, D))
    return dh, gr


_U_SPLIT = (2 * D + SD, 2 * D + SD + XBC, 2 * D + SD + XBC + H)


def _pad_rows(a, rows):
    return jnp.concatenate([a, jnp.zeros((rows - a.shape[0],) + a.shape[1:], a.dtype)], axis=0)


def _pad_lanes(a, lanes):
    return jnp.concatenate([a, jnp.zeros(a.shape[:-1] + (lanes - a.shape[-1],), a.dtype)], axis=-1)


def _layer_weights(full, l):
    e0, e1, e2 = _U_SPLIT
    w_in = full["w_in"][l]
    w_p = jnp.concatenate([w_in[:, :e0], w_in[:, e2:], w_in[:, e0:e1]], axis=1)
    w_dt = _pad_lanes(w_in[:, e1:e2], HP)
    row = lambda a: a.reshape(1, -1)
    ssm_w = full["ssm_conv_w"][l]
    ffn_w = full["ffn_dw_w"][l]
    ssm_b = full["ssm_conv_b"][l]
    ffn_b = full["ffn_dw_b"][l]
    W = dict(
        w_p=w_p, w_dt=w_dt, w_co=full["w_conv_out"][l], w_so=full["w_ssm_out"][l], w_o=full["w_o"][l],
        w_up=full["w_ffn_up"][l], w_dn=full["w_ffn_down"][l],
        conv_w=_pad_rows(full["conv_dw_w"][l], 32)[None],
        conv_b=row(full["conv_dw_b"][l]), conv_ln_g=row(full["conv_ln_g"][l]), conv_ln_b=row(full["conv_ln_b"][l]),
        ssm_w=jnp.stack([_pad_rows(ssm_w[:, p * D:(p + 1) * D], 8) for p in range(3)]),
        ssm_b=[row(ssm_b[p * D:(p + 1) * D]) for p in range(3)],
        a_log=_pad_lanes(row(full["ssm_a_log"][l]), HP), dt_bias=_pad_lanes(row(full["ssm_dt_bias"][l]), HP),
        d_skip=_pad_lanes(row(full["ssm_d"][l]), HP), norm_w=row(full["ssm_norm_w"][l]),
        ln1_g=row(full["ln1_g"][l]), ln1_b=row(full["ln1_b"][l]),
        ffn_w=jnp.stack([_pad_rows(ffn_w[:, p * FFN:(p + 1) * FFN], 8) for p in range(2)]),
        ffn_b=[row(ffn_b[p * FFN:(p + 1) * FFN]) for p in range(2)],
        ln2_g=row(full["ln2_g"][l]), ln2_b=row(full["ln2_b"][l]),
    )
    return W


def _layer_grads_to_reference_layout(gr):
    e0, e1, e2 = _U_SPLIT
    nx = XBC
    wp = gr["w_p"]
    w_in = jnp.concatenate([wp[:, :e0], wp[:, e0 + 2 * D:e0 + 2 * D + nx], gr["w_dt"][:, :H], wp[:, e0:e0 + 2 * D]], axis=1)
    return dict(
        w_in=w_in, conv_dw_w=gr["conv_w"][0, :CONV_K], conv_dw_b=gr["conv_b"][0], conv_ln_g=gr["conv_ln_g"][0],
        conv_ln_b=gr["conv_ln_b"][0], w_conv_out=gr["w_co"],
        ssm_conv_w=jnp.concatenate([gr["ssm_w"][p, :SSM_K] for p in range(3)], axis=1),
        ssm_conv_b=jnp.concatenate([b[0] for b in gr["ssm_b"]]),
        ssm_dt_bias=gr["dt_bias"][0, :H], ssm_a_log=gr["a_log"][0, :H], ssm_d=gr["d_skip"][0, :H],
        ssm_norm_w=gr["norm_w"][0], w_ssm_out=gr["w_so"], w_o=gr["w_o"], ln1_g=gr["ln1_g"][0], ln1_b=gr["ln1_b"][0],
        w_ffn_up=gr["w_up"], ffn_dw_w=jnp.concatenate([gr["ffn_w"][p, :FFN_K] for p in range(2)], axis=1),
        ffn_dw_b=jnp.concatenate([b[0] for b in gr["ffn_b"]]), w_ffn_down=gr["w_dn"], ln2_g=gr["ln2_g"][0],
        ln2_b=gr["ln2_b"][0],
    )


_BIG = dict(w_in=2, w_conv_out=1, w_ssm_out=1, w_o=1, w_ffn_up=2, w_ffn_down=1)
_SMALL_SHARDED = dict(conv_dw_w=2, ssm_conv_w=2, ffn_dw_w=2)
_REPLICATED = ("ln_in_g", "ln_in_b", "conv_dw_b", "conv_ln_g", "conv_ln_b", "ssm_conv_b", "ssm_dt_bias", "ssm_a_log",
               "ssm_d", "ssm_norm_w", "ln1_g", "ln1_b", "ffn_dw_b", "ln2_g", "ln2_b")
_WEIGHTS = ("ln_in_g", "ln_in_b", "w_in", "conv_dw_w", "conv_dw_b", "conv_ln_g", "conv_ln_b", "w_conv_out", "ssm_conv_w",
            "ssm_conv_b", "ssm_dt_bias", "ssm_a_log", "ssm_d", "ssm_norm_w", "w_ssm_out", "w_o", "ln1_g", "ln1_b",
            "w_ffn_up", "ffn_dw_w", "ffn_dw_b", "w_ffn_down", "ln2_g", "ln2_b")


def _split_chips(a, axis):
    rows, cols = a.shape
    if axis == 0:
        return a.reshape(NCHIP, rows // NCHIP, cols)
    return a.reshape(rows, NCHIP, cols // NCHIP).transpose(1, 0, 2)


def kernel(x, ln_in_g, ln_in_b, w_in, conv_dw_w, conv_dw_b, conv_ln_g, conv_ln_b, w_conv_out, ssm_conv_w, ssm_conv_b, ssm_dt_bias, ssm_a_log, ssm_d, ssm_norm_w, w_ssm_out, w_o, ln1_g, ln1_b, w_ffn_up, ffn_dw_w, ffn_dw_b, w_ffn_down, ln2_g, ln2_b, loss_target, m_ln_in_g, m_ln_in_b, m_w_in, m_conv_dw_w, m_conv_dw_b, m_conv_ln_g, m_conv_ln_b, m_w_conv_out, m_ssm_conv_w, m_ssm_conv_b, m_ssm_dt_bias, m_ssm_a_log, m_ssm_d, m_ssm_norm_w, m_w_ssm_out, m_w_o, m_ln1_g, m_ln1_b, m_w_ffn_up, m_ffn_dw_w, m_ffn_dw_b, m_w_ffn_down, m_ln2_g, m_ln2_b, v_ln_in_g, v_ln_in_b, v_w_in, v_conv_dw_w, v_conv_dw_b, v_conv_ln_g, v_conv_ln_b, v_w_conv_out, v_ssm_conv_w, v_ssm_conv_b, v_ssm_dt_bias, v_ssm_a_log, v_ssm_d, v_ssm_norm_w, v_w_ssm_out, v_w_o, v_ln1_g, v_ln1_b, v_w_ffn_up, v_ffn_dw_w, v_ffn_dw_b, v_w_ffn_down, v_ln2_g, v_ln2_b):
    args = locals()
    w = {n: args[n] for n in _WEIGHTS}
    mom = {n: args["m_" + n] for n in _WEIGHTS}
    vel = {n: args["v_" + n] for n in _WEIGHTS}
    T = x.shape[1]
    tm = min(TM, T)
    nt = T // tm
    chip = 2 * lax.axis_index("x") + lax.axis_index("y")

    big_names, small_names = list(_BIG), list(_SMALL_SHARDED)
    got = _chip_exchange([w[n].astype(BF16) for n in big_names] + [w[n] for n in small_names], scatter=False,
                         name="gather_weights")
    full = {}
    for n, gk in zip(big_names + small_names, got):
        ax = _BIG[n] if n in _BIG else _SMALL_SHARDED[n]
        full[n] = jnp.concatenate([gk[j] for j in range(NCHIP)], axis=ax)
    for n in _REPLICATED:
        full[n] = w[n]
    Ws = [_layer_weights(full, l) for l in range(DEPTH)]

    x2 = x.reshape(T, D)
    g_in, b_in = ln_in_g.reshape(1, D), ln_in_b.reshape(1, D)
    h, hb = _rows_fwd(lambda a, g, b: _ln_fn(a, g, b) * 2, [_r2(x2, tm)], [g_in, b_in],
                      [_o2(T, D, F32, tm), _o2(T, D, BF16, tm)], name="ln_in", nt=nt)
    saved = []
    for l in range(DEPTH):
        h, hb, sv = _layer_fwd(h, hb, Ws[l], l)
        saved.append(sv)
    loss_row, dh = _loss_head(h, loss_target.reshape(T, D), name="loss")

    layer_grads = [None] * DEPTH
    for l in reversed(range(DEPTH)):
        dh, gr = _layer_bwd(dh, Ws[l], saved[l], l)
        layer_grads[l] = _layer_grads_to_reference_layout(gr)
    (grad_x2,), (d_g_in, d_b_in) = _rows_bwd(_ln_fn, [_r2(x2, tm)], [g_in, b_in], [_r2(dh, tm)], [_o2(T, D, F32, tm)],
                                             name="ln_in_b", nt=nt)
    local = {n: jnp.stack([layer_grads[l][n] for l in range(DEPTH)]) for n in _WEIGHTS[2:] if n not in _BIG}
    local["ln_in_g"], local["ln_in_b"] = d_g_in[0], d_b_in[0]
    res = [{}, {}, {}, {}]

    send = [jnp.stack([_split_chips(layer_grads[l][n], _BIG[n] - 1) for l in range(DEPTH)], axis=1) for n in big_names]
    got = _chip_exchange(send, scatter=True, name="exchange_grads")
    mine = [_sum_slots(gk.reshape(NCHIP, -1, gk.shape[-1]), name="sum_chips_" + n) for n, gk in zip(big_names, got)]
    other = _core_swap(mine, name="swap_cores")
    flat = lambda a: a.reshape(-1, a.shape[-1])
    for k, n in enumerate(big_names):
        outs = _adamw([mine[k], other[k]], flat(w[n]), flat(mom[n]), flat(vel[n]), name="adamw_" + n)
        for q in range(4):
            res[q][n] = outs[q].reshape(w[n].shape)

    rest_names = list(_REPLICATED) + small_names
    part = _pack([loss_row] + [local[n] for n in rest_names], F32, 8)
    parts = _all_gather8(part, name="gather_small")
    total = _sum_slots(parts, name="sum_devices")
    tot = _unpack(total, [loss_row.shape] + [local[n].shape for n in rest_names])
    loss = tot[0][0, 0]
    g_rest = {}
    for n, t in zip(rest_names, tot[1:]):
        if n in _SMALL_SHARDED:
            ax = _SMALL_SHARDED[n]
            t = lax.dynamic_slice_in_dim(t, chip * w[n].shape[ax], w[n].shape[ax], axis=ax)
        g_rest[n] = t
    pk = lambda d: _pack([d[n] for n in rest_names], F32, 8)
    rest_out = _adamw([pk(g_rest)], pk(w), pk(mom), pk(vel), name="adamw_rest")
    rest_out = [_unpack(o, [w[n].shape for n in rest_names]) for o in rest_out]

    for q in range(4):
        for k, n in enumerate(rest_names):
            res[q][n] = rest_out[q][k]
    grad_x = grad_x2.reshape(x.shape)
    return (loss, grad_x, *[res[0][n] for n in _WEIGHTS], *[res[1][n] for n in _WEIGHTS],
            *[res[2][n] for n in _WEIGHTS], *[res[3][n] for n in _WEIGHTS])
```

```python
import functools
import math

import jax
import jax.numpy as jnp
from jax import lax
from jax.experimental import pallas as pl
from jax.experimental.pallas import tpu as pltpu

F32 = jnp.float32
BF16 = jnp.bfloat16
HI = lax.Precision.HIGHEST

D = 1024
DEPTH = 2
CONV_K = 31
SD = 2 * D
P = 64
H = SD // P
G = 4
R = H // G
N = 128
RP = R * P
SSM_K = 4
L = 128
XBC = SD + 2 * G * N
FFN = 2816
FFN_K = 3
IN_DIM = 2 * D + SD + XBC + H + 2 * D
ALPHA = (2 * DEPTH) ** 0.25
LN_EPS = 1e-5
RMS_EPS = 1e-5
ADAM_LR, ADAM_B1, ADAM_B2, ADAM_EPS, ADAM_WD, ADAM_STEP = 0.001, 0.9, 0.999, 1e-08, 0.01, 10

HP = 128
NCHIP = 4
PACK_W = 1024
VMEM_LIMIT = 56 * 1024 * 1024
TM = 512
TM_X = 256
TM_FFN = 128
TK = 1024
EW_ROWS = 512

assert D == 2 * RP and 2 * G * N == D and XBC == 3 * D and H <= HP


def _pcall(body, *, name, grid=(), in_specs, out_specs, out_shape, scratch_shapes=()):
    params = pltpu.CompilerParams(vmem_limit_bytes=VMEM_LIMIT, dimension_semantics=("arbitrary",) * len(grid))
    return pl.pallas_call(body, name=name, grid=grid, in_specs=in_specs, out_specs=out_specs, out_shape=out_shape,
                          scratch_shapes=list(scratch_shapes), compiler_params=params)


def _pcall_carrying(body, comm, *, name, grid, in_specs, out_specs, out_shape, scratch_shapes=()):
    in_specs, out_specs, out_shape = list(in_specs), list(out_specs), list(out_shape)
    scratch_shapes = list(scratch_shapes)
    n_in, n_out, n_scr = len(in_specs), len(out_specs), len(scratch_shapes)
    nci, nco = len(comm["ins"]), len(comm["outs"])

    def wrapped(*refs):
        ins, cin = refs[:n_in], refs[n_in:n_in + nci]
        outs = refs[n_in + nci:n_in + nci + n_out]
        cout = refs[n_in + nci + n_out:n_in + nci + n_out + nco]
        scr = refs[n_in + nci + n_out + nco:n_in + nci + n_out + nco + n_scr]
        csem = refs[n_in + nci + n_out + nco + n_scr:]
        ids = [pl.program_id(ax) for ax in range(len(grid))]
        first = functools.reduce(jnp.logical_and, [i == 0 for i in ids])
        last = functools.reduce(jnp.logical_and, [i == g - 1 for i, g in zip(ids, grid)])

        @pl.when(first)
        def _():
            comm["start"](cin, cout, csem)

        body(*ins, *outs, *scr)

        @pl.when(last)
        def _():
            comm["wait"](cin, cout, csem)

    call = _pcall(wrapped, name=name, grid=grid, in_specs=in_specs + [_HBM] * nci, out_specs=out_specs + [_HBM] * nco,
                  out_shape=out_shape + list(comm["outs"]), scratch_shapes=scratch_shapes + list(comm["sems"]))

    def run(*operands):
        res = call(*operands, *comm["ins"])
        return list(res[:n_out]), list(res[n_out:])

    return run


def _ccall(body, *, name, in_specs, out_specs, out_shape, scratch_shapes):
    return pl.pallas_call(body, name=name, in_specs=in_specs, out_specs=out_specs, out_shape=out_shape,
                          scratch_shapes=list(scratch_shapes))


def _full_spec(a):
    nd = a.ndim
    return pl.BlockSpec(a.shape, lambda *_: (0,) * nd)


def _sds(shape, dtype):
    return jax.ShapeDtypeStruct(tuple(shape), dtype)


def _mm(a, b, *, name, grid, a_spec, b_spec, o_spec, out_shape, acc_shape, trans_a=False, trans_b=False, add=None,
        add_spec=None, comm=None):
    nk = grid[2]
    dn = (((0 if trans_a else 1,), (1 if trans_b else 0,)), ((), ()))
    has_add = add is not None

    def body(*refs):
        a_ref, b_ref = refs[0], refs[1]
        add_ref = refs[2] if has_add else None
        o_ref = refs[3] if has_add else refs[2]
        part = lax.dot_general(a_ref[...].astype(BF16), b_ref[...].astype(BF16), dn, preferred_element_type=F32)

        def finish(res):
            if has_add:
                res = res + add_ref[...]
            o_ref[...] = res.astype(o_ref.dtype)

        if nk == 1:
            finish(part)
        else:
            acc = refs[-1]
            k = pl.program_id(2)

            @pl.when(k == 0)
            def _():
                acc[...] = part

            @pl.when(k > 0)
            def _():
                acc[...] += part

            @pl.when(k == nk - 1)
            def _():
                finish(acc[...])

    ins = [a, b] + ([add] if has_add else [])
    specs = [a_spec, b_spec] + ([add_spec] if has_add else [])
    scratch = [pltpu.VMEM(acc_shape, F32)] if nk > 1 else []
    if comm is not None:
        (out,), got = _pcall_carrying(body, comm, name=name, grid=grid, in_specs=specs, out_specs=[o_spec],
                                      out_shape=[out_shape], scratch_shapes=scratch)(*ins)
        return out, got
    return _pcall(body, name=name, grid=grid, in_specs=specs, out_specs=o_spec, out_shape=out_shape,
                  scratch_shapes=scratch)(*ins)


def _mm_nn(a, b, *, name, out_dtype=F32, tn=None, tk=None, add=None):
    M, K = a.shape
    Nn = b.shape[1]
    tm = min(TM, M)
    tn = Nn if tn is None else tn
    tk = K if tk is None else tk
    grid = (M // tm, Nn // tn, K // tk)
    return _mm(a, b, name=name, grid=grid,
               a_spec=pl.BlockSpec((tm, tk), lambda i, j, k: (i, k)),
               b_spec=pl.BlockSpec((tk, tn), lambda i, j, k: (k, j)),
               o_spec=pl.BlockSpec((tm, tn), lambda i, j, k: (i, j)),
               out_shape=_sds((M, Nn), out_dtype), acc_shape=(tm, tn), add=add,
               add_spec=pl.BlockSpec((tm, tn), lambda i, j, k: (i, j)))


def _mm_nn_slab_out(a, b, *, name, width, comm=None):
    M, K = a.shape
    S = b.shape[1] // width
    tm = min(TM, M)
    return _mm(a, b, name=name, grid=(M // tm, S, 1),
               a_spec=pl.BlockSpec((tm, K), lambda i, j, k: (i, 0)),
               b_spec=pl.BlockSpec((K, width), lambda i, j, k: (0, j)),
               o_spec=pl.BlockSpec((None, tm, width), lambda i, j, k: (j, i, 0)),
               out_shape=_sds((S, M, width), F32), acc_shape=(tm, width), comm=comm)


def _mm_nt(a, b, *, name, add=None):
    M, K = a.shape
    Nn = b.shape[0]
    tm = min(TM, M)
    return _mm(a, b, name=name, grid=(M // tm, 1, 1), trans_b=True,
               a_spec=pl.BlockSpec((tm, K), lambda i, j, k: (i, 0)),
               b_spec=pl.BlockSpec((Nn, K), lambda i, j, k: (0, 0)),
               o_spec=pl.BlockSpec((tm, Nn), lambda i, j, k: (i, 0)),
               out_shape=_sds((M, Nn), F32), acc_shape=(tm, Nn), add=add,
               add_spec=pl.BlockSpec((tm, Nn), lambda i, j, k: (i, 0)))


def _mm_slab_in_nt(a3, b, *, name, add, comm=None):
    S, M, width = a3.shape
    Nn = b.shape[0]
    tm = min(TM, M)
    return _mm(a3, b, name=name, grid=(M // tm, 1, S), trans_b=True,
               a_spec=pl.BlockSpec((None, tm, width), lambda i, j, k: (k, i, 0)),
               b_spec=pl.BlockSpec((Nn, width), lambda i, j, k: (0, k)),
               o_spec=pl.BlockSpec((tm, Nn), lambda i, j, k: (i, 0)),
               out_shape=_sds((M, Nn), F32), acc_shape=(tm, Nn), add=add,
               add_spec=pl.BlockSpec((tm, Nn), lambda i, j, k: (i, 0)), comm=comm)


def _mm_tn(a, b, *, name, tmo, tn=None, comm=None):
    T, M = a.shape
    Nn = b.shape[1]
    tn = Nn if tn is None else tn
    tk = min(TK, T)
    return _mm(a, b, name=name, grid=(M // tmo, Nn // tn, T // tk), trans_a=True,
               a_spec=pl.BlockSpec((tk, tmo), lambda i, j, k: (k, i)),
               b_spec=pl.BlockSpec((tk, tn), lambda i, j, k: (k, j)),
               o_spec=pl.BlockSpec((tmo, tn), lambda i, j, k: (i, j)),
               out_shape=_sds((M, Nn), BF16), acc_shape=(tmo, tn), comm=comm)


def _mm_tn_slab(a, b3, *, name, tmo, comm=None):
    T, M = a.shape
    S, _, width = b3.shape
    tk = min(TK, T)
    return _mm(a, b3, name=name, grid=(M // tmo, S, T // tk), trans_a=True,
               a_spec=pl.BlockSpec((tk, tmo), lambda i, j, k: (k, i)),
               b_spec=pl.BlockSpec((None, tk, width), lambda i, j, k: (j, k, 0)),
               o_spec=pl.BlockSpec((tmo, width), lambda i, j, k: (i, j)),
               out_shape=_sds((M, S * width), BF16), acc_shape=(tmo, width), comm=comm)


def _r2(a, tm):
    return (a, (tm, a.shape[1]), lambda i: (i, 0))


def _slab(a3, s, tm):
    return (a3, (None, tm, a3.shape[2]), lambda i: (s, i, 0))


def _o2(T, C, dtype, tm):
    return ((T, C), dtype, (tm, C), lambda i: (i, 0))


def _rows_fwd(fn, row_ins, par_ins, outs, *, name, nt):
    nr, npar = len(row_ins), len(par_ins)

    def body(*refs):
        vals = [r[...] for r in refs[:nr + npar]]
        res = fn(*vals)
        for o_ref, v in zip(refs[nr + npar:], res):
            o_ref[...] = v.astype(o_ref.dtype)

    return _pcall(body, name=name, grid=(nt,),
                  in_specs=[pl.BlockSpec(bs, im) for (_, bs, im) in row_ins] + [_full_spec(p) for p in par_ins],
                  out_specs=[pl.BlockSpec(bs, im) for (_, _, bs, im) in outs],
                  out_shape=[_sds(s, d) for (s, d, _, _) in outs])(*[r[0] for r in row_ins], *par_ins)


def _rows_bwd(fn, row_ins, par_ins, cot_ins, drow_outs, *, name, nt):
    nr, npar, nc = len(row_ins), len(par_ins), len(cot_ins)
    keep = [k for k, o in enumerate(drow_outs) if o is not None]

    def body(*refs):
        vals = [r[...].astype(F32) for r in refs[:nr + npar]]
        cots = [r[...].astype(F32) for r in refs[nr + npar:nr + npar + nc]]
        orefs = refs[nr + npar + nc:]
        _, vjp = jax.vjp(fn, *vals)
        grads = vjp(tuple(cots))
        for o_ref, k in zip(orefs[:len(keep)], keep):
            o_ref[...] = grads[k].astype(o_ref.dtype)
        prefs = orefs[len(keep):]

        @pl.when(pl.program_id(0) == 0)
        def _():
            for p_ref in prefs:
                p_ref[...] = jnp.zeros_like(p_ref)

        for p_ref, g in zip(prefs, grads[nr:]):
            p_ref[...] += g

    outs = [drow_outs[k] for k in keep]
    res = _pcall(body, name=name, grid=(nt,),
                 in_specs=[pl.BlockSpec(bs, im) for (_, bs, im) in row_ins] + [_full_spec(p) for p in par_ins]
                 + [pl.BlockSpec(bs, im) for (_, bs, im) in cot_ins],
                 out_specs=[pl.BlockSpec(bs, im) for (_, _, bs, im) in outs] + [_full_spec(p) for p in par_ins],
                 out_shape=[_sds(s, d) for (s, d, _, _) in outs] + [_sds(p.shape, F32) for p in par_ins],
                 )(*[r[0] for r in row_ins], *par_ins, *[c[0] for c in cot_ins])
    return list(res[:len(keep)]), list(res[len(keep):])


def _layer_norm(v, g, b):
    mu = jnp.mean(v, axis=-1, keepdims=True)
    var = jnp.mean(jnp.square(v - mu), axis=-1, keepdims=True)
    return (v - mu) * lax.rsqrt(var + LN_EPS) * g + b


def _silu(v):
    return v * jax.nn.sigmoid(v)


def _softplus(v):
    return jnp.maximum(v, 0.0) + jnp.log1p(jnp.exp(-jnp.abs(v)))


def _halo_of(K):
    return 8 * ((K - 1 + 7) // 8)


def _conv_taps(buf, p, w_ref, K, halo, tm):
    acc = None
    for k in range(K):
        term = buf[p, pl.ds(halo - (K - 1) + k, tm), :] * w_ref[p, k:k + 1, :]
        acc = term if acc is None else acc + term
    return acc


def _conv_fwd(pre, post, row_ins, w, par_ins, outs, *, K, C, name, tm, nt):
    nparts = w.shape[0]
    halo = _halo_of(K)
    nr, npar = len(row_ins), len(par_ins)

    def body(*refs):
        rows = [r[...] for r in refs[:nr]]
        w_ref = refs[nr]
        pars = [r[...] for r in refs[nr + 1:nr + 1 + npar]]
        orefs = refs[nr + 1 + npar:-1]
        buf = refs[-1]
        i = pl.program_id(0)
        xin = pre(*rows)
        cs = []
        for p in range(nparts):
            @pl.when(i == 0)
            def _():
                buf[p, pl.ds(0, halo), :] = jnp.zeros((halo, C), F32)

            @pl.when(i > 0)
            def _():
                buf[p, pl.ds(0, halo), :] = buf[p, pl.ds(tm, halo), :]

            buf[p, pl.ds(halo, tm), :] = xin[p]
            cs.append(_conv_taps(buf, p, w_ref, K, halo, tm))
        res = post(cs, *pars)
        for o_ref, v in zip(orefs, res):
            o_ref[...] = v.astype(o_ref.dtype)

    return _pcall(body, name=name, grid=(nt,),
                  in_specs=[pl.BlockSpec(bs, im) for (_, bs, im) in row_ins] + [_full_spec(w)] + [_full_spec(p) for p in par_ins],
                  out_specs=[pl.BlockSpec(bs, im) for (_, _, bs, im) in outs],
                  out_shape=[_sds(s, d) for (s, d, _, _) in outs],
                  scratch_shapes=[pltpu.VMEM((nparts, halo + tm, C), F32)])(*[r[0] for r in row_ins], w, *par_ins)


def _conv_bwd(pre, post, row_ins, halo_ins, w, par_ins, cot_ins, drow_outs, *, K, C, name, tm, nt):
    nparts = w.shape[0]
    halo = _halo_of(K)
    nr, npar, nc = len(row_ins), len(par_ins), len(cot_ins)

    def body(*refs):
        rows = [r[...].astype(F32) for r in refs[:nr]]
        halos = [r[...].astype(F32) for r in refs[nr:2 * nr]]
        w_ref = refs[2 * nr]
        pars = [r[...] for r in refs[2 * nr + 1:2 * nr + 1 + npar]]
        cots = [r[...].astype(F32) for r in refs[2 * nr + 1 + npar:2 * nr + 1 + npar + nc]]
        rest = refs[2 * nr + 1 + npar + nc:]
        drow_refs, dw_ref, dpar_refs, bufx, bufd = rest[:nr], rest[nr], rest[nr + 1:nr + 1 + npar], rest[-2], rest[-1]
        s = pl.program_id(0)
        first_tile = s == nt - 1

        @pl.when(s == 0)
        def _():
            dw_ref[...] = jnp.zeros_like(dw_ref)
            for p_ref in dpar_refs:
                p_ref[...] = jnp.zeros_like(p_ref)
            for p in range(nparts):
                bufd[p, pl.ds(tm, halo), :] = jnp.zeros((halo, C), F32)

        xin, pre_vjp = jax.vjp(pre, *rows)
        xh = pre(*halos)
        cs = []
        for p in range(nparts):
            bufx[p, pl.ds(0, halo), :] = jnp.where(first_tile, 0.0, xh[p])
            bufx[p, pl.ds(halo, tm), :] = xin[p]
            cs.append(_conv_taps(bufx, p, w_ref, K, halo, tm))
        _, post_vjp = jax.vjp(lambda c, q: post(c, *q), cs, pars)
        dcs, dpars = post_vjp(tuple(cots))
        dxin = []
        for p in range(nparts):
            dc = dcs[p]
            bufd[p, pl.ds(0, tm), :] = dc
            acc = None
            for k in range(K):
                term = bufd[p, pl.ds(K - 1 - k, tm), :] * w_ref[p, k:k + 1, :]
                acc = term if acc is None else acc + term
                dw_ref[p, k:k + 1, :] += jnp.sum(dc * bufx[p, pl.ds(halo - (K - 1) + k, tm), :], axis=0, keepdims=True)
            dxin.append(acc)
            bufd[p, pl.ds(tm, halo), :] = dc[0:halo, :]
        drows = pre_vjp(dxin)
        for o_ref, g in zip(drow_refs, drows):
            o_ref[...] = g.astype(o_ref.dtype)
        for p_ref, g in zip(dpar_refs, dpars):
            p_ref[...] += g

    rev = lambda im: (lambda s: im(nt - 1 - s))
    res = _pcall(body, name=name, grid=(nt,),
                 in_specs=[pl.BlockSpec(bs, rev(im)) for (_, bs, im) in row_ins]
                 + [pl.BlockSpec(bs, rev(im)) for (_, bs, im) in halo_ins]
                 + [_full_spec(w)] + [_full_spec(p) for p in par_ins]
                 + [pl.BlockSpec(bs, rev(im)) for (_, bs, im) in cot_ins],
                 out_specs=[pl.BlockSpec(bs, rev(im)) for (_, _, bs, im) in drow_outs] + [_full_spec(w)]
                 + [_full_spec(p) for p in par_ins],
                 out_shape=[_sds(sh, d) for (sh, d, _, _) in drow_outs] + [_sds(w.shape, F32)]
                 + [_sds(p.shape, F32) for p in par_ins],
                 scratch_shapes=[pltpu.VMEM((nparts, halo + tm, C), F32), pltpu.VMEM((nparts, tm + halo, C), F32)],
                 )(*[r[0] for r in row_ins], *[r[0] for r in halo_ins], w, *par_ins, *[c[0] for c in cot_ins])
    return list(res[:nr]), res[nr], list(res[nr + 1:])


def _halo_r2(a, tm, halo):
    q = tm // halo
    return (a, (halo, a.shape[1]), lambda i: (jnp.maximum(i * q - 1, 0), 0))


def _halo_slab(a3, s, tm, halo):
    q = tm // halo
    return (a3, (None, halo, a3.shape[2]), lambda i: (s, jnp.maximum(i * q - 1, 0), 0))


def _dg(a, b, ca, cb):
    return lax.dot_general(a.astype(BF16), b.astype(BF16), (((ca,), (cb,)), ((), ())), preferred_element_type=F32)


@jax.custom_vjp
def _dot_nn(a, b):
    return _dg(a, b, 1, 0)


_dot_nn.defvjp(lambda a, b: (_dg(a, b, 1, 0), (a, b)),
               lambda res, g: (_dg(g, res[1], 1, 1), _dg(res[0], g, 0, 0)))


@jax.custom_vjp
def _dot_nt(a, b):
    return _dg(a, b, 1, 1)


_dot_nt.defvjp(lambda a, b: (_dg(a, b, 1, 1), (a, b)),
               lambda res, g: (_dg(g, res[1], 1, 0), _dg(g, res[0], 0, 0)))


@jax.custom_vjp
def _dot_tn(a, b):
    return _dg(a, b, 0, 0)


_dot_tn.defvjp(lambda a, b: (_dg(a, b, 0, 0), (a, b)),
               lambda res, g: (_dg(res[1], g, 1, 1), _dg(res[0], g, 1, 0)))


def _split3(v):
    hi = v.astype(BF16)
    r = v - hi.astype(F32)
    mid = r.astype(BF16)
    return hi, mid, (r - mid.astype(F32)).astype(BF16)


def _x01(v, m, cv, cm, m_left=False):
    acc = None
    for piece in _split3(v):
        t = _dg(m, piece, cm, cv) if m_left else _dg(piece, m, cv, cm)
        acc = t if acc is None else acc + t
    return acc


@jax.custom_vjp
def _expand01(v, m):
    return _x01(v, m, 1, 0)


_expand01.defvjp(lambda v, m: (_x01(v, m, 1, 0), m),
                 lambda m, g: (_x01(g, m, 1, 1), jnp.zeros_like(m)))


@jax.custom_vjp
def _mix01(m, v):
    return _x01(v, m, 0, 1, m_left=True)


_mix01.defvjp(lambda m, v: (_x01(v, m, 0, 1, m_left=True), m),
              lambda m, g: (jnp.zeros_like(m), _x01(g, m, 0, 0, m_left=True)))


def _ssd_group(xs, dtr, Bg, Cg, zg, sp, alog, dtb, dsk, nwg, *, g):
    li = lax.broadcasted_iota(jnp.int32, (L, L), 0)
    si = lax.broadcasted_iota(jnp.int32, (L, L), 1)
    causal = li >= si
    tri = causal.astype(F32)
    hi = lax.broadcasted_iota(jnp.int32, (HP, RP), 0)
    ci = lax.broadcasted_iota(jnp.int32, (HP, RP), 1)
    lo = (hi - g * R) * P
    E = ((ci >= lo) & (ci < lo + P)).astype(F32)

    dt = _softplus(dtr + dtb)
    a = dt * (-jnp.exp(alog))
    a_cs = _mix01(tri, a)
    acs_e = _expand01(a_cs, E)
    dt_e = _expand01(dt, E)
    alast_e = acs_e[L - 1:L, :]
    xdt = xs * dt_e
    a_csT = a_cs.T
    cb = _dot_nt(Cg, Bg)
    y_off = _dot_nn(Cg, sp) * jnp.exp(acs_e)
    yd = []
    for r in range(R):
        h = g * R + r
        seg = a_cs[:, h:h + 1] - a_csT[h:h + 1, :]
        dec = jnp.where(causal, jnp.exp(jnp.where(causal, seg, 0.0)), 0.0)
        yd.append(_dot_nn(cb * dec, xdt[:, r * P:(r + 1) * P]))
    y = jnp.concatenate(yd, axis=1) + y_off + xs * _expand01(jnp.broadcast_to(dsk, (8, HP)), E)[0:1, :]
    yg = y * _silu(zg)
    yn = yg * lax.rsqrt(jnp.mean(jnp.square(yg), axis=-1, keepdims=True) + RMS_EPS) * nwg
    sc = _dot_tn(Bg, xdt * jnp.exp(alast_e - acs_e))
    return yn, jnp.exp(alast_e) * sp + sc


def _group_cols(g):
    return g // 2, (g % 2) * RP


def _ssd_fwd(x0, x1, bc, dtr, u3, alog, dtb, dsk, nw, *, name):
    T = x0.shape[0]
    nc = T // L

    def body(x0_ref, x1_ref, bc_ref, dtr_ref, z0_ref, z1_ref, alog_ref, dtb_ref, dsk_ref, nw_ref, yn_ref, sp_ref, S):
        @pl.when(pl.program_id(0) == 0)
        def _():
            S[...] = jnp.zeros_like(S)

        xr, zr = (x0_ref, x1_ref), (z0_ref, z1_ref)
        for g in range(G):
            s, off = _group_cols(g)
            sp = S[g]
            sp_ref[0, g] = sp
            yn, s_next = _ssd_group(xr[s][:, off:off + RP], dtr_ref[...], bc_ref[:, g * N:(g + 1) * N],
                                    bc_ref[:, G * N + g * N:G * N + (g + 1) * N], zr[s][:, off:off + RP], sp,
                                    alog_ref[...], dtb_ref[...], dsk_ref[...], nw_ref[:, g * RP:(g + 1) * RP], g=g)
            yn_ref[:, g * RP:(g + 1) * RP] = yn.astype(yn_ref.dtype)
            S[g] = s_next

    row = lambda C: pl.BlockSpec((L, C), lambda c: (c, 0))
    zspec = lambda s: pl.BlockSpec((None, L, D), lambda c: (s, c, 0))
    pars = [alog, dtb, dsk, nw]
    return _pcall(body, name=name, grid=(nc,),
                  in_specs=[row(D), row(D), row(D), row(HP), zspec(2), zspec(3)] + [_full_spec(p) for p in pars],
                  out_specs=[row(SD), pl.BlockSpec((1, G, N, RP), lambda c: (c, 0, 0, 0))],
                  out_shape=[_sds((T, SD), BF16), _sds((nc, G, N, RP), F32)],
                  scratch_shapes=[pltpu.VMEM((G, N, RP), F32)])(x0, x1, bc, dtr, u3, u3, *pars)


def _ssd_bwd(x0, x1, bc, dtr, u3, sprev, dyn, alog, dtb, dsk, nw, *, name):
    T = x0.shape[0]
    nc = T // L

    def body(x0_ref, x1_ref, bc_ref, dtr_ref, z0_ref, z1_ref, sp_ref, dyn_ref, alog_ref, dtb_ref, dsk_ref, nw_ref,
             dx0_ref, dx1_ref, dbc_ref, ddtr_ref, dz0_ref, dz1_ref, dalog_ref, ddtb_ref, ddsk_ref, dnw_ref, dS):
        @pl.when(pl.program_id(0) == 0)
        def _():
            dS[...] = jnp.zeros_like(dS)
            for r in (dalog_ref, ddtb_ref, ddsk_ref, dnw_ref):
                r[...] = jnp.zeros_like(r)

        xr, zr = (x0_ref, x1_ref), (z0_ref, z1_ref)
        dxr, dzr = (dx0_ref, dx1_ref), (dz0_ref, dz1_ref)
        ddtr = jnp.zeros((L, HP), F32)
        for g in range(G):
            s, off = _group_cols(g)
            _, vjp = jax.vjp(functools.partial(_ssd_group, g=g), xr[s][:, off:off + RP], dtr_ref[...],
                             bc_ref[:, g * N:(g + 1) * N], bc_ref[:, G * N + g * N:G * N + (g + 1) * N],
                             zr[s][:, off:off + RP], sp_ref[0, g], alog_ref[...], dtb_ref[...], dsk_ref[...],
                             nw_ref[:, g * RP:(g + 1) * RP])
            dxs, ddt_g, dB, dC, dz, dsp, dal, ddb, dds, dnwg = vjp((dyn_ref[:, g * RP:(g + 1) * RP], dS[g]))
            dxr[s][:, off:off + RP] = dxs
            dzr[s][:, off:off + RP] = dz.astype(dz0_ref.dtype)
            dbc_ref[:, g * N:(g + 1) * N] = dB
            dbc_ref[:, G * N + g * N:G * N + (g + 1) * N] = dC
            dS[g] = dsp
            ddtr = ddtr + ddt_g
            dalog_ref[...] += dal
            ddtb_ref[...] += ddb
            ddsk_ref[...] += dds
            dnw_ref[:, g * RP:(g + 1) * RP] += dnwg
        ddtr_ref[...] = ddtr.astype(ddtr_ref.dtype)

    row = lambda C: pl.BlockSpec((L, C), lambda c: (nc - 1 - c, 0))
    zspec = lambda s: pl.BlockSpec((None, L, D), lambda c: (s, nc - 1 - c, 0))
    pars = [alog, dtb, dsk, nw]
    return _pcall(body, name=name, grid=(nc,),
                  in_specs=[row(D), row(D), row(D), row(HP), zspec(2), zspec(3),
                            pl.BlockSpec((1, G, N, RP), lambda c: (nc - 1 - c, 0, 0, 0)), row(SD)]
                  + [_full_spec(p) for p in pars],
                  out_specs=[row(D), row(D), row(D), row(HP), row(D), row(D)] + [_full_spec(p) for p in pars],
                  out_shape=[_sds((T, D), F32)] * 3 + [_sds((T, HP), BF16), _sds((T, D), BF16), _sds((T, D), BF16)]
                  + [_sds(p.shape, F32) for p in pars],
                  scratch_shapes=[pltpu.VMEM((G, N, RP), F32)])(x0, x1, bc, dtr, u3, u3, sprev, dyn, *pars)


def _loss_head(y, target, *, name):
    T = y.shape[0]
    tm = min(TM, T)

    def body(y_ref, t_ref, loss_ref, dy_ref):
        e = y_ref[...] - t_ref[...]
        dy_ref[...] = e * (1.0 / D)

        @pl.when(pl.program_id(0) == 0)
        def _():
            loss_ref[...] = jnp.zeros_like(loss_ref)

        loss_ref[...] += 0.5 * jnp.sum(jnp.mean(jnp.square(e), axis=-1, keepdims=True), axis=0, keepdims=True)

    row = pl.BlockSpec((tm, D), lambda i: (i, 0))
    return _pcall(body, name=name, grid=(T // tm,), in_specs=[row, row],
                  out_specs=[pl.BlockSpec((1, 128), lambda i: (0, 0)), row],
                  out_shape=[_sds((1, 128), F32), _sds((T, D), F32)])(y, target)


_HBM = pl.BlockSpec(memory_space=pltpu.HBM)
_MESH = pl.DeviceIdType.MESH


def _exchange_comm(bufs, *, scatter):
    nb = len(bufs)

    def copies(in_refs, out_refs, sems, with_arrivals):
        send_sems, recv_sems, local_sems = sems
        x, y, c = lax.axis_index("x"), lax.axis_index("y"), lax.axis_index("c")
        me = 2 * x + y
        peers = [(1 - x, y), (x, 1 - y), (1 - x, 1 - y)]
        own, sends, arrivals = [], [], []
        for b in range(nb):
            src_own = in_refs[b].at[me] if scatter else in_refs[b]
            own.append(pltpu.make_async_copy(src_own, out_refs[b].at[me], local_sems.at[b]))
            for k, (px, py) in enumerate(peers):
                src = in_refs[b].at[2 * px + py] if scatter else in_refs[b]
                sends.append(pltpu.make_async_remote_copy(
                    src_ref=src, dst_ref=out_refs[b].at[me], send_sem=send_sems.at[b, k], recv_sem=recv_sems.at[b, k],
                    device_id=(px, py, c), device_id_type=_MESH))
                if with_arrivals:
                    slot = out_refs[b].at[2 * px + py]
                    arrivals.append(pltpu.make_async_remote_copy(
                        src_ref=slot, dst_ref=slot, send_sem=send_sems.at[b, k], recv_sem=recv_sems.at[b, k],
                        device_id=(px, py, c), device_id_type=_MESH))
        return own, sends, arrivals

    def start(in_refs, out_refs, sems):
        own, sends, _ = copies(in_refs, out_refs, sems, False)
        for cp in own + sends:
            cp.start()

    def wait(in_refs, out_refs, sems):
        own, sends, arrivals = copies(in_refs, out_refs, sems, True)
        for cp in arrivals:
            cp.wait_recv()
        for cp in sends:
            cp.wait_send()
        for cp in own:
            cp.wait()

    return dict(ins=list(bufs), outs=[_sds(b.shape if scatter else (NCHIP,) + b.shape, b.dtype) for b in bufs],
                sems=[pltpu.SemaphoreType.DMA((nb, 3)), pltpu.SemaphoreType.DMA((nb, 3)), pltpu.SemaphoreType.DMA((nb,))],
                start=start, wait=wait)


def _chip_exchange(bufs, *, scatter, name):
    comm = _exchange_comm(bufs, scatter=scatter)
    nb = len(bufs)

    def body(*refs):
        comm["start"](refs[:nb], refs[nb:2 * nb], refs[2 * nb:])
        comm["wait"](refs[:nb], refs[nb:2 * nb], refs[2 * nb:])

    return _ccall(body, name=name, in_specs=[_HBM] * nb, out_specs=[_HBM] * nb, out_shape=comm["outs"],
                  scratch_shapes=comm["sems"])(*bufs)


def _core_swap(bufs, *, name):
    nb = len(bufs)

    def body(*refs):
        in_refs, out_refs, send_sems, recv_sems = refs[:nb], refs[nb:2 * nb], refs[2 * nb], refs[2 * nb + 1]
        x, y, c = lax.axis_index("x"), lax.axis_index("y"), lax.axis_index("c")
        cps = [pltpu.make_async_remote_copy(src_ref=in_refs[b], dst_ref=out_refs[b], send_sem=send_sems.at[b],
                                            recv_sem=recv_sems.at[b], device_id=(x, y, 1 - c), device_id_type=_MESH)
               for b in range(nb)]
        for cp in cps:
            cp.start()
        for cp in cps:
            cp.wait()

    return _ccall(body, name=name, in_specs=[_HBM] * nb, out_specs=[_HBM] * nb,
                  out_shape=[_sds(b.shape, b.dtype) for b in bufs],
                  scratch_shapes=[pltpu.SemaphoreType.DMA((nb,)), pltpu.SemaphoreType.DMA((nb,))])(*bufs)


def _all_gather8(buf, *, name):
    def body(in_ref, out_ref, send_sems, recv_sems, local_sem):
        x, y, c = lax.axis_index("x"), lax.axis_index("y"), lax.axis_index("c")
        me = 4 * x + 2 * y + c
        own = pltpu.make_async_copy(in_ref, out_ref.at[me], local_sem)
        own.start()
        flips = [(fx, fy, fc) for fx in (0, 1) for fy in (0, 1) for fc in (0, 1)][1:]
        peers = [(x ^ fx, y ^ fy, c ^ fc) for fx, fy, fc in flips]
        sends = []
        for k, peer in enumerate(peers):
            cp = pltpu.make_async_remote_copy(src_ref=in_ref, dst_ref=out_ref.at[me], send_sem=send_sems.at[k],
                                              recv_sem=recv_sems.at[k], device_id=peer, device_id_type=_MESH)
            cp.start()
            sends.append(cp)
        for k, (px, py, pc) in enumerate(peers):
            slot = out_ref.at[4 * px + 2 * py + pc]
            pltpu.make_async_remote_copy(src_ref=slot, dst_ref=slot, send_sem=send_sems.at[k], recv_sem=recv_sems.at[k],
                                         device_id=(px, py, pc), device_id_type=_MESH).wait_recv()
        for cp in sends:
            cp.wait_send()
        own.wait()

    return _ccall(body, name=name, in_specs=[_HBM], out_specs=_HBM, out_shape=_sds((8,) + buf.shape, buf.dtype),
                  scratch_shapes=[pltpu.SemaphoreType.DMA((7,)), pltpu.SemaphoreType.DMA((7,)), pltpu.SemaphoreType.DMA])(buf)


def _row_tile(rows, cap):
    if rows <= cap:
        return rows
    return max(t for t in range(16, cap + 1, 16) if rows % t == 0)


def _sum_slots(stack, *, name, cap=256):
    S, Rr, C = stack.shape
    tr = _row_tile(Rr, cap)

    def body(s_ref, o_ref):
        acc = s_ref[0].astype(F32)
        for j in range(1, S):
            acc = acc + s_ref[j].astype(F32)
        o_ref[...] = acc

    return _pcall(body, name=name, grid=(Rr // tr,), in_specs=[pl.BlockSpec((S, tr, C), lambda i: (0, i, 0))],
                  out_specs=pl.BlockSpec((tr, C), lambda i: (i, 0)), out_shape=_sds((Rr, C), F32))(stack)


def _adamw(g_parts, w, m, v, *, name, cap=128):
    Rr, C = w.shape
    tr = _row_tile(Rr, cap)
    ng = len(g_parts)
    c1 = 1.0 / (1.0 - ADAM_B1 ** ADAM_STEP)
    c2 = 1.0 / (1.0 - ADAM_B2 ** ADAM_STEP)

    def body(*refs):
        g = refs[0][...]
        for r in refs[1:ng]:
            g = g + r[...]
        w_ref, m_ref, v_ref, g_out, d_out, m_out, v_out = refs[ng:]
        mn = ADAM_B1 * m_ref[...] + (1.0 - ADAM_B1) * g
        vn = ADAM_B2 * v_ref[...] + (1.0 - ADAM_B2) * jnp.square(g)
        g_out[...] = g
        m_out[...] = mn
        v_out[...] = vn
        d_out[...] = -ADAM_LR * ((mn * c1) / (jnp.sqrt(vn * c2) + ADAM_EPS) + ADAM_WD * w_ref[...])

    spec = pl.BlockSpec((tr, C), lambda i: (i, 0))
    return _pcall(body, name=name, grid=(Rr // tr,), in_specs=[spec] * (ng + 3), out_specs=[spec] * 4,
                  out_shape=[_sds((Rr, C), F32)] * 4)(*g_parts, w, m, v)


def _adamw_layers(mine, other, w3, m3, v3, *, name, cap=128):
    _, Rr, C = w3.shape
    tr = _row_tile(Rr, cap)
    nt = Rr // tr
    c1 = 1.0 / (1.0 - ADAM_B1 ** ADAM_STEP)
    c2 = 1.0 / (1.0 - ADAM_B2 ** ADAM_STEP)

    def body(m0, m1, o0, o1, w_ref, m_ref, v_ref, g_out, d_out, m_out, v_out):
        g = jnp.where(pl.program_id(0) == 0, m0[...] + o0[...], m1[...] + o1[...])
        mn = ADAM_B1 * m_ref[...] + (1.0 - ADAM_B1) * g
        vn = ADAM_B2 * v_ref[...] + (1.0 - ADAM_B2) * jnp.square(g)
        g_out[...] = g
        m_out[...] = mn
        v_out[...] = vn
        d_out[...] = -ADAM_LR * ((mn * c1) / (jnp.sqrt(vn * c2) + ADAM_EPS) + ADAM_WD * w_ref[...])

    g0 = pl.BlockSpec((tr, C), lambda l, i: (jnp.where(l == 0, i, nt - 1), 0))
    g1 = pl.BlockSpec((tr, C), lambda l, i: (jnp.where(l == 1, i, 0), 0))
    s3 = pl.BlockSpec((None, tr, C), lambda l, i: (l, i, 0))
    return _pcall(body, name=name, grid=(2, nt), in_specs=[g0, g1, g0, g1, s3, s3, s3], out_specs=[s3] * 4,
                  out_shape=[_sds(w3.shape, F32)] * 4)(mine[0], mine[1], other[0], other[1], w3, m3, v3)


def _pack(arrs, dtype, row_mult):
    flat = jnp.concatenate([a.reshape(-1).astype(dtype) for a in arrs])
    n = flat.shape[0]
    unit = row_mult * PACK_W
    total = unit * ((n + unit - 1) // unit)
    if total > n:
        flat = jnp.concatenate([flat, jnp.zeros((total - n,), dtype)])
    return flat.reshape(-1, PACK_W)


def _unpack(buf, shapes):
    flat = buf.reshape(-1)
    out, off = [], 0
    for s in shapes:
        n = math.prod(s)
        out.append(flat[off:off + n].reshape(s))
        off += n
    return out


def _conf_pre(a, g):
    return [a * jax.nn.sigmoid(g)]


def _conf_post(cs, cb, lg, lb):
    return (_silu(_layer_norm(cs[0] + cb, lg, lb)),)


def _xbc_pre(x0, x1, x2):
    return [x0, x1, x2]


def _xbc_post(cs, b0, b1, b2):
    return (_silu(cs[0] + b0), _silu(cs[1] + b1), _silu(cs[2] + b2))


def _ffn_pre(gate, val):
    return [gate, val]


def _ffn_post(cs, bg, bv):
    return (_silu(cs[0] + bg) * (cs[1] + bv),)


def _mix_fn(ga, gb, ya, yb):
    return (jax.nn.sigmoid(ga) * ya + jax.nn.sigmoid(gb) * yb,)


def _res_ln_fn(h, r, g, b):
    return (_layer_norm(ALPHA * h + r, g, b),)


def _ln_fn(x, g, b):
    return (_layer_norm(x, g, b),)


def _carrying(carry, key, gr, call):
    if key not in carry:
        return call(None)
    comm, done = carry[key](gr)
    out, got = call(comm)
    done(got)
    return out


def _layer_fwd(h, hb, W, l, carry):
    T = h.shape[0]
    tm = min(TM, T)
    nt = T // tm
    tmf = min(TM_FFN, T)
    ntf = T // tmf
    nm = lambda s: f"l{l}_{s}"
    u3 = _carrying(carry, "u", None, lambda comm: _mm_nn_slab_out(hb, W["w_p"], name=nm("u"), width=D, comm=comm))
    dtr = _mm_nn(hb, W["w_dt"], name=nm("dt"))
    (v3,) = _conv_fwd(_conf_pre, _conf_post, [_slab(u3, 0, tm), _slab(u3, 1, tm)], W["conv_w"],
                      [W["conv_b"], W["conv_ln_g"], W["conv_ln_b"]], [_o2(T, D, BF16, tm)],
                      K=CONV_K, C=D, name=nm("conf"), tm=tm, nt=nt)
    ya = _mm_nn(v3, W["w_co"], name=nm("ya"))
    tmx = min(TM_X, T)
    x0, x1, bc = _conv_fwd(_xbc_pre, _xbc_post, [_slab(u3, 6, tmx), _slab(u3, 7, tmx), _slab(u3, 8, tmx)], W["ssm_w"],
                           W["ssm_b"], [_o2(T, D, F32, tmx)] * 3, K=SSM_K, C=D, name=nm("xbc"), tm=tmx, nt=T // tmx)
    yn, sprev = _ssd_fwd(x0, x1, bc, dtr, u3, W["a_log"], W["dt_bias"], W["d_skip"], W["norm_w"], name=nm("ssd"))
    yb = _mm_nn(yn, W["w_so"], name=nm("yb"), tk=min(SD, 1024))
    (m,) = _rows_fwd(_mix_fn, [_slab(u3, 4, tm), _slab(u3, 5, tm), _r2(ya, tm), _r2(yb, tm)], [],
                     [_o2(T, D, BF16, tm)], name=nm("mix"), nt=nt)
    mix = _mm_nn(m, W["w_o"], name=nm("wo"))
    h1, h1b = _rows_fwd(lambda a, r, g, b: _res_ln_fn(a, r, g, b) * 2, [_r2(h, tm), _r2(mix, tm)],
                        [W["ln1_g"], W["ln1_b"]], [_o2(T, D, F32, tm), _o2(T, D, BF16, tm)], name=nm("ln1"), nt=nt)
    up3 = _carrying(carry, "up", None, lambda comm: _mm_nn_slab_out(h1b, W["w_up"], name=nm("up"), width=FFN, comm=comm))
    (f,) = _conv_fwd(_ffn_pre, _ffn_post, [_slab(up3, 0, tmf), _slab(up3, 1, tmf)], W["ffn_w"], W["ffn_b"],
                     [_o2(T, FFN, BF16, tmf)], K=FFN_K, C=FFN, name=nm("ffnact"), tm=tmf, nt=ntf)
    ffn = _mm_nn(f, W["w_dn"], name=nm("dn"))
    h2, h2b = _rows_fwd(lambda a, r, g, b: _res_ln_fn(a, r, g, b) * 2, [_r2(h1, tm), _r2(ffn, tm)],
                        [W["ln2_g"], W["ln2_b"]], [_o2(T, D, F32, tm), _o2(T, D, BF16, tm)], name=nm("ln2"), nt=nt)
    saved = dict(h=h, hb=hb, u3=u3, dtr=dtr, v3=v3, ya=ya, x0=x0, x1=x1, bc=bc, sprev=sprev, yn=yn, yb=yb, m=m,
                 mix=mix, h1=h1, h1b=h1b, up3=up3, f=f, ffn=ffn)
    return h2, h2b, saved


def _layer_bwd(dh2, W, sv, l, carry):
    T = dh2.shape[0]
    tm = min(TM, T)
    nt = T // tm
    tmf = min(TM_FFN, T)
    ntf = T // tmf
    nm = lambda s: f"l{l}_{s}"
    gr = {}
    (dres2, dffn), (gr["ln2_g"], gr["ln2_b"]) = _rows_bwd(
        _res_ln_fn, [_r2(sv["h1"], tm), _r2(sv["ffn"], tm)], [W["ln2_g"], W["ln2_b"]], [_r2(dh2, tm)],
        [_o2(T, D, F32, tm), _o2(T, D, BF16, tm)], name=nm("ln2_b"), nt=nt)
    df = _mm_nt(dffn, W["w_dn"], name=nm("dn_dx"))
    gr["w_dn"] = _carrying(carry, "dn_dw", gr,
                           lambda comm: _mm_tn(sv["f"], dffn, name=nm("dn_dw"), tmo=FFN // 2, comm=comm))
    up3 = sv["up3"]
    h8 = _halo_of(FFN_K)
    (dgate, dval), gr["ffn_w"], gr["ffn_b"] = _conv_bwd(
        _ffn_pre, _ffn_post, [_slab(up3, 0, tmf), _slab(up3, 1, tmf)],
        [_halo_slab(up3, 0, tmf, h8), _halo_slab(up3, 1, tmf, h8)], W["ffn_w"], W["ffn_b"], [_r2(df, tmf)],
        [_o2(T, FFN, BF16, tmf)] * 2, K=FFN_K, C=FFN, name=nm("ffnact_b"), tm=tmf, nt=ntf)
    dup3 = jnp.stack([dgate, dval])
    dh1 = _carrying(carry, "up_dx", gr,
                    lambda comm: _mm_slab_in_nt(dup3, W["w_up"], name=nm("up_dx"), add=dres2, comm=comm))
    gr["w_up"] = _carrying(carry, "up_dw", gr,
                           lambda comm: _mm_tn_slab(sv["h1b"], dup3, name=nm("up_dw"), tmo=min(512, D), comm=comm))
    (dres1, dmix), (gr["ln1_g"], gr["ln1_b"]) = _rows_bwd(
        _res_ln_fn, [_r2(sv["h"], tm), _r2(sv["mix"], tm)], [W["ln1_g"], W["ln1_b"]], [_r2(dh1, tm)],
        [_o2(T, D, F32, tm), _o2(T, D, BF16, tm)], name=nm("ln1_b"), nt=nt)
    dm = _mm_nt(dmix, W["w_o"], name=nm("wo_dx"))
    gr["w_o"] = _mm_tn(sv["m"], dmix, name=nm("wo_dw"), tmo=min(512, D))
    u3 = sv["u3"]
    (dga, dgb, dya, dyb), _ = _rows_bwd(
        _mix_fn, [_slab(u3, 4, tm), _slab(u3, 5, tm), _r2(sv["ya"], tm), _r2(sv["yb"], tm)], [], [_r2(dm, tm)],
        [_o2(T, D, BF16, tm)] * 4, name=nm("mix_b"), nt=nt)
    dv3 = _mm_nt(dya, W["w_co"], name=nm("ya_dx"))
    gr["w_co"] = _mm_tn(sv["v3"], dya, name=nm("ya_dw"), tmo=min(512, D))
    h32 = _halo_of(CONV_K)
    (da, dg), gr["conv_w"], (gr["conv_b"], gr["conv_ln_g"], gr["conv_ln_b"]) = _conv_bwd(
        _conf_pre, _conf_post, [_slab(u3, 0, tm), _slab(u3, 1, tm)],
        [_halo_slab(u3, 0, tm, h32), _halo_slab(u3, 1, tm, h32)], W["conv_w"],
        [W["conv_b"], W["conv_ln_g"], W["conv_ln_b"]], [_r2(dv3, tm)], [_o2(T, D, BF16, tm)] * 2,
        K=CONV_K, C=D, name=nm("conf_b"), tm=tm, nt=nt)
    dyn = _mm_nt(dyb, W["w_so"], name=nm("yb_dx"))
    gr["w_so"] = _mm_tn(sv["yn"], dyb, name=nm("yb_dw"), tmo=min(512, SD))
    (dx0, dx1, dbc, ddtr, dz0, dz1, gr["a_log"], gr["dt_bias"], gr["d_skip"], gr["norm_w"]) = _ssd_bwd(
        sv["x0"], sv["x1"], sv["bc"], sv["dtr"], u3, sv["sprev"], dyn, W["a_log"], W["dt_bias"], W["d_skip"],
        W["norm_w"], name=nm("ssd_b"))
    h8s = _halo_of(SSM_K)
    tmx = min(TM_X, T)
    (du6, du7, du8), gr["ssm_w"], gr["ssm_b"] = _conv_bwd(
        _xbc_pre, _xbc_post, [_slab(u3, 6, tmx), _slab(u3, 7, tmx), _slab(u3, 8, tmx)],
        [_halo_slab(u3, 6, tmx, h8s), _halo_slab(u3, 7, tmx, h8s), _halo_slab(u3, 8, tmx, h8s)], W["ssm_w"], W["ssm_b"],
        [_r2(dx0, tmx), _r2(dx1, tmx), _r2(dbc, tmx)], [_o2(T, D, BF16, tmx)] * 3, K=SSM_K, C=D, name=nm("xbc_b"),
        tm=tmx, nt=T // tmx)
    du3 = jnp.stack([da, dg, dz0, dz1, dga, dgb, du6, du7, du8])
    dh_a = _mm_nt(ddtr, W["w_dt"], name=nm("dt_dx"), add=dres1)
    dh = _carrying(carry, "u_dx", gr, lambda comm: _mm_slab_in_nt(du3, W["w_p"], name=nm("u_dx"), add=dh_a, comm=comm))
    gr["w_p"] = _mm_tn_slab(sv["hb"], du3, name=nm("u_dw"), tmo=min(512, D))
    gr["w_dt"] = _mm_tn(sv["hb"], ddtr, name=nm("dt_dw"), tmo=min(512, D))
    return dh, gr


_U_SPLIT = (2 * D + SD, 2 * D + SD + XBC, 2 * D + SD + XBC + H)


def _pad_rows(a, rows):
    return jnp.concatenate([a, jnp.zeros((rows - a.shape[0],) + a.shape[1:], a.dtype)], axis=0)


def _pad_lanes(a, lanes):
    return jnp.concatenate([a, jnp.zeros(a.shape[:-1] + (lanes - a.shape[-1],), a.dtype)], axis=-1)


def _w_in_layout(w_in):
    e0, e1, e2 = _U_SPLIT
    return dict(w_p=jnp.concatenate([w_in[:, :e0], w_in[:, e2:], w_in[:, e0:e1]], axis=1),
                w_dt=_pad_lanes(w_in[:, e1:e2], HP))


_MM_KEY = dict(w_conv_out="w_co", w_ssm_out="w_so", w_o="w_o", w_ffn_up="w_up", w_ffn_down="w_dn")


def _small_layer_weights(full, l):
    row = lambda a: a.reshape(1, -1)
    ssm_w = full["ssm_conv_w"][l]
    ffn_w = full["ffn_dw_w"][l]
    ssm_b = full["ssm_conv_b"][l]
    ffn_b = full["ffn_dw_b"][l]
    W = dict(
        conv_w=_pad_rows(full["conv_dw_w"][l], 32)[None],
        conv_b=row(full["conv_dw_b"][l]), conv_ln_g=row(full["conv_ln_g"][l]), conv_ln_b=row(full["conv_ln_b"][l]),
        ssm_w=jnp.stack([_pad_rows(ssm_w[:, p * D:(p + 1) * D], 8) for p in range(3)]),
        ssm_b=[row(ssm_b[p * D:(p + 1) * D]) for p in range(3)],
        a_log=_pad_lanes(row(full["ssm_a_log"][l]), HP), dt_bias=_pad_lanes(row(full["ssm_dt_bias"][l]), HP),
        d_skip=_pad_lanes(row(full["ssm_d"][l]), HP), norm_w=row(full["ssm_norm_w"][l]),
        ln1_g=row(full["ln1_g"][l]), ln1_b=row(full["ln1_b"][l]),
        ffn_w=jnp.stack([_pad_rows(ffn_w[:, p * FFN:(p + 1) * FFN], 8) for p in range(2)]),
        ffn_b=[row(ffn_b[p * FFN:(p + 1) * FFN]) for p in range(2)],
        ln2_g=row(full["ln2_g"][l]), ln2_b=row(full["ln2_b"][l]),
    )
    return W


def _layer_grads_to_reference_layout(gr):
    e0, e1, e2 = _U_SPLIT
    nx = XBC
    wp = gr["w_p"]
    w_in = jnp.concatenate([wp[:, :e0], wp[:, e0 + 2 * D:e0 + 2 * D + nx], gr["w_dt"][:, :H], wp[:, e0:e0 + 2 * D]], axis=1)
    return dict(
        w_in=w_in, conv_dw_w=gr["conv_w"][0, :CONV_K], conv_dw_b=gr["conv_b"][0], conv_ln_g=gr["conv_ln_g"][0],
        conv_ln_b=gr["conv_ln_b"][0], w_conv_out=gr["w_co"],
        ssm_conv_w=jnp.concatenate([gr["ssm_w"][p, :SSM_K] for p in range(3)], axis=1),
        ssm_conv_b=jnp.concatenate([b[0] for b in gr["ssm_b"]]),
        ssm_dt_bias=gr["dt_bias"][0, :H], ssm_a_log=gr["a_log"][0, :H], ssm_d=gr["d_skip"][0, :H],
        ssm_norm_w=gr["norm_w"][0], w_ssm_out=gr["w_so"], w_o=gr["w_o"], ln1_g=gr["ln1_g"][0], ln1_b=gr["ln1_b"][0],
        w_ffn_up=gr["w_up"], ffn_dw_w=jnp.concatenate([gr["ffn_w"][p, :FFN_K] for p in range(2)], axis=1),
        ffn_dw_b=jnp.concatenate([b[0] for b in gr["ffn_b"]]), w_ffn_down=gr["w_dn"], ln2_g=gr["ln2_g"][0],
        ln2_b=gr["ln2_b"][0],
    )


_BIG = dict(w_in=2, w_conv_out=1, w_ssm_out=1, w_o=1, w_ffn_up=2, w_ffn_down=1)
_SMALL_SHARDED = dict(conv_dw_w=2, ssm_conv_w=2, ffn_dw_w=2)
_REPLICATED = ("ln_in_g", "ln_in_b", "conv_dw_b", "conv_ln_g", "conv_ln_b", "ssm_conv_b", "ssm_dt_bias", "ssm_a_log",
               "ssm_d", "ssm_norm_w", "ln1_g", "ln1_b", "ffn_dw_b", "ln2_g", "ln2_b")
_WEIGHTS = ("ln_in_g", "ln_in_b", "w_in", "conv_dw_w", "conv_dw_b", "conv_ln_g", "conv_ln_b", "w_conv_out", "ssm_conv_w",
            "ssm_conv_b", "ssm_dt_bias", "ssm_a_log", "ssm_d", "ssm_norm_w", "w_ssm_out", "w_o", "ln1_g", "ln1_b",
            "w_ffn_up", "ffn_dw_w", "ffn_dw_b", "w_ffn_down", "ln2_g", "ln2_b")


def _split_chips(a, axis):
    rows, cols = a.shape
    if axis == 0:
        return a.reshape(NCHIP, rows // NCHIP, cols)
    return a.reshape(rows, NCHIP, cols // NCHIP).transpose(1, 0, 2)


def kernel(x, ln_in_g, ln_in_b, w_in, conv_dw_w, conv_dw_b, conv_ln_g, conv_ln_b, w_conv_out, ssm_conv_w, ssm_conv_b, ssm_dt_bias, ssm_a_log, ssm_d, ssm_norm_w, w_ssm_out, w_o, ln1_g, ln1_b, w_ffn_up, ffn_dw_w, ffn_dw_b, w_ffn_down, ln2_g, ln2_b, loss_target, m_ln_in_g, m_ln_in_b, m_w_in, m_conv_dw_w, m_conv_dw_b, m_conv_ln_g, m_conv_ln_b, m_w_conv_out, m_ssm_conv_w, m_ssm_conv_b, m_ssm_dt_bias, m_ssm_a_log, m_ssm_d, m_ssm_norm_w, m_w_ssm_out, m_w_o, m_ln1_g, m_ln1_b, m_w_ffn_up, m_ffn_dw_w, m_ffn_dw_b, m_w_ffn_down, m_ln2_g, m_ln2_b, v_ln_in_g, v_ln_in_b, v_w_in, v_conv_dw_w, v_conv_dw_b, v_conv_ln_g, v_conv_ln_b, v_w_conv_out, v_ssm_conv_w, v_ssm_conv_b, v_ssm_dt_bias, v_ssm_a_log, v_ssm_d, v_ssm_norm_w, v_w_ssm_out, v_w_o, v_ln1_g, v_ln1_b, v_w_ffn_up, v_ffn_dw_w, v_ffn_dw_b, v_w_ffn_down, v_ln2_g, v_ln2_b):
    args = locals()
    w = {n: args[n] for n in _WEIGHTS}
    mom = {n: args["m_" + n] for n in _WEIGHTS}
    vel = {n: args["v_" + n] for n in _WEIGHTS}
    T = x.shape[1]
    tm = min(TM, T)
    nt = T // tm
    chip = 2 * lax.axis_index("x") + lax.axis_index("y")

    assert DEPTH == 2
    big_names, small_names = list(_BIG), list(_SMALL_SHARDED)
    rest_big = [n for n in big_names if n != "w_in"]
    bf = {n: w[n].astype(BF16) for n in big_names}
    join = lambda got, axis: jnp.concatenate([got[j] for j in range(NCHIP)], axis=axis)
    first = _chip_exchange([bf["w_in"][0]] + [w[n] for n in small_names], scatter=False, name="gather_first")
    full = {n: join(gk, _SMALL_SHARDED[n]) for n, gk in zip(small_names, first[1:])}
    for n in _REPLICATED:
        full[n] = w[n]
    Ws = [_small_layer_weights(full, l) for l in range(DEPTH)]
    Ws[0].update(_w_in_layout(join(first[0], 1)))

    def rest_arrived(l):
        def done(got):
            for n, gk in zip(rest_big, got):
                Ws[l][_MM_KEY[n]] = join(gk, _BIG[n] - 1)
        return done

    carry_fwd = [
        {"u": lambda gr: (_exchange_comm([bf[n][0] for n in rest_big], scatter=False), rest_arrived(0)),
         "up": lambda gr: (_exchange_comm([bf["w_in"][1]], scatter=False),
                           lambda got: Ws[1].update(_w_in_layout(join(got[0], 1))))},
        {"u": lambda gr: (_exchange_comm([bf[n][1] for n in rest_big], scatter=False), rest_arrived(1))},
    ]

    x2 = x.reshape(T, D)
    g_in, b_in = ln_in_g.reshape(1, D), ln_in_b.reshape(1, D)
    h, hb = _rows_fwd(lambda a, g, b: _ln_fn(a, g, b) * 2, [_r2(x2, tm)], [g_in, b_in],
                      [_o2(T, D, F32, tm), _o2(T, D, BF16, tm)], name="ln_in", nt=nt)
    saved = []
    for l in range(DEPTH):
        h, hb, sv = _layer_fwd(h, hb, Ws[l], l, carry_fwd[l])
        saved.append(sv)
    loss_row, dh = _loss_head(h, loss_target.reshape(T, D), name="loss")

    arrived = {}

    def exchange(names, l, grads):
        def make(gr):
            src = grads(gr)
            def done(got):
                for n, gk in zip(names, got):
                    arrived[(n, l)] = gk
            return _exchange_comm([_split_chips(src[n], _BIG[n] - 1) for n in names], scatter=True), done
        return make

    layer_grads = [None] * DEPTH
    dh, gr = _layer_bwd(dh, Ws[1], saved[1], 1, {})
    layer_grads[1] = _layer_grads_to_reference_layout(gr)
    g1 = lambda gr: layer_grads[1]
    g0 = lambda gr: {n: gr[_MM_KEY[n]] for n in rest_big}
    dh, gr = _layer_bwd(dh, Ws[0], saved[0], 0, {
        "dn_dw": exchange(["w_conv_out", "w_ssm_out", "w_o"], 1, g1),
        "up_dx": exchange(["w_in"], 1, g1),
        "up_dw": exchange(["w_ffn_up", "w_ffn_down"], 1, g1),
        "u_dx": exchange(rest_big, 0, g0)})
    layer_grads[0] = _layer_grads_to_reference_layout(gr)
    (arrived[("w_in", 0)],) = _chip_exchange([_split_chips(layer_grads[0]["w_in"], 1)], scatter=True, name="exchange_last")
    (grad_x2,), (d_g_in, d_b_in) = _rows_bwd(_ln_fn, [_r2(x2, tm)], [g_in, b_in], [_r2(dh, tm)], [_o2(T, D, F32, tm)],
                                             name="ln_in_b", nt=nt)
    local = {n: jnp.stack([layer_grads[l][n] for l in range(DEPTH)]) for n in _WEIGHTS[2:] if n not in _BIG}
    local["ln_in_g"], local["ln_in_b"] = d_g_in[0], d_b_in[0]
    res = [{}, {}, {}, {}]

    keys = [(n, l) for n in big_names for l in range(DEPTH)]
    mine = [_sum_slots(arrived[k], name=f"sum_chips_{k[0]}_{k[1]}") for k in keys]
    other = _core_swap(mine, name="swap_cores")
    for i, n in enumerate(big_names):
        outs = _adamw_layers(mine[2 * i:2 * i + 2], other[2 * i:2 * i + 2], w[n], mom[n], vel[n], name="adamw_" + n)
        for q in range(4):
            res[q][n] = outs[q]

    rest_names = list(_REPLICATED) + small_names
    part = _pack([loss_row] + [local[n] for n in rest_names], F32, 8)
    parts = _all_gather8(part, name="gather_small")
    total = _sum_slots(parts, name="sum_devices")
    tot = _unpack(total, [loss_row.shape] + [local[n].shape for n in rest_names])
    loss = tot[0][0, 0]
    g_rest = {}
    for n, t in zip(rest_names, tot[1:]):
        if n in _SMALL_SHARDED:
            ax = _SMALL_SHARDED[n]
            t = lax.dynamic_slice_in_dim(t, chip * w[n].shape[ax], w[n].shape[ax], axis=ax)
        g_rest[n] = t
    pk = lambda d: _pack([d[n] for n in rest_names], F32, 8)
    rest_out = _adamw([pk(g_rest)], pk(w), pk(mom), pk(vel), name="adamw_rest")
    rest_out = [_unpack(o, [w[n].shape for n in rest_names]) for o in rest_out]

    for q in range(4):
        for k, n in enumerate(rest_names):
            res[q][n] = rest_out[q][k]
    grad_x = grad_x2.reshape(x.shape)
    return (loss, grad_x, *[res[0][n] for n in _WEIGHTS], *[res[1][n] for n in _WEIGHTS],
            *[res[2][n] for n in _WEIGHTS], *[res[3][n] for n in _WEIGHTS])
```

```python
import functools
import math

import jax
import jax.numpy as jnp
from jax import lax
from jax.experimental import pallas as pl
from jax.experimental.pallas import tpu as pltpu

F32 = jnp.float32
BF16 = jnp.bfloat16
HI = lax.Precision.HIGHEST

D = 1024
DEPTH = 2
CONV_K = 31
SD = 2 * D
P = 64
H = SD // P
G = 4
R = H // G
N = 128
RP = R * P
SSM_K = 4
L = 128
XBC = SD + 2 * G * N
FFN = 2816
FFN_K = 3
IN_DIM = 2 * D + SD + XBC + H + 2 * D
ALPHA = (2 * DEPTH) ** 0.25
LN_EPS = 1e-5
RMS_EPS = 1e-5
ADAM_LR, ADAM_B1, ADAM_B2, ADAM_EPS, ADAM_WD, ADAM_STEP = 0.001, 0.9, 0.999, 1e-08, 0.01, 10

HP = 128
NCHIP = 4
PACK_W = 1024
VMEM_LIMIT = 56 * 1024 * 1024
TM = 512
TM_WIDE = 1024
TM_X = 256
TM_FFN = 128
TK = 1024
EW_ROWS = 512

assert D == 2 * RP and 2 * G * N == D and XBC == 3 * D and H <= HP


def _pcall(body, *, name, grid=(), in_specs, out_specs, out_shape, scratch_shapes=()):
    params = pltpu.CompilerParams(vmem_limit_bytes=VMEM_LIMIT, dimension_semantics=("arbitrary",) * len(grid))
    return pl.pallas_call(body, name=name, grid=grid, in_specs=in_specs, out_specs=out_specs, out_shape=out_shape,
                          scratch_shapes=list(scratch_shapes), compiler_params=params)


def _pcall_carrying(body, comm, *, name, grid, in_specs, out_specs, out_shape, scratch_shapes=()):
    in_specs, out_specs, out_shape = list(in_specs), list(out_specs), list(out_shape)
    scratch_shapes = list(scratch_shapes)
    n_in, n_out, n_scr = len(in_specs), len(out_specs), len(scratch_shapes)
    nci, nco = len(comm["ins"]), len(comm["outs"])

    def wrapped(*refs):
        ins, cin = refs[:n_in], refs[n_in:n_in + nci]
        outs = refs[n_in + nci:n_in + nci + n_out]
        cout = refs[n_in + nci + n_out:n_in + nci + n_out + nco]
        scr = refs[n_in + nci + n_out + nco:n_in + nci + n_out + nco + n_scr]
        csem = refs[n_in + nci + n_out + nco + n_scr:]
        ids = [pl.program_id(ax) for ax in range(len(grid))]
        first = functools.reduce(jnp.logical_and, [i == 0 for i in ids])
        last = functools.reduce(jnp.logical_and, [i == g - 1 for i, g in zip(ids, grid)])

        @pl.when(first)
        def _():
            comm["start"](cin, cout, csem)

        body(*ins, *outs, *scr)

        @pl.when(last)
        def _():
            comm["wait"](cin, cout, csem)

    call = _pcall(wrapped, name=name, grid=grid, in_specs=in_specs + [_HBM] * nci, out_specs=out_specs + [_HBM] * nco,
                  out_shape=out_shape + list(comm["outs"]), scratch_shapes=scratch_shapes + list(comm["sems"]))

    def run(*operands):
        res = call(*operands, *comm["ins"])
        return list(res[:n_out]), list(res[n_out:])

    return run


def _ccall(body, *, name, in_specs, out_specs, out_shape, scratch_shapes):
    return pl.pallas_call(body, name=name, in_specs=in_specs, out_specs=out_specs, out_shape=out_shape,
                          scratch_shapes=list(scratch_shapes))


def _full_spec(a):
    nd = a.ndim
    return pl.BlockSpec(a.shape, lambda *_: (0,) * nd)


def _sds(shape, dtype):
    return jax.ShapeDtypeStruct(tuple(shape), dtype)


def _mm(a, b, *, name, grid, a_spec, b_spec, o_spec, out_shape, acc_shape, trans_a=False, trans_b=False, add=None,
        add_spec=None, comm=None):
    nk = grid[2]
    dn = (((0 if trans_a else 1,), (1 if trans_b else 0,)), ((), ()))
    has_add = add is not None

    def body(*refs):
        a_ref, b_ref = refs[0], refs[1]
        add_ref = refs[2] if has_add else None
        o_ref = refs[3] if has_add else refs[2]
        part = lax.dot_general(a_ref[...].astype(BF16), b_ref[...].astype(BF16), dn, preferred_element_type=F32)

        def finish(res):
            if has_add:
                res = res + add_ref[...]
            o_ref[...] = res.astype(o_ref.dtype)

        if nk == 1:
            finish(part)
        else:
            acc = refs[-1]
            k = pl.program_id(2)

            @pl.when(k == 0)
            def _():
                acc[...] = part

            @pl.when(k > 0)
            def _():
                acc[...] += part

            @pl.when(k == nk - 1)
            def _():
                finish(acc[...])

    ins = [a, b] + ([add] if has_add else [])
    specs = [a_spec, b_spec] + ([add_spec] if has_add else [])
    scratch = [pltpu.VMEM(acc_shape, F32)] if nk > 1 else []
    if comm is not None:
        (out,), got = _pcall_carrying(body, comm, name=name, grid=grid, in_specs=specs, out_specs=[o_spec],
                                      out_shape=[out_shape], scratch_shapes=scratch)(*ins)
        return out, got
    return _pcall(body, name=name, grid=grid, in_specs=specs, out_specs=o_spec, out_shape=out_shape,
                  scratch_shapes=scratch)(*ins)


def _mm_nn(a, b, *, name, out_dtype=F32, tn=None, tk=None, add=None):
    M, K = a.shape
    Nn = b.shape[1]
    tm = min(TM, M)
    tn = Nn if tn is None else tn
    tk = K if tk is None else tk
    grid = (M // tm, Nn // tn, K // tk)
    return _mm(a, b, name=name, grid=grid,
               a_spec=pl.BlockSpec((tm, tk), lambda i, j, k: (i, k)),
               b_spec=pl.BlockSpec((tk, tn), lambda i, j, k: (k, j)),
               o_spec=pl.BlockSpec((tm, tn), lambda i, j, k: (i, j)),
               out_shape=_sds((M, Nn), out_dtype), acc_shape=(tm, tn), add=add,
               add_spec=pl.BlockSpec((tm, tn), lambda i, j, k: (i, j)))


def _mm_nn_slab_out(a, b, *, name, width, comm=None):
    M, K = a.shape
    S = b.shape[1] // width
    tm = min(TM_WIDE, M)
    return _mm(a, b, name=name, grid=(M // tm, S, 1),
               a_spec=pl.BlockSpec((tm, K), lambda i, j, k: (i, 0)),
               b_spec=pl.BlockSpec((K, width), lambda i, j, k: (0, j)),
               o_spec=pl.BlockSpec((None, tm, width), lambda i, j, k: (j, i, 0)),
               out_shape=_sds((S, M, width), F32), acc_shape=(tm, width), comm=comm)


def _mm_nt(a, b, *, name, add=None):
    M, K = a.shape
    Nn = b.shape[0]
    tm = min(TM, M)
    return _mm(a, b, name=name, grid=(M // tm, 1, 1), trans_b=True,
               a_spec=pl.BlockSpec((tm, K), lambda i, j, k: (i, 0)),
               b_spec=pl.BlockSpec((Nn, K), lambda i, j, k: (0, 0)),
               o_spec=pl.BlockSpec((tm, Nn), lambda i, j, k: (i, 0)),
               out_shape=_sds((M, Nn), F32), acc_shape=(tm, Nn), add=add,
               add_spec=pl.BlockSpec((tm, Nn), lambda i, j, k: (i, 0)))


def _mm_slab_in_nt(a3, b, *, name, add, comm=None):
    S, M, width = a3.shape
    Nn = b.shape[0]
    tm = min(TM_WIDE, M)
    return _mm(a3, b, name=name, grid=(M // tm, 1, S), trans_b=True,
               a_spec=pl.BlockSpec((None, tm, width), lambda i, j, k: (k, i, 0)),
               b_spec=pl.BlockSpec((Nn, width), lambda i, j, k: (0, k)),
               o_spec=pl.BlockSpec((tm, Nn), lambda i, j, k: (i, 0)),
               out_shape=_sds((M, Nn), F32), acc_shape=(tm, Nn), add=add,
               add_spec=pl.BlockSpec((tm, Nn), lambda i, j, k: (i, 0)), comm=comm)


def _mm_tn(a, b, *, name, tmo, tn=None, comm=None):
    T, M = a.shape
    Nn = b.shape[1]
    tn = Nn if tn is None else tn
    tk = min(TK, T)
    return _mm(a, b, name=name, grid=(M // tmo, Nn // tn, T // tk), trans_a=True,
               a_spec=pl.BlockSpec((tk, tmo), lambda i, j, k: (k, i)),
               b_spec=pl.BlockSpec((tk, tn), lambda i, j, k: (k, j)),
               o_spec=pl.BlockSpec((tmo, tn), lambda i, j, k: (i, j)),
               out_shape=_sds((M, Nn), BF16), acc_shape=(tmo, tn), comm=comm)


def _mm_tn_slab(a, b3, *, name, tmo, comm=None):
    T, M = a.shape
    S, _, width = b3.shape
    tk = min(TK, T)
    return _mm(a, b3, name=name, grid=(M // tmo, S, T // tk), trans_a=True,
               a_spec=pl.BlockSpec((tk, tmo), lambda i, j, k: (k, i)),
               b_spec=pl.BlockSpec((None, tk, width), lambda i, j, k: (j, k, 0)),
               o_spec=pl.BlockSpec((tmo, width), lambda i, j, k: (i, j)),
               out_shape=_sds((M, S * width), BF16), acc_shape=(tmo, width), comm=comm)


def _r2(a, tm):
    return (a, (tm, a.shape[1]), lambda i: (i, 0))


def _slab(a3, s, tm):
    return (a3, (None, tm, a3.shape[2]), lambda i: (s, i, 0))


def _o2(T, C, dtype, tm):
    return ((T, C), dtype, (tm, C), lambda i: (i, 0))


def _rows_fwd(fn, row_ins, par_ins, outs, *, name, nt):
    nr, npar = len(row_ins), len(par_ins)

    def body(*refs):
        vals = [r[...] for r in refs[:nr + npar]]
        res = fn(*vals)
        for o_ref, v in zip(refs[nr + npar:], res):
            o_ref[...] = v.astype(o_ref.dtype)

    return _pcall(body, name=name, grid=(nt,),
                  in_specs=[pl.BlockSpec(bs, im) for (_, bs, im) in row_ins] + [_full_spec(p) for p in par_ins],
                  out_specs=[pl.BlockSpec(bs, im) for (_, _, bs, im) in outs],
                  out_shape=[_sds(s, d) for (s, d, _, _) in outs])(*[r[0] for r in row_ins], *par_ins)


def _rows_bwd(fn, row_ins, par_ins, cot_ins, drow_outs, *, name, nt):
    nr, npar, nc = len(row_ins), len(par_ins), len(cot_ins)
    keep = [k for k, o in enumerate(drow_outs) if o is not None]

    def body(*refs):
        vals = [r[...].astype(F32) for r in refs[:nr + npar]]
        cots = [r[...].astype(F32) for r in refs[nr + npar:nr + npar + nc]]
        orefs = refs[nr + npar + nc:]
        _, vjp = jax.vjp(fn, *vals)
        grads = vjp(tuple(cots))
        for o_ref, k in zip(orefs[:len(keep)], keep):
            o_ref[...] = grads[k].astype(o_ref.dtype)
        prefs = orefs[len(keep):]

        @pl.when(pl.program_id(0) == 0)
        def _():
            for p_ref in prefs:
                p_ref[...] = jnp.zeros_like(p_ref)

        for p_ref, g in zip(prefs, grads[nr:]):
            p_ref[...] += g

    outs = [drow_outs[k] for k in keep]
    res = _pcall(body, name=name, grid=(nt,),
                 in_specs=[pl.BlockSpec(bs, im) for (_, bs, im) in row_ins] + [_full_spec(p) for p in par_ins]
                 + [pl.BlockSpec(bs, im) for (_, bs, im) in cot_ins],
                 out_specs=[pl.BlockSpec(bs, im) for (_, _, bs, im) in outs] + [_full_spec(p) for p in par_ins],
                 out_shape=[_sds(s, d) for (s, d, _, _) in outs] + [_sds(p.shape, F32) for p in par_ins],
                 )(*[r[0] for r in row_ins], *par_ins, *[c[0] for c in cot_ins])
    return list(res[:len(keep)]), list(res[len(keep):])


def _layer_norm(v, g, b):
    mu = jnp.mean(v, axis=-1, keepdims=True)
    var = jnp.mean(jnp.square(v - mu), axis=-1, keepdims=True)
    return (v - mu) * lax.rsqrt(var + LN_EPS) * g + b


def _silu(v):
    return v * jax.nn.sigmoid(v)


def _softplus(v):
    return jnp.maximum(v, 0.0) + jnp.log1p(jnp.exp(-jnp.abs(v)))


def _halo_of(K):
    return 8 * ((K - 1 + 7) // 8)


def _conv_taps(buf, p, w_ref, K, halo, tm):
    acc = None
    for k in range(K):
        term = buf[p, pl.ds(halo - (K - 1) + k, tm), :] * w_ref[p, k:k + 1, :]
        acc = term if acc is None else acc + term
    return acc


def _conv_fwd(pre, post, row_ins, w, par_ins, outs, *, K, C, name, tm, nt):
    nparts = w.shape[0]
    halo = _halo_of(K)
    nr, npar = len(row_ins), len(par_ins)

    def body(*refs):
        rows = [r[...] for r in refs[:nr]]
        w_ref = refs[nr]
        pars = [r[...] for r in refs[nr + 1:nr + 1 + npar]]
        orefs = refs[nr + 1 + npar:-1]
        buf = refs[-1]
        i = pl.program_id(0)
        xin = pre(*rows)
        cs = []
        for p in range(nparts):
            @pl.when(i == 0)
            def _():
                buf[p, pl.ds(0, halo), :] = jnp.zeros((halo, C), F32)

            @pl.when(i > 0)
            def _():
                buf[p, pl.ds(0, halo), :] = buf[p, pl.ds(tm, halo), :]

            buf[p, pl.ds(halo, tm), :] = xin[p]
            cs.append(_conv_taps(buf, p, w_ref, K, halo, tm))
        res = post(cs, *pars)
        for o_ref, v in zip(orefs, res):
            o_ref[...] = v.astype(o_ref.dtype)

    return _pcall(body, name=name, grid=(nt,),
                  in_specs=[pl.BlockSpec(bs, im) for (_, bs, im) in row_ins] + [_full_spec(w)] + [_full_spec(p) for p in par_ins],
                  out_specs=[pl.BlockSpec(bs, im) for (_, _, bs, im) in outs],
                  out_shape=[_sds(s, d) for (s, d, _, _) in outs],
                  scratch_shapes=[pltpu.VMEM((nparts, halo + tm, C), F32)])(*[r[0] for r in row_ins], w, *par_ins)


def _conv_bwd(pre, post, row_ins, halo_ins, w, par_ins, cot_ins, drow_outs, *, K, C, name, tm, nt):
    nparts = w.shape[0]
    halo = _halo_of(K)
    nr, npar, nc = len(row_ins), len(par_ins), len(cot_ins)

    def body(*refs):
        rows = [r[...].astype(F32) for r in refs[:nr]]
        halos = [r[...].astype(F32) for r in refs[nr:2 * nr]]
        w_ref = refs[2 * nr]
        pars = [r[...] for r in refs[2 * nr + 1:2 * nr + 1 + npar]]
        cots = [r[...].astype(F32) for r in refs[2 * nr + 1 + npar:2 * nr + 1 + npar + nc]]
        rest = refs[2 * nr + 1 + npar + nc:]
        drow_refs, dw_ref, dpar_refs, bufx, bufd = rest[:nr], rest[nr], rest[nr + 1:nr + 1 + npar], rest[-2], rest[-1]
        s = pl.program_id(0)
        first_tile = s == nt - 1

        @pl.when(s == 0)
        def _():
            dw_ref[...] = jnp.zeros_like(dw_ref)
            for p_ref in dpar_refs:
                p_ref[...] = jnp.zeros_like(p_ref)
            for p in range(nparts):
                bufd[p, pl.ds(tm, halo), :] = jnp.zeros((halo, C), F32)

        xin, pre_vjp = jax.vjp(pre, *rows)
        xh = pre(*halos)
        cs = []
        for p in range(nparts):
            bufx[p, pl.ds(0, halo), :] = jnp.where(first_tile, 0.0, xh[p])
            bufx[p, pl.ds(halo, tm), :] = xin[p]
            cs.append(_conv_taps(bufx, p, w_ref, K, halo, tm))
        _, post_vjp = jax.vjp(lambda c, q: post(c, *q), cs, pars)
        dcs, dpars = post_vjp(tuple(cots))
        dxin = []
        for p in range(nparts):
            dc = dcs[p]
            bufd[p, pl.ds(0, tm), :] = dc
            acc = None
            for k in range(K):
                term = bufd[p, pl.ds(K - 1 - k, tm), :] * w_ref[p, k:k + 1, :]
                acc = term if acc is None else acc + term
                dw_ref[p, k:k + 1, :] += jnp.sum(dc * bufx[p, pl.ds(halo - (K - 1) + k, tm), :], axis=0, keepdims=True)
            dxin.append(acc)
            bufd[p, pl.ds(tm, halo), :] = dc[0:halo, :]
        drows = pre_vjp(dxin)
        for o_ref, g in zip(drow_refs, drows):
            o_ref[...] = g.astype(o_ref.dtype)
        for p_ref, g in zip(dpar_refs, dpars):
            p_ref[...] += g

    rev = lambda im: (lambda s: im(nt - 1 - s))
    res = _pcall(body, name=name, grid=(nt,),
                 in_specs=[pl.BlockSpec(bs, rev(im)) for (_, bs, im) in row_ins]
                 + [pl.BlockSpec(bs, rev(im)) for (_, bs, im) in halo_ins]
                 + [_full_spec(w)] + [_full_spec(p) for p in par_ins]
                 + [pl.BlockSpec(bs, rev(im)) for (_, bs, im) in cot_ins],
                 out_specs=[pl.BlockSpec(bs, rev(im)) for (_, _, bs, im) in drow_outs] + [_full_spec(w)]
                 + [_full_spec(p) for p in par_ins],
                 out_shape=[_sds(sh, d) for (sh, d, _, _) in drow_outs] + [_sds(w.shape, F32)]
                 + [_sds(p.shape, F32) for p in par_ins],
                 scratch_shapes=[pltpu.VMEM((nparts, halo + tm, C), F32), pltpu.VMEM((nparts, tm + halo, C), F32)],
                 )(*[r[0] for r in row_ins], *[r[0] for r in halo_ins], w, *par_ins, *[c[0] for c in cot_ins])
    return list(res[:nr]), res[nr], list(res[nr + 1:])


def _halo_r2(a, tm, halo):
    q = tm // halo
    return (a, (halo, a.shape[1]), lambda i: (jnp.maximum(i * q - 1, 0), 0))


def _halo_slab(a3, s, tm, halo):
    q = tm // halo
    return (a3, (None, halo, a3.shape[2]), lambda i: (s, jnp.maximum(i * q - 1, 0), 0))


CONF_HALO = _halo_of(CONV_K)
CONF_RB = 32


def _shifted_copies(buf, shifted, rows):
    for j in range(1, 8):
        shifted[j - 1, pl.ds(0, rows), :] = buf[pl.ds(j, rows), :]


def _shifted_rows(buf, shifted, s, base, nrows):
    j, q = s % 8, s // 8
    if j == 0:
        return buf[pl.ds(base + 8 * q, nrows), :]
    return shifted[j - 1, pl.ds(base + 8 * q, nrows), :]


def _conf_fwd(u3, w, cb, lg, lb, *, name):
    T = u3.shape[1]
    tm = min(TM_X, T)
    nt = T // tm
    K, halo, RB = CONV_K, CONF_HALO, min(CONF_RB, tm)

    def body(a_ref, g_ref, w_ref, cb_ref, lg_ref, lb_ref, v3_ref, c_ref, bufx, xs):
        i = pl.program_id(0)

        @pl.when(i == 0)
        def _():
            bufx[pl.ds(0, halo), :] = jnp.zeros((halo, D), F32)

        @pl.when(i > 0)
        def _():
            bufx[pl.ds(0, halo), :] = bufx[pl.ds(tm, halo), :]

        bufx[pl.ds(halo, tm), :] = a_ref[...] * jax.nn.sigmoid(g_ref[...])
        _shifted_copies(bufx, xs, halo + tm - 8)

        def group(r, carry):
            base = pl.multiple_of(r * RB, RB)
            acc = None
            for k in range(K):
                term = _shifted_rows(bufx, xs, halo - (K - 1) + k, base, RB) * w_ref[k:k + 1, :]
                acc = term if acc is None else acc + term
            c_ref[pl.ds(base, RB), :] = acc
            return carry

        lax.fori_loop(0, tm // RB, group, 0)
        v3_ref[...] = _conf_post([c_ref[...]], cb_ref[...], lg_ref[...], lb_ref[...])[0].astype(v3_ref.dtype)

    slab = lambda s: pl.BlockSpec((None, tm, D), lambda i: (s, i, 0))
    row = pl.BlockSpec((tm, D), lambda i: (i, 0))
    pars = [w, cb, lg, lb]
    return _pcall(body, name=name, grid=(nt,), in_specs=[slab(0), slab(1)] + [_full_spec(p) for p in pars],
                  out_specs=[row, row], out_shape=[_sds((T, D), BF16), _sds((T, D), F32)],
                  scratch_shapes=[pltpu.VMEM((halo + tm, D), F32), pltpu.VMEM((7, halo + tm - 8, D), F32)],
                  )(u3, u3, *pars)


def _conf_bwd(u3, c, dv3, w, cb, lg, lb, *, name):
    T = u3.shape[1]
    tm = min(TM_X, T)
    nt = T // tm
    K, halo, RB = CONV_K, CONF_HALO, min(CONF_RB, tm)
    q = tm // halo

    def body(a_ref, g_ref, ah_ref, gh_ref, c_ref, dv3_ref, w_ref, cb_ref, lg_ref, lb_ref,
             da_ref, dg_ref, dw_ref, dcb_ref, dlg_ref, dlb_ref, bufx, xs, bufd, ds, dv0):
        s = pl.program_id(0)
        first_tile = s == nt - 1

        @pl.when(s == 0)
        def _():
            for r in (dw_ref, dcb_ref, dlg_ref, dlb_ref):
                r[...] = jnp.zeros_like(r)
            bufd[pl.ds(tm, halo), :] = jnp.zeros((halo, D), F32)

        a, g = a_ref[...], g_ref[...]
        xin, pre_vjp = jax.vjp(lambda p, q_: _conf_pre(p, q_)[0], a, g)
        bufx[pl.ds(0, halo), :] = jnp.where(first_tile, 0.0, _conf_pre(ah_ref[...], gh_ref[...])[0])
        bufx[pl.ds(halo, tm), :] = xin
        _shifted_copies(bufx, xs, halo + tm - 8)

        _, post_vjp = jax.vjp(lambda cc, b_, g_, l_: _conf_post([cc], b_, g_, l_)[0],
                              c_ref[...], cb_ref[...], lg_ref[...], lb_ref[...])
        dc, dcb, dlg, dlb = post_vjp(dv3_ref[...])
        dcb_ref[...] += dcb
        dlg_ref[...] += dlg
        dlb_ref[...] += dlb
        bufd[pl.ds(0, tm), :] = dc
        _shifted_copies(bufd, ds, tm + halo - 8)

        def dx_group(r, carry):
            base = pl.multiple_of(r * RB, RB)
            acc = None
            for k in range(K):
                term = _shifted_rows(bufd, ds, K - 1 - k, base, RB) * w_ref[k:k + 1, :]
                acc = term if acc is None else acc + term
            dv0[pl.ds(base, RB), :] = acc
            return carry

        lax.fori_loop(0, tm // RB, dx_group, 0)

        for k in range(K):
            def dw_group(r, acc):
                base = pl.multiple_of(r * RB, RB)
                prod = bufd[pl.ds(base, RB), :] * _shifted_rows(bufx, xs, halo - (K - 1) + k, base, RB)
                for v in range(RB // 8):
                    acc = acc + prod[v * 8:(v + 1) * 8, :]
                return acc

            acc = lax.fori_loop(0, tm // RB, dw_group, jnp.zeros((8, D), F32))
            dw_ref[k:k + 1, :] += jnp.sum(acc, axis=0, keepdims=True)

        bufd[pl.ds(tm, halo), :] = bufd[pl.ds(0, halo), :]
        da, dg = pre_vjp(dv0[...])
        da_ref[...] = da.astype(da_ref.dtype)
        dg_ref[...] = dg.astype(dg_ref.dtype)

    slab = lambda sl: pl.BlockSpec((None, tm, D), lambda s: (sl, nt - 1 - s, 0))
    hslab = lambda sl: pl.BlockSpec((None, halo, D), lambda s: (sl, jnp.maximum((nt - 1 - s) * q - 1, 0), 0))
    row = pl.BlockSpec((tm, D), lambda s: (nt - 1 - s, 0))
    pars = [w, cb, lg, lb]
    res = _pcall(body, name=name, grid=(nt,),
                 in_specs=[slab(0), slab(1), hslab(0), hslab(1), row, row] + [_full_spec(p) for p in pars],
                 out_specs=[row, row] + [_full_spec(p) for p in pars],
                 out_shape=[_sds((T, D), BF16)] * 2 + [_sds(p.shape, F32) for p in pars],
                 scratch_shapes=[pltpu.VMEM((halo + tm, D), F32), pltpu.VMEM((7, halo + tm - 8, D), F32),
                                 pltpu.VMEM((tm + halo, D), F32), pltpu.VMEM((7, tm + halo - 8, D), F32),
                                 pltpu.VMEM((tm, D), F32)],
                 )(u3, u3, u3, u3, c, dv3, *pars)
    return res


def _dg(a, b, ca, cb):
    return lax.dot_general(a.astype(BF16), b.astype(BF16), (((ca,), (cb,)), ((), ())), preferred_element_type=F32)


@jax.custom_vjp
def _dot_nn(a, b):
    return _dg(a, b, 1, 0)


_dot_nn.defvjp(lambda a, b: (_dg(a, b, 1, 0), (a, b)),
               lambda res, g: (_dg(g, res[1], 1, 1), _dg(res[0], g, 0, 0)))


@jax.custom_vjp
def _dot_nt(a, b):
    return _dg(a, b, 1, 1)


_dot_nt.defvjp(lambda a, b: (_dg(a, b, 1, 1), (a, b)),
               lambda res, g: (_dg(g, res[1], 1, 0), _dg(g, res[0], 0, 0)))


@jax.custom_vjp
def _dot_tn(a, b):
    return _dg(a, b, 0, 0)


_dot_tn.defvjp(lambda a, b: (_dg(a, b, 0, 0), (a, b)),
               lambda res, g: (_dg(res[1], g, 1, 1), _dg(res[0], g, 1, 0)))


def _split3(v):
    hi = v.astype(BF16)
    r = v - hi.astype(F32)
    mid = r.astype(BF16)
    return hi, mid, (r - mid.astype(F32)).astype(BF16)


def _x01(v, m, cv, cm, m_left=False):
    acc = None
    for piece in _split3(v):
        t = _dg(m, piece, cm, cv) if m_left else _dg(piece, m, cv, cm)
        acc = t if acc is None else acc + t
    return acc


@jax.custom_vjp
def _expand01(v, m):
    return _x01(v, m, 1, 0)


_expand01.defvjp(lambda v, m: (_x01(v, m, 1, 0), m),
                 lambda m, g: (_x01(g, m, 1, 1), jnp.zeros_like(m)))


@jax.custom_vjp
def _mix01(m, v):
    return _x01(v, m, 0, 1, m_left=True)


_mix01.defvjp(lambda m, v: (_x01(v, m, 0, 1, m_left=True), m),
              lambda m, g: (jnp.zeros_like(m), _x01(g, m, 0, 0, m_left=True)))


def _ssd_group(xs, dtr, Bg, Cg, zg, sp, alog, dtb, dsk, nwg, *, g):
    li = lax.broadcasted_iota(jnp.int32, (L, L), 0)
    si = lax.broadcasted_iota(jnp.int32, (L, L), 1)
    causal = li >= si
    tri = causal.astype(F32)
    hi = lax.broadcasted_iota(jnp.int32, (HP, RP), 0)
    ci = lax.broadcasted_iota(jnp.int32, (HP, RP), 1)
    lo = (hi - g * R) * P
    E = ((ci >= lo) & (ci < lo + P)).astype(F32)

    dt = _softplus(dtr + dtb)
    a = dt * (-jnp.exp(alog))
    a_cs = _mix01(tri, a)
    acs_e = _expand01(a_cs, E)
    dt_e = _expand01(dt, E)
    alast_e = acs_e[L - 1:L, :]
    xdt = xs * dt_e
    a_csT = a_cs.T
    cb = _dot_nt(Cg, Bg)
    y_off = _dot_nn(Cg, sp) * jnp.exp(acs_e)
    yd = []
    for r in range(R):
        h = g * R + r
        seg = a_cs[:, h:h + 1] - a_csT[h:h + 1, :]
        dec = jnp.where(causal, jnp.exp(jnp.where(causal, seg, 0.0)), 0.0)
        yd.append(_dot_nn(cb * dec, xdt[:, r * P:(r + 1) * P]))
    y = jnp.concatenate(yd, axis=1) + y_off + xs * _expand01(jnp.broadcast_to(dsk, (8, HP)), E)[0:1, :]
    yg = y * _silu(zg)
    yn = yg * lax.rsqrt(jnp.mean(jnp.square(yg), axis=-1, keepdims=True) + RMS_EPS) * nwg
    sc = _dot_tn(Bg, xdt * jnp.exp(alast_e - acs_e))
    return yn, jnp.exp(alast_e) * sp + sc


def _group_cols(g):
    return g // 2, (g % 2) * RP


def _ssd_fwd(x0, x1, bc, dtr, u3, alog, dtb, dsk, nw, *, name):
    T = x0.shape[0]
    nc = T // L

    def body(x0_ref, x1_ref, bc_ref, dtr_ref, z0_ref, z1_ref, alog_ref, dtb_ref, dsk_ref, nw_ref, yn_ref, sp_ref, S):
        @pl.when(pl.program_id(0) == 0)
        def _():
            S[...] = jnp.zeros_like(S)

        xr, zr = (x0_ref, x1_ref), (z0_ref, z1_ref)
        for g in range(G):
            s, off = _group_cols(g)
            sp = S[g]
            sp_ref[0, g] = sp
            yn, s_next = _ssd_group(xr[s][:, off:off + RP], dtr_ref[...], bc_ref[:, g * N:(g + 1) * N],
                                    bc_ref[:, G * N + g * N:G * N + (g + 1) * N], zr[s][:, off:off + RP], sp,
                                    alog_ref[...], dtb_ref[...], dsk_ref[...], nw_ref[:, g * RP:(g + 1) * RP], g=g)
            yn_ref[:, g * RP:(g + 1) * RP] = yn.astype(yn_ref.dtype)
            S[g] = s_next

    row = lambda C: pl.BlockSpec((L, C), lambda c: (c, 0))
    zspec = lambda s: pl.BlockSpec((None, L, D), lambda c: (s, c, 0))
    pars = [alog, dtb, dsk, nw]
    return _pcall(body, name=name, grid=(nc,),
                  in_specs=[row(D), row(D), row(D), row(HP), zspec(2), zspec(3)] + [_full_spec(p) for p in pars],
                  out_specs=[row(SD), pl.BlockSpec((1, G, N, RP), lambda c: (c, 0, 0, 0))],
                  out_shape=[_sds((T, SD), BF16), _sds((nc, G, N, RP), F32)],
                  scratch_shapes=[pltpu.VMEM((G, N, RP), F32)])(x0, x1, bc, dtr, u3, u3, *pars)


def _ssd_bwd(x0, x1, bc, dtr, u3, sprev, dyn, alog, dtb, dsk, nw, *, name):
    T = x0.shape[0]
    nc = T // L

    def body(x0_ref, x1_ref, bc_ref, dtr_ref, z0_ref, z1_ref, sp_ref, dyn_ref, alog_ref, dtb_ref, dsk_ref, nw_ref,
             dx0_ref, dx1_ref, dbc_ref, ddtr_ref, dz0_ref, dz1_ref, dalog_ref, ddtb_ref, ddsk_ref, dnw_ref, dS):
        @pl.when(pl.program_id(0) == 0)
        def _():
            dS[...] = jnp.zeros_like(dS)
            for r in (dalog_ref, ddtb_ref, ddsk_ref, dnw_ref):
                r[...] = jnp.zeros_like(r)

        xr, zr = (x0_ref, x1_ref), (z0_ref, z1_ref)
        dxr, dzr = (dx0_ref, dx1_ref), (dz0_ref, dz1_ref)
        ddtr = jnp.zeros((L, HP), F32)
        for g in range(G):
            s, off = _group_cols(g)
            _, vjp = jax.vjp(functools.partial(_ssd_group, g=g), xr[s][:, off:off + RP], dtr_ref[...],
                             bc_ref[:, g * N:(g + 1) * N], bc_ref[:, G * N + g * N:G * N + (g + 1) * N],
                             zr[s][:, off:off + RP], sp_ref[0, g], alog_ref[...], dtb_ref[...], dsk_ref[...],
                             nw_ref[:, g * RP:(g + 1) * RP])
            dxs, ddt_g, dB, dC, dz, dsp, dal, ddb, dds, dnwg = vjp((dyn_ref[:, g * RP:(g + 1) * RP], dS[g]))
            dxr[s][:, off:off + RP] = dxs
            dzr[s][:, off:off + RP] = dz.astype(dz0_ref.dtype)
            dbc_ref[:, g * N:(g + 1) * N] = dB
            dbc_ref[:, G * N + g * N:G * N + (g + 1) * N] = dC
            dS[g] = dsp
            ddtr = ddtr + ddt_g
            dalog_ref[...] += dal
            ddtb_ref[...] += ddb
            ddsk_ref[...] += dds
            dnw_ref[:, g * RP:(g + 1) * RP] += dnwg
        ddtr_ref[...] = ddtr.astype(ddtr_ref.dtype)

    row = lambda C: pl.BlockSpec((L, C), lambda c: (nc - 1 - c, 0))
    zspec = lambda s: pl.BlockSpec((None, L, D), lambda c: (s, nc - 1 - c, 0))
    pars = [alog, dtb, dsk, nw]
    return _pcall(body, name=name, grid=(nc,),
                  in_specs=[row(D), row(D), row(D), row(HP), zspec(2), zspec(3),
                            pl.BlockSpec((1, G, N, RP), lambda c: (nc - 1 - c, 0, 0, 0)), row(SD)]
                  + [_full_spec(p) for p in pars],
                  out_specs=[row(D), row(D), row(D), row(HP), row(D), row(D)] + [_full_spec(p) for p in pars],
                  out_shape=[_sds((T, D), F32)] * 3 + [_sds((T, HP), BF16), _sds((T, D), BF16), _sds((T, D), BF16)]
                  + [_sds(p.shape, F32) for p in pars],
                  scratch_shapes=[pltpu.VMEM((G, N, RP), F32)])(x0, x1, bc, dtr, u3, u3, sprev, dyn, *pars)


def _loss_head(y, target, *, name):
    T = y.shape[0]
    tm = min(TM, T)

    def body(y_ref, t_ref, loss_ref, dy_ref):
        e = y_ref[...] - t_ref[...]
        dy_ref[...] = e * (1.0 / D)

        @pl.when(pl.program_id(0) == 0)
        def _():
            loss_ref[...] = jnp.zeros_like(loss_ref)

        loss_ref[...] += 0.5 * jnp.sum(jnp.mean(jnp.square(e), axis=-1, keepdims=True), axis=0, keepdims=True)

    row = pl.BlockSpec((tm, D), lambda i: (i, 0))
    return _pcall(body, name=name, grid=(T // tm,), in_specs=[row, row],
                  out_specs=[pl.BlockSpec((1, 128), lambda i: (0, 0)), row],
                  out_shape=[_sds((1, 128), F32), _sds((T, D), F32)])(y, target)


_HBM = pl.BlockSpec(memory_space=pltpu.HBM)
_MESH = pl.DeviceIdType.MESH


def _exchange_comm(bufs, *, scatter):
    nb = len(bufs)

    def copies(in_refs, out_refs, sems, with_arrivals):
        send_sems, recv_sems, local_sems = sems
        x, y, c = lax.axis_index("x"), lax.axis_index("y"), lax.axis_index("c")
        me = 2 * x + y
        peers = [(1 - x, y), (x, 1 - y), (1 - x, 1 - y)]
        own, sends, arrivals = [], [], []
        for b in range(nb):
            src_own = in_refs[b].at[me] if scatter else in_refs[b]
            own.append(pltpu.make_async_copy(src_own, out_refs[b].at[me], local_sems.at[b]))
            for k, (px, py) in enumerate(peers):
                src = in_refs[b].at[2 * px + py] if scatter else in_refs[b]
                sends.append(pltpu.make_async_remote_copy(
                    src_ref=src, dst_ref=out_refs[b].at[me], send_sem=send_sems.at[b, k], recv_sem=recv_sems.at[b, k],
                    device_id=(px, py, c), device_id_type=_MESH))
                if with_arrivals:
                    slot = out_refs[b].at[2 * px + py]
                    arrivals.append(pltpu.make_async_remote_copy(
                        src_ref=slot, dst_ref=slot, send_sem=send_sems.at[b, k], recv_sem=recv_sems.at[b, k],
                        device_id=(px, py, c), device_id_type=_MESH))
        return own, sends, arrivals

    def start(in_refs, out_refs, sems):
        own, sends, _ = copies(in_refs, out_refs, sems, False)
        for cp in own + sends:
            cp.start()

    def wait(in_refs, out_refs, sems):
        own, sends, arrivals = copies(in_refs, out_refs, sems, True)
        for cp in arrivals:
            cp.wait_recv()
        for cp in sends:
            cp.wait_send()
        for cp in own:
            cp.wait()

    return dict(ins=list(bufs), outs=[_sds(b.shape if scatter else (NCHIP,) + b.shape, b.dtype) for b in bufs],
                sems=[pltpu.SemaphoreType.DMA((nb, 3)), pltpu.SemaphoreType.DMA((nb, 3)), pltpu.SemaphoreType.DMA((nb,))],
                start=start, wait=wait)


def _chip_exchange(bufs, *, scatter, name):
    comm = _exchange_comm(bufs, scatter=scatter)
    nb = len(bufs)

    def body(*refs):
        comm["start"](refs[:nb], refs[nb:2 * nb], refs[2 * nb:])
        comm["wait"](refs[:nb], refs[nb:2 * nb], refs[2 * nb:])

    return _ccall(body, name=name, in_specs=[_HBM] * nb, out_specs=[_HBM] * nb, out_shape=comm["outs"],
                  scratch_shapes=comm["sems"])(*bufs)


def _core_swap(bufs, *, name):
    nb = len(bufs)

    def body(*refs):
        in_refs, out_refs, send_sems, recv_sems = refs[:nb], refs[nb:2 * nb], refs[2 * nb], refs[2 * nb + 1]
        x, y, c = lax.axis_index("x"), lax.axis_index("y"), lax.axis_index("c")
        cps = [pltpu.make_async_remote_copy(src_ref=in_refs[b], dst_ref=out_refs[b], send_sem=send_sems.at[b],
                                            recv_sem=recv_sems.at[b], device_id=(x, y, 1 - c), device_id_type=_MESH)
               for b in range(nb)]
        for cp in cps:
            cp.start()
        for cp in cps:
            cp.wait()

    return _ccall(body, name=name, in_specs=[_HBM] * nb, out_specs=[_HBM] * nb,
                  out_shape=[_sds(b.shape, b.dtype) for b in bufs],
                  scratch_shapes=[pltpu.SemaphoreType.DMA((nb,)), pltpu.SemaphoreType.DMA((nb,))])(*bufs)


def _all_gather8(buf, *, name):
    def body(in_ref, out_ref, send_sems, recv_sems, local_sem):
        x, y, c = lax.axis_index("x"), lax.axis_index("y"), lax.axis_index("c")
        me = 4 * x + 2 * y + c
        own = pltpu.make_async_copy(in_ref, out_ref.at[me], local_sem)
        own.start()
        flips = [(fx, fy, fc) for fx in (0, 1) for fy in (0, 1) for fc in (0, 1)][1:]
        peers = [(x ^ fx, y ^ fy, c ^ fc) for fx, fy, fc in flips]
        sends = []
        for k, peer in enumerate(peers):
            cp = pltpu.make_async_remote_copy(src_ref=in_ref, dst_ref=out_ref.at[me], send_sem=send_sems.at[k],
                                              recv_sem=recv_sems.at[k], device_id=peer, device_id_type=_MESH)
            cp.start()
            sends.append(cp)
        for k, (px, py, pc) in enumerate(peers):
            slot = out_ref.at[4 * px + 2 * py + pc]
            pltpu.make_async_remote_copy(src_ref=slot, dst_ref=slot, send_sem=send_sems.at[k], recv_sem=recv_sems.at[k],
                                         device_id=(px, py, pc), device_id_type=_MESH).wait_recv()
        for cp in sends:
            cp.wait_send()
        own.wait()

    return _ccall(body, name=name, in_specs=[_HBM], out_specs=_HBM, out_shape=_sds((8,) + buf.shape, buf.dtype),
                  scratch_shapes=[pltpu.SemaphoreType.DMA((7,)), pltpu.SemaphoreType.DMA((7,)), pltpu.SemaphoreType.DMA])(buf)


def _row_tile(rows, cap):
    if rows <= cap:
        return rows
    return max(t for t in range(16, cap + 1, 16) if rows % t == 0)


def _sum_slots(stack, *, name, cap=256):
    S, Rr, C = stack.shape
    tr = _row_tile(Rr, cap)

    def body(s_ref, o_ref):
        acc = s_ref[0].astype(F32)
        for j in range(1, S):
            acc = acc + s_ref[j].astype(F32)
        o_ref[...] = acc

    return _pcall(body, name=name, grid=(Rr // tr,), in_specs=[pl.BlockSpec((S, tr, C), lambda i: (0, i, 0))],
                  out_specs=pl.BlockSpec((tr, C), lambda i: (i, 0)), out_shape=_sds((Rr, C), F32))(stack)


def _adamw(g_parts, w, m, v, *, name, cap=128):
    Rr, C = w.shape
    tr = _row_tile(Rr, cap)
    ng = len(g_parts)
    c1 = 1.0 / (1.0 - ADAM_B1 ** ADAM_STEP)
    c2 = 1.0 / (1.0 - ADAM_B2 ** ADAM_STEP)

    def body(*refs):
        g = refs[0][...]
        for r in refs[1:ng]:
            g = g + r[...]
        w_ref, m_ref, v_ref, g_out, d_out, m_out, v_out = refs[ng:]
        mn = ADAM_B1 * m_ref[...] + (1.0 - ADAM_B1) * g
        vn = ADAM_B2 * v_ref[...] + (1.0 - ADAM_B2) * jnp.square(g)
        g_out[...] = g
        m_out[...] = mn
        v_out[...] = vn
        d_out[...] = -ADAM_LR * ((mn * c1) / (jnp.sqrt(vn * c2) + ADAM_EPS) + ADAM_WD * w_ref[...])

    spec = pl.BlockSpec((tr, C), lambda i: (i, 0))
    return _pcall(body, name=name, grid=(Rr // tr,), in_specs=[spec] * (ng + 3), out_specs=[spec] * 4,
                  out_shape=[_sds((Rr, C), F32)] * 4)(*g_parts, w, m, v)


def _adamw_layers(mine, other, w3, m3, v3, *, name, cap=128):
    _, Rr, C = w3.shape
    tr = _row_tile(Rr, cap)
    nt = Rr // tr
    c1 = 1.0 / (1.0 - ADAM_B1 ** ADAM_STEP)
    c2 = 1.0 / (1.0 - ADAM_B2 ** ADAM_STEP)

    def body(m0, m1, o0, o1, w_ref, m_ref, v_ref, g_out, d_out, m_out, v_out):
        g = jnp.where(pl.program_id(0) == 0, m0[...] + o0[...], m1[...] + o1[...])
        mn = ADAM_B1 * m_ref[...] + (1.0 - ADAM_B1) * g
        vn = ADAM_B2 * v_ref[...] + (1.0 - ADAM_B2) * jnp.square(g)
        g_out[...] = g
        m_out[...] = mn
        v_out[...] = vn
        d_out[...] = -ADAM_LR * ((mn * c1) / (jnp.sqrt(vn * c2) + ADAM_EPS) + ADAM_WD * w_ref[...])

    g0 = pl.BlockSpec((tr, C), lambda l, i: (jnp.where(l == 0, i, nt - 1), 0))
    g1 = pl.BlockSpec((tr, C), lambda l, i: (jnp.where(l == 1, i, 0), 0))
    s3 = pl.BlockSpec((None, tr, C), lambda l, i: (l, i, 0))
    return _pcall(body, name=name, grid=(2, nt), in_specs=[g0, g1, g0, g1, s3, s3, s3], out_specs=[s3] * 4,
                  out_shape=[_sds(w3.shape, F32)] * 4)(mine[0], mine[1], other[0], other[1], w3, m3, v3)


def _pack(arrs, dtype, row_mult):
    flat = jnp.concatenate([a.reshape(-1).astype(dtype) for a in arrs])
    n = flat.shape[0]
    unit = row_mult * PACK_W
    total = unit * ((n + unit - 1) // unit)
    if total > n:
        flat = jnp.concatenate([flat, jnp.zeros((total - n,), dtype)])
    return flat.reshape(-1, PACK_W)


def _unpack(buf, shapes):
    flat = buf.reshape(-1)
    out, off = [], 0
    for s in shapes:
        n = math.prod(s)
        out.append(flat[off:off + n].reshape(s))
        off += n
    return out


def _conf_pre(a, g):
    return [a * jax.nn.sigmoid(g)]


def _conf_post(cs, cb, lg, lb):
    return (_silu(_layer_norm(cs[0] + cb, lg, lb)),)


def _xbc_pre(x0, x1, x2):
    return [x0, x1, x2]


def _xbc_post(cs, b0, b1, b2):
    return (_silu(cs[0] + b0), _silu(cs[1] + b1), _silu(cs[2] + b2))


def _ffn_pre(gate, val):
    return [gate, val]


def _ffn_post(cs, bg, bv):
    return (_silu(cs[0] + bg) * (cs[1] + bv),)


def _mix_fn(ga, gb, ya, yb):
    return (jax.nn.sigmoid(ga) * ya + jax.nn.sigmoid(gb) * yb,)


def _res_ln_fn(h, r, g, b):
    return (_layer_norm(ALPHA * h + r, g, b),)


def _ln_fn(x, g, b):
    return (_layer_norm(x, g, b),)


def _carrying(carry, key, gr, call):
    if key not in carry:
        return call(None)
    comm, done = carry[key](gr)
    out, got = call(comm)
    done(got)
    return out


def _layer_fwd(h, hb, W, l, carry):
    T = h.shape[0]
    tm = min(TM, T)
    nt = T // tm
    tmf = min(TM_FFN, T)
    ntf = T // tmf
    nm = lambda s: f"l{l}_{s}"
    u3 = _carrying(carry, "u", None, lambda comm: _mm_nn_slab_out(hb, W["w_p"], name=nm("u"), width=D, comm=comm))
    dtr = _mm_nn(hb, W["w_dt"], name=nm("dt"))
    v3, cconv = _conf_fwd(u3, W["conv_w"], W["conv_b"], W["conv_ln_g"], W["conv_ln_b"], name=nm("conf"))
    ya = _mm_nn(v3, W["w_co"], name=nm("ya"))
    tmx = min(TM_X, T)
    x0, x1, bc = _conv_fwd(_xbc_pre, _xbc_post, [_slab(u3, 6, tmx), _slab(u3, 7, tmx), _slab(u3, 8, tmx)], W["ssm_w"],
                           W["ssm_b"], [_o2(T, D, F32, tmx)] * 3, K=SSM_K, C=D, name=nm("xbc"), tm=tmx, nt=T // tmx)
    yn, sprev = _ssd_fwd(x0, x1, bc, dtr, u3, W["a_log"], W["dt_bias"], W["d_skip"], W["norm_w"], name=nm("ssd"))
    yb = _mm_nn(yn, W["w_so"], name=nm("yb"), tk=min(SD, 1024))
    (m,) = _rows_fwd(_mix_fn, [_slab(u3, 4, tm), _slab(u3, 5, tm), _r2(ya, tm), _r2(yb, tm)], [],
                     [_o2(T, D, BF16, tm)], name=nm("mix"), nt=nt)
    mix = _mm_nn(m, W["w_o"], name=nm("wo"))
    h1, h1b = _rows_fwd(lambda a, r, g, b: _res_ln_fn(a, r, g, b) * 2, [_r2(h, tm), _r2(mix, tm)],
                        [W["ln1_g"], W["ln1_b"]], [_o2(T, D, F32, tm), _o2(T, D, BF16, tm)], name=nm("ln1"), nt=nt)
    up3 = _carrying(carry, "up", None, lambda comm: _mm_nn_slab_out(h1b, W["w_up"], name=nm("up"), width=FFN, comm=comm))
    (f,) = _conv_fwd(_ffn_pre, _ffn_post, [_slab(up3, 0, tmf), _slab(up3, 1, tmf)], W["ffn_w"], W["ffn_b"],
                     [_o2(T, FFN, BF16, tmf)], K=FFN_K, C=FFN, name=nm("ffnact"), tm=tmf, nt=ntf)
    ffn = _mm_nn(f, W["w_dn"], name=nm("dn"))
    h2, h2b = _rows_fwd(lambda a, r, g, b: _res_ln_fn(a, r, g, b) * 2, [_r2(h1, tm), _r2(ffn, tm)],
                        [W["ln2_g"], W["ln2_b"]], [_o2(T, D, F32, tm), _o2(T, D, BF16, tm)], name=nm("ln2"), nt=nt)
    saved = dict(h=h, hb=hb, u3=u3, dtr=dtr, v3=v3, cconv=cconv, ya=ya, x0=x0, x1=x1, bc=bc, sprev=sprev, yn=yn, yb=yb, m=m,
                 mix=mix, h1=h1, h1b=h1b, up3=up3, f=f, ffn=ffn)
    return h2, h2b, saved


def _layer_bwd(dh2, W, sv, l, carry):
    T = dh2.shape[0]
    tm = min(TM, T)
    nt = T // tm
    tmf = min(TM_FFN, T)
    ntf = T // tmf
    nm = lambda s: f"l{l}_{s}"
    gr = {}
    (dres2, dffn), (gr["ln2_g"], gr["ln2_b"]) = _rows_bwd(
        _res_ln_fn, [_r2(sv["h1"], tm), _r2(sv["ffn"], tm)], [W["ln2_g"], W["ln2_b"]], [_r2(dh2, tm)],
        [_o2(T, D, F32, tm), _o2(T, D, BF16, tm)], name=nm("ln2_b"), nt=nt)
    df = _mm_nt(dffn, W["w_dn"], name=nm("dn_dx"))
    gr["w_dn"] = _carrying(carry, "dn_dw", gr,
                           lambda comm: _mm_tn(sv["f"], dffn, name=nm("dn_dw"), tmo=FFN // 2, comm=comm))
    up3 = sv["up3"]
    h8 = _halo_of(FFN_K)
    (dgate, dval), gr["ffn_w"], gr["ffn_b"] = _conv_bwd(
        _ffn_pre, _ffn_post, [_slab(up3, 0, tmf), _slab(up3, 1, tmf)],
        [_halo_slab(up3, 0, tmf, h8), _halo_slab(up3, 1, tmf, h8)], W["ffn_w"], W["ffn_b"], [_r2(df, tmf)],
        [_o2(T, FFN, BF16, tmf)] * 2, K=FFN_K, C=FFN, name=nm("ffnact_b"), tm=tmf, nt=ntf)
    dup3 = jnp.stack([dgate, dval])
    dh1 = _carrying(carry, "up_dx", gr,
                    lambda comm: _mm_slab_in_nt(dup3, W["w_up"], name=nm("up_dx"), add=dres2, comm=comm))
    gr["w_up"] = _carrying(carry, "up_dw", gr,
                           lambda comm: _mm_tn_slab(sv["h1b"], dup3, name=nm("up_dw"), tmo=min(512, D), comm=comm))
    (dres1, dmix), (gr["ln1_g"], gr["ln1_b"]) = _rows_bwd(
        _res_ln_fn, [_r2(sv["h"], tm), _r2(sv["mix"], tm)], [W["ln1_g"], W["ln1_b"]], [_r2(dh1, tm)],
        [_o2(T, D, F32, tm), _o2(T, D, BF16, tm)], name=nm("ln1_b"), nt=nt)
    dm = _mm_nt(dmix, W["w_o"], name=nm("wo_dx"))
    gr["w_o"] = _mm_tn(sv["m"], dmix, name=nm("wo_dw"), tmo=min(512, D))
    u3 = sv["u3"]
    (dga, dgb, dya, dyb), _ = _rows_bwd(
        _mix_fn, [_slab(u3, 4, tm), _slab(u3, 5, tm), _r2(sv["ya"], tm), _r2(sv["yb"], tm)], [], [_r2(dm, tm)],
        [_o2(T, D, BF16, tm)] * 4, name=nm("mix_b"), nt=nt)
    dv3 = _mm_nt(dya, W["w_co"], name=nm("ya_dx"))
    gr["w_co"] = _mm_tn(sv["v3"], dya, name=nm("ya_dw"), tmo=min(512, D))
    da, dg, gr["conv_w"], gr["conv_b"], gr["conv_ln_g"], gr["conv_ln_b"] = _conf_bwd(
        u3, sv["cconv"], dv3, W["conv_w"], W["conv_b"], W["conv_ln_g"], W["conv_ln_b"], name=nm("conf_b"))
    dyn = _mm_nt(dyb, W["w_so"], name=nm("yb_dx"))
    gr["w_so"] = _mm_tn(sv["yn"], dyb, name=nm("yb_dw"), tmo=min(512, SD))
    (dx0, dx1, dbc, ddtr, dz0, dz1, gr["a_log"], gr["dt_bias"], gr["d_skip"], gr["norm_w"]) = _ssd_bwd(
        sv["x0"], sv["x1"], sv["bc"], sv["dtr"], u3, sv["sprev"], dyn, W["a_log"], W["dt_bias"], W["d_skip"],
        W["norm_w"], name=nm("ssd_b"))
    h8s = _halo_of(SSM_K)
    tmx = min(TM_X, T)
    (du6, du7, du8), gr["ssm_w"], gr["ssm_b"] = _conv_bwd(
        _xbc_pre, _xbc_post, [_slab(u3, 6, tmx), _slab(u3, 7, tmx), _slab(u3, 8, tmx)],
        [_halo_slab(u3, 6, tmx, h8s), _halo_slab(u3, 7, tmx, h8s), _halo_slab(u3, 8, tmx, h8s)], W["ssm_w"], W["ssm_b"],
        [_r2(dx0, tmx), _r2(dx1, tmx), _r2(dbc, tmx)], [_o2(T, D, BF16, tmx)] * 3, K=SSM_K, C=D, name=nm("xbc_b"),
        tm=tmx, nt=T // tmx)
    du3 = jnp.stack([da, dg, dz0, dz1, dga, dgb, du6, du7, du8])
    dh_a = _mm_nt(ddtr, W["w_dt"], name=nm("dt_dx"), add=dres1)
    dh = _carrying(carry, "u_dx", gr, lambda comm: _mm_slab_in_nt(du3, W["w_p"], name=nm("u_dx"), add=dh_a, comm=comm))
    gr["w_p"] = _mm_tn_slab(sv["hb"], du3, name=nm("u_dw"), tmo=min(512, D))
    gr["w_dt"] = _mm_tn(sv["hb"], ddtr, name=nm("dt_dw"), tmo=min(512, D))
    return dh, gr


_U_SPLIT = (2 * D + SD, 2 * D + SD + XBC, 2 * D + SD + XBC + H)


def _pad_rows(a, rows):
    return jnp.concatenate([a, jnp.zeros((rows - a.shape[0],) + a.shape[1:], a.dtype)], axis=0)


def _pad_lanes(a, lanes):
    return jnp.concatenate([a, jnp.zeros(a.shape[:-1] + (lanes - a.shape[-1],), a.dtype)], axis=-1)


def _w_in_layout(w_in):
    e0, e1, e2 = _U_SPLIT
    return dict(w_p=jnp.concatenate([w_in[:, :e0], w_in[:, e2:], w_in[:, e0:e1]], axis=1),
                w_dt=_pad_lanes(w_in[:, e1:e2], HP))


_MM_KEY = dict(w_conv_out="w_co", w_ssm_out="w_so", w_o="w_o", w_ffn_up="w_up", w_ffn_down="w_dn")


def _small_layer_weights(full, l):
    row = lambda a: a.reshape(1, -1)
    ssm_w = full["ssm_conv_w"][l]
    ffn_w = full["ffn_dw_w"][l]
    ssm_b = full["ssm_conv_b"][l]
    ffn_b = full["ffn_dw_b"][l]
    W = dict(
        conv_w=_pad_rows(full["conv_dw_w"][l], 32),
        conv_b=row(full["conv_dw_b"][l]), conv_ln_g=row(full["conv_ln_g"][l]), conv_ln_b=row(full["conv_ln_b"][l]),
        ssm_w=jnp.stack([_pad_rows(ssm_w[:, p * D:(p + 1) * D], 8) for p in range(3)]),
        ssm_b=[row(ssm_b[p * D:(p + 1) * D]) for p in range(3)],
        a_log=_pad_lanes(row(full["ssm_a_log"][l]), HP), dt_bias=_pad_lanes(row(full["ssm_dt_bias"][l]), HP),
        d_skip=_pad_lanes(row(full["ssm_d"][l]), HP), norm_w=row(full["ssm_norm_w"][l]),
        ln1_g=row(full["ln1_g"][l]), ln1_b=row(full["ln1_b"][l]),
        ffn_w=jnp.stack([_pad_rows(ffn_w[:, p * FFN:(p + 1) * FFN], 8) for p in range(2)]),
        ffn_b=[row(ffn_b[p * FFN:(p + 1) * FFN]) for p in range(2)],
        ln2_g=row(full["ln2_g"][l]), ln2_b=row(full["ln2_b"][l]),
    )
    return W


def _layer_grads_to_reference_layout(gr):
    e0, e1, e2 = _U_SPLIT
    nx = XBC
    wp = gr["w_p"]
    w_in = jnp.concatenate([wp[:, :e0], wp[:, e0 + 2 * D:e0 + 2 * D + nx], gr["w_dt"][:, :H], wp[:, e0:e0 + 2 * D]], axis=1)
    return dict(
        w_in=w_in, conv_dw_w=gr["conv_w"][:CONV_K], conv_dw_b=gr["conv_b"][0], conv_ln_g=gr["conv_ln_g"][0],
        conv_ln_b=gr["conv_ln_b"][0], w_conv_out=gr["w_co"],
        ssm_conv_w=jnp.concatenate([gr["ssm_w"][p, :SSM_K] for p in range(3)], axis=1),
        ssm_conv_b=jnp.concatenate([b[0] for b in gr["ssm_b"]]),
        ssm_dt_bias=gr["dt_bias"][0, :H], ssm_a_log=gr["a_log"][0, :H], ssm_d=gr["d_skip"][0, :H],
        ssm_norm_w=gr["norm_w"][0], w_ssm_out=gr["w_so"], w_o=gr["w_o"], ln1_g=gr["ln1_g"][0], ln1_b=gr["ln1_b"][0],
        w_ffn_up=gr["w_up"], ffn_dw_w=jnp.concatenate([gr["ffn_w"][p, :FFN_K] for p in range(2)], axis=1),
        ffn_dw_b=jnp.concatenate([b[0] for b in gr["ffn_b"]]), w_ffn_down=gr["w_dn"], ln2_g=gr["ln2_g"][0],
        ln2_b=gr["ln2_b"][0],
    )


_BIG = dict(w_in=2, w_conv_out=1, w_ssm_out=1, w_o=1, w_ffn_up=2, w_ffn_down=1)
_SMALL_SHARDED = dict(conv_dw_w=2, ssm_conv_w=2, ffn_dw_w=2)
_REPLICATED = ("ln_in_g", "ln_in_b", "conv_dw_b", "conv_ln_g", "conv_ln_b", "ssm_conv_b", "ssm_dt_bias", "ssm_a_log",
               "ssm_d", "ssm_norm_w", "ln1_g", "ln1_b", "ffn_dw_b", "ln2_g", "ln2_b")
_WEIGHTS = ("ln_in_g", "ln_in_b", "w_in", "conv_dw_w", "conv_dw_b", "conv_ln_g", "conv_ln_b", "w_conv_out", "ssm_conv_w",
            "ssm_conv_b", "ssm_dt_bias", "ssm_a_log", "ssm_d", "ssm_norm_w", "w_ssm_out", "w_o", "ln1_g", "ln1_b",
            "w_ffn_up", "ffn_dw_w", "ffn_dw_b", "w_ffn_down", "ln2_g", "ln2_b")


def _split_chips(a, axis):
    rows, cols = a.shape
    if axis == 0:
        return a.reshape(NCHIP, rows // NCHIP, cols)
    return a.reshape(rows, NCHIP, cols // NCHIP).transpose(1, 0, 2)


def kernel(x, ln_in_g, ln_in_b, w_in, conv_dw_w, conv_dw_b, conv_ln_g, conv_ln_b, w_conv_out, ssm_conv_w, ssm_conv_b, ssm_dt_bias, ssm_a_log, ssm_d, ssm_norm_w, w_ssm_out, w_o, ln1_g, ln1_b, w_ffn_up, ffn_dw_w, ffn_dw_b, w_ffn_down, ln2_g, ln2_b, loss_target, m_ln_in_g, m_ln_in_b, m_w_in, m_conv_dw_w, m_conv_dw_b, m_conv_ln_g, m_conv_ln_b, m_w_conv_out, m_ssm_conv_w, m_ssm_conv_b, m_ssm_dt_bias, m_ssm_a_log, m_ssm_d, m_ssm_norm_w, m_w_ssm_out, m_w_o, m_ln1_g, m_ln1_b, m_w_ffn_up, m_ffn_dw_w, m_ffn_dw_b, m_w_ffn_down, m_ln2_g, m_ln2_b, v_ln_in_g, v_ln_in_b, v_w_in, v_conv_dw_w, v_conv_dw_b, v_conv_ln_g, v_conv_ln_b, v_w_conv_out, v_ssm_conv_w, v_ssm_conv_b, v_ssm_dt_bias, v_ssm_a_log, v_ssm_d, v_ssm_norm_w, v_w_ssm_out, v_w_o, v_ln1_g, v_ln1_b, v_w_ffn_up, v_ffn_dw_w, v_ffn_dw_b, v_w_ffn_down, v_ln2_g, v_ln2_b):
    args = locals()
    w = {n: args[n] for n in _WEIGHTS}
    mom = {n: args["m_" + n] for n in _WEIGHTS}
    vel = {n: args["v_" + n] for n in _WEIGHTS}
    T = x.shape[1]
    tm = min(TM, T)
    nt = T // tm
    chip = 2 * lax.axis_index("x") + lax.axis_index("y")

    assert DEPTH == 2
    big_names, small_names = list(_BIG), list(_SMALL_SHARDED)
    rest_big = [n for n in big_names if n != "w_in"]
    bf = {n: w[n].astype(BF16) for n in big_names}
    join = lambda got, axis: jnp.concatenate([got[j] for j in range(NCHIP)], axis=axis)
    first = _chip_exchange([bf["w_in"][0]] + [w[n] for n in small_names], scatter=False, name="gather_first")
    full = {n: join(gk, _SMALL_SHARDED[n]) for n, gk in zip(small_names, first[1:])}
    for n in _REPLICATED:
        full[n] = w[n]
    Ws = [_small_layer_weights(full, l) for l in range(DEPTH)]
    Ws[0].update(_w_in_layout(join(first[0], 1)))

    def rest_arrived(l):
        def done(got):
            for n, gk in zip(rest_big, got):
                Ws[l][_MM_KEY[n]] = join(gk, _BIG[n] - 1)
        return done

    carry_fwd = [
        {"u": lambda gr: (_exchange_comm([bf[n][0] for n in rest_big], scatter=False), rest_arrived(0)),
         "up": lambda gr: (_exchange_comm([bf["w_in"][1]], scatter=False),
                           lambda got: Ws[1].update(_w_in_layout(join(got[0], 1))))},
        {"u": lambda gr: (_exchange_comm([bf[n][1] for n in rest_big], scatter=False), rest_arrived(1))},
    ]

    x2 = x.reshape(T, D)
    g_in, b_in = ln_in_g.reshape(1, D), ln_in_b.reshape(1, D)
    h, hb = _rows_fwd(lambda a, g, b: _ln_fn(a, g, b) * 2, [_r2(x2, tm)], [g_in, b_in],
                      [_o2(T, D, F32, tm), _o2(T, D, BF16, tm)], name="ln_in", nt=nt)
    saved = []
    for l in range(DEPTH):
        h, hb, sv = _layer_fwd(h, hb, Ws[l], l, carry_fwd[l])
        saved.append(sv)
    loss_row, dh = _loss_head(h, loss_target.reshape(T, D), name="loss")

    arrived = {}

    def exchange(names, l, grads):
        def make(gr):
            src = grads(gr)
            def done(got):
                for n, gk in zip(names, got):
                    arrived[(n, l)] = gk
            return _exchange_comm([_split_chips(src[n], _BIG[n] - 1) for n in names], scatter=True), done
        return make

    layer_grads = [None] * DEPTH
    dh, gr = _layer_bwd(dh, Ws[1], saved[1], 1, {})
    layer_grads[1] = _layer_grads_to_reference_layout(gr)
    g1 = lambda gr: layer_grads[1]
    g0 = lambda gr: {n: gr[_MM_KEY[n]] for n in rest_big}
    dh, gr = _layer_bwd(dh, Ws[0], saved[0], 0, {
        "dn_dw": exchange(["w_conv_out", "w_ssm_out", "w_o"], 1, g1),
        "up_dx": exchange(["w_in"], 1, g1),
        "up_dw": exchange(["w_ffn_up", "w_ffn_down"], 1, g1),
        "u_dx": exchange(rest_big, 0, g0)})
    layer_grads[0] = _layer_grads_to_reference_layout(gr)
    (arrived[("w_in", 0)],) = _chip_exchange([_split_chips(layer_grads[0]["w_in"], 1)], scatter=True, name="exchange_last")
    (grad_x2,), (d_g_in, d_b_in) = _rows_bwd(_ln_fn, [_r2(x2, tm)], [g_in, b_in], [_r2(dh, tm)], [_o2(T, D, F32, tm)],
                                             name="ln_in_b", nt=nt)
    local = {n: jnp.stack([layer_grads[l][n] for l in range(DEPTH)]) for n in _WEIGHTS[2:] if n not in _BIG}
    local["ln_in_g"], local["ln_in_b"] = d_g_in[0], d_b_in[0]
    res = [{}, {}, {}, {}]

    keys = [(n, l) for n in big_names for l in range(DEPTH)]
    mine = [_sum_slots(arrived[k], name=f"sum_chips_{k[0]}_{k[1]}") for k in keys]
    other = _core_swap(mine, name="swap_cores")
    for i, n in enumerate(big_names):
        outs = _adamw_layers(mine[2 * i:2 * i + 2], other[2 * i:2 * i + 2], w[n], mom[n], vel[n], name="adamw_" + n)
        for q in range(4):
            res[q][n] = outs[q]

    rest_names = list(_REPLICATED) + small_names
    part = _pack([loss_row] + [local[n] for n in rest_names], F32, 8)
    parts = _all_gather8(part, name="gather_small")
    total = _sum_slots(parts, name="sum_devices")
    tot = _unpack(total, [loss_row.shape] + [local[n].shape for n in rest_names])
    loss = tot[0][0, 0]
    g_rest = {}
    for n, t in zip(rest_names, tot[1:]):
        if n in _SMALL_SHARDED:
            ax = _SMALL_SHARDED[n]
            t = lax.dynamic_slice_in_dim(t, chip * w[n].shape[ax], w[n].shape[ax], axis=ax)
        g_rest[n] = t
    pk = lambda d: _pack([d[n] for n in rest_names], F32, 8)
    rest_out = _adamw([pk(g_rest)], pk(w), pk(mom), pk(vel), name="adamw_rest")
    rest_out = [_unpack(o, [w[n].shape for n in rest_names]) for o in rest_out]

    for q in range(4):
        for k, n in enumerate(rest_names):
            res[q][n] = rest_out[q][k]
    grad_x = grad_x2.reshape(x.shape)
    return (loss, grad_x, *[res[0][n] for n in _WEIGHTS], *[res[1][n] for n in _WEIGHTS],
            *[res[2][n] for n in _WEIGHTS], *[res[3][n] for n in _WEIGHTS])
```

```python
import functools
import math

import jax
import jax.numpy as jnp
from jax import lax
from jax.experimental import pallas as pl
from jax.experimental.pallas import tpu as pltpu

F32 = jnp.float32
BF16 = jnp.bfloat16
HI = lax.Precision.HIGHEST

D = 1024
DEPTH = 2
CONV_K = 31
SD = 2 * D
P = 64
H = SD // P
G = 4
R = H // G
N = 128
RP = R * P
SSM_K = 4
L = 128
XBC = SD + 2 * G * N
FFN = 2816
FFN_K = 3
IN_DIM = 2 * D + SD + XBC + H + 2 * D
ALPHA = (2 * DEPTH) ** 0.25
LN_EPS = 1e-5
RMS_EPS = 1e-5
ADAM_LR, ADAM_B1, ADAM_B2, ADAM_EPS, ADAM_WD, ADAM_STEP = 0.001, 0.9, 0.999, 1e-08, 0.01, 10

HP = 128
NCHIP = 4
PACK_W = 1024
VMEM_LIMIT = 56 * 1024 * 1024
TM = 512
TM_WIDE = 1024
TM_X = 256
TM_FFN = 256
TK = 1024
EW_ROWS = 512

assert D == 2 * RP and 2 * G * N == D and XBC == 3 * D and H <= HP


def _pcall(body, *, name, grid=(), in_specs, out_specs, out_shape, scratch_shapes=()):
    params = pltpu.CompilerParams(vmem_limit_bytes=VMEM_LIMIT, dimension_semantics=("arbitrary",) * len(grid))
    return pl.pallas_call(body, name=name, grid=grid, in_specs=in_specs, out_specs=out_specs, out_shape=out_shape,
                          scratch_shapes=list(scratch_shapes), compiler_params=params)


def _pcall_carrying(body, comm, *, name, grid, in_specs, out_specs, out_shape, scratch_shapes=()):
    in_specs, out_specs, out_shape = list(in_specs), list(out_specs), list(out_shape)
    scratch_shapes = list(scratch_shapes)
    n_in, n_out, n_scr = len(in_specs), len(out_specs), len(scratch_shapes)
    nci, nco = len(comm["ins"]), len(comm["outs"])

    def wrapped(*refs):
        ins, cin = refs[:n_in], refs[n_in:n_in + nci]
        outs = refs[n_in + nci:n_in + nci + n_out]
        cout = refs[n_in + nci + n_out:n_in + nci + n_out + nco]
        scr = refs[n_in + nci + n_out + nco:n_in + nci + n_out + nco + n_scr]
        csem = refs[n_in + nci + n_out + nco + n_scr:]
        ids = [pl.program_id(ax) for ax in range(len(grid))]
        first = functools.reduce(jnp.logical_and, [i == 0 for i in ids])
        last = functools.reduce(jnp.logical_and, [i == g - 1 for i, g in zip(ids, grid)])

        @pl.when(first)
        def _():
            comm["start"](cin, cout, csem)

        body(*ins, *outs, *scr)

        @pl.when(last)
        def _():
            comm["wait"](cin, cout, csem)

    call = _pcall(wrapped, name=name, grid=grid, in_specs=in_specs + [_HBM] * nci, out_specs=out_specs + [_HBM] * nco,
                  out_shape=out_shape + list(comm["outs"]), scratch_shapes=scratch_shapes + list(comm["sems"]))

    def run(*operands):
        res = call(*operands, *comm["ins"])
        return list(res[:n_out]), list(res[n_out:])

    return run


def _ccall(body, *, name, in_specs, out_specs, out_shape, scratch_shapes):
    return pl.pallas_call(body, name=name, in_specs=in_specs, out_specs=out_specs, out_shape=out_shape,
                          scratch_shapes=list(scratch_shapes))


def _full_spec(a):
    nd = a.ndim
    return pl.BlockSpec(a.shape, lambda *_: (0,) * nd)


def _sds(shape, dtype):
    return jax.ShapeDtypeStruct(tuple(shape), dtype)


def _mm(a, b, *, name, grid, a_spec, b_spec, o_spec, out_shape, acc_shape, trans_a=False, trans_b=False, add=None,
        add_spec=None, comm=None):
    nk = grid[2]
    dn = (((0 if trans_a else 1,), (1 if trans_b else 0,)), ((), ()))
    has_add = add is not None

    def body(*refs):
        a_ref, b_ref = refs[0], refs[1]
        add_ref = refs[2] if has_add else None
        o_ref = refs[3] if has_add else refs[2]
        part = lax.dot_general(a_ref[...].astype(BF16), b_ref[...].astype(BF16), dn, preferred_element_type=F32)

        def finish(res):
            if has_add:
                res = res + add_ref[...]
            o_ref[...] = res.astype(o_ref.dtype)

        if nk == 1:
            finish(part)
        else:
            acc = refs[-1]
            k = pl.program_id(2)

            @pl.when(k == 0)
            def _():
                acc[...] = part

            @pl.when(k > 0)
            def _():
                acc[...] += part

            @pl.when(k == nk - 1)
            def _():
                finish(acc[...])

    ins = [a, b] + ([add] if has_add else [])
    specs = [a_spec, b_spec] + ([add_spec] if has_add else [])
    scratch = [pltpu.VMEM(acc_shape, F32)] if nk > 1 else []
    if comm is not None:
        (out,), got = _pcall_carrying(body, comm, name=name, grid=grid, in_specs=specs, out_specs=[o_spec],
                                      out_shape=[out_shape], scratch_shapes=scratch)(*ins)
        return out, got
    return _pcall(body, name=name, grid=grid, in_specs=specs, out_specs=o_spec, out_shape=out_shape,
                  scratch_shapes=scratch)(*ins)


def _mm_nn(a, b, *, name, out_dtype=F32, tn=None, tk=None, add=None):
    M, K = a.shape
    Nn = b.shape[1]
    tm = min(TM, M)
    tn = Nn if tn is None else tn
    tk = K if tk is None else tk
    grid = (M // tm, Nn // tn, K // tk)
    return _mm(a, b, name=name, grid=grid,
               a_spec=pl.BlockSpec((tm, tk), lambda i, j, k: (i, k)),
               b_spec=pl.BlockSpec((tk, tn), lambda i, j, k: (k, j)),
               o_spec=pl.BlockSpec((tm, tn), lambda i, j, k: (i, j)),
               out_shape=_sds((M, Nn), out_dtype), acc_shape=(tm, tn), add=add,
               add_spec=pl.BlockSpec((tm, tn), lambda i, j, k: (i, j)))


def _mm_nn_slab_out(a, b, *, name, width, comm=None):
    M, K = a.shape
    S = b.shape[1] // width
    tm = min(TM_WIDE, M)
    return _mm(a, b, name=name, grid=(M // tm, S, 1),
               a_spec=pl.BlockSpec((tm, K), lambda i, j, k: (i, 0)),
               b_spec=pl.BlockSpec((K, width), lambda i, j, k: (0, j)),
               o_spec=pl.BlockSpec((None, tm, width), lambda i, j, k: (j, i, 0)),
               out_shape=_sds((S, M, width), F32), acc_shape=(tm, width), comm=comm)


def _mm_nt(a, b, *, name, add=None):
    M, K = a.shape
    Nn = b.shape[0]
    tm = min(TM, M)
    return _mm(a, b, name=name, grid=(M // tm, 1, 1), trans_b=True,
               a_spec=pl.BlockSpec((tm, K), lambda i, j, k: (i, 0)),
               b_spec=pl.BlockSpec((Nn, K), lambda i, j, k: (0, 0)),
               o_spec=pl.BlockSpec((tm, Nn), lambda i, j, k: (i, 0)),
               out_shape=_sds((M, Nn), F32), acc_shape=(tm, Nn), add=add,
               add_spec=pl.BlockSpec((tm, Nn), lambda i, j, k: (i, 0)))


def _mm_slab_in_nt(a3, b, *, name, add, comm=None):
    S, M, width = a3.shape
    Nn = b.shape[0]
    tm = min(TM_WIDE, M)
    return _mm(a3, b, name=name, grid=(M // tm, 1, S), trans_b=True,
               a_spec=pl.BlockSpec((None, tm, width), lambda i, j, k: (k, i, 0)),
               b_spec=pl.BlockSpec((Nn, width), lambda i, j, k: (0, k)),
               o_spec=pl.BlockSpec((tm, Nn), lambda i, j, k: (i, 0)),
               out_shape=_sds((M, Nn), F32), acc_shape=(tm, Nn), add=add,
               add_spec=pl.BlockSpec((tm, Nn), lambda i, j, k: (i, 0)), comm=comm)


def _mm_tn(a, b, *, name, tmo, tn=None, comm=None):
    T, M = a.shape
    Nn = b.shape[1]
    tn = Nn if tn is None else tn
    tk = min(TK, T)
    return _mm(a, b, name=name, grid=(M // tmo, Nn // tn, T // tk), trans_a=True,
               a_spec=pl.BlockSpec((tk, tmo), lambda i, j, k: (k, i)),
               b_spec=pl.BlockSpec((tk, tn), lambda i, j, k: (k, j)),
               o_spec=pl.BlockSpec((tmo, tn), lambda i, j, k: (i, j)),
               out_shape=_sds((M, Nn), BF16), acc_shape=(tmo, tn), comm=comm)


def _mm_tn_slab(a, b3, *, name, tmo, comm=None):
    T, M = a.shape
    S, _, width = b3.shape
    tk = min(TK, T)
    return _mm(a, b3, name=name, grid=(M // tmo, S, T // tk), trans_a=True,
               a_spec=pl.BlockSpec((tk, tmo), lambda i, j, k: (k, i)),
               b_spec=pl.BlockSpec((None, tk, width), lambda i, j, k: (j, k, 0)),
               o_spec=pl.BlockSpec((tmo, width), lambda i, j, k: (i, j)),
               out_shape=_sds((M, S * width), BF16), acc_shape=(tmo, width), comm=comm)


def _r2(a, tm):
    return (a, (tm, a.shape[1]), lambda i: (i, 0))


def _slab(a3, s, tm):
    return (a3, (None, tm, a3.shape[2]), lambda i: (s, i, 0))


def _o2(T, C, dtype, tm):
    return ((T, C), dtype, (tm, C), lambda i: (i, 0))


def _rows_fwd(fn, row_ins, par_ins, outs, *, name, nt):
    nr, npar = len(row_ins), len(par_ins)

    def body(*refs):
        vals = [r[...] for r in refs[:nr + npar]]
        res = fn(*vals)
        for o_ref, v in zip(refs[nr + npar:], res):
            o_ref[...] = v.astype(o_ref.dtype)

    return _pcall(body, name=name, grid=(nt,),
                  in_specs=[pl.BlockSpec(bs, im) for (_, bs, im) in row_ins] + [_full_spec(p) for p in par_ins],
                  out_specs=[pl.BlockSpec(bs, im) for (_, _, bs, im) in outs],
                  out_shape=[_sds(s, d) for (s, d, _, _) in outs])(*[r[0] for r in row_ins], *par_ins)


def _rows_bwd(fn, row_ins, par_ins, cot_ins, drow_outs, *, name, nt):
    nr, npar, nc = len(row_ins), len(par_ins), len(cot_ins)
    keep = [k for k, o in enumerate(drow_outs) if o is not None]

    def body(*refs):
        vals = [r[...].astype(F32) for r in refs[:nr + npar]]
        cots = [r[...].astype(F32) for r in refs[nr + npar:nr + npar + nc]]
        orefs = refs[nr + npar + nc:]
        _, vjp = jax.vjp(fn, *vals)
        grads = vjp(tuple(cots))
        for o_ref, k in zip(orefs[:len(keep)], keep):
            o_ref[...] = grads[k].astype(o_ref.dtype)
        prefs = orefs[len(keep):]

        @pl.when(pl.program_id(0) == 0)
        def _():
            for p_ref in prefs:
                p_ref[...] = jnp.zeros_like(p_ref)

        for p_ref, g in zip(prefs, grads[nr:]):
            p_ref[...] += g

    outs = [drow_outs[k] for k in keep]
    res = _pcall(body, name=name, grid=(nt,),
                 in_specs=[pl.BlockSpec(bs, im) for (_, bs, im) in row_ins] + [_full_spec(p) for p in par_ins]
                 + [pl.BlockSpec(bs, im) for (_, bs, im) in cot_ins],
                 out_specs=[pl.BlockSpec(bs, im) for (_, _, bs, im) in outs] + [_full_spec(p) for p in par_ins],
                 out_shape=[_sds(s, d) for (s, d, _, _) in outs] + [_sds(p.shape, F32) for p in par_ins],
                 )(*[r[0] for r in row_ins], *par_ins, *[c[0] for c in cot_ins])
    return list(res[:len(keep)]), list(res[len(keep):])


def _layer_norm(v, g, b):
    mu = jnp.mean(v, axis=-1, keepdims=True)
    var = jnp.mean(jnp.square(v - mu), axis=-1, keepdims=True)
    return (v - mu) * lax.rsqrt(var + LN_EPS) * g + b


def _silu(v):
    return v * jax.nn.sigmoid(v)


def _softplus(v):
    return jnp.maximum(v, 0.0) + jnp.log1p(jnp.exp(-jnp.abs(v)))


def _halo_of(K):
    return 8 * ((K - 1 + 7) // 8)


DW_HALO = 8
DW_RB = 16
DW_LC = 256


def _dw_taps(win, w_ref, p, ls, K, shift_of):
    acc = None
    for k in range(K):
        o = shift_of(k)
        term = win[o:o + DW_RB, :] * w_ref[p, k:k + 1, ls]
        acc = term if acc is None else acc + term
    return acc


def _dwconv_fwd(post, part_ins, w, biases, outs, *, K, C, name, tm, nt):
    nparts, halo = len(part_ins), DW_HALO

    def body(*refs):
        x_refs, w_ref = refs[:nparts], refs[nparts]
        b_refs = refs[nparts + 1:2 * nparts + 1]
        orefs, buf = refs[2 * nparts + 1:-1], refs[-1]
        i = pl.program_id(0)
        for p in range(nparts):
            @pl.when(i == 0)
            def _():
                buf[p, pl.ds(0, halo), :] = jnp.zeros((halo, C), F32)

            @pl.when(i > 0)
            def _():
                buf[p, pl.ds(0, halo), :] = buf[p, pl.ds(tm, halo), :]

            buf[p, pl.ds(halo, tm), :] = x_refs[p][...]

        def group(r, carry):
            base = pl.multiple_of(r * DW_RB, DW_RB)
            for cj in range(C // DW_LC):
                ls = slice(cj * DW_LC, (cj + 1) * DW_LC)
                cs = [_dw_taps(buf[p, pl.ds(base, DW_RB + halo), ls], w_ref, p, ls, K, lambda k: halo - (K - 1) + k)
                      + b_refs[p][:, ls] for p in range(nparts)]
                for o_ref, v in zip(orefs, post(cs)):
                    o_ref[pl.ds(base, DW_RB), ls] = v.astype(o_ref.dtype)
            return carry

        lax.fori_loop(0, tm // DW_RB, group, 0)

    return _pcall(body, name=name, grid=(nt,),
                  in_specs=[pl.BlockSpec(bs, im) for (_, bs, im) in part_ins] + [_full_spec(w)] + [_full_spec(b) for b in biases],
                  out_specs=[pl.BlockSpec(bs, im) for (_, _, bs, im) in outs],
                  out_shape=[_sds(s, d) for (s, d, _, _) in outs],
                  scratch_shapes=[pltpu.VMEM((nparts, halo + tm, C), F32)])(*[r[0] for r in part_ins], w, *biases)


def _dwconv_bwd(post, part_ins, halo_ins, w, biases, cot_ins, *, K, C, name, tm, nt):
    nparts, halo, nc, RB = len(part_ins), DW_HALO, len(cot_ins), DW_RB
    T = nt * tm

    def body(*refs):
        x_refs, h_refs, w_ref = refs[:nparts], refs[nparts:2 * nparts], refs[2 * nparts]
        b_refs = refs[2 * nparts + 1:3 * nparts + 1]
        cot_refs = refs[3 * nparts + 1:3 * nparts + 1 + nc]
        rest = refs[3 * nparts + 1 + nc:]
        dx_refs, dw_ref, db_refs = rest[:nparts], rest[nparts], rest[nparts + 1:2 * nparts + 1]
        bufx, bufd, acc = rest[-3], rest[-2], rest[-1]
        s = pl.program_id(0)
        first_tile = s == nt - 1

        @pl.when(s == 0)
        def _():
            acc[...] = jnp.zeros_like(acc)
            for p in range(nparts):
                bufd[p, pl.ds(tm, halo), :] = jnp.zeros((halo, C), F32)

        for p in range(nparts):
            bufx[p, pl.ds(0, halo), :] = jnp.where(first_tile, 0.0, h_refs[p][...])
            bufx[p, pl.ds(halo, tm), :] = x_refs[p][...]
        fold = lambda v: v[0:8, :] + v[8:16, :]

        def conv_out_grads(r, carry):
            base = pl.multiple_of(r * RB, RB)
            for cj in range(C // DW_LC):
                ls = slice(cj * DW_LC, (cj + 1) * DW_LC)
                wins = [bufx[p, pl.ds(base, RB + halo), ls] for p in range(nparts)]
                cs = [_dw_taps(wins[p], w_ref, p, ls, K, lambda k: halo - (K - 1) + k) + b_refs[p][:, ls]
                      for p in range(nparts)]
                _, vjp = jax.vjp(lambda *c: post(list(c)), *cs)
                dcs = vjp(tuple(cr[pl.ds(base, RB), ls].astype(F32) for cr in cot_refs))
                for p in range(nparts):
                    bufd[p, pl.ds(base, RB), ls] = dcs[p]
                    for k in range(K):
                        o = halo - (K - 1) + k
                        acc[p, k, :, ls] += fold(dcs[p] * wins[p][o:o + RB, :])
                    acc[p, K, :, ls] += fold(dcs[p])
            return carry

        lax.fori_loop(0, tm // RB, conv_out_grads, 0)

        def input_grads(r, carry):
            base = pl.multiple_of(r * RB, RB)
            for cj in range(C // DW_LC):
                ls = slice(cj * DW_LC, (cj + 1) * DW_LC)
                for p in range(nparts):
                    dx = _dw_taps(bufd[p, pl.ds(base, RB + halo), ls], w_ref, p, ls, K, lambda k: K - 1 - k)
                    dx_refs[p][pl.ds(base, RB), ls] = dx.astype(dx_refs[p].dtype)
            return carry

        lax.fori_loop(0, tm // RB, input_grads, 0)
        for p in range(nparts):
            bufd[p, pl.ds(tm, halo), :] = bufd[p, pl.ds(0, halo), :]

        @pl.when(s == nt - 1)
        def _():
            dw_ref[...] = jnp.zeros_like(dw_ref)
            for p in range(nparts):
                for k in range(K):
                    dw_ref[p, k:k + 1, :] = jnp.sum(acc[p, k], axis=0, keepdims=True)
                db_refs[p][...] = jnp.sum(acc[p, K], axis=0, keepdims=True)

    rev = lambda im: (lambda s: im(nt - 1 - s))
    row = pl.BlockSpec((tm, C), lambda s: (nt - 1 - s, 0))
    res = _pcall(body, name=name, grid=(nt,),
                 in_specs=[pl.BlockSpec(bs, rev(im)) for (_, bs, im) in part_ins]
                 + [pl.BlockSpec(bs, rev(im)) for (_, bs, im) in halo_ins]
                 + [_full_spec(w)] + [_full_spec(b) for b in biases]
                 + [pl.BlockSpec(bs, rev(im)) for (_, bs, im) in cot_ins],
                 out_specs=[row] * nparts + [_full_spec(w)] + [_full_spec(b) for b in biases],
                 out_shape=[_sds((T, C), BF16)] * nparts + [_sds(w.shape, F32)] + [_sds(b.shape, F32) for b in biases],
                 scratch_shapes=[pltpu.VMEM((nparts, halo + tm, C), F32), pltpu.VMEM((nparts, tm + halo, C), F32),
                                 pltpu.VMEM((nparts, K + 1, 8, C), F32)],
                 )(*[r[0] for r in part_ins], *[r[0] for r in halo_ins], w, *biases, *[c[0] for c in cot_ins])
    return list(res[:nparts]), res[nparts], list(res[nparts + 1:])


def _halo_slab(a3, s, tm, halo):
    q = tm // halo
    return (a3, (None, halo, a3.shape[2]), lambda i: (s, jnp.maximum(i * q - 1, 0), 0))


CONF_HALO = _halo_of(CONV_K)
CONF_RB = 32


def _shifted_copies(buf, shifted, rows):
    for j in range(1, 8):
        shifted[j - 1, pl.ds(0, rows), :] = buf[pl.ds(j, rows), :]


def _shifted_rows(buf, shifted, s, base, nrows):
    j, q = s % 8, s // 8
    if j == 0:
        return buf[pl.ds(base + 8 * q, nrows), :]
    return shifted[j - 1, pl.ds(base + 8 * q, nrows), :]


def _conf_fwd(u3, w, cb, lg, lb, *, name):
    T = u3.shape[1]
    tm = min(TM_X, T)
    nt = T // tm
    K, halo, RB = CONV_K, CONF_HALO, min(CONF_RB, tm)

    def body(a_ref, g_ref, w_ref, cb_ref, lg_ref, lb_ref, v3_ref, c_ref, bufx, xs):
        i = pl.program_id(0)

        @pl.when(i == 0)
        def _():
            bufx[pl.ds(0, halo), :] = jnp.zeros((halo, D), F32)

        @pl.when(i > 0)
        def _():
            bufx[pl.ds(0, halo), :] = bufx[pl.ds(tm, halo), :]

        bufx[pl.ds(halo, tm), :] = a_ref[...] * jax.nn.sigmoid(g_ref[...])
        _shifted_copies(bufx, xs, halo + tm - 8)

        def group(r, carry):
            base = pl.multiple_of(r * RB, RB)
            acc = None
            for k in range(K):
                term = _shifted_rows(bufx, xs, halo - (K - 1) + k, base, RB) * w_ref[k:k + 1, :]
                acc = term if acc is None else acc + term
            c_ref[pl.ds(base, RB), :] = acc
            return carry

        lax.fori_loop(0, tm // RB, group, 0)
        v3_ref[...] = _conf_post([c_ref[...]], cb_ref[...], lg_ref[...], lb_ref[...])[0].astype(v3_ref.dtype)

    slab = lambda s: pl.BlockSpec((None, tm, D), lambda i: (s, i, 0))
    row = pl.BlockSpec((tm, D), lambda i: (i, 0))
    pars = [w, cb, lg, lb]
    return _pcall(body, name=name, grid=(nt,), in_specs=[slab(0), slab(1)] + [_full_spec(p) for p in pars],
                  out_specs=[row, row], out_shape=[_sds((T, D), BF16), _sds((T, D), F32)],
                  scratch_shapes=[pltpu.VMEM((halo + tm, D), F32), pltpu.VMEM((7, halo + tm - 8, D), F32)],
                  )(u3, u3, *pars)


def _conf_bwd(u3, c, dv3, w, cb, lg, lb, *, name):
    T = u3.shape[1]
    tm = min(TM_X, T)
    nt = T // tm
    K, halo, RB = CONV_K, CONF_HALO, min(CONF_RB, tm)
    q = tm // halo

    def body(a_ref, g_ref, ah_ref, gh_ref, c_ref, dv3_ref, w_ref, cb_ref, lg_ref, lb_ref,
             da_ref, dg_ref, dw_ref, dcb_ref, dlg_ref, dlb_ref, bufx, xs, bufd, ds, dv0):
        s = pl.program_id(0)
        first_tile = s == nt - 1

        @pl.when(s == 0)
        def _():
            for r in (dw_ref, dcb_ref, dlg_ref, dlb_ref):
                r[...] = jnp.zeros_like(r)
            bufd[pl.ds(tm, halo), :] = jnp.zeros((halo, D), F32)

        a, g = a_ref[...], g_ref[...]
        xin, pre_vjp = jax.vjp(lambda p, q_: _conf_pre(p, q_)[0], a, g)
        bufx[pl.ds(0, halo), :] = jnp.where(first_tile, 0.0, _conf_pre(ah_ref[...], gh_ref[...])[0])
        bufx[pl.ds(halo, tm), :] = xin
        _shifted_copies(bufx, xs, halo + tm - 8)

        _, post_vjp = jax.vjp(lambda cc, b_, g_, l_: _conf_post([cc], b_, g_, l_)[0],
                              c_ref[...], cb_ref[...], lg_ref[...], lb_ref[...])
        dc, dcb, dlg, dlb = post_vjp(dv3_ref[...])
        dcb_ref[...] += dcb
        dlg_ref[...] += dlg
        dlb_ref[...] += dlb
        bufd[pl.ds(0, tm), :] = dc
        _shifted_copies(bufd, ds, tm + halo - 8)

        def dx_group(r, carry):
            base = pl.multiple_of(r * RB, RB)
            acc = None
            for k in range(K):
                term = _shifted_rows(bufd, ds, K - 1 - k, base, RB) * w_ref[k:k + 1, :]
                acc = term if acc is None else acc + term
            dv0[pl.ds(base, RB), :] = acc
            return carry

        lax.fori_loop(0, tm // RB, dx_group, 0)

        for k in range(K):
            def dw_group(r, acc):
                base = pl.multiple_of(r * RB, RB)
                prod = bufd[pl.ds(base, RB), :] * _shifted_rows(bufx, xs, halo - (K - 1) + k, base, RB)
                for v in range(RB // 8):
                    acc = acc + prod[v * 8:(v + 1) * 8, :]
                return acc

            acc = lax.fori_loop(0, tm // RB, dw_group, jnp.zeros((8, D), F32))
            dw_ref[k:k + 1, :] += jnp.sum(acc, axis=0, keepdims=True)

        bufd[pl.ds(tm, halo), :] = bufd[pl.ds(0, halo), :]
        da, dg = pre_vjp(dv0[...])
        da_ref[...] = da.astype(da_ref.dtype)
        dg_ref[...] = dg.astype(dg_ref.dtype)

    slab = lambda sl: pl.BlockSpec((None, tm, D), lambda s: (sl, nt - 1 - s, 0))
    hslab = lambda sl: pl.BlockSpec((None, halo, D), lambda s: (sl, jnp.maximum((nt - 1 - s) * q - 1, 0), 0))
    row = pl.BlockSpec((tm, D), lambda s: (nt - 1 - s, 0))
    pars = [w, cb, lg, lb]
    res = _pcall(body, name=name, grid=(nt,),
                 in_specs=[slab(0), slab(1), hslab(0), hslab(1), row, row] + [_full_spec(p) for p in pars],
                 out_specs=[row, row] + [_full_spec(p) for p in pars],
                 out_shape=[_sds((T, D), BF16)] * 2 + [_sds(p.shape, F32) for p in pars],
                 scratch_shapes=[pltpu.VMEM((halo + tm, D), F32), pltpu.VMEM((7, halo + tm - 8, D), F32),
                                 pltpu.VMEM((tm + halo, D), F32), pltpu.VMEM((7, tm + halo - 8, D), F32),
                                 pltpu.VMEM((tm, D), F32)],
                 )(u3, u3, u3, u3, c, dv3, *pars)
    return res


def _dg(a, b, ca, cb):
    return lax.dot_general(a.astype(BF16), b.astype(BF16), (((ca,), (cb,)), ((), ())), preferred_element_type=F32)


@jax.custom_vjp
def _dot_nn(a, b):
    return _dg(a, b, 1, 0)


_dot_nn.defvjp(lambda a, b: (_dg(a, b, 1, 0), (a, b)),
               lambda res, g: (_dg(g, res[1], 1, 1), _dg(res[0], g, 0, 0)))


@jax.custom_vjp
def _dot_nt(a, b):
    return _dg(a, b, 1, 1)


_dot_nt.defvjp(lambda a, b: (_dg(a, b, 1, 1), (a, b)),
               lambda res, g: (_dg(g, res[1], 1, 0), _dg(g, res[0], 0, 0)))


@jax.custom_vjp
def _dot_tn(a, b):
    return _dg(a, b, 0, 0)


_dot_tn.defvjp(lambda a, b: (_dg(a, b, 0, 0), (a, b)),
               lambda res, g: (_dg(res[1], g, 1, 1), _dg(res[0], g, 1, 0)))


def _split3(v):
    hi = v.astype(BF16)
    r = v - hi.astype(F32)
    mid = r.astype(BF16)
    return hi, mid, (r - mid.astype(F32)).astype(BF16)


def _x01(v, m, cv, cm, m_left=False):
    acc = None
    for piece in _split3(v):
        t = _dg(m, piece, cm, cv) if m_left else _dg(piece, m, cv, cm)
        acc = t if acc is None else acc + t
    return acc


@jax.custom_vjp
def _expand01(v, m):
    return _x01(v, m, 1, 0)


_expand01.defvjp(lambda v, m: (_x01(v, m, 1, 0), m),
                 lambda m, g: (_x01(g, m, 1, 1), jnp.zeros_like(m)))


@jax.custom_vjp
def _mix01(m, v):
    return _x01(v, m, 0, 1, m_left=True)


_mix01.defvjp(lambda m, v: (_x01(v, m, 0, 1, m_left=True), m),
              lambda m, g: (jnp.zeros_like(m), _x01(g, m, 0, 0, m_left=True)))


def _causal():
    return lax.broadcasted_iota(jnp.int32, (L, L), 0) >= lax.broadcasted_iota(jnp.int32, (L, L), 1)


def _ssd_chunk_prep(dtr, alog, dtb):
    dt = _softplus(dtr + dtb)
    a_cs = _mix01(_causal().astype(F32), dt * (-jnp.exp(alog)))
    return dt, a_cs, a_cs.T


def _ssd_group(xs, dt, a_cs, a_csT, Bg, Cg, zg, sp, dsk, nwg, *, g):
    causal = _causal()
    hi = lax.broadcasted_iota(jnp.int32, (HP, RP), 0)
    ci = lax.broadcasted_iota(jnp.int32, (HP, RP), 1)
    lo = (hi - g * R) * P
    E = ((ci >= lo) & (ci < lo + P)).astype(F32)

    acs_e = _expand01(a_cs, E)
    dt_e = _expand01(dt, E)
    alast_e = acs_e[L - 1:L, :]
    xdt = xs * dt_e
    cb = _dot_nt(Cg, Bg)
    y_off = _dot_nn(Cg, sp) * jnp.exp(acs_e)
    yd = []
    for r in range(R):
        h = g * R + r
        seg = a_cs[:, h:h + 1] - a_csT[h:h + 1, :]
        dec = jnp.where(causal, jnp.exp(jnp.where(causal, seg, 0.0)), 0.0)
        yd.append(_dot_nn(cb * dec, xdt[:, r * P:(r + 1) * P]))
    y = jnp.concatenate(yd, axis=1) + y_off + xs * _expand01(jnp.broadcast_to(dsk, (8, HP)), E)[0:1, :]
    yg = y * _silu(zg)
    yn = yg * lax.rsqrt(jnp.mean(jnp.square(yg), axis=-1, keepdims=True) + RMS_EPS) * nwg
    sc = _dot_tn(Bg, xdt * jnp.exp(alast_e - acs_e))
    return yn, jnp.exp(alast_e) * sp + sc


def _group_cols(g):
    return g // 2, (g % 2) * RP


def _ssd_fwd(x0, x1, bc, dtr, u3, alog, dtb, dsk, nw, *, name):
    T = x0.shape[0]
    nc = T // L

    def body(x0_ref, x1_ref, bc_ref, dtr_ref, z0_ref, z1_ref, alog_ref, dtb_ref, dsk_ref, nw_ref, yn_ref, sp_ref, S):
        @pl.when(pl.program_id(0) == 0)
        def _():
            S[...] = jnp.zeros_like(S)

        xr, zr = (x0_ref, x1_ref), (z0_ref, z1_ref)
        dt, a_cs, a_csT = _ssd_chunk_prep(dtr_ref[...], alog_ref[...], dtb_ref[...])
        for g in range(G):
            s, off = _group_cols(g)
            sp = S[g]
            sp_ref[0, g] = sp
            yn, s_next = _ssd_group(xr[s][:, off:off + RP], dt, a_cs, a_csT, bc_ref[:, g * N:(g + 1) * N],
                                    bc_ref[:, G * N + g * N:G * N + (g + 1) * N], zr[s][:, off:off + RP], sp,
                                    dsk_ref[...], nw_ref[:, g * RP:(g + 1) * RP], g=g)
            yn_ref[:, g * RP:(g + 1) * RP] = yn.astype(yn_ref.dtype)
            S[g] = s_next

    row = lambda C: pl.BlockSpec((L, C), lambda c: (c, 0))
    zspec = lambda s: pl.BlockSpec((None, L, D), lambda c: (s, c, 0))
    pars = [alog, dtb, dsk, nw]
    return _pcall(body, name=name, grid=(nc,),
                  in_specs=[row(D), row(D), row(D), row(HP), zspec(2), zspec(3)] + [_full_spec(p) for p in pars],
                  out_specs=[row(SD), pl.BlockSpec((1, G, N, RP), lambda c: (c, 0, 0, 0))],
                  out_shape=[_sds((T, SD), BF16), _sds((nc, G, N, RP), F32)],
                  scratch_shapes=[pltpu.VMEM((G, N, RP), F32)])(x0, x1, bc, dtr, u3, u3, *pars)


def _ssd_bwd(x0, x1, bc, dtr, u3, sprev, dyn, alog, dtb, dsk, nw, *, name):
    T = x0.shape[0]
    nc = T // L

    def body(x0_ref, x1_ref, bc_ref, dtr_ref, z0_ref, z1_ref, sp_ref, dyn_ref, alog_ref, dtb_ref, dsk_ref, nw_ref,
             dx0_ref, dx1_ref, dbc_ref, ddtr_ref, dz0_ref, dz1_ref, dalog_ref, ddtb_ref, ddsk_ref, dnw_ref, dS):
        @pl.when(pl.program_id(0) == 0)
        def _():
            dS[...] = jnp.zeros_like(dS)
            for r in (dalog_ref, ddtb_ref, ddsk_ref, dnw_ref):
                r[...] = jnp.zeros_like(r)

        xr, zr = (x0_ref, x1_ref), (z0_ref, z1_ref)
        dxr, dzr = (dx0_ref, dx1_ref), (dz0_ref, dz1_ref)
        (dt, a_cs, a_csT), prep_vjp = jax.vjp(_ssd_chunk_prep, dtr_ref[...], alog_ref[...], dtb_ref[...])
        d_dt, d_acs, d_acsT = jnp.zeros((L, HP), F32), jnp.zeros((L, HP), F32), jnp.zeros((HP, L), F32)
        for g in range(G):
            s, off = _group_cols(g)
            _, vjp = jax.vjp(functools.partial(_ssd_group, g=g), xr[s][:, off:off + RP], dt, a_cs, a_csT,
                             bc_ref[:, g * N:(g + 1) * N], bc_ref[:, G * N + g * N:G * N + (g + 1) * N],
                             zr[s][:, off:off + RP], sp_ref[0, g], dsk_ref[...], nw_ref[:, g * RP:(g + 1) * RP])
            dxs, ddt_g, dacs_g, dacsT_g, dB, dC, dz, dsp, dds, dnwg = vjp((dyn_ref[:, g * RP:(g + 1) * RP], dS[g]))
            dxr[s][:, off:off + RP] = dxs
            dzr[s][:, off:off + RP] = dz.astype(dz0_ref.dtype)
            dbc_ref[:, g * N:(g + 1) * N] = dB
            dbc_ref[:, G * N + g * N:G * N + (g + 1) * N] = dC
            dS[g] = dsp
            d_dt, d_acs, d_acsT = d_dt + ddt_g, d_acs + dacs_g, d_acsT + dacsT_g
            ddsk_ref[...] += dds
            dnw_ref[:, g * RP:(g + 1) * RP] += dnwg
        ddtr, dal, ddb = prep_vjp((d_dt, d_acs, d_acsT))
        dalog_ref[...] += dal
        ddtb_ref[...] += ddb
        ddtr_ref[...] = ddtr.astype(ddtr_ref.dtype)

    row = lambda C: pl.BlockSpec((L, C), lambda c: (nc - 1 - c, 0))
    zspec = lambda s: pl.BlockSpec((None, L, D), lambda c: (s, nc - 1 - c, 0))
    pars = [alog, dtb, dsk, nw]
    return _pcall(body, name=name, grid=(nc,),
                  in_specs=[row(D), row(D), row(D), row(HP), zspec(2), zspec(3),
                            pl.BlockSpec((1, G, N, RP), lambda c: (nc - 1 - c, 0, 0, 0)), row(SD)]
                  + [_full_spec(p) for p in pars],
                  out_specs=[row(D), row(D), row(D), row(HP), row(D), row(D)] + [_full_spec(p) for p in pars],
                  out_shape=[_sds((T, D), F32)] * 3 + [_sds((T, HP), BF16), _sds((T, D), BF16), _sds((T, D), BF16)]
                  + [_sds(p.shape, F32) for p in pars],
                  scratch_shapes=[pltpu.VMEM((G, N, RP), F32)])(x0, x1, bc, dtr, u3, u3, sprev, dyn, *pars)


def _loss_head(y, target, *, name):
    T = y.shape[0]
    tm = min(TM, T)

    def body(y_ref, t_ref, loss_ref, dy_ref):
        e = y_ref[...] - t_ref[...]
        dy_ref[...] = e * (1.0 / D)

        @pl.when(pl.program_id(0) == 0)
        def _():
            loss_ref[...] = jnp.zeros_like(loss_ref)

        loss_ref[...] += 0.5 * jnp.sum(jnp.mean(jnp.square(e), axis=-1, keepdims=True), axis=0, keepdims=True)

    row = pl.BlockSpec((tm, D), lambda i: (i, 0))
    return _pcall(body, name=name, grid=(T // tm,), in_specs=[row, row],
                  out_specs=[pl.BlockSpec((1, 128), lambda i: (0, 0)), row],
                  out_shape=[_sds((1, 128), F32), _sds((T, D), F32)])(y, target)


_HBM = pl.BlockSpec(memory_space=pltpu.HBM)
_MESH = pl.DeviceIdType.MESH


def _exchange_comm(bufs, *, scatter):
    nb = len(bufs)

    def copies(in_refs, out_refs, sems, with_arrivals):
        send_sems, recv_sems, local_sems = sems
        x, y, c = lax.axis_index("x"), lax.axis_index("y"), lax.axis_index("c")
        me = 2 * x + y
        peers = [(1 - x, y), (x, 1 - y), (1 - x, 1 - y)]
        own, sends, arrivals = [], [], []
        for b in range(nb):
            src_own = in_refs[b].at[me] if scatter else in_refs[b]
            own.append(pltpu.make_async_copy(src_own, out_refs[b].at[me], local_sems.at[b]))
            for k, (px, py) in enumerate(peers):
                src = in_refs[b].at[2 * px + py] if scatter else in_refs[b]
                sends.append(pltpu.make_async_remote_copy(
                    src_ref=src, dst_ref=out_refs[b].at[me], send_sem=send_sems.at[b, k], recv_sem=recv_sems.at[b, k],
                    device_id=(px, py, c), device_id_type=_MESH))
                if with_arrivals:
                    slot = out_refs[b].at[2 * px + py]
                    arrivals.append(pltpu.make_async_remote_copy(
                        src_ref=slot, dst_ref=slot, send_sem=send_sems.at[b, k], recv_sem=recv_sems.at[b, k],
                        device_id=(px, py, c), device_id_type=_MESH))
        return own, sends, arrivals

    def start(in_refs, out_refs, sems):
        own, sends, _ = copies(in_refs, out_refs, sems, False)
        for cp in own + sends:
            cp.start()

    def wait(in_refs, out_refs, sems):
        own, sends, arrivals = copies(in_refs, out_refs, sems, True)
        for cp in arrivals:
            cp.wait_recv()
        for cp in sends:
            cp.wait_send()
        for cp in own:
            cp.wait()

    return dict(ins=list(bufs), outs=[_sds(b.shape if scatter else (NCHIP,) + b.shape, b.dtype) for b in bufs],
                sems=[pltpu.SemaphoreType.DMA((nb, 3)), pltpu.SemaphoreType.DMA((nb, 3)), pltpu.SemaphoreType.DMA((nb,))],
                start=start, wait=wait)


def _chip_exchange(bufs, *, scatter, name):
    comm = _exchange_comm(bufs, scatter=scatter)
    nb = len(bufs)

    def body(*refs):
        comm["start"](refs[:nb], refs[nb:2 * nb], refs[2 * nb:])
        comm["wait"](refs[:nb], refs[nb:2 * nb], refs[2 * nb:])

    return _ccall(body, name=name, in_specs=[_HBM] * nb, out_specs=[_HBM] * nb, out_shape=comm["outs"],
                  scratch_shapes=comm["sems"])(*bufs)


def _core_swap(bufs, *, name):
    nb = len(bufs)

    def body(*refs):
        in_refs, out_refs, send_sems, recv_sems = refs[:nb], refs[nb:2 * nb], refs[2 * nb], refs[2 * nb + 1]
        x, y, c = lax.axis_index("x"), lax.axis_index("y"), lax.axis_index("c")
        cps = [pltpu.make_async_remote_copy(src_ref=in_refs[b], dst_ref=out_refs[b], send_sem=send_sems.at[b],
                                            recv_sem=recv_sems.at[b], device_id=(x, y, 1 - c), device_id_type=_MESH)
               for b in range(nb)]
        for cp in cps:
            cp.start()
        for cp in cps:
            cp.wait()

    return _ccall(body, name=name, in_specs=[_HBM] * nb, out_specs=[_HBM] * nb,
                  out_shape=[_sds(b.shape, b.dtype) for b in bufs],
                  scratch_shapes=[pltpu.SemaphoreType.DMA((nb,)), pltpu.SemaphoreType.DMA((nb,))])(*bufs)


def _all_gather8(buf, *, name):
    def body(in_ref, out_ref, send_sems, recv_sems, local_sem):
        x, y, c = lax.axis_index("x"), lax.axis_index("y"), lax.axis_index("c")
        me = 4 * x + 2 * y + c
        own = pltpu.make_async_copy(in_ref, out_ref.at[me], local_sem)
        own.start()
        flips = [(fx, fy, fc) for fx in (0, 1) for fy in (0, 1) for fc in (0, 1)][1:]
        peers = [(x ^ fx, y ^ fy, c ^ fc) for fx, fy, fc in flips]
        sends = []
        for k, peer in enumerate(peers):
            cp = pltpu.make_async_remote_copy(src_ref=in_ref, dst_ref=out_ref.at[me], send_sem=send_sems.at[k],
                                              recv_sem=recv_sems.at[k], device_id=peer, device_id_type=_MESH)
            cp.start()
            sends.append(cp)
        for k, (px, py, pc) in enumerate(peers):
            slot = out_ref.at[4 * px + 2 * py + pc]
            pltpu.make_async_remote_copy(src_ref=slot, dst_ref=slot, send_sem=send_sems.at[k], recv_sem=recv_sems.at[k],
                                         device_id=(px, py, pc), device_id_type=_MESH).wait_recv()
        for cp in sends:
            cp.wait_send()
        own.wait()

    return _ccall(body, name=name, in_specs=[_HBM], out_specs=_HBM, out_shape=_sds((8,) + buf.shape, buf.dtype),
                  scratch_shapes=[pltpu.SemaphoreType.DMA((7,)), pltpu.SemaphoreType.DMA((7,)), pltpu.SemaphoreType.DMA])(buf)


def _row_tile(rows, cap):
    if rows <= cap:
        return rows
    return max(t for t in range(16, cap + 1, 16) if rows % t == 0)


def _sum_slots(stack, *, name, cap=256):
    S, Rr, C = stack.shape
    tr = _row_tile(Rr, cap)

    def body(s_ref, o_ref):
        acc = s_ref[0].astype(F32)
        for j in range(1, S):
            acc = acc + s_ref[j].astype(F32)
        o_ref[...] = acc

    return _pcall(body, name=name, grid=(Rr // tr,), in_specs=[pl.BlockSpec((S, tr, C), lambda i: (0, i, 0))],
                  out_specs=pl.BlockSpec((tr, C), lambda i: (i, 0)), out_shape=_sds((Rr, C), F32))(stack)


def _adamw(g_parts, w, m, v, *, name, cap=128):
    Rr, C = w.shape
    tr = _row_tile(Rr, cap)
    ng = len(g_parts)
    c1 = 1.0 / (1.0 - ADAM_B1 ** ADAM_STEP)
    c2 = 1.0 / (1.0 - ADAM_B2 ** ADAM_STEP)

    def body(*refs):
        g = refs[0][...]
        for r in refs[1:ng]:
            g = g + r[...]
        w_ref, m_ref, v_ref, g_out, d_out, m_out, v_out = refs[ng:]
        mn = ADAM_B1 * m_ref[...] + (1.0 - ADAM_B1) * g
        vn = ADAM_B2 * v_ref[...] + (1.0 - ADAM_B2) * jnp.square(g)
        g_out[...] = g
        m_out[...] = mn
        v_out[...] = vn
        d_out[...] = -ADAM_LR * ((mn * c1) / (jnp.sqrt(vn * c2) + ADAM_EPS) + ADAM_WD * w_ref[...])

    spec = pl.BlockSpec((tr, C), lambda i: (i, 0))
    return _pcall(body, name=name, grid=(Rr // tr,), in_specs=[spec] * (ng + 3), out_specs=[spec] * 4,
                  out_shape=[_sds((Rr, C), F32)] * 4)(*g_parts, w, m, v)


def _adamw_layers(mine, other, w3, m3, v3, *, name, cap=128):
    _, Rr, C = w3.shape
    tr = _row_tile(Rr, cap)
    nt = Rr // tr
    c1 = 1.0 / (1.0 - ADAM_B1 ** ADAM_STEP)
    c2 = 1.0 / (1.0 - ADAM_B2 ** ADAM_STEP)

    def body(m0, m1, o0, o1, w_ref, m_ref, v_ref, g_out, d_out, m_out, v_out):
        g = jnp.where(pl.program_id(0) == 0, m0[...] + o0[...], m1[...] + o1[...])
        mn = ADAM_B1 * m_ref[...] + (1.0 - ADAM_B1) * g
        vn = ADAM_B2 * v_ref[...] + (1.0 - ADAM_B2) * jnp.square(g)
        g_out[...] = g
        m_out[...] = mn
        v_out[...] = vn
        d_out[...] = -ADAM_LR * ((mn * c1) / (jnp.sqrt(vn * c2) + ADAM_EPS) + ADAM_WD * w_ref[...])

    g0 = pl.BlockSpec((tr, C), lambda l, i: (jnp.where(l == 0, i, nt - 1), 0))
    g1 = pl.BlockSpec((tr, C), lambda l, i: (jnp.where(l == 1, i, 0), 0))
    s3 = pl.BlockSpec((None, tr, C), lambda l, i: (l, i, 0))
    return _pcall(body, name=name, grid=(2, nt), in_specs=[g0, g1, g0, g1, s3, s3, s3], out_specs=[s3] * 4,
                  out_shape=[_sds(w3.shape, F32)] * 4)(mine[0], mine[1], other[0], other[1], w3, m3, v3)


def _pack(arrs, dtype, row_mult):
    flat = jnp.concatenate([a.reshape(-1).astype(dtype) for a in arrs])
    n = flat.shape[0]
    unit = row_mult * PACK_W
    total = unit * ((n + unit - 1) // unit)
    if total > n:
        flat = jnp.concatenate([flat, jnp.zeros((total - n,), dtype)])
    return flat.reshape(-1, PACK_W)


def _unpack(buf, shapes):
    flat = buf.reshape(-1)
    out, off = [], 0
    for s in shapes:
        n = math.prod(s)
        out.append(flat[off:off + n].reshape(s))
        off += n
    return out


def _conf_pre(a, g):
    return [a * jax.nn.sigmoid(g)]


def _conf_post(cs, cb, lg, lb):
    return (_silu(_layer_norm(cs[0] + cb, lg, lb)),)


def _xbc_post(cs):
    return tuple(_silu(c) for c in cs)


def _ffn_post(cs):
    return (_silu(cs[0]) * cs[1],)


def _mix_fn(ga, gb, ya, yb):
    return (jax.nn.sigmoid(ga) * ya + jax.nn.sigmoid(gb) * yb,)


def _res_ln_fn(h, r, g, b):
    return (_layer_norm(ALPHA * h + r, g, b),)


def _ln_fn(x, g, b):
    return (_layer_norm(x, g, b),)


def _carrying(carry, key, gr, call):
    if key not in carry:
        return call(None)
    comm, done = carry[key](gr)
    out, got = call(comm)
    done(got)
    return out


def _layer_fwd(h, hb, W, l, carry):
    T = h.shape[0]
    tm = min(TM, T)
    nt = T // tm
    tmf = min(TM_FFN, T)
    ntf = T // tmf
    nm = lambda s: f"l{l}_{s}"
    u3 = _carrying(carry, "u", None, lambda comm: _mm_nn_slab_out(hb, W["w_p"], name=nm("u"), width=D, comm=comm))
    dtr = _mm_nn(hb, W["w_dt"], name=nm("dt"))
    v3, cconv = _conf_fwd(u3, W["conv_w"], W["conv_b"], W["conv_ln_g"], W["conv_ln_b"], name=nm("conf"))
    ya = _mm_nn(v3, W["w_co"], name=nm("ya"))
    tmx = min(TM_X, T)
    x0, x1, bc = _dwconv_fwd(_xbc_post, [_slab(u3, 6, tmx), _slab(u3, 7, tmx), _slab(u3, 8, tmx)], W["ssm_w"],
                             W["ssm_b"], [_o2(T, D, F32, tmx)] * 3, K=SSM_K, C=D, name=nm("xbc"), tm=tmx, nt=T // tmx)
    yn, sprev = _ssd_fwd(x0, x1, bc, dtr, u3, W["a_log"], W["dt_bias"], W["d_skip"], W["norm_w"], name=nm("ssd"))
    yb = _mm_nn(yn, W["w_so"], name=nm("yb"), tk=min(SD, 1024))
    (m,) = _rows_fwd(_mix_fn, [_slab(u3, 4, tm), _slab(u3, 5, tm), _r2(ya, tm), _r2(yb, tm)], [],
                     [_o2(T, D, BF16, tm)], name=nm("mix"), nt=nt)
    mix = _mm_nn(m, W["w_o"], name=nm("wo"))
    h1, h1b = _rows_fwd(lambda a, r, g, b: _res_ln_fn(a, r, g, b) * 2, [_r2(h, tm), _r2(mix, tm)],
                        [W["ln1_g"], W["ln1_b"]], [_o2(T, D, F32, tm), _o2(T, D, BF16, tm)], name=nm("ln1"), nt=nt)
    up3 = _carrying(carry, "up", None, lambda comm: _mm_nn_slab_out(h1b, W["w_up"], name=nm("up"), width=FFN, comm=comm))
    (f,) = _dwconv_fwd(_ffn_post, [_slab(up3, 0, tmf), _slab(up3, 1, tmf)], W["ffn_w"], W["ffn_b"],
                       [_o2(T, FFN, BF16, tmf)], K=FFN_K, C=FFN, name=nm("ffnact"), tm=tmf, nt=ntf)
    ffn = _mm_nn(f, W["w_dn"], name=nm("dn"))
    h2, h2b = _rows_fwd(lambda a, r, g, b: _res_ln_fn(a, r, g, b) * 2, [_r2(h1, tm), _r2(ffn, tm)],
                        [W["ln2_g"], W["ln2_b"]], [_o2(T, D, F32, tm), _o2(T, D, BF16, tm)], name=nm("ln2"), nt=nt)
    saved = dict(h=h, hb=hb, u3=u3, dtr=dtr, v3=v3, cconv=cconv, ya=ya, x0=x0, x1=x1, bc=bc, sprev=sprev, yn=yn, yb=yb, m=m,
                 mix=mix, h1=h1, h1b=h1b, up3=up3, f=f, ffn=ffn)
    return h2, h2b, saved


def _layer_bwd(dh2, W, sv, l, carry):
    T = dh2.shape[0]
    tm = min(TM, T)
    nt = T // tm
    tmf = min(TM_FFN, T)
    ntf = T // tmf
    nm = lambda s: f"l{l}_{s}"
    gr = {}
    (dres2, dffn), (gr["ln2_g"], gr["ln2_b"]) = _rows_bwd(
        _res_ln_fn, [_r2(sv["h1"], tm), _r2(sv["ffn"], tm)], [W["ln2_g"], W["ln2_b"]], [_r2(dh2, tm)],
        [_o2(T, D, F32, tm), _o2(T, D, BF16, tm)], name=nm("ln2_b"), nt=nt)
    df = _mm_nt(dffn, W["w_dn"], name=nm("dn_dx"))
    gr["w_dn"] = _carrying(carry, "dn_dw", gr,
                           lambda comm: _mm_tn(sv["f"], dffn, name=nm("dn_dw"), tmo=FFN // 2, comm=comm))
    up3 = sv["up3"]
    (dgate, dval), gr["ffn_w"], gr["ffn_b"] = _dwconv_bwd(
        _ffn_post, [_slab(up3, 0, tmf), _slab(up3, 1, tmf)],
        [_halo_slab(up3, 0, tmf, DW_HALO), _halo_slab(up3, 1, tmf, DW_HALO)], W["ffn_w"], W["ffn_b"], [_r2(df, tmf)],
        K=FFN_K, C=FFN, name=nm("ffnact_b"), tm=tmf, nt=ntf)
    dup3 = jnp.stack([dgate, dval])
    dh1 = _carrying(carry, "up_dx", gr,
                    lambda comm: _mm_slab_in_nt(dup3, W["w_up"], name=nm("up_dx"), add=dres2, comm=comm))
    gr["w_up"] = _carrying(carry, "up_dw", gr,
                           lambda comm: _mm_tn_slab(sv["h1b"], dup3, name=nm("up_dw"), tmo=min(512, D), comm=comm))
    (dres1, dmix), (gr["ln1_g"], gr["ln1_b"]) = _rows_bwd(
        _res_ln_fn, [_r2(sv["h"], tm), _r2(sv["mix"], tm)], [W["ln1_g"], W["ln1_b"]], [_r2(dh1, tm)],
        [_o2(T, D, F32, tm), _o2(T, D, BF16, tm)], name=nm("ln1_b"), nt=nt)
    dm = _mm_nt(dmix, W["w_o"], name=nm("wo_dx"))
    gr["w_o"] = _mm_tn(sv["m"], dmix, name=nm("wo_dw"), tmo=min(512, D))
    u3 = sv["u3"]
    (dga, dgb, dya, dyb), _ = _rows_bwd(
        _mix_fn, [_slab(u3, 4, tm), _slab(u3, 5, tm), _r2(sv["ya"], tm), _r2(sv["yb"], tm)], [], [_r2(dm, tm)],
        [_o2(T, D, BF16, tm)] * 4, name=nm("mix_b"), nt=nt)
    dv3 = _mm_nt(dya, W["w_co"], name=nm("ya_dx"))
    gr["w_co"] = _mm_tn(sv["v3"], dya, name=nm("ya_dw"), tmo=min(512, D))
    da, dg, gr["conv_w"], gr["conv_b"], gr["conv_ln_g"], gr["conv_ln_b"] = _conf_bwd(
        u3, sv["cconv"], dv3, W["conv_w"], W["conv_b"], W["conv_ln_g"], W["conv_ln_b"], name=nm("conf_b"))
    dyn = _mm_nt(dyb, W["w_so"], name=nm("yb_dx"))
    gr["w_so"] = _mm_tn(sv["yn"], dyb, name=nm("yb_dw"), tmo=min(512, SD))
    (dx0, dx1, dbc, ddtr, dz0, dz1, gr["a_log"], gr["dt_bias"], gr["d_skip"], gr["norm_w"]) = _ssd_bwd(
        sv["x0"], sv["x1"], sv["bc"], sv["dtr"], u3, sv["sprev"], dyn, W["a_log"], W["dt_bias"], W["d_skip"],
        W["norm_w"], name=nm("ssd_b"))
    tmx = min(TM_X, T)
    (du6, du7, du8), gr["ssm_w"], gr["ssm_b"] = _dwconv_bwd(
        _xbc_post, [_slab(u3, 6, tmx), _slab(u3, 7, tmx), _slab(u3, 8, tmx)],
        [_halo_slab(u3, 6, tmx, DW_HALO), _halo_slab(u3, 7, tmx, DW_HALO), _halo_slab(u3, 8, tmx, DW_HALO)],
        W["ssm_w"], W["ssm_b"], [_r2(dx0, tmx), _r2(dx1, tmx), _r2(dbc, tmx)], K=SSM_K, C=D, name=nm("xbc_b"),
        tm=tmx, nt=T // tmx)
    du3 = jnp.stack([da, dg, dz0, dz1, dga, dgb, du6, du7, du8])
    dh_a = _mm_nt(ddtr, W["w_dt"], name=nm("dt_dx"), add=dres1)
    dh = _carrying(carry, "u_dx", gr, lambda comm: _mm_slab_in_nt(du3, W["w_p"], name=nm("u_dx"), add=dh_a, comm=comm))
    gr["w_p"] = _mm_tn_slab(sv["hb"], du3, name=nm("u_dw"), tmo=min(512, D))
    gr["w_dt"] = _mm_tn(sv["hb"], ddtr, name=nm("dt_dw"), tmo=min(512, D))
    return dh, gr


_U_SPLIT = (2 * D + SD, 2 * D + SD + XBC, 2 * D + SD + XBC + H)


def _pad_rows(a, rows):
    return jnp.concatenate([a, jnp.zeros((rows - a.shape[0],) + a.shape[1:], a.dtype)], axis=0)


def _pad_lanes(a, lanes):
    return jnp.concatenate([a, jnp.zeros(a.shape[:-1] + (lanes - a.shape[-1],), a.dtype)], axis=-1)


def _w_in_layout(w_in):
    e0, e1, e2 = _U_SPLIT
    return dict(w_p=jnp.concatenate([w_in[:, :e0], w_in[:, e2:], w_in[:, e0:e1]], axis=1),
                w_dt=_pad_lanes(w_in[:, e1:e2], HP))


_MM_KEY = dict(w_conv_out="w_co", w_ssm_out="w_so", w_o="w_o", w_ffn_up="w_up", w_ffn_down="w_dn")


def _small_layer_weights(full, l):
    row = lambda a: a.reshape(1, -1)
    ssm_w = full["ssm_conv_w"][l]
    ffn_w = full["ffn_dw_w"][l]
    ssm_b = full["ssm_conv_b"][l]
    ffn_b = full["ffn_dw_b"][l]
    W = dict(
        conv_w=_pad_rows(full["conv_dw_w"][l], 32),
        conv_b=row(full["conv_dw_b"][l]), conv_ln_g=row(full["conv_ln_g"][l]), conv_ln_b=row(full["conv_ln_b"][l]),
        ssm_w=jnp.stack([_pad_rows(ssm_w[:, p * D:(p + 1) * D], 8) for p in range(3)]),
        ssm_b=[row(ssm_b[p * D:(p + 1) * D]) for p in range(3)],
        a_log=_pad_lanes(row(full["ssm_a_log"][l]), HP), dt_bias=_pad_lanes(row(full["ssm_dt_bias"][l]), HP),
        d_skip=_pad_lanes(row(full["ssm_d"][l]), HP), norm_w=row(full["ssm_norm_w"][l]),
        ln1_g=row(full["ln1_g"][l]), ln1_b=row(full["ln1_b"][l]),
        ffn_w=jnp.stack([_pad_rows(ffn_w[:, p * FFN:(p + 1) * FFN], 8) for p in range(2)]),
        ffn_b=[row(ffn_b[p * FFN:(p + 1) * FFN]) for p in range(2)],
        ln2_g=row(full["ln2_g"][l]), ln2_b=row(full["ln2_b"][l]),
    )
    return W


def _layer_grads_to_reference_layout(gr):
    e0, e1, e2 = _U_SPLIT
    nx = XBC
    wp = gr["w_p"]
    w_in = jnp.concatenate([wp[:, :e0], wp[:, e0 + 2 * D:e0 + 2 * D + nx], gr["w_dt"][:, :H], wp[:, e0:e0 + 2 * D]], axis=1)
    return dict(
        w_in=w_in, conv_dw_w=gr["conv_w"][:CONV_K], conv_dw_b=gr["conv_b"][0], conv_ln_g=gr["conv_ln_g"][0],
        conv_ln_b=gr["conv_ln_b"][0], w_conv_out=gr["w_co"],
        ssm_conv_w=jnp.concatenate([gr["ssm_w"][p, :SSM_K] for p in range(3)], axis=1),
        ssm_conv_b=jnp.concatenate([b[0] for b in gr["ssm_b"]]),
        ssm_dt_bias=gr["dt_bias"][0, :H], ssm_a_log=gr["a_log"][0, :H], ssm_d=gr["d_skip"][0, :H],
        ssm_norm_w=gr["norm_w"][0], w_ssm_out=gr["w_so"], w_o=gr["w_o"], ln1_g=gr["ln1_g"][0], ln1_b=gr["ln1_b"][0],
        w_ffn_up=gr["w_up"], ffn_dw_w=jnp.concatenate([gr["ffn_w"][p, :FFN_K] for p in range(2)], axis=1),
        ffn_dw_b=jnp.concatenate([b[0] for b in gr["ffn_b"]]), w_ffn_down=gr["w_dn"], ln2_g=gr["ln2_g"][0],
        ln2_b=gr["ln2_b"][0],
    )


_BIG = dict(w_in=2, w_conv_out=1, w_ssm_out=1, w_o=1, w_ffn_up=2, w_ffn_down=1)
_SMALL_SHARDED = dict(conv_dw_w=2, ssm_conv_w=2, ffn_dw_w=2)
_REPLICATED = ("ln_in_g", "ln_in_b", "conv_dw_b", "conv_ln_g", "conv_ln_b", "ssm_conv_b", "ssm_dt_bias", "ssm_a_log",
               "ssm_d", "ssm_norm_w", "ln1_g", "ln1_b", "ffn_dw_b", "ln2_g", "ln2_b")
_WEIGHTS = ("ln_in_g", "ln_in_b", "w_in", "conv_dw_w", "conv_dw_b", "conv_ln_g", "conv_ln_b", "w_conv_out", "ssm_conv_w",
            "ssm_conv_b", "ssm_dt_bias", "ssm_a_log", "ssm_d", "ssm_norm_w", "w_ssm_out", "w_o", "ln1_g", "ln1_b",
            "w_ffn_up", "ffn_dw_w", "ffn_dw_b", "w_ffn_down", "ln2_g", "ln2_b")


def _split_chips(a, axis):
    rows, cols = a.shape
    if axis == 0:
        return a.reshape(NCHIP, rows // NCHIP, cols)
    return a.reshape(rows, NCHIP, cols // NCHIP).transpose(1, 0, 2)


def kernel(x, ln_in_g, ln_in_b, w_in, conv_dw_w, conv_dw_b, conv_ln_g, conv_ln_b, w_conv_out, ssm_conv_w, ssm_conv_b, ssm_dt_bias, ssm_a_log, ssm_d, ssm_norm_w, w_ssm_out, w_o, ln1_g, ln1_b, w_ffn_up, ffn_dw_w, ffn_dw_b, w_ffn_down, ln2_g, ln2_b, loss_target, m_ln_in_g, m_ln_in_b, m_w_in, m_conv_dw_w, m_conv_dw_b, m_conv_ln_g, m_conv_ln_b, m_w_conv_out, m_ssm_conv_w, m_ssm_conv_b, m_ssm_dt_bias, m_ssm_a_log, m_ssm_d, m_ssm_norm_w, m_w_ssm_out, m_w_o, m_ln1_g, m_ln1_b, m_w_ffn_up, m_ffn_dw_w, m_ffn_dw_b, m_w_ffn_down, m_ln2_g, m_ln2_b, v_ln_in_g, v_ln_in_b, v_w_in, v_conv_dw_w, v_conv_dw_b, v_conv_ln_g, v_conv_ln_b, v_w_conv_out, v_ssm_conv_w, v_ssm_conv_b, v_ssm_dt_bias, v_ssm_a_log, v_ssm_d, v_ssm_norm_w, v_w_ssm_out, v_w_o, v_ln1_g, v_ln1_b, v_w_ffn_up, v_ffn_dw_w, v_ffn_dw_b, v_w_ffn_down, v_ln2_g, v_ln2_b):
    args = locals()
    w = {n: args[n] for n in _WEIGHTS}
    mom = {n: args["m_" + n] for n in _WEIGHTS}
    vel = {n: args["v_" + n] for n in _WEIGHTS}
    T = x.shape[1]
    tm = min(TM, T)
    nt = T // tm
    chip = 2 * lax.axis_index("x") + lax.axis_index("y")

    assert DEPTH == 2
    big_names, small_names = list(_BIG), list(_SMALL_SHARDED)
    rest_big = [n for n in big_names if n != "w_in"]
    bf = {n: w[n].astype(BF16) for n in big_names}
    join = lambda got, axis: jnp.concatenate([got[j] for j in range(NCHIP)], axis=axis)
    first = _chip_exchange([bf["w_in"][0]] + [w[n] for n in small_names], scatter=False, name="gather_first")
    full = {n: join(gk, _SMALL_SHARDED[n]) for n, gk in zip(small_names, first[1:])}
    for n in _REPLICATED:
        full[n] = w[n]
    Ws = [_small_layer_weights(full, l) for l in range(DEPTH)]
    Ws[0].update(_w_in_layout(join(first[0], 1)))

    def rest_arrived(l):
        def done(got):
            for n, gk in zip(rest_big, got):
                Ws[l][_MM_KEY[n]] = join(gk, _BIG[n] - 1)
        return done

    carry_fwd = [
        {"u": lambda gr: (_exchange_comm([bf[n][0] for n in rest_big], scatter=False), rest_arrived(0)),
         "up": lambda gr: (_exchange_comm([bf["w_in"][1]], scatter=False),
                           lambda got: Ws[1].update(_w_in_layout(join(got[0], 1))))},
        {"u": lambda gr: (_exchange_comm([bf[n][1] for n in rest_big], scatter=False), rest_arrived(1))},
    ]

    x2 = x.reshape(T, D)
    g_in, b_in = ln_in_g.reshape(1, D), ln_in_b.reshape(1, D)
    h, hb = _rows_fwd(lambda a, g, b: _ln_fn(a, g, b) * 2, [_r2(x2, tm)], [g_in, b_in],
                      [_o2(T, D, F32, tm), _o2(T, D, BF16, tm)], name="ln_in", nt=nt)
    saved = []
    for l in range(DEPTH):
        h, hb, sv = _layer_fwd(h, hb, Ws[l], l, carry_fwd[l])
        saved.append(sv)
    loss_row, dh = _loss_head(h, loss_target.reshape(T, D), name="loss")

    arrived = {}

    def exchange(names, l, grads):
        def make(gr):
            src = grads(gr)
            def done(got):
                for n, gk in zip(names, got):
                    arrived[(n, l)] = gk
            return _exchange_comm([_split_chips(src[n], _BIG[n] - 1) for n in names], scatter=True), done
        return make

    layer_grads = [None] * DEPTH
    dh, gr = _layer_bwd(dh, Ws[1], saved[1], 1, {})
    layer_grads[1] = _layer_grads_to_reference_layout(gr)
    g1 = lambda gr: layer_grads[1]
    g0 = lambda gr: {n: gr[_MM_KEY[n]] for n in rest_big}
    dh, gr = _layer_bwd(dh, Ws[0], saved[0], 0, {
        "dn_dw": exchange(["w_conv_out", "w_ssm_out", "w_o"], 1, g1),
        "up_dx": exchange(["w_in"], 1, g1),
        "up_dw": exchange(["w_ffn_up", "w_ffn_down"], 1, g1),
        "u_dx": exchange(rest_big, 0, g0)})
    layer_grads[0] = _layer_grads_to_reference_layout(gr)
    (arrived[("w_in", 0)],) = _chip_exchange([_split_chips(layer_grads[0]["w_in"], 1)], scatter=True, name="exchange_last")
    (grad_x2,), (d_g_in, d_b_in) = _rows_bwd(_ln_fn, [_r2(x2, tm)], [g_in, b_in], [_r2(dh, tm)], [_o2(T, D, F32, tm)],
                                             name="ln_in_b", nt=nt)
    local = {n: jnp.stack([layer_grads[l][n] for l in range(DEPTH)]) for n in _WEIGHTS[2:] if n not in _BIG}
    local["ln_in_g"], local["ln_in_b"] = d_g_in[0], d_b_in[0]
    res = [{}, {}, {}, {}]

    keys = [(n, l) for n in big_names for l in range(DEPTH)]
    mine = [_sum_slots(arrived[k], name=f"sum_chips_{k[0]}_{k[1]}") for k in keys]
    other = _core_swap(mine, name="swap_cores")
    for i, n in enumerate(big_names):
        outs = _adamw_layers(mine[2 * i:2 * i + 2], other[2 * i:2 * i + 2], w[n], mom[n], vel[n], name="adamw_" + n)
        for q in range(4):
            res[q][n] = outs[q]

    rest_names = list(_REPLICATED) + small_names
    part = _pack([loss_row] + [local[n] for n in rest_names], F32, 8)
    parts = _all_gather8(part, name="gather_small")
    total = _sum_slots(parts, name="sum_devices")
    tot = _unpack(total, [loss_row.shape] + [local[n].shape for n in rest_names])
    loss = tot[0][0, 0]
    g_rest = {}
    for n, t in zip(rest_names, tot[1:]):
        if n in _SMALL_SHARDED:
            ax = _SMALL_SHARDED[n]
            t = lax.dynamic_slice_in_dim(t, chip * w[n].shape[ax], w[n].shape[ax], axis=ax)
        g_rest[n] = t
    pk = lambda d: _pack([d[n] for n in rest_names], F32, 8)
    rest_out = _adamw([pk(g_rest)], pk(w), pk(mom), pk(vel), name="adamw_rest")
    rest_out = [_unpack(o, [w[n].shape for n in rest_names]) for o in rest_out]

    for q in range(4):
        for k, n in enumerate(rest_names):
            res[q][n] = rest_out[q][k]
    grad_x = grad_x2.reshape(x.shape)
    return (loss, grad_x, *[res[0][n] for n in _WEIGHTS], *[res[1][n] for n in _WEIGHTS],
            *[res[2][n] for n in _WEIGHTS], *[res[3][n] for n in _WEIGHTS])
```

```python
import functools
import math

import jax
import jax.numpy as jnp
from jax import lax
from jax.experimental import pallas as pl
from jax.experimental.pallas import tpu as pltpu

F32 = jnp.float32
BF16 = jnp.bfloat16

D = 1024
DEPTH = 2
CONV_K = 31
SD = 2 * D
P = 64
H = SD // P
G = 4
R = H // G
N = 128
RP = R * P
SSM_K = 4
L = 128
XBC = SD + 2 * G * N
FFN = 2816
FFN_K = 3
IN_DIM = 2 * D + SD + XBC + H + 2 * D
ALPHA = (2 * DEPTH) ** 0.25
LN_EPS = 1e-5
RMS_EPS = 1e-5
ADAM_LR, ADAM_B1, ADAM_B2, ADAM_EPS, ADAM_WD, ADAM_STEP = 0.001, 0.9, 0.999, 1e-08, 0.01, 10

HP = 128
NCHIP = 4
PACK_W = 1024
VMEM_LIMIT = 56 * 1024 * 1024
TM = 512
TM_X = 256
TM_FFN = 256
TK = 1024

assert D == 2 * RP and 2 * G * N == D and XBC == 3 * D and H <= HP


def _pcall(body, *, name, grid=(), in_specs, out_specs, out_shape, scratch_shapes=()):
    params = pltpu.CompilerParams(vmem_limit_bytes=VMEM_LIMIT, dimension_semantics=("arbitrary",) * len(grid))
    return pl.pallas_call(body, name=name, grid=grid, in_specs=in_specs, out_specs=out_specs, out_shape=out_shape,
                          scratch_shapes=list(scratch_shapes), compiler_params=params)


def _pcall_carrying(body, comm, *, name, grid, in_specs, out_specs, out_shape, scratch_shapes=()):
    in_specs, out_specs, out_shape = list(in_specs), list(out_specs), list(out_shape)
    scratch_shapes = list(scratch_shapes)
    n_in, n_out, n_scr = len(in_specs), len(out_specs), len(scratch_shapes)
    nci, nco = len(comm["ins"]), len(comm["outs"])

    def wrapped(*refs):
        ins, cin = refs[:n_in], refs[n_in:n_in + nci]
        outs = refs[n_in + nci:n_in + nci + n_out]
        cout = refs[n_in + nci + n_out:n_in + nci + n_out + nco]
        scr = refs[n_in + nci + n_out + nco:n_in + nci + n_out + nco + n_scr]
        csem = refs[n_in + nci + n_out + nco + n_scr:]
        ids = [pl.program_id(ax) for ax in range(len(grid))]
        first = functools.reduce(jnp.logical_and, [i == 0 for i in ids])
        last = functools.reduce(jnp.logical_and, [i == g - 1 for i, g in zip(ids, grid)])

        @pl.when(first)
        def _():
            comm["start"](cin, cout, csem)

        body(*ins, *outs, *scr)

        @pl.when(last)
        def _():
            comm["wait"](cin, cout, csem)

    call = _pcall(wrapped, name=name, grid=grid, in_specs=in_specs + [_HBM] * nci, out_specs=out_specs + [_HBM] * nco,
                  out_shape=out_shape + list(comm["outs"]), scratch_shapes=scratch_shapes + list(comm["sems"]))

    def run(*operands):
        res = call(*operands, *comm["ins"])
        return list(res[:n_out]), list(res[n_out:])

    return run


def _ccall(body, *, name, in_specs, out_specs, out_shape, scratch_shapes):
    return pl.pallas_call(body, name=name, in_specs=in_specs, out_specs=out_specs, out_shape=out_shape,
                          scratch_shapes=list(scratch_shapes))


def _full_spec(a):
    nd = a.ndim
    return pl.BlockSpec(a.shape, lambda *_: (0,) * nd)


def _sds(shape, dtype):
    return jax.ShapeDtypeStruct(tuple(shape), dtype)


def _mm(a, b, *, name, grid, a_spec, b_spec, o_spec, out_shape, acc_shape, trans_a=False, trans_b=False, add=None,
        add_spec=None, comm=None):
    nk = grid[2]
    dn = (((0 if trans_a else 1,), (1 if trans_b else 0,)), ((), ()))
    has_add = add is not None

    def body(*refs):
        a_ref, b_ref = refs[0], refs[1]
        add_ref = refs[2] if has_add else None
        o_ref = refs[3] if has_add else refs[2]
        part = lax.dot_general(a_ref[...].astype(BF16), b_ref[...].astype(BF16), dn, preferred_element_type=F32)

        def finish(res):
            if has_add:
                res = res + add_ref[...]
            o_ref[...] = res.astype(o_ref.dtype)

        if nk == 1:
            finish(part)
        else:
            acc = refs[-1]
            k = pl.program_id(2)

            @pl.when(k == 0)
            def _():
                acc[...] = part

            @pl.when(k > 0)
            def _():
                acc[...] += part

            @pl.when(k == nk - 1)
            def _():
                finish(acc[...])

    ins = [a, b] + ([add] if has_add else [])
    specs = [a_spec, b_spec] + ([add_spec] if has_add else [])
    scratch = [pltpu.VMEM(acc_shape, F32)] if nk > 1 else []
    if comm is not None:
        (out,), got = _pcall_carrying(body, comm, name=name, grid=grid, in_specs=specs, out_specs=[o_spec],
                                      out_shape=[out_shape], scratch_shapes=scratch)(*ins)
        return out, got
    return _pcall(body, name=name, grid=grid, in_specs=specs, out_specs=o_spec, out_shape=out_shape,
                  scratch_shapes=scratch)(*ins)


def _mm_nn(a, b, *, name, out_dtype=F32, tn=None, tk=None, add=None):
    M, K = a.shape
    Nn = b.shape[1]
    tm = min(TM, M)
    tn = Nn if tn is None else tn
    tk = K if tk is None else tk
    grid = (M // tm, Nn // tn, K // tk)
    return _mm(a, b, name=name, grid=grid,
               a_spec=pl.BlockSpec((tm, tk), lambda i, j, k: (i, k)),
               b_spec=pl.BlockSpec((tk, tn), lambda i, j, k: (k, j)),
               o_spec=pl.BlockSpec((tm, tn), lambda i, j, k: (i, j)),
               out_shape=_sds((M, Nn), out_dtype), acc_shape=(tm, tn), add=add,
               add_spec=pl.BlockSpec((tm, tn), lambda i, j, k: (i, j)))


def _mm_nt(a, b, *, name, add=None):
    M, K = a.shape
    Nn = b.shape[0]
    tm = min(TM, M)
    return _mm(a, b, name=name, grid=(M // tm, 1, 1), trans_b=True,
               a_spec=pl.BlockSpec((tm, K), lambda i, j, k: (i, 0)),
               b_spec=pl.BlockSpec((Nn, K), lambda i, j, k: (0, 0)),
               o_spec=pl.BlockSpec((tm, Nn), lambda i, j, k: (i, 0)),
               out_shape=_sds((M, Nn), F32), acc_shape=(tm, Nn), add=add,
               add_spec=pl.BlockSpec((tm, Nn), lambda i, j, k: (i, 0)))


def _mm_resident_slab_out(a, w, *, name, width, tm, comm=None):
    M, K = a.shape
    S = w.shape[1] // width
    tm = min(tm, M)

    def body(a_ref, w_hbm, o_ref, w_vmem, sem):
        @pl.when(pl.program_id(0) == 0)
        def _():
            cp = pltpu.make_async_copy(w_hbm, w_vmem, sem)
            cp.start()
            cp.wait()

        av = a_ref[...].astype(BF16)
        for s in range(S):
            o_ref[s] = jnp.dot(av, w_vmem[:, s * width:(s + 1) * width], preferred_element_type=F32)

    kw = dict(name=name, grid=(M // tm,),
              in_specs=[pl.BlockSpec((tm, K), lambda i: (i, 0)), pl.BlockSpec(memory_space=pl.ANY)],
              scratch_shapes=[pltpu.VMEM(w.shape, w.dtype), pltpu.SemaphoreType.DMA])
    o_spec = pl.BlockSpec((S, tm, width), lambda i: (0, i, 0))
    if comm is not None:
        (out,), got = _pcall_carrying(body, comm, out_specs=[o_spec], out_shape=[_sds((S, M, width), F32)], **kw)(a, w)
        return out, got
    return _pcall(body, out_specs=o_spec, out_shape=_sds((S, M, width), F32), **kw)(a, w)


def _mm_cols_nt(a_list, w, *, name, add, comm=None):
    S = len(a_list)
    M, width = a_list[0].shape
    Nn = w.shape[0]
    tm = min(TM, M)

    def body(*refs):
        a_refs, w_hbm, add_ref, o_ref, w_vmem, sem = refs[:S], refs[S], refs[S + 1], refs[S + 2], refs[S + 3], refs[S + 4]

        @pl.when(pl.program_id(0) == 0)
        def _():
            cp = pltpu.make_async_copy(w_hbm, w_vmem, sem)
            cp.start()
            cp.wait()

        acc = add_ref[...]
        for s in range(S):
            acc = acc + lax.dot_general(a_refs[s][...].astype(BF16), w_vmem[:, s * width:(s + 1) * width],
                                        (((1,), (1,)), ((), ())), preferred_element_type=F32)
        o_ref[...] = acc

    row = lambda C: pl.BlockSpec((tm, C), lambda i: (i, 0))
    kw = dict(name=name, grid=(M // tm,), in_specs=[row(width)] * S + [pl.BlockSpec(memory_space=pl.ANY), row(Nn)],
              scratch_shapes=[pltpu.VMEM(w.shape, w.dtype), pltpu.SemaphoreType.DMA])
    if comm is not None:
        (out,), got = _pcall_carrying(body, comm, out_specs=[row(Nn)], out_shape=[_sds((M, Nn), F32)], **kw)(*a_list, w, add)
        return out, got
    return _pcall(body, out_specs=row(Nn), out_shape=_sds((M, Nn), F32), **kw)(*a_list, w, add)


def _mm_tn_cols(a, b_list, *, name, tmo, comm=None):
    T, M = a.shape
    S = len(b_list)
    width = b_list[0].shape[1]
    tk = min(TK, T)
    nk = T // tk

    def body(*refs):
        a_ref, b_refs, o_ref, acc = refs[0], refs[1:S + 1], refs[S + 1], refs[S + 2]
        j, k = pl.program_id(1), pl.program_id(2)
        for s in range(S):
            @pl.when(j == s)
            def _():
                part = lax.dot_general(a_ref[...], b_refs[s][...], (((0,), (0,)), ((), ())), preferred_element_type=F32)

                @pl.when(k == 0)
                def _():
                    acc[...] = part

                @pl.when(k > 0)
                def _():
                    acc[...] += part

        @pl.when(k == nk - 1)
        def _():
            o_ref[...] = acc[...].astype(o_ref.dtype)

    def b_spec(s):
        return pl.BlockSpec((tk, width), lambda i, j, k: (jnp.where(j == s, k, jnp.where(j < s, 0, nk - 1)), 0))

    kw = dict(name=name, grid=(M // tmo, S, nk),
              in_specs=[pl.BlockSpec((tk, tmo), lambda i, j, k: (k, i))] + [b_spec(s) for s in range(S)],
              scratch_shapes=[pltpu.VMEM((tmo, width), F32)])
    o_spec = pl.BlockSpec((tmo, width), lambda i, j, k: (i, j))
    if comm is not None:
        (out,), got = _pcall_carrying(body, comm, out_specs=[o_spec], out_shape=[_sds((M, S * width), BF16)], **kw)(a, *b_list)
        return out, got
    return _pcall(body, out_specs=o_spec, out_shape=_sds((M, S * width), BF16), **kw)(a, *b_list)


def _mm_tn(a, b, *, name, tmo, tn=None, comm=None):
    T, M = a.shape
    Nn = b.shape[1]
    tn = Nn if tn is None else tn
    tk = min(TK, T)
    return _mm(a, b, name=name, grid=(M // tmo, Nn // tn, T // tk), trans_a=True,
               a_spec=pl.BlockSpec((tk, tmo), lambda i, j, k: (k, i)),
               b_spec=pl.BlockSpec((tk, tn), lambda i, j, k: (k, j)),
               o_spec=pl.BlockSpec((tmo, tn), lambda i, j, k: (i, j)),
               out_shape=_sds((M, Nn), BF16), acc_shape=(tmo, tn), comm=comm)


def _r2(a, tm):
    return (a, (tm, a.shape[1]), lambda i: (i, 0))


def _slab(a3, s, tm):
    return (a3, (None, tm, a3.shape[2]), lambda i: (s, i, 0))


def _o2(T, C, dtype, tm):
    return ((T, C), dtype, (tm, C), lambda i: (i, 0))


def _rows_fwd(fn, row_ins, par_ins, outs, *, name, nt):
    nr, npar = len(row_ins), len(par_ins)

    def body(*refs):
        vals = [r[...] for r in refs[:nr + npar]]
        res = fn(*vals)
        for o_ref, v in zip(refs[nr + npar:], res):
            o_ref[...] = v.astype(o_ref.dtype)

    return _pcall(body, name=name, grid=(nt,),
                  in_specs=[pl.BlockSpec(bs, im) for (_, bs, im) in row_ins] + [_full_spec(p) for p in par_ins],
                  out_specs=[pl.BlockSpec(bs, im) for (_, _, bs, im) in outs],
                  out_shape=[_sds(s, d) for (s, d, _, _) in outs])(*[r[0] for r in row_ins], *par_ins)


def _rows_bwd(fn, row_ins, par_ins, cot_ins, drow_outs, *, name, nt):
    nr, npar, nc = len(row_ins), len(par_ins), len(cot_ins)
    keep = [k for k, o in enumerate(drow_outs) if o is not None]

    def body(*refs):
        vals = [r[...].astype(F32) for r in refs[:nr + npar]]
        cots = [r[...].astype(F32) for r in refs[nr + npar:nr + npar + nc]]
        orefs = refs[nr + npar + nc:]
        _, vjp = jax.vjp(fn, *vals)
        grads = vjp(tuple(cots))
        for o_ref, k in zip(orefs[:len(keep)], keep):
            o_ref[...] = grads[k].astype(o_ref.dtype)
        prefs = orefs[len(keep):]

        @pl.when(pl.program_id(0) == 0)
        def _():
            for p_ref in prefs:
                p_ref[...] = jnp.zeros_like(p_ref)

        for p_ref, g in zip(prefs, grads[nr:]):
            p_ref[...] += g

    outs = [drow_outs[k] for k in keep]
    res = _pcall(body, name=name, grid=(nt,),
                 in_specs=[pl.BlockSpec(bs, im) for (_, bs, im) in row_ins] + [_full_spec(p) for p in par_ins]
                 + [pl.BlockSpec(bs, im) for (_, bs, im) in cot_ins],
                 out_specs=[pl.BlockSpec(bs, im) for (_, _, bs, im) in outs] + [_full_spec(p) for p in par_ins],
                 out_shape=[_sds(s, d) for (s, d, _, _) in outs] + [_sds(p.shape, F32) for p in par_ins],
                 )(*[r[0] for r in row_ins], *par_ins, *[c[0] for c in cot_ins])
    return list(res[:len(keep)]), list(res[len(keep):])


def _layer_norm(v, g, b):
    mu = jnp.mean(v, axis=-1, keepdims=True)
    var = jnp.mean(jnp.square(v - mu), axis=-1, keepdims=True)
    return (v - mu) * lax.rsqrt(var + LN_EPS) * g + b


def _silu(v):
    return v * jax.nn.sigmoid(v)


def _softplus(v):
    return jnp.maximum(v, 0.0) + jnp.log1p(jnp.exp(-jnp.abs(v)))


def _halo_of(K):
    return 8 * ((K - 1 + 7) // 8)


DW_HALO = 8
DW_RB = 16
DW_LC = 256


def _dw_taps(win, w_ref, p, ls, K, shift_of):
    acc = None
    for k in range(K):
        o = shift_of(k)
        term = win[o:o + DW_RB, :] * w_ref[p, k:k + 1, ls]
        acc = term if acc is None else acc + term
    return acc


def _dwconv_fwd(post, part_ins, w, biases, outs, *, K, C, name, tm, nt):
    nparts, halo = len(part_ins), DW_HALO

    def body(*refs):
        x_refs, w_ref = refs[:nparts], refs[nparts]
        b_refs = refs[nparts + 1:2 * nparts + 1]
        orefs, buf = refs[2 * nparts + 1:-1], refs[-1]
        i = pl.program_id(0)
        for p in range(nparts):
            @pl.when(i == 0)
            def _():
                buf[p, pl.ds(0, halo), :] = jnp.zeros((halo, C), F32)

            @pl.when(i > 0)
            def _():
                buf[p, pl.ds(0, halo), :] = buf[p, pl.ds(tm, halo), :]

            buf[p, pl.ds(halo, tm), :] = x_refs[p][...]

        def group(r, carry):
            base = pl.multiple_of(r * DW_RB, DW_RB)
            for cj in range(C // DW_LC):
                ls = slice(cj * DW_LC, (cj + 1) * DW_LC)
                cs = [_dw_taps(buf[p, pl.ds(base, DW_RB + halo), ls], w_ref, p, ls, K, lambda k: halo - (K - 1) + k)
                      + b_refs[p][:, ls] for p in range(nparts)]
                for o_ref, v in zip(orefs, post(cs)):
                    o_ref[pl.ds(base, DW_RB), ls] = v.astype(o_ref.dtype)
            return carry

        lax.fori_loop(0, tm // DW_RB, group, 0)

    return _pcall(body, name=name, grid=(nt,),
                  in_specs=[pl.BlockSpec(bs, im) for (_, bs, im) in part_ins] + [_full_spec(w)] + [_full_spec(b) for b in biases],
                  out_specs=[pl.BlockSpec(bs, im) for (_, _, bs, im) in outs],
                  out_shape=[_sds(s, d) for (s, d, _, _) in outs],
                  scratch_shapes=[pltpu.VMEM((nparts, halo + tm, C), F32)])(*[r[0] for r in part_ins], w, *biases)


def _dwconv_bwd(post, part_ins, halo_ins, w, biases, cot_ins, *, K, C, name, tm, nt):
    nparts, halo, nc, RB = len(part_ins), DW_HALO, len(cot_ins), DW_RB
    T = nt * tm

    def body(*refs):
        x_refs, h_refs, w_ref = refs[:nparts], refs[nparts:2 * nparts], refs[2 * nparts]
        b_refs = refs[2 * nparts + 1:3 * nparts + 1]
        cot_refs = refs[3 * nparts + 1:3 * nparts + 1 + nc]
        rest = refs[3 * nparts + 1 + nc:]
        dx_refs, dw_ref, db_refs = rest[:nparts], rest[nparts], rest[nparts + 1:2 * nparts + 1]
        bufx, bufd, acc = rest[-3], rest[-2], rest[-1]
        s = pl.program_id(0)
        first_tile = s == nt - 1

        @pl.when(s == 0)
        def _():
            acc[...] = jnp.zeros_like(acc)
            for p in range(nparts):
                bufd[p, pl.ds(tm, halo), :] = jnp.zeros((halo, C), F32)

        for p in range(nparts):
            bufx[p, pl.ds(0, halo), :] = jnp.where(first_tile, 0.0, h_refs[p][...])
            bufx[p, pl.ds(halo, tm), :] = x_refs[p][...]
        fold = lambda v: v[0:8, :] + v[8:16, :]

        def conv_out_grads(r, carry):
            base = pl.multiple_of(r * RB, RB)
            for cj in range(C // DW_LC):
                ls = slice(cj * DW_LC, (cj + 1) * DW_LC)
                wins = [bufx[p, pl.ds(base, RB + halo), ls] for p in range(nparts)]
                cs = [_dw_taps(wins[p], w_ref, p, ls, K, lambda k: halo - (K - 1) + k) + b_refs[p][:, ls]
                      for p in range(nparts)]
                _, vjp = jax.vjp(lambda *c: post(list(c)), *cs)
                dcs = vjp(tuple(cr[pl.ds(base, RB), ls].astype(F32) for cr in cot_refs))
                for p in range(nparts):
                    bufd[p, pl.ds(base, RB), ls] = dcs[p]
                    for k in range(K):
                        o = halo - (K - 1) + k
                        acc[p, k, :, ls] += fold(dcs[p] * wins[p][o:o + RB, :])
                    acc[p, K, :, ls] += fold(dcs[p])
            return carry

        lax.fori_loop(0, tm // RB, conv_out_grads, 0)

        def input_grads(r, carry):
            base = pl.multiple_of(r * RB, RB)
            for cj in range(C // DW_LC):
                ls = slice(cj * DW_LC, (cj + 1) * DW_LC)
                for p in range(nparts):
                    dx = _dw_taps(bufd[p, pl.ds(base, RB + halo), ls], w_ref, p, ls, K, lambda k: K - 1 - k)
                    dx_refs[p][pl.ds(base, RB), ls] = dx.astype(dx_refs[p].dtype)
            return carry

        lax.fori_loop(0, tm // RB, input_grads, 0)
        for p in range(nparts):
            bufd[p, pl.ds(tm, halo), :] = bufd[p, pl.ds(0, halo), :]

        @pl.when(s == nt - 1)
        def _():
            dw_ref[...] = jnp.zeros_like(dw_ref)
            for p in range(nparts):
                for k in range(K):
                    dw_ref[p, k:k + 1, :] = jnp.sum(acc[p, k], axis=0, keepdims=True)
                db_refs[p][...] = jnp.sum(acc[p, K], axis=0, keepdims=True)

    rev = lambda im: (lambda s: im(nt - 1 - s))
    row = pl.BlockSpec((tm, C), lambda s: (nt - 1 - s, 0))
    res = _pcall(body, name=name, grid=(nt,),
                 in_specs=[pl.BlockSpec(bs, rev(im)) for (_, bs, im) in part_ins]
                 + [pl.BlockSpec(bs, rev(im)) for (_, bs, im) in halo_ins]
                 + [_full_spec(w)] + [_full_spec(b) for b in biases]
                 + [pl.BlockSpec(bs, rev(im)) for (_, bs, im) in cot_ins],
                 out_specs=[row] * nparts + [_full_spec(w)] + [_full_spec(b) for b in biases],
                 out_shape=[_sds((T, C), BF16)] * nparts + [_sds(w.shape, F32)] + [_sds(b.shape, F32) for b in biases],
                 scratch_shapes=[pltpu.VMEM((nparts, halo + tm, C), F32), pltpu.VMEM((nparts, tm + halo, C), F32),
                                 pltpu.VMEM((nparts, K + 1, 8, C), F32)],
                 )(*[r[0] for r in part_ins], *[r[0] for r in halo_ins], w, *biases, *[c[0] for c in cot_ins])
    return list(res[:nparts]), res[nparts], list(res[nparts + 1:])


def _halo_slab(a3, s, tm, halo):
    q = tm // halo
    return (a3, (None, halo, a3.shape[2]), lambda i: (s, jnp.maximum(i * q - 1, 0), 0))


CONF_HALO = _halo_of(CONV_K)
CONF_RB = 32


def _shifted_copies(buf, shifted, rows):
    for j in range(1, 8):
        shifted[j - 1, pl.ds(0, rows), :] = buf[pl.ds(j, rows), :]


def _shifted_rows(buf, shifted, s, base, nrows):
    j, q = s % 8, s // 8
    if j == 0:
        return buf[pl.ds(base + 8 * q, nrows), :]
    return shifted[j - 1, pl.ds(base + 8 * q, nrows), :]


def _conf_fwd(u3, w, cb, lg, lb, *, name):
    T = u3.shape[1]
    tm = min(TM_X, T)
    nt = T // tm
    K, halo, RB = CONV_K, CONF_HALO, min(CONF_RB, tm)

    def body(a_ref, g_ref, w_ref, cb_ref, lg_ref, lb_ref, v3_ref, c_ref, bufx, xs):
        i = pl.program_id(0)

        @pl.when(i == 0)
        def _():
            bufx[pl.ds(0, halo), :] = jnp.zeros((halo, D), F32)

        @pl.when(i > 0)
        def _():
            bufx[pl.ds(0, halo), :] = bufx[pl.ds(tm, halo), :]

        bufx[pl.ds(halo, tm), :] = a_ref[...] * jax.nn.sigmoid(g_ref[...])
        _shifted_copies(bufx, xs, halo + tm - 8)

        def group(r, carry):
            base = pl.multiple_of(r * RB, RB)
            acc = None
            for k in range(K):
                term = _shifted_rows(bufx, xs, halo - (K - 1) + k, base, RB) * w_ref[k:k + 1, :]
                acc = term if acc is None else acc + term
            c_ref[pl.ds(base, RB), :] = acc
            return carry

        lax.fori_loop(0, tm // RB, group, 0)
        v3_ref[...] = _conf_post([c_ref[...]], cb_ref[...], lg_ref[...], lb_ref[...])[0].astype(v3_ref.dtype)

    slab = lambda s: pl.BlockSpec((None, tm, D), lambda i: (s, i, 0))
    row = pl.BlockSpec((tm, D), lambda i: (i, 0))
    pars = [w, cb, lg, lb]
    return _pcall(body, name=name, grid=(nt,), in_specs=[slab(0), slab(1)] + [_full_spec(p) for p in pars],
                  out_specs=[row, row], out_shape=[_sds((T, D), BF16), _sds((T, D), F32)],
                  scratch_shapes=[pltpu.VMEM((halo + tm, D), F32), pltpu.VMEM((7, halo + tm - 8, D), F32)],
                  )(u3, u3, *pars)


def _conf_bwd(u3, c, dv3, w, cb, lg, lb, *, name):
    T = u3.shape[1]
    tm = min(TM_X, T)
    nt = T // tm
    K, halo, RB = CONV_K, CONF_HALO, min(CONF_RB, tm)
    q = tm // halo

    def body(a_ref, g_ref, ah_ref, gh_ref, c_ref, dv3_ref, w_ref, cb_ref, lg_ref, lb_ref,
             da_ref, dg_ref, dw_ref, dcb_ref, dlg_ref, dlb_ref, bufx, xs, bufd, ds, dv0):
        s = pl.program_id(0)
        first_tile = s == nt - 1

        @pl.when(s == 0)
        def _():
            for r in (dw_ref, dcb_ref, dlg_ref, dlb_ref):
                r[...] = jnp.zeros_like(r)
            bufd[pl.ds(tm, halo), :] = jnp.zeros((halo, D), F32)

        a, g = a_ref[...], g_ref[...]
        xin, pre_vjp = jax.vjp(lambda p, q_: _conf_pre(p, q_)[0], a, g)
        bufx[pl.ds(0, halo), :] = jnp.where(first_tile, 0.0, _conf_pre(ah_ref[...], gh_ref[...])[0])
        bufx[pl.ds(halo, tm), :] = xin
        _shifted_copies(bufx, xs, halo + tm - 8)

        _, post_vjp = jax.vjp(lambda cc, b_, g_, l_: _conf_post([cc], b_, g_, l_)[0],
                              c_ref[...], cb_ref[...], lg_ref[...], lb_ref[...])
        dc, dcb, dlg, dlb = post_vjp(dv3_ref[...])
        dcb_ref[...] += dcb
        dlg_ref[...] += dlg
        dlb_ref[...] += dlb
        bufd[pl.ds(0, tm), :] = dc
        _shifted_copies(bufd, ds, tm + halo - 8)

        def dx_group(r, carry):
            base = pl.multiple_of(r * RB, RB)
            acc = None
            for k in range(K):
                term = _shifted_rows(bufd, ds, K - 1 - k, base, RB) * w_ref[k:k + 1, :]
                acc = term if acc is None else acc + term
            dv0[pl.ds(base, RB), :] = acc
            return carry

        lax.fori_loop(0, tm // RB, dx_group, 0)

        for k in range(K):
            def dw_group(r, acc):
                base = pl.multiple_of(r * RB, RB)
                prod = bufd[pl.ds(base, RB), :] * _shifted_rows(bufx, xs, halo - (K - 1) + k, base, RB)
                for v in range(RB // 8):
                    acc = acc + prod[v * 8:(v + 1) * 8, :]
                return acc

            acc = lax.fori_loop(0, tm // RB, dw_group, jnp.zeros((8, D), F32))
            dw_ref[k:k + 1, :] += jnp.sum(acc, axis=0, keepdims=True)

        bufd[pl.ds(tm, halo), :] = bufd[pl.ds(0, halo), :]
        da, dg = pre_vjp(dv0[...])
        da_ref[...] = da.astype(da_ref.dtype)
        dg_ref[...] = dg.astype(dg_ref.dtype)

    slab = lambda sl: pl.BlockSpec((None, tm, D), lambda s: (sl, nt - 1 - s, 0))
    hslab = lambda sl: pl.BlockSpec((None, halo, D), lambda s: (sl, jnp.maximum((nt - 1 - s) * q - 1, 0), 0))
    row = pl.BlockSpec((tm, D), lambda s: (nt - 1 - s, 0))
    pars = [w, cb, lg, lb]
    res = _pcall(body, name=name, grid=(nt,),
                 in_specs=[slab(0), slab(1), hslab(0), hslab(1), row, row] + [_full_spec(p) for p in pars],
                 out_specs=[row, row] + [_full_spec(p) for p in pars],
                 out_shape=[_sds((T, D), BF16)] * 2 + [_sds(p.shape, F32) for p in pars],
                 scratch_shapes=[pltpu.VMEM((halo + tm, D), F32), pltpu.VMEM((7, halo + tm - 8, D), F32),
                                 pltpu.VMEM((tm + halo, D), F32), pltpu.VMEM((7, tm + halo - 8, D), F32),
                                 pltpu.VMEM((tm, D), F32)],
                 )(u3, u3, u3, u3, c, dv3, *pars)
    return res


def _dg(a, b, ca, cb):
    return lax.dot_general(a.astype(BF16), b.astype(BF16), (((ca,), (cb,)), ((), ())), preferred_element_type=F32)


@jax.custom_vjp
def _dot_nn(a, b):
    return _dg(a, b, 1, 0)


_dot_nn.defvjp(lambda a, b: (_dg(a, b, 1, 0), (a, b)),
               lambda res, g: (_dg(g, res[1], 1, 1), _dg(res[0], g, 0, 0)))


@jax.custom_vjp
def _dot_nt(a, b):
    return _dg(a, b, 1, 1)


_dot_nt.defvjp(lambda a, b: (_dg(a, b, 1, 1), (a, b)),
               lambda res, g: (_dg(g, res[1], 1, 0), _dg(g, res[0], 0, 0)))


@jax.custom_vjp
def _dot_tn(a, b):
    return _dg(a, b, 0, 0)


_dot_tn.defvjp(lambda a, b: (_dg(a, b, 0, 0), (a, b)),
               lambda res, g: (_dg(res[1], g, 1, 1), _dg(res[0], g, 1, 0)))


def _split3(v):
    hi = v.astype(BF16)
    r = v - hi.astype(F32)
    mid = r.astype(BF16)
    return hi, mid, (r - mid.astype(F32)).astype(BF16)


def _x01(v, m, cv, cm, m_left=False):
    acc = None
    for piece in _split3(v):
        t = _dg(m, piece, cm, cv) if m_left else _dg(piece, m, cv, cm)
        acc = t if acc is None else acc + t
    return acc


@jax.custom_vjp
def _expand01(v, m):
    return _x01(v, m, 1, 0)


_expand01.defvjp(lambda v, m: (_x01(v, m, 1, 0), m),
                 lambda m, g: (_x01(g, m, 1, 1), jnp.zeros_like(m)))


@jax.custom_vjp
def _mix01(m, v):
    return _x01(v, m, 0, 1, m_left=True)


_mix01.defvjp(lambda m, v: (_x01(v, m, 0, 1, m_left=True), m),
              lambda m, g: (jnp.zeros_like(m), _x01(g, m, 0, 0, m_left=True)))


def _causal():
    return lax.broadcasted_iota(jnp.int32, (L, L), 0) >= lax.broadcasted_iota(jnp.int32, (L, L), 1)


def _ssd_chunk_prep(dtr, alog, dtb):
    dt = _softplus(dtr + dtb)
    a_cs = _mix01(_causal().astype(F32), dt * (-jnp.exp(alog)))
    return dt, a_cs, a_cs.T


def _ssd_group(xs, dt, a_cs, a_csT, Bg, Cg, zg, sp, dsk, nwg, *, g):
    causal = _causal()
    hi = lax.broadcasted_iota(jnp.int32, (HP, RP), 0)
    ci = lax.broadcasted_iota(jnp.int32, (HP, RP), 1)
    lo = (hi - g * R) * P
    E = ((ci >= lo) & (ci < lo + P)).astype(F32)

    acs_e = _expand01(a_cs, E)
    dt_e = _expand01(dt, E)
    alast_e = acs_e[L - 1:L, :]
    xdt = xs * dt_e
    cb = _dot_nt(Cg, Bg)
    y_off = _dot_nn(Cg, sp) * jnp.exp(acs_e)
    yd = []
    for r in range(R):
        h = g * R + r
        seg = a_cs[:, h:h + 1] - a_csT[h:h + 1, :]
        dec = jnp.where(causal, jnp.exp(jnp.where(causal, seg, 0.0)), 0.0)
        yd.append(_dot_nn(cb * dec, xdt[:, r * P:(r + 1) * P]))
    y = jnp.concatenate(yd, axis=1) + y_off + xs * _expand01(jnp.broadcast_to(dsk, (8, HP)), E)[0:1, :]
    yg = y * _silu(zg)
    yn = yg * lax.rsqrt(jnp.mean(jnp.square(yg), axis=-1, keepdims=True) + RMS_EPS) * nwg
    sc = _dot_tn(Bg, xdt * jnp.exp(alast_e - acs_e))
    return yn, jnp.exp(alast_e) * sp + sc


def _group_cols(g):
    return g // 2, (g % 2) * RP


def _ssd_fwd(x0, x1, bc, dtr, u3, alog, dtb, dsk, nw, *, name):
    T = x0.shape[0]
    nc = T // L

    def body(x0_ref, x1_ref, bc_ref, dtr_ref, z0_ref, z1_ref, alog_ref, dtb_ref, dsk_ref, nw_ref, yn_ref, sp_ref, S):
        @pl.when(pl.program_id(0) == 0)
        def _():
            S[...] = jnp.zeros_like(S)

        xr, zr = (x0_ref, x1_ref), (z0_ref, z1_ref)
        dt, a_cs, a_csT = _ssd_chunk_prep(dtr_ref[...], alog_ref[...], dtb_ref[...])
        for g in range(G):
            s, off = _group_cols(g)
            sp = S[g]
            sp_ref[0, g] = sp
            yn, s_next = _ssd_group(xr[s][:, off:off + RP], dt, a_cs, a_csT, bc_ref[:, g * N:(g + 1) * N],
                                    bc_ref[:, G * N + g * N:G * N + (g + 1) * N], zr[s][:, off:off + RP], sp,
                                    dsk_ref[...], nw_ref[:, g * RP:(g + 1) * RP], g=g)
            yn_ref[:, g * RP:(g + 1) * RP] = yn.astype(yn_ref.dtype)
            S[g] = s_next

    row = lambda C: pl.BlockSpec((L, C), lambda c: (c, 0))
    zspec = lambda s: pl.BlockSpec((None, L, D), lambda c: (s, c, 0))
    pars = [alog, dtb, dsk, nw]
    return _pcall(body, name=name, grid=(nc,),
                  in_specs=[row(D), row(D), row(D), row(HP), zspec(2), zspec(3)] + [_full_spec(p) for p in pars],
                  out_specs=[row(SD), pl.BlockSpec((1, G, N, RP), lambda c: (c, 0, 0, 0))],
                  out_shape=[_sds((T, SD), BF16), _sds((nc, G, N, RP), F32)],
                  scratch_shapes=[pltpu.VMEM((G, N, RP), F32)])(x0, x1, bc, dtr, u3, u3, *pars)


def _ssd_bwd(x0, x1, bc, dtr, u3, sprev, dyn, alog, dtb, dsk, nw, *, name):
    T = x0.shape[0]
    nc = T // L

    def body(x0_ref, x1_ref, bc_ref, dtr_ref, z0_ref, z1_ref, sp_ref, dyn_ref, alog_ref, dtb_ref, dsk_ref, nw_ref,
             dx0_ref, dx1_ref, dbc_ref, ddtr_ref, dz0_ref, dz1_ref, dalog_ref, ddtb_ref, ddsk_ref, dnw_ref, dS):
        @pl.when(pl.program_id(0) == 0)
        def _():
            dS[...] = jnp.zeros_like(dS)
            for r in (dalog_ref, ddtb_ref, ddsk_ref, dnw_ref):
                r[...] = jnp.zeros_like(r)

        xr, zr = (x0_ref, x1_ref), (z0_ref, z1_ref)
        dxr, dzr = (dx0_ref, dx1_ref), (dz0_ref, dz1_ref)
        (dt, a_cs, a_csT), prep_vjp = jax.vjp(_ssd_chunk_prep, dtr_ref[...], alog_ref[...], dtb_ref[...])
        d_dt, d_acs, d_acsT = jnp.zeros((L, HP), F32), jnp.zeros((L, HP), F32), jnp.zeros((HP, L), F32)
        for g in range(G):
            s, off = _group_cols(g)
            _, vjp = jax.vjp(functools.partial(_ssd_group, g=g), xr[s][:, off:off + RP], dt, a_cs, a_csT,
                             bc_ref[:, g * N:(g + 1) * N], bc_ref[:, G * N + g * N:G * N + (g + 1) * N],
                             zr[s][:, off:off + RP], sp_ref[0, g], dsk_ref[...], nw_ref[:, g * RP:(g + 1) * RP])
            dxs, ddt_g, dacs_g, dacsT_g, dB, dC, dz, dsp, dds, dnwg = vjp((dyn_ref[:, g * RP:(g + 1) * RP], dS[g]))
            dxr[s][:, off:off + RP] = dxs
            dzr[s][:, off:off + RP] = dz.astype(dz0_ref.dtype)
            dbc_ref[:, g * N:(g + 1) * N] = dB
            dbc_ref[:, G * N + g * N:G * N + (g + 1) * N] = dC
            dS[g] = dsp
            d_dt, d_acs, d_acsT = d_dt + ddt_g, d_acs + dacs_g, d_acsT + dacsT_g
            ddsk_ref[...] += dds
            dnw_ref[:, g * RP:(g + 1) * RP] += dnwg
        ddtr, dal, ddb = prep_vjp((d_dt, d_acs, d_acsT))
        dalog_ref[...] += dal
        ddtb_ref[...] += ddb
        ddtr_ref[...] = ddtr.astype(ddtr_ref.dtype)

    row = lambda C: pl.BlockSpec((L, C), lambda c: (nc - 1 - c, 0))
    zspec = lambda s: pl.BlockSpec((None, L, D), lambda c: (s, nc - 1 - c, 0))
    pars = [alog, dtb, dsk, nw]
    return _pcall(body, name=name, grid=(nc,),
                  in_specs=[row(D), row(D), row(D), row(HP), zspec(2), zspec(3),
                            pl.BlockSpec((1, G, N, RP), lambda c: (nc - 1 - c, 0, 0, 0)), row(SD)]
                  + [_full_spec(p) for p in pars],
                  out_specs=[row(D), row(D), row(D), row(HP), row(D), row(D)] + [_full_spec(p) for p in pars],
                  out_shape=[_sds((T, D), F32)] * 3 + [_sds((T, HP), BF16), _sds((T, D), BF16), _sds((T, D), BF16)]
                  + [_sds(p.shape, F32) for p in pars],
                  scratch_shapes=[pltpu.VMEM((G, N, RP), F32)])(x0, x1, bc, dtr, u3, u3, sprev, dyn, *pars)


def _loss_head(y, target, *, name):
    T = y.shape[0]
    tm = min(TM, T)

    def body(y_ref, t_ref, loss_ref, dy_ref):
        e = y_ref[...] - t_ref[...]
        dy_ref[...] = e * (1.0 / D)

        @pl.when(pl.program_id(0) == 0)
        def _():
            loss_ref[...] = jnp.zeros_like(loss_ref)

        loss_ref[...] += 0.5 * jnp.sum(jnp.mean(jnp.square(e), axis=-1, keepdims=True), axis=0, keepdims=True)

    row = pl.BlockSpec((tm, D), lambda i: (i, 0))
    return _pcall(body, name=name, grid=(T // tm,), in_specs=[row, row],
                  out_specs=[pl.BlockSpec((1, 128), lambda i: (0, 0)), row],
                  out_shape=[_sds((1, 128), F32), _sds((T, D), F32)])(y, target)


_HBM = pl.BlockSpec(memory_space=pltpu.HBM)
_MESH = pl.DeviceIdType.MESH


def _exchange_comm(bufs, *, scatter):
    nb = len(bufs)

    def copies(in_refs, out_refs, sems, with_arrivals):
        send_sems, recv_sems, local_sems = sems
        x, y, c = lax.axis_index("x"), lax.axis_index("y"), lax.axis_index("c")
        me = 2 * x + y
        peers = [(1 - x, y), (x, 1 - y), (1 - x, 1 - y)]
        own, sends, arrivals = [], [], []
        for b in range(nb):
            src_own = in_refs[b].at[me] if scatter else in_refs[b]
            own.append(pltpu.make_async_copy(src_own, out_refs[b].at[me], local_sems.at[b]))
            for k, (px, py) in enumerate(peers):
                src = in_refs[b].at[2 * px + py] if scatter else in_refs[b]
                sends.append(pltpu.make_async_remote_copy(
                    src_ref=src, dst_ref=out_refs[b].at[me], send_sem=send_sems.at[b, k], recv_sem=recv_sems.at[b, k],
                    device_id=(px, py, c), device_id_type=_MESH))
                if with_arrivals:
                    slot = out_refs[b].at[2 * px + py]
                    arrivals.append(pltpu.make_async_remote_copy(
                        src_ref=slot, dst_ref=slot, send_sem=send_sems.at[b, k], recv_sem=recv_sems.at[b, k],
                        device_id=(px, py, c), device_id_type=_MESH))
        return own, sends, arrivals

    def start(in_refs, out_refs, sems):
        own, sends, _ = copies(in_refs, out_refs, sems, False)
        for cp in own + sends:
            cp.start()

    def wait(in_refs, out_refs, sems):
        own, sends, arrivals = copies(in_refs, out_refs, sems, True)
        for cp in arrivals:
            cp.wait_recv()
        for cp in sends:
            cp.wait_send()
        for cp in own:
            cp.wait()

    return dict(ins=list(bufs), outs=[_sds(b.shape if scatter else (NCHIP,) + b.shape, b.dtype) for b in bufs],
                sems=[pltpu.SemaphoreType.DMA((nb, 3)), pltpu.SemaphoreType.DMA((nb, 3)), pltpu.SemaphoreType.DMA((nb,))],
                start=start, wait=wait)


def _chip_exchange(bufs, *, scatter, name):
    comm = _exchange_comm(bufs, scatter=scatter)
    nb = len(bufs)

    def body(*refs):
        comm["start"](refs[:nb], refs[nb:2 * nb], refs[2 * nb:])
        comm["wait"](refs[:nb], refs[nb:2 * nb], refs[2 * nb:])

    return _ccall(body, name=name, in_specs=[_HBM] * nb, out_specs=[_HBM] * nb, out_shape=comm["outs"],
                  scratch_shapes=comm["sems"])(*bufs)


def _core_swap(bufs, *, name):
    nb = len(bufs)

    def body(*refs):
        in_refs, out_refs, send_sems, recv_sems = refs[:nb], refs[nb:2 * nb], refs[2 * nb], refs[2 * nb + 1]
        x, y, c = lax.axis_index("x"), lax.axis_index("y"), lax.axis_index("c")
        cps = [pltpu.make_async_remote_copy(src_ref=in_refs[b], dst_ref=out_refs[b], send_sem=send_sems.at[b],
                                            recv_sem=recv_sems.at[b], device_id=(x, y, 1 - c), device_id_type=_MESH)
               for b in range(nb)]
        for cp in cps:
            cp.start()
        for cp in cps:
            cp.wait()

    return _ccall(body, name=name, in_specs=[_HBM] * nb, out_specs=[_HBM] * nb,
                  out_shape=[_sds(b.shape, b.dtype) for b in bufs],
                  scratch_shapes=[pltpu.SemaphoreType.DMA((nb,)), pltpu.SemaphoreType.DMA((nb,))])(*bufs)


def _all_gather8(buf, *, name):
    def body(in_ref, out_ref, send_sems, recv_sems, local_sem):
        x, y, c = lax.axis_index("x"), lax.axis_index("y"), lax.axis_index("c")
        me = 4 * x + 2 * y + c
        own = pltpu.make_async_copy(in_ref, out_ref.at[me], local_sem)
        own.start()
        flips = [(fx, fy, fc) for fx in (0, 1) for fy in (0, 1) for fc in (0, 1)][1:]
        peers = [(x ^ fx, y ^ fy, c ^ fc) for fx, fy, fc in flips]
        sends = []
        for k, peer in enumerate(peers):
            cp = pltpu.make_async_remote_copy(src_ref=in_ref, dst_ref=out_ref.at[me], send_sem=send_sems.at[k],
                                              recv_sem=recv_sems.at[k], device_id=peer, device_id_type=_MESH)
            cp.start()
            sends.append(cp)
        for k, (px, py, pc) in enumerate(peers):
            slot = out_ref.at[4 * px + 2 * py + pc]
            pltpu.make_async_remote_copy(src_ref=slot, dst_ref=slot, send_sem=send_sems.at[k], recv_sem=recv_sems.at[k],
                                         device_id=(px, py, pc), device_id_type=_MESH).wait_recv()
        for cp in sends:
            cp.wait_send()
        own.wait()

    return _ccall(body, name=name, in_specs=[_HBM], out_specs=_HBM, out_shape=_sds((8,) + buf.shape, buf.dtype),
                  scratch_shapes=[pltpu.SemaphoreType.DMA((7,)), pltpu.SemaphoreType.DMA((7,)), pltpu.SemaphoreType.DMA])(buf)


def _row_tile(rows, cap):
    if rows <= cap:
        return rows
    return max(t for t in range(16, cap + 1, 16) if rows % t == 0)


def _sum_slots(stack, *, name, cap=256):
    S, Rr, C = stack.shape
    tr = _row_tile(Rr, cap)

    def body(s_ref, o_ref):
        acc = s_ref[0].astype(F32)
        for j in range(1, S):
            acc = acc + s_ref[j].astype(F32)
        o_ref[...] = acc

    return _pcall(body, name=name, grid=(Rr // tr,), in_specs=[pl.BlockSpec((S, tr, C), lambda i: (0, i, 0))],
                  out_specs=pl.BlockSpec((tr, C), lambda i: (i, 0)), out_shape=_sds((Rr, C), F32))(stack)


def _adamw(g_parts, w, m, v, *, name, cap=128):
    Rr, C = w.shape
    tr = _row_tile(Rr, cap)
    ng = len(g_parts)
    c1 = 1.0 / (1.0 - ADAM_B1 ** ADAM_STEP)
    c2 = 1.0 / (1.0 - ADAM_B2 ** ADAM_STEP)

    def body(*refs):
        g = refs[0][...]
        for r in refs[1:ng]:
            g = g + r[...]
        w_ref, m_ref, v_ref, g_out, d_out, m_out, v_out = refs[ng:]
        mn = ADAM_B1 * m_ref[...] + (1.0 - ADAM_B1) * g
        vn = ADAM_B2 * v_ref[...] + (1.0 - ADAM_B2) * jnp.square(g)
        g_out[...] = g
        m_out[...] = mn
        v_out[...] = vn
        d_out[...] = -ADAM_LR * ((mn * c1) / (jnp.sqrt(vn * c2) + ADAM_EPS) + ADAM_WD * w_ref[...])

    spec = pl.BlockSpec((tr, C), lambda i: (i, 0))
    return _pcall(body, name=name, grid=(Rr // tr,), in_specs=[spec] * (ng + 3), out_specs=[spec] * 4,
                  out_shape=[_sds((Rr, C), F32)] * 4)(*g_parts, w, m, v)


def _adamw_layers(mine, other, w3, m3, v3, *, name, cap=128):
    _, Rr, C = w3.shape
    tr = _row_tile(Rr, cap)
    nt = Rr // tr
    c1 = 1.0 / (1.0 - ADAM_B1 ** ADAM_STEP)
    c2 = 1.0 / (1.0 - ADAM_B2 ** ADAM_STEP)

    def body(m0, m1, o0, o1, w_ref, m_ref, v_ref, g_out, d_out, m_out, v_out):
        g = jnp.where(pl.program_id(0) == 0, m0[...] + o0[...], m1[...] + o1[...])
        mn = ADAM_B1 * m_ref[...] + (1.0 - ADAM_B1) * g
        vn = ADAM_B2 * v_ref[...] + (1.0 - ADAM_B2) * jnp.square(g)
        g_out[...] = g
        m_out[...] = mn
        v_out[...] = vn
        d_out[...] = -ADAM_LR * ((mn * c1) / (jnp.sqrt(vn * c2) + ADAM_EPS) + ADAM_WD * w_ref[...])

    g0 = pl.BlockSpec((tr, C), lambda l, i: (jnp.where(l == 0, i, nt - 1), 0))
    g1 = pl.BlockSpec((tr, C), lambda l, i: (jnp.where(l == 1, i, 0), 0))
    s3 = pl.BlockSpec((None, tr, C), lambda l, i: (l, i, 0))
    return _pcall(body, name=name, grid=(2, nt), in_specs=[g0, g1, g0, g1, s3, s3, s3], out_specs=[s3] * 4,
                  out_shape=[_sds(w3.shape, F32)] * 4)(mine[0], mine[1], other[0], other[1], w3, m3, v3)


def _pack(arrs, dtype, row_mult):
    flat = jnp.concatenate([a.reshape(-1).astype(dtype) for a in arrs])
    n = flat.shape[0]
    unit = row_mult * PACK_W
    total = unit * ((n + unit - 1) // unit)
    if total > n:
        flat = jnp.concatenate([flat, jnp.zeros((total - n,), dtype)])
    return flat.reshape(-1, PACK_W)


def _unpack(buf, shapes):
    flat = buf.reshape(-1)
    out, off = [], 0
    for s in shapes:
        n = math.prod(s)
        out.append(flat[off:off + n].reshape(s))
        off += n
    return out


def _conf_pre(a, g):
    return [a * jax.nn.sigmoid(g)]


def _conf_post(cs, cb, lg, lb):
    return (_silu(_layer_norm(cs[0] + cb, lg, lb)),)


def _xbc_post(cs):
    return tuple(_silu(c) for c in cs)


def _ffn_post(cs):
    return (_silu(cs[0]) * cs[1],)


def _mix_fn(ga, gb, ya, yb):
    return (jax.nn.sigmoid(ga) * ya + jax.nn.sigmoid(gb) * yb,)


def _res_ln_fn(h, r, g, b):
    return (_layer_norm(ALPHA * h + r, g, b),)


def _ln_fn(x, g, b):
    return (_layer_norm(x, g, b),)


def _carrying(carry, key, gr, call):
    if key not in carry:
        return call(None)
    comm, done = carry[key](gr)
    out, got = call(comm)
    done(got)
    return out


def _layer_fwd(h, hb, W, l, carry):
    T = h.shape[0]
    tm = min(TM, T)
    nt = T // tm
    tmf = min(TM_FFN, T)
    ntf = T // tmf
    nm = lambda s: f"l{l}_{s}"
    u3 = _carrying(carry, "u", None,
                   lambda comm: _mm_resident_slab_out(hb, W["w_p"], name=nm("u"), width=D, tm=TM_X, comm=comm))
    dtr = _mm_nn(hb, W["w_dt"], name=nm("dt"))
    v3, cconv = _conf_fwd(u3, W["conv_w"], W["conv_b"], W["conv_ln_g"], W["conv_ln_b"], name=nm("conf"))
    ya = _mm_nn(v3, W["w_co"], name=nm("ya"))
    tmx = min(TM_X, T)
    x0, x1, bc = _dwconv_fwd(_xbc_post, [_slab(u3, 6, tmx), _slab(u3, 7, tmx), _slab(u3, 8, tmx)], W["ssm_w"],
                             W["ssm_b"], [_o2(T, D, F32, tmx)] * 3, K=SSM_K, C=D, name=nm("xbc"), tm=tmx, nt=T // tmx)
    yn, sprev = _ssd_fwd(x0, x1, bc, dtr, u3, W["a_log"], W["dt_bias"], W["d_skip"], W["norm_w"], name=nm("ssd"))
    yb = _mm_nn(yn, W["w_so"], name=nm("yb"), tk=min(SD, 1024))
    (m,) = _rows_fwd(_mix_fn, [_slab(u3, 4, tm), _slab(u3, 5, tm), _r2(ya, tm), _r2(yb, tm)], [],
                     [_o2(T, D, BF16, tm)], name=nm("mix"), nt=nt)
    mix = _mm_nn(m, W["w_o"], name=nm("wo"))
    h1, h1b = _rows_fwd(lambda a, r, g, b: _res_ln_fn(a, r, g, b) * 2, [_r2(h, tm), _r2(mix, tm)],
                        [W["ln1_g"], W["ln1_b"]], [_o2(T, D, F32, tm), _o2(T, D, BF16, tm)], name=nm("ln1"), nt=nt)
    up3 = _carrying(carry, "up", None,
                    lambda comm: _mm_resident_slab_out(h1b, W["w_up"], name=nm("up"), width=FFN, tm=TM, comm=comm))
    (f,) = _dwconv_fwd(_ffn_post, [_slab(up3, 0, tmf), _slab(up3, 1, tmf)], W["ffn_w"], W["ffn_b"],
                       [_o2(T, FFN, BF16, tmf)], K=FFN_K, C=FFN, name=nm("ffnact"), tm=tmf, nt=ntf)
    ffn = _mm_nn(f, W["w_dn"], name=nm("dn"))
    h2, h2b = _rows_fwd(lambda a, r, g, b: _res_ln_fn(a, r, g, b) * 2, [_r2(h1, tm), _r2(ffn, tm)],
                        [W["ln2_g"], W["ln2_b"]], [_o2(T, D, F32, tm), _o2(T, D, BF16, tm)], name=nm("ln2"), nt=nt)
    saved = dict(h=h, hb=hb, u3=u3, dtr=dtr, v3=v3, cconv=cconv, ya=ya, x0=x0, x1=x1, bc=bc, sprev=sprev, yn=yn, yb=yb, m=m,
                 mix=mix, h1=h1, h1b=h1b, up3=up3, f=f, ffn=ffn)
    return h2, h2b, saved


def _layer_bwd(dh2, W, sv, l, carry):
    T = dh2.shape[0]
    tm = min(TM, T)
    nt = T // tm
    tmf = min(TM_FFN, T)
    ntf = T // tmf
    nm = lambda s: f"l{l}_{s}"
    gr = {}
    (dres2, dffn), (gr["ln2_g"], gr["ln2_b"]) = _rows_bwd(
        _res_ln_fn, [_r2(sv["h1"], tm), _r2(sv["ffn"], tm)], [W["ln2_g"], W["ln2_b"]], [_r2(dh2, tm)],
        [_o2(T, D, F32, tm), _o2(T, D, BF16, tm)], name=nm("ln2_b"), nt=nt)
    df = _mm_nt(dffn, W["w_dn"], name=nm("dn_dx"))
    gr["w_dn"] = _carrying(carry, "dn_dw", gr,
                           lambda comm: _mm_tn(sv["f"], dffn, name=nm("dn_dw"), tmo=FFN // 2, comm=comm))
    up3 = sv["up3"]
    (dgate, dval), gr["ffn_w"], gr["ffn_b"] = _dwconv_bwd(
        _ffn_post, [_slab(up3, 0, tmf), _slab(up3, 1, tmf)],
        [_halo_slab(up3, 0, tmf, DW_HALO), _halo_slab(up3, 1, tmf, DW_HALO)], W["ffn_w"], W["ffn_b"], [_r2(df, tmf)],
        K=FFN_K, C=FFN, name=nm("ffnact_b"), tm=tmf, nt=ntf)
    dh1 = _carrying(carry, "up_dx", gr,
                    lambda comm: _mm_cols_nt([dgate, dval], W["w_up"], name=nm("up_dx"), add=dres2, comm=comm))
    gr["w_up"] = _carrying(carry, "up_dw", gr, lambda comm: _mm_tn_cols(sv["h1b"], [dgate, dval], name=nm("up_dw"),
                                                                        tmo=min(512, D), comm=comm))
    (dres1, dmix), (gr["ln1_g"], gr["ln1_b"]) = _rows_bwd(
        _res_ln_fn, [_r2(sv["h"], tm), _r2(sv["mix"], tm)], [W["ln1_g"], W["ln1_b"]], [_r2(dh1, tm)],
        [_o2(T, D, F32, tm), _o2(T, D, BF16, tm)], name=nm("ln1_b"), nt=nt)
    dm = _mm_nt(dmix, W["w_o"], name=nm("wo_dx"))
    gr["w_o"] = _mm_tn(sv["m"], dmix, name=nm("wo_dw"), tmo=min(512, D))
    u3 = sv["u3"]
    (dga, dgb, dya, dyb), _ = _rows_bwd(
        _mix_fn, [_slab(u3, 4, tm), _slab(u3, 5, tm), _r2(sv["ya"], tm), _r2(sv["yb"], tm)], [], [_r2(dm, tm)],
        [_o2(T, D, BF16, tm)] * 4, name=nm("mix_b"), nt=nt)
    dv3 = _mm_nt(dya, W["w_co"], name=nm("ya_dx"))
    gr["w_co"] = _mm_tn(sv["v3"], dya, name=nm("ya_dw"), tmo=min(512, D))
    da, dg, gr["conv_w"], gr["conv_b"], gr["conv_ln_g"], gr["conv_ln_b"] = _conf_bwd(
        u3, sv["cconv"], dv3, W["conv_w"], W["conv_b"], W["conv_ln_g"], W["conv_ln_b"], name=nm("conf_b"))
    dyn = _mm_nt(dyb, W["w_so"], name=nm("yb_dx"))
    gr["w_so"] = _mm_tn(sv["yn"], dyb, name=nm("yb_dw"), tmo=min(512, SD))
    (dx0, dx1, dbc, ddtr, dz0, dz1, gr["a_log"], gr["dt_bias"], gr["d_skip"], gr["norm_w"]) = _ssd_bwd(
        sv["x0"], sv["x1"], sv["bc"], sv["dtr"], u3, sv["sprev"], dyn, W["a_log"], W["dt_bias"], W["d_skip"],
        W["norm_w"], name=nm("ssd_b"))
    tmx = min(TM_X, T)
    (du6, du7, du8), gr["ssm_w"], gr["ssm_b"] = _dwconv_bwd(
        _xbc_post, [_slab(u3, 6, tmx), _slab(u3, 7, tmx), _slab(u3, 8, tmx)],
        [_halo_slab(u3, 6, tmx, DW_HALO), _halo_slab(u3, 7, tmx, DW_HALO), _halo_slab(u3, 8, tmx, DW_HALO)],
        W["ssm_w"], W["ssm_b"], [_r2(dx0, tmx), _r2(dx1, tmx), _r2(dbc, tmx)], K=SSM_K, C=D, name=nm("xbc_b"),
        tm=tmx, nt=T // tmx)
    du = [da, dg, dz0, dz1, dga, dgb, du6, du7, du8]
    dh_a = _mm_nt(ddtr, W["w_dt"], name=nm("dt_dx"), add=dres1)
    dh = _carrying(carry, "u_dx", gr, lambda comm: _mm_cols_nt(du, W["w_p"], name=nm("u_dx"), add=dh_a, comm=comm))
    gr["w_p"] = _mm_tn_cols(sv["hb"], du, name=nm("u_dw"), tmo=min(512, D))
    gr["w_dt"] = _mm_tn(sv["hb"], ddtr, name=nm("dt_dw"), tmo=min(512, D))
    return dh, gr


_U_SPLIT = (2 * D + SD, 2 * D + SD + XBC, 2 * D + SD + XBC + H)


def _pad_rows(a, rows):
    return jnp.concatenate([a, jnp.zeros((rows - a.shape[0],) + a.shape[1:], a.dtype)], axis=0)


def _pad_lanes(a, lanes):
    return jnp.concatenate([a, jnp.zeros(a.shape[:-1] + (lanes - a.shape[-1],), a.dtype)], axis=-1)


def _w_in_layout(w_in):
    e0, e1, e2 = _U_SPLIT
    return dict(w_p=jnp.concatenate([w_in[:, :e0], w_in[:, e2:], w_in[:, e0:e1]], axis=1),
                w_dt=_pad_lanes(w_in[:, e1:e2], HP))


_MM_KEY = dict(w_conv_out="w_co", w_ssm_out="w_so", w_o="w_o", w_ffn_up="w_up", w_ffn_down="w_dn")


def _small_layer_weights(full, l):
    row = lambda a: a.reshape(1, -1)
    ssm_w = full["ssm_conv_w"][l]
    ffn_w = full["ffn_dw_w"][l]
    ssm_b = full["ssm_conv_b"][l]
    ffn_b = full["ffn_dw_b"][l]
    W = dict(
        conv_w=_pad_rows(full["conv_dw_w"][l], 32),
        conv_b=row(full["conv_dw_b"][l]), conv_ln_g=row(full["conv_ln_g"][l]), conv_ln_b=row(full["conv_ln_b"][l]),
        ssm_w=jnp.stack([_pad_rows(ssm_w[:, p * D:(p + 1) * D], 8) for p in range(3)]),
        ssm_b=[row(ssm_b[p * D:(p + 1) * D]) for p in range(3)],
        a_log=_pad_lanes(row(full["ssm_a_log"][l]), HP), dt_bias=_pad_lanes(row(full["ssm_dt_bias"][l]), HP),
        d_skip=_pad_lanes(row(full["ssm_d"][l]), HP), norm_w=row(full["ssm_norm_w"][l]),
        ln1_g=row(full["ln1_g"][l]), ln1_b=row(full["ln1_b"][l]),
        ffn_w=jnp.stack([_pad_rows(ffn_w[:, p * FFN:(p + 1) * FFN], 8) for p in range(2)]),
        ffn_b=[row(ffn_b[p * FFN:(p + 1) * FFN]) for p in range(2)],
        ln2_g=row(full["ln2_g"][l]), ln2_b=row(full["ln2_b"][l]),
    )
    return W


def _layer_grads_to_reference_layout(gr):
    e0, e1, e2 = _U_SPLIT
    nx = XBC
    wp = gr["w_p"]
    w_in = jnp.concatenate([wp[:, :e0], wp[:, e0 + 2 * D:e0 + 2 * D + nx], gr["w_dt"][:, :H], wp[:, e0:e0 + 2 * D]], axis=1)
    return dict(
        w_in=w_in, conv_dw_w=gr["conv_w"][:CONV_K], conv_dw_b=gr["conv_b"][0], conv_ln_g=gr["conv_ln_g"][0],
        conv_ln_b=gr["conv_ln_b"][0], w_conv_out=gr["w_co"],
        ssm_conv_w=jnp.concatenate([gr["ssm_w"][p, :SSM_K] for p in range(3)], axis=1),
        ssm_conv_b=jnp.concatenate([b[0] for b in gr["ssm_b"]]),
        ssm_dt_bias=gr["dt_bias"][0, :H], ssm_a_log=gr["a_log"][0, :H], ssm_d=gr["d_skip"][0, :H],
        ssm_norm_w=gr["norm_w"][0], w_ssm_out=gr["w_so"], w_o=gr["w_o"], ln1_g=gr["ln1_g"][0], ln1_b=gr["ln1_b"][0],
        w_ffn_up=gr["w_up"], ffn_dw_w=jnp.concatenate([gr["ffn_w"][p, :FFN_K] for p in range(2)], axis=1),
        ffn_dw_b=jnp.concatenate([b[0] for b in gr["ffn_b"]]), w_ffn_down=gr["w_dn"], ln2_g=gr["ln2_g"][0],
        ln2_b=gr["ln2_b"][0],
    )


_BIG = dict(w_in=2, w_conv_out=1, w_ssm_out=1, w_o=1, w_ffn_up=2, w_ffn_down=1)
_SMALL_SHARDED = dict(conv_dw_w=2, ssm_conv_w=2, ffn_dw_w=2)
_REPLICATED = ("ln_in_g", "ln_in_b", "conv_dw_b", "conv_ln_g", "conv_ln_b", "ssm_conv_b", "ssm_dt_bias", "ssm_a_log",
               "ssm_d", "ssm_norm_w", "ln1_g", "ln1_b", "ffn_dw_b", "ln2_g", "ln2_b")
_WEIGHTS = ("ln_in_g", "ln_in_b", "w_in", "conv_dw_w", "conv_dw_b", "conv_ln_g", "conv_ln_b", "w_conv_out", "ssm_conv_w",
            "ssm_conv_b", "ssm_dt_bias", "ssm_a_log", "ssm_d", "ssm_norm_w", "w_ssm_out", "w_o", "ln1_g", "ln1_b",
            "w_ffn_up", "ffn_dw_w", "ffn_dw_b", "w_ffn_down", "ln2_g", "ln2_b")


def _split_chips(a, axis):
    rows, cols = a.shape
    if axis == 0:
        return a.reshape(NCHIP, rows // NCHIP, cols)
    return a.reshape(rows, NCHIP, cols // NCHIP).transpose(1, 0, 2)


def kernel(x, ln_in_g, ln_in_b, w_in, conv_dw_w, conv_dw_b, conv_ln_g, conv_ln_b, w_conv_out, ssm_conv_w, ssm_conv_b, ssm_dt_bias, ssm_a_log, ssm_d, ssm_norm_w, w_ssm_out, w_o, ln1_g, ln1_b, w_ffn_up, ffn_dw_w, ffn_dw_b, w_ffn_down, ln2_g, ln2_b, loss_target, m_ln_in_g, m_ln_in_b, m_w_in, m_conv_dw_w, m_conv_dw_b, m_conv_ln_g, m_conv_ln_b, m_w_conv_out, m_ssm_conv_w, m_ssm_conv_b, m_ssm_dt_bias, m_ssm_a_log, m_ssm_d, m_ssm_norm_w, m_w_ssm_out, m_w_o, m_ln1_g, m_ln1_b, m_w_ffn_up, m_ffn_dw_w, m_ffn_dw_b, m_w_ffn_down, m_ln2_g, m_ln2_b, v_ln_in_g, v_ln_in_b, v_w_in, v_conv_dw_w, v_conv_dw_b, v_conv_ln_g, v_conv_ln_b, v_w_conv_out, v_ssm_conv_w, v_ssm_conv_b, v_ssm_dt_bias, v_ssm_a_log, v_ssm_d, v_ssm_norm_w, v_w_ssm_out, v_w_o, v_ln1_g, v_ln1_b, v_w_ffn_up, v_ffn_dw_w, v_ffn_dw_b, v_w_ffn_down, v_ln2_g, v_ln2_b):
    args = locals()
    w = {n: args[n] for n in _WEIGHTS}
    mom = {n: args["m_" + n] for n in _WEIGHTS}
    vel = {n: args["v_" + n] for n in _WEIGHTS}
    T = x.shape[1]
    tm = min(TM, T)
    nt = T // tm
    chip = 2 * lax.axis_index("x") + lax.axis_index("y")

    assert DEPTH == 2
    big_names, small_names = list(_BIG), list(_SMALL_SHARDED)
    rest_big = [n for n in big_names if n != "w_in"]
    bf = {n: w[n].astype(BF16) for n in big_names}
    join = lambda got, axis: jnp.concatenate([got[j] for j in range(NCHIP)], axis=axis)
    first = _chip_exchange([bf["w_in"][0]] + [w[n] for n in small_names], scatter=False, name="gather_first")
    full = {n: join(gk, _SMALL_SHARDED[n]) for n, gk in zip(small_names, first[1:])}
    for n in _REPLICATED:
        full[n] = w[n]
    Ws = [_small_layer_weights(full, l) for l in range(DEPTH)]
    Ws[0].update(_w_in_layout(join(first[0], 1)))

    def rest_arrived(l):
        def done(got):
            for n, gk in zip(rest_big, got):
                Ws[l][_MM_KEY[n]] = join(gk, _BIG[n] - 1)
        return done

    carry_fwd = [
        {"u": lambda gr: (_exchange_comm([bf[n][0] for n in rest_big], scatter=False), rest_arrived(0)),
         "up": lambda gr: (_exchange_comm([bf["w_in"][1]], scatter=False),
                           lambda got: Ws[1].update(_w_in_layout(join(got[0], 1))))},
        {"u": lambda gr: (_exchange_comm([bf[n][1] for n in rest_big], scatter=False), rest_arrived(1))},
    ]

    x2 = x.reshape(T, D)
    g_in, b_in = ln_in_g.reshape(1, D), ln_in_b.reshape(1, D)
    h, hb = _rows_fwd(lambda a, g, b: _ln_fn(a, g, b) * 2, [_r2(x2, tm)], [g_in, b_in],
                      [_o2(T, D, F32, tm), _o2(T, D, BF16, tm)], name="ln_in", nt=nt)
    saved = []
    for l in range(DEPTH):
        h, hb, sv = _layer_fwd(h, hb, Ws[l], l, carry_fwd[l])
        saved.append(sv)
    loss_row, dh = _loss_head(h, loss_target.reshape(T, D), name="loss")

    arrived = {}

    def exchange(names, l, grads):
        def make(gr):
            src = grads(gr)
            def done(got):
                for n, gk in zip(names, got):
                    arrived[(n, l)] = gk
            return _exchange_comm([_split_chips(src[n], _BIG[n] - 1) for n in names], scatter=True), done
        return make

    layer_grads = [None] * DEPTH
    dh, gr = _layer_bwd(dh, Ws[1], saved[1], 1, {})
    layer_grads[1] = _layer_grads_to_reference_layout(gr)
    g1 = lambda gr: layer_grads[1]
    g0 = lambda gr: {n: gr[_MM_KEY[n]] for n in rest_big}
    dh, gr = _layer_bwd(dh, Ws[0], saved[0], 0, {
        "dn_dw": exchange(["w_conv_out", "w_ssm_out", "w_o"], 1, g1),
        "up_dx": exchange(["w_in"], 1, g1),
        "up_dw": exchange(["w_ffn_up", "w_ffn_down"], 1, g1),
        "u_dx": exchange(rest_big, 0, g0)})
    layer_grads[0] = _layer_grads_to_reference_layout(gr)
    (arrived[("w_in", 0)],) = _chip_exchange([_split_chips(layer_grads[0]["w_in"], 1)], scatter=True, name="exchange_last")
    (grad_x2,), (d_g_in, d_b_in) = _rows_bwd(_ln_fn, [_r2(x2, tm)], [g_in, b_in], [_r2(dh, tm)], [_o2(T, D, F32, tm)],
                                             name="ln_in_b", nt=nt)
    local = {n: jnp.stack([layer_grads[l][n] for l in range(DEPTH)]) for n in _WEIGHTS[2:] if n not in _BIG}
    local["ln_in_g"], local["ln_in_b"] = d_g_in[0], d_b_in[0]
    res = [{}, {}, {}, {}]

    keys = [(n, l) for n in big_names for l in range(DEPTH)]
    mine = [_sum_slots(arrived[k], name=f"sum_chips_{k[0]}_{k[1]}") for k in keys]
    other = _core_swap(mine, name="swap_cores")
    for i, n in enumerate(big_names):
        outs = _adamw_layers(mine[2 * i:2 * i + 2], other[2 * i:2 * i + 2], w[n], mom[n], vel[n], name="adamw_" + n)
        for q in range(4):
            res[q][n] = outs[q]

    rest_names = list(_REPLICATED) + small_names
    part = _pack([loss_row] + [local[n] for n in rest_names], F32, 8)
    parts = _all_gather8(part, name="gather_small")
    total = _sum_slots(parts, name="sum_devices")
    tot = _unpack(total, [loss_row.shape] + [local[n].shape for n in rest_names])
    loss = tot[0][0, 0]
    g_rest = {}
    for n, t in zip(rest_names, tot[1:]):
        if n in _SMALL_SHARDED:
            ax = _SMALL_SHARDED[n]
            t = lax.dynamic_slice_in_dim(t, chip * w[n].shape[ax], w[n].shape[ax], axis=ax)
        g_rest[n] = t
    pk = lambda d: _pack([d[n] for n in rest_names], F32, 8)
    rest_out = _adamw([pk(g_rest)], pk(w), pk(mom), pk(vel), name="adamw_rest")
    rest_out = [_unpack(o, [w[n].shape for n in rest_names]) for o in rest_out]

    for q in range(4):
        for k, n in enumerate(rest_names):
            res[q][n] = rest_out[q][k]
    grad_x = grad_x2.reshape(x.shape)
    return (loss, grad_x, *[res[0][n] for n in _WEIGHTS], *[res[1][n] for n in _WEIGHTS],
            *[res[2][n] for n in _WEIGHTS], *[res[3][n] for n in _WEIGHTS])
```

```python
import functools
import math

import jax
import jax.numpy as jnp
from jax import lax
from jax.experimental import pallas as pl
from jax.experimental.pallas import tpu as pltpu

F32 = jnp.float32
BF16 = jnp.bfloat16

D = 1024
DEPTH = 2
CONV_K = 31
SD = 2 * D
P = 64
H = SD // P
G = 4
R = H // G
N = 128
RP = R * P
SSM_K = 4
L = 128
XBC = SD + 2 * G * N
FFN = 2816
FFN_K = 3
IN_DIM = 2 * D + SD + XBC + H + 2 * D
ALPHA = (2 * DEPTH) ** 0.25
LN_EPS = 1e-5
RMS_EPS = 1e-5
ADAM_LR, ADAM_B1, ADAM_B2, ADAM_EPS, ADAM_WD, ADAM_STEP = 0.001, 0.9, 0.999, 1e-08, 0.01, 10

HP = 128
NCHIP = 4
PACK_W = 1024
VMEM_LIMIT = 56 * 1024 * 1024
TM = 512
TM_X = 256
TM_FFN = 256
TK = 1024

assert D == 2 * RP and 2 * G * N == D and XBC == 3 * D and H <= HP


def _pcall(body, *, name, grid=(), in_specs, out_specs, out_shape, scratch_shapes=()):
    params = pltpu.CompilerParams(vmem_limit_bytes=VMEM_LIMIT, dimension_semantics=("arbitrary",) * len(grid))
    return pl.pallas_call(body, name=name, grid=grid, in_specs=in_specs, out_specs=out_specs, out_shape=out_shape,
                          scratch_shapes=list(scratch_shapes), compiler_params=params)


def _pcall_carrying(body, comm, *, name, grid, in_specs, out_specs, out_shape, scratch_shapes=()):
    in_specs, out_specs, out_shape = list(in_specs), list(out_specs), list(out_shape)
    scratch_shapes = list(scratch_shapes)
    n_in, n_out, n_scr = len(in_specs), len(out_specs), len(scratch_shapes)
    nci, nco = len(comm["ins"]), len(comm["outs"])

    def wrapped(*refs):
        ins, cin = refs[:n_in], refs[n_in:n_in + nci]
        outs = refs[n_in + nci:n_in + nci + n_out]
        cout = refs[n_in + nci + n_out:n_in + nci + n_out + nco]
        scr = refs[n_in + nci + n_out + nco:n_in + nci + n_out + nco + n_scr]
        csem = refs[n_in + nci + n_out + nco + n_scr:]
        ids = [pl.program_id(ax) for ax in range(len(grid))]
        first = functools.reduce(jnp.logical_and, [i == 0 for i in ids])
        last = functools.reduce(jnp.logical_and, [i == g - 1 for i, g in zip(ids, grid)])

        @pl.when(first)
        def _():
            comm["start"](cin, cout, csem)

        body(*ins, *outs, *scr)

        @pl.when(last)
        def _():
            comm["wait"](cin, cout, csem)

    call = _pcall(wrapped, name=name, grid=grid, in_specs=in_specs + [_HBM] * nci, out_specs=out_specs + [_HBM] * nco,
                  out_shape=out_shape + list(comm["outs"]), scratch_shapes=scratch_shapes + list(comm["sems"]))

    def run(*operands):
        res = call(*operands, *comm["ins"])
        return list(res[:n_out]), list(res[n_out:])

    return run


def _pcall_maybe_carrying(body, comm, **kw):
    if comm is not None:
        return _pcall_carrying(body, comm, **kw)
    call = _pcall(body, **kw)
    return lambda *operands: (list(call(*operands)), None)


def _ccall(body, *, name, in_specs, out_specs, out_shape, scratch_shapes):
    return pl.pallas_call(body, name=name, in_specs=in_specs, out_specs=out_specs, out_shape=out_shape,
                          scratch_shapes=list(scratch_shapes))


def _full_spec(a):
    nd = a.ndim
    return pl.BlockSpec(a.shape, lambda *_: (0,) * nd)


def _sds(shape, dtype):
    return jax.ShapeDtypeStruct(tuple(shape), dtype)


def _mm(a, b, *, name, grid, a_spec, b_spec, o_spec, out_shape, acc_shape, trans_a=False, trans_b=False, add=None,
        add_spec=None, comm=None):
    nk = grid[2]
    dn = (((0 if trans_a else 1,), (1 if trans_b else 0,)), ((), ()))
    has_add = add is not None

    def body(*refs):
        a_ref, b_ref = refs[0], refs[1]
        add_ref = refs[2] if has_add else None
        o_ref = refs[3] if has_add else refs[2]
        part = lax.dot_general(a_ref[...].astype(BF16), b_ref[...].astype(BF16), dn, preferred_element_type=F32)

        def finish(res):
            if has_add:
                res = res + add_ref[...]
            o_ref[...] = res.astype(o_ref.dtype)

        if nk == 1:
            finish(part)
        else:
            acc = refs[-1]
            k = pl.program_id(2)

            @pl.when(k == 0)
            def _():
                acc[...] = part

            @pl.when(k > 0)
            def _():
                acc[...] += part

            @pl.when(k == nk - 1)
            def _():
                finish(acc[...])

    ins = [a, b] + ([add] if has_add else [])
    specs = [a_spec, b_spec] + ([add_spec] if has_add else [])
    scratch = [pltpu.VMEM(acc_shape, F32)] if nk > 1 else []
    if comm is not None:
        (out,), got = _pcall_carrying(body, comm, name=name, grid=grid, in_specs=specs, out_specs=[o_spec],
                                      out_shape=[out_shape], scratch_shapes=scratch)(*ins)
        return out, got
    return _pcall(body, name=name, grid=grid, in_specs=specs, out_specs=o_spec, out_shape=out_shape,
                  scratch_shapes=scratch)(*ins)


def _mm_nn(a, b, *, name, out_dtype=F32, tn=None, tk=None, add=None):
    M, K = a.shape
    Nn = b.shape[1]
    tm = min(TM, M)
    tn = Nn if tn is None else tn
    tk = K if tk is None else tk
    grid = (M // tm, Nn // tn, K // tk)
    return _mm(a, b, name=name, grid=grid,
               a_spec=pl.BlockSpec((tm, tk), lambda i, j, k: (i, k)),
               b_spec=pl.BlockSpec((tk, tn), lambda i, j, k: (k, j)),
               o_spec=pl.BlockSpec((tm, tn), lambda i, j, k: (i, j)),
               out_shape=_sds((M, Nn), out_dtype), acc_shape=(tm, tn), add=add,
               add_spec=pl.BlockSpec((tm, tn), lambda i, j, k: (i, j)))


def _mm_nt(a, b, *, name, add=None):
    M, K = a.shape
    Nn = b.shape[0]
    tm = min(TM, M)
    return _mm(a, b, name=name, grid=(M // tm, 1, 1), trans_b=True,
               a_spec=pl.BlockSpec((tm, K), lambda i, j, k: (i, 0)),
               b_spec=pl.BlockSpec((Nn, K), lambda i, j, k: (0, 0)),
               o_spec=pl.BlockSpec((tm, Nn), lambda i, j, k: (i, 0)),
               out_shape=_sds((M, Nn), F32), acc_shape=(tm, Nn), add=add,
               add_spec=pl.BlockSpec((tm, Nn), lambda i, j, k: (i, 0)))


def _mm_resident_slab_out(a, w, *, name, width, tm, comm=None):
    M, K = a.shape
    S = w.shape[1] // width
    tm = min(tm, M)

    def body(a_ref, w_hbm, o_ref, w_vmem, sem):
        @pl.when(pl.program_id(0) == 0)
        def _():
            cp = pltpu.make_async_copy(w_hbm, w_vmem, sem)
            cp.start()
            cp.wait()

        av = a_ref[...].astype(BF16)
        for s in range(S):
            o_ref[s] = jnp.dot(av, w_vmem[:, s * width:(s + 1) * width], preferred_element_type=F32)

    kw = dict(name=name, grid=(M // tm,),
              in_specs=[pl.BlockSpec((tm, K), lambda i: (i, 0)), pl.BlockSpec(memory_space=pl.ANY)],
              scratch_shapes=[pltpu.VMEM(w.shape, w.dtype), pltpu.SemaphoreType.DMA])
    o_spec = pl.BlockSpec((S, tm, width), lambda i: (0, i, 0))
    if comm is not None:
        (out,), got = _pcall_carrying(body, comm, out_specs=[o_spec], out_shape=[_sds((S, M, width), F32)], **kw)(a, w)
        return out, got
    return _pcall(body, out_specs=o_spec, out_shape=_sds((S, M, width), F32), **kw)(a, w)


def _mm_cols_nt(a_list, w, *, name, add, comm=None):
    S = len(a_list)
    M, width = a_list[0].shape
    Nn = w.shape[0]
    tm = min(TM, M)

    def body(*refs):
        a_refs, w_hbm, add_ref, o_ref, w_vmem, sem = refs[:S], refs[S], refs[S + 1], refs[S + 2], refs[S + 3], refs[S + 4]

        @pl.when(pl.program_id(0) == 0)
        def _():
            cp = pltpu.make_async_copy(w_hbm, w_vmem, sem)
            cp.start()
            cp.wait()

        acc = add_ref[...]
        for s in range(S):
            acc = acc + lax.dot_general(a_refs[s][...].astype(BF16), w_vmem[:, s * width:(s + 1) * width],
                                        (((1,), (1,)), ((), ())), preferred_element_type=F32)
        o_ref[...] = acc

    row = lambda C: pl.BlockSpec((tm, C), lambda i: (i, 0))
    kw = dict(name=name, grid=(M // tm,), in_specs=[row(width)] * S + [pl.BlockSpec(memory_space=pl.ANY), row(Nn)],
              scratch_shapes=[pltpu.VMEM(w.shape, w.dtype), pltpu.SemaphoreType.DMA])
    if comm is not None:
        (out,), got = _pcall_carrying(body, comm, out_specs=[row(Nn)], out_shape=[_sds((M, Nn), F32)], **kw)(*a_list, w, add)
        return out, got
    return _pcall(body, out_specs=row(Nn), out_shape=_sds((M, Nn), F32), **kw)(*a_list, w, add)


def _mm_tn_cols(a, b_list, *, name, tmo, comm=None):
    T, M = a.shape
    S = len(b_list)
    width = b_list[0].shape[1]
    tk = min(TK, T)
    nk = T // tk

    def body(*refs):
        a_ref, b_refs, o_ref, acc = refs[0], refs[1:S + 1], refs[S + 1], refs[S + 2]
        j, k = pl.program_id(1), pl.program_id(2)
        for s in range(S):
            @pl.when(j == s)
            def _():
                part = lax.dot_general(a_ref[...], b_refs[s][...], (((0,), (0,)), ((), ())), preferred_element_type=F32)

                @pl.when(k == 0)
                def _():
                    acc[...] = part

                @pl.when(k > 0)
                def _():
                    acc[...] += part

        @pl.when(k == nk - 1)
        def _():
            o_ref[...] = acc[...].astype(o_ref.dtype)

    def b_spec(s):
        return pl.BlockSpec((tk, width), lambda i, j, k: (jnp.where(j == s, k, jnp.where(j < s, 0, nk - 1)), 0))

    kw = dict(name=name, grid=(M // tmo, S, nk),
              in_specs=[pl.BlockSpec((tk, tmo), lambda i, j, k: (k, i))] + [b_spec(s) for s in range(S)],
              scratch_shapes=[pltpu.VMEM((tmo, width), F32)])
    o_spec = pl.BlockSpec((tmo, width), lambda i, j, k: (i, j))
    if comm is not None:
        (out,), got = _pcall_carrying(body, comm, out_specs=[o_spec], out_shape=[_sds((M, S * width), BF16)], **kw)(a, *b_list)
        return out, got
    return _pcall(body, out_specs=o_spec, out_shape=_sds((M, S * width), BF16), **kw)(a, *b_list)


def _mm_tn(a, b, *, name, tmo, tn=None, comm=None):
    T, M = a.shape
    Nn = b.shape[1]
    tn = Nn if tn is None else tn
    tk = min(TK, T)
    return _mm(a, b, name=name, grid=(M // tmo, Nn // tn, T // tk), trans_a=True,
               a_spec=pl.BlockSpec((tk, tmo), lambda i, j, k: (k, i)),
               b_spec=pl.BlockSpec((tk, tn), lambda i, j, k: (k, j)),
               o_spec=pl.BlockSpec((tmo, tn), lambda i, j, k: (i, j)),
               out_shape=_sds((M, Nn), BF16), acc_shape=(tmo, tn), comm=comm)


def _r2(a, tm):
    return (a, (tm, a.shape[1]), lambda i: (i, 0))


def _slab(a3, s, tm):
    return (a3, (None, tm, a3.shape[2]), lambda i: (s, i, 0))


def _o2(T, C, dtype, tm):
    return ((T, C), dtype, (tm, C), lambda i: (i, 0))


def _rows_fwd(fn, row_ins, par_ins, outs, *, name, nt):
    nr, npar = len(row_ins), len(par_ins)

    def body(*refs):
        vals = [r[...] for r in refs[:nr + npar]]
        res = fn(*vals)
        for o_ref, v in zip(refs[nr + npar:], res):
            o_ref[...] = v.astype(o_ref.dtype)

    return _pcall(body, name=name, grid=(nt,),
                  in_specs=[pl.BlockSpec(bs, im) for (_, bs, im) in row_ins] + [_full_spec(p) for p in par_ins],
                  out_specs=[pl.BlockSpec(bs, im) for (_, _, bs, im) in outs],
                  out_shape=[_sds(s, d) for (s, d, _, _) in outs])(*[r[0] for r in row_ins], *par_ins)


def _rows_bwd(fn, row_ins, par_ins, cot_ins, drow_outs, *, name, nt):
    nr, npar, nc = len(row_ins), len(par_ins), len(cot_ins)
    keep = [k for k, o in enumerate(drow_outs) if o is not None]

    def body(*refs):
        vals = [r[...].astype(F32) for r in refs[:nr + npar]]
        cots = [r[...].astype(F32) for r in refs[nr + npar:nr + npar + nc]]
        orefs = refs[nr + npar + nc:]
        _, vjp = jax.vjp(fn, *vals)
        grads = vjp(tuple(cots))
        for o_ref, k in zip(orefs[:len(keep)], keep):
            o_ref[...] = grads[k].astype(o_ref.dtype)
        prefs = orefs[len(keep):]

        @pl.when(pl.program_id(0) == 0)
        def _():
            for p_ref in prefs:
                p_ref[...] = jnp.zeros_like(p_ref)

        for p_ref, g in zip(prefs, grads[nr:]):
            p_ref[...] += g

    outs = [drow_outs[k] for k in keep]
    res = _pcall(body, name=name, grid=(nt,),
                 in_specs=[pl.BlockSpec(bs, im) for (_, bs, im) in row_ins] + [_full_spec(p) for p in par_ins]
                 + [pl.BlockSpec(bs, im) for (_, bs, im) in cot_ins],
                 out_specs=[pl.BlockSpec(bs, im) for (_, _, bs, im) in outs] + [_full_spec(p) for p in par_ins],
                 out_shape=[_sds(s, d) for (s, d, _, _) in outs] + [_sds(p.shape, F32) for p in par_ins],
                 )(*[r[0] for r in row_ins], *par_ins, *[c[0] for c in cot_ins])
    return list(res[:len(keep)]), list(res[len(keep):])


def _layer_norm(v, g, b):
    mu = jnp.mean(v, axis=-1, keepdims=True)
    var = jnp.mean(jnp.square(v - mu), axis=-1, keepdims=True)
    return (v - mu) * lax.rsqrt(var + LN_EPS) * g + b


def _silu(v):
    return v * jax.nn.sigmoid(v)


def _softplus(v):
    return jnp.maximum(v, 0.0) + jnp.log1p(jnp.exp(-jnp.abs(v)))


def _halo_of(K):
    return 8 * ((K - 1 + 7) // 8)


DW_HALO = 8
DW_RB = 16
DW_LC = 256


def _dw_taps(win, w_ref, p, ls, K, shift_of):
    acc = None
    for k in range(K):
        o = shift_of(k)
        term = win[o:o + DW_RB, :] * w_ref[p, k:k + 1, ls]
        acc = term if acc is None else acc + term
    return acc


def _dwconv_fwd(post, part_ins, w, biases, outs, *, K, C, name, tm, nt):
    nparts, halo = len(part_ins), DW_HALO

    def body(*refs):
        x_refs, w_ref = refs[:nparts], refs[nparts]
        b_refs = refs[nparts + 1:2 * nparts + 1]
        orefs, buf = refs[2 * nparts + 1:-1], refs[-1]
        i = pl.program_id(0)
        for p in range(nparts):
            @pl.when(i == 0)
            def _():
                buf[p, pl.ds(0, halo), :] = jnp.zeros((halo, C), F32)

            @pl.when(i > 0)
            def _():
                buf[p, pl.ds(0, halo), :] = buf[p, pl.ds(tm, halo), :]

            buf[p, pl.ds(halo, tm), :] = x_refs[p][...]

        def group(r, carry):
            base = pl.multiple_of(r * DW_RB, DW_RB)
            for cj in range(C // DW_LC):
                ls = slice(cj * DW_LC, (cj + 1) * DW_LC)
                cs = [_dw_taps(buf[p, pl.ds(base, DW_RB + halo), ls], w_ref, p, ls, K, lambda k: halo - (K - 1) + k)
                      + b_refs[p][:, ls] for p in range(nparts)]
                for o_ref, v in zip(orefs, post(cs)):
                    o_ref[pl.ds(base, DW_RB), ls] = v.astype(o_ref.dtype)
            return carry

        lax.fori_loop(0, tm // DW_RB, group, 0)

    return _pcall(body, name=name, grid=(nt,),
                  in_specs=[pl.BlockSpec(bs, im) for (_, bs, im) in part_ins] + [_full_spec(w)] + [_full_spec(b) for b in biases],
                  out_specs=[pl.BlockSpec(bs, im) for (_, _, bs, im) in outs],
                  out_shape=[_sds(s, d) for (s, d, _, _) in outs],
                  scratch_shapes=[pltpu.VMEM((nparts, halo + tm, C), F32)])(*[r[0] for r in part_ins], w, *biases)


def _dwconv_bwd(post, part_ins, halo_ins, w, biases, cot_ins, *, K, C, name, tm, nt, comm=None):
    nparts, halo, nc, RB = len(part_ins), DW_HALO, len(cot_ins), DW_RB
    T = nt * tm

    def body(*refs):
        x_refs, h_refs, w_ref = refs[:nparts], refs[nparts:2 * nparts], refs[2 * nparts]
        b_refs = refs[2 * nparts + 1:3 * nparts + 1]
        cot_refs = refs[3 * nparts + 1:3 * nparts + 1 + nc]
        rest = refs[3 * nparts + 1 + nc:]
        dx_refs, dw_ref, db_refs = rest[:nparts], rest[nparts], rest[nparts + 1:2 * nparts + 1]
        bufx, bufd, acc = rest[-3], rest[-2], rest[-1]
        s = pl.program_id(0)
        first_tile = s == nt - 1

        @pl.when(s == 0)
        def _():
            acc[...] = jnp.zeros_like(acc)
            for p in range(nparts):
                bufd[p, pl.ds(tm, halo), :] = jnp.zeros((halo, C), F32)

        for p in range(nparts):
            bufx[p, pl.ds(0, halo), :] = jnp.where(first_tile, 0.0, h_refs[p][...])
            bufx[p, pl.ds(halo, tm), :] = x_refs[p][...]
        fold = lambda v: v[0:8, :] + v[8:16, :]

        def conv_out_grads(r, carry):
            base = pl.multiple_of(r * RB, RB)
            for cj in range(C // DW_LC):
                ls = slice(cj * DW_LC, (cj + 1) * DW_LC)
                wins = [bufx[p, pl.ds(base, RB + halo), ls] for p in range(nparts)]
                cs = [_dw_taps(wins[p], w_ref, p, ls, K, lambda k: halo - (K - 1) + k) + b_refs[p][:, ls]
                      for p in range(nparts)]
                _, vjp = jax.vjp(lambda *c: post(list(c)), *cs)
                dcs = vjp(tuple(cr[pl.ds(base, RB), ls].astype(F32) for cr in cot_refs))
                for p in range(nparts):
                    bufd[p, pl.ds(base, RB), ls] = dcs[p]
                    for k in range(K):
                        o = halo - (K - 1) + k
                        acc[p, k, :, ls] += fold(dcs[p] * wins[p][o:o + RB, :])
                    acc[p, K, :, ls] += fold(dcs[p])
            return carry

        lax.fori_loop(0, tm // RB, conv_out_grads, 0)

        def input_grads(r, carry):
            base = pl.multiple_of(r * RB, RB)
            for cj in range(C // DW_LC):
                ls = slice(cj * DW_LC, (cj + 1) * DW_LC)
                for p in range(nparts):
                    dx = _dw_taps(bufd[p, pl.ds(base, RB + halo), ls], w_ref, p, ls, K, lambda k: K - 1 - k)
                    dx_refs[p][pl.ds(base, RB), ls] = dx.astype(dx_refs[p].dtype)
            return carry

        lax.fori_loop(0, tm // RB, input_grads, 0)
        for p in range(nparts):
            bufd[p, pl.ds(tm, halo), :] = bufd[p, pl.ds(0, halo), :]

        @pl.when(s == nt - 1)
        def _():
            dw_ref[...] = jnp.zeros_like(dw_ref)
            for p in range(nparts):
                for k in range(K):
                    dw_ref[p, k:k + 1, :] = jnp.sum(acc[p, k], axis=0, keepdims=True)
                db_refs[p][...] = jnp.sum(acc[p, K], axis=0, keepdims=True)

    rev = lambda im: (lambda s: im(nt - 1 - s))
    row = pl.BlockSpec((tm, C), lambda s: (nt - 1 - s, 0))
    res, got = _pcall_maybe_carrying(
        body, comm, name=name, grid=(nt,),
        in_specs=[pl.BlockSpec(bs, rev(im)) for (_, bs, im) in part_ins]
        + [pl.BlockSpec(bs, rev(im)) for (_, bs, im) in halo_ins]
        + [_full_spec(w)] + [_full_spec(b) for b in biases]
        + [pl.BlockSpec(bs, rev(im)) for (_, bs, im) in cot_ins],
        out_specs=[row] * nparts + [_full_spec(w)] + [_full_spec(b) for b in biases],
        out_shape=[_sds((T, C), BF16)] * nparts + [_sds(w.shape, F32)] + [_sds(b.shape, F32) for b in biases],
        scratch_shapes=[pltpu.VMEM((nparts, halo + tm, C), F32), pltpu.VMEM((nparts, tm + halo, C), F32),
                        pltpu.VMEM((nparts, K + 1, 8, C), F32)],
    )(*[r[0] for r in part_ins], *[r[0] for r in halo_ins], w, *biases, *[c[0] for c in cot_ins])
    out = (list(res[:nparts]), res[nparts], list(res[nparts + 1:]))
    return out if comm is None else (out, got)


def _halo_slab(a3, s, tm, halo):
    q = tm // halo
    return (a3, (None, halo, a3.shape[2]), lambda i: (s, jnp.maximum(i * q - 1, 0), 0))


CONF_HALO = _halo_of(CONV_K)
CONF_RB = 32


def _shifted_copies(buf, shifted, rows):
    for j in range(1, 8):
        shifted[j - 1, pl.ds(0, rows), :] = buf[pl.ds(j, rows), :]


def _shifted_rows(buf, shifted, s, base, nrows):
    j, q = s % 8, s // 8
    if j == 0:
        return buf[pl.ds(base + 8 * q, nrows), :]
    return shifted[j - 1, pl.ds(base + 8 * q, nrows), :]


def _conf_fwd(u3, w, cb, lg, lb, *, name):
    T = u3.shape[1]
    tm = min(TM_X, T)
    nt = T // tm
    K, halo, RB = CONV_K, CONF_HALO, min(CONF_RB, tm)

    def body(a_ref, g_ref, w_ref, cb_ref, lg_ref, lb_ref, v3_ref, c_ref, bufx, xs):
        i = pl.program_id(0)

        @pl.when(i == 0)
        def _():
            bufx[pl.ds(0, halo), :] = jnp.zeros((halo, D), F32)

        @pl.when(i > 0)
        def _():
            bufx[pl.ds(0, halo), :] = bufx[pl.ds(tm, halo), :]

        bufx[pl.ds(halo, tm), :] = a_ref[...] * jax.nn.sigmoid(g_ref[...])
        _shifted_copies(bufx, xs, halo + tm - 8)

        def group(r, carry):
            base = pl.multiple_of(r * RB, RB)
            acc = None
            for k in range(K):
                term = _shifted_rows(bufx, xs, halo - (K - 1) + k, base, RB) * w_ref[k:k + 1, :]
                acc = term if acc is None else acc + term
            c_ref[pl.ds(base, RB), :] = acc
            return carry

        lax.fori_loop(0, tm // RB, group, 0)
        v3_ref[...] = _conf_post([c_ref[...]], cb_ref[...], lg_ref[...], lb_ref[...])[0].astype(v3_ref.dtype)

    slab = lambda s: pl.BlockSpec((None, tm, D), lambda i: (s, i, 0))
    row = pl.BlockSpec((tm, D), lambda i: (i, 0))
    pars = [w, cb, lg, lb]
    return _pcall(body, name=name, grid=(nt,), in_specs=[slab(0), slab(1)] + [_full_spec(p) for p in pars],
                  out_specs=[row, row], out_shape=[_sds((T, D), BF16), _sds((T, D), F32)],
                  scratch_shapes=[pltpu.VMEM((halo + tm, D), F32), pltpu.VMEM((7, halo + tm - 8, D), F32)],
                  )(u3, u3, *pars)


def _conf_bwd(u3, c, dv3, w, cb, lg, lb, *, name, comm=None):
    T = u3.shape[1]
    tm = min(TM_X, T)
    nt = T // tm
    K, halo, RB = CONV_K, CONF_HALO, min(CONF_RB, tm)
    q = tm // halo

    def body(a_ref, g_ref, ah_ref, gh_ref, c_ref, dv3_ref, w_ref, cb_ref, lg_ref, lb_ref,
             da_ref, dg_ref, dw_ref, dcb_ref, dlg_ref, dlb_ref, bufx, xs, bufd, ds, dv0):
        s = pl.program_id(0)
        first_tile = s == nt - 1

        @pl.when(s == 0)
        def _():
            for r in (dw_ref, dcb_ref, dlg_ref, dlb_ref):
                r[...] = jnp.zeros_like(r)
            bufd[pl.ds(tm, halo), :] = jnp.zeros((halo, D), F32)

        a, g = a_ref[...], g_ref[...]
        xin, pre_vjp = jax.vjp(lambda p, q_: _conf_pre(p, q_)[0], a, g)
        bufx[pl.ds(0, halo), :] = jnp.where(first_tile, 0.0, _conf_pre(ah_ref[...], gh_ref[...])[0])
        bufx[pl.ds(halo, tm), :] = xin
        _shifted_copies(bufx, xs, halo + tm - 8)

        _, post_vjp = jax.vjp(lambda cc, b_, g_, l_: _conf_post([cc], b_, g_, l_)[0],
                              c_ref[...], cb_ref[...], lg_ref[...], lb_ref[...])
        dc, dcb, dlg, dlb = post_vjp(dv3_ref[...])
        dcb_ref[...] += dcb
        dlg_ref[...] += dlg
        dlb_ref[...] += dlb
        bufd[pl.ds(0, tm), :] = dc
        _shifted_copies(bufd, ds, tm + halo - 8)

        def dx_group(r, carry):
            base = pl.multiple_of(r * RB, RB)
            acc = None
            for k in range(K):
                term = _shifted_rows(bufd, ds, K - 1 - k, base, RB) * w_ref[k:k + 1, :]
                acc = term if acc is None else acc + term
            dv0[pl.ds(base, RB), :] = acc
            return carry

        lax.fori_loop(0, tm // RB, dx_group, 0)

        for k in range(K):
            def dw_group(r, acc):
                base = pl.multiple_of(r * RB, RB)
                prod = bufd[pl.ds(base, RB), :] * _shifted_rows(bufx, xs, halo - (K - 1) + k, base, RB)
                for v in range(RB // 8):
                    acc = acc + prod[v * 8:(v + 1) * 8, :]
                return acc

            acc = lax.fori_loop(0, tm // RB, dw_group, jnp.zeros((8, D), F32))
            dw_ref[k:k + 1, :] += jnp.sum(acc, axis=0, keepdims=True)

        bufd[pl.ds(tm, halo), :] = bufd[pl.ds(0, halo), :]
        da, dg = pre_vjp(dv0[...])
        da_ref[...] = da.astype(da_ref.dtype)
        dg_ref[...] = dg.astype(dg_ref.dtype)

    slab = lambda sl: pl.BlockSpec((None, tm, D), lambda s: (sl, nt - 1 - s, 0))
    hslab = lambda sl: pl.BlockSpec((None, halo, D), lambda s: (sl, jnp.maximum((nt - 1 - s) * q - 1, 0), 0))
    row = pl.BlockSpec((tm, D), lambda s: (nt - 1 - s, 0))
    pars = [w, cb, lg, lb]
    res, got = _pcall_maybe_carrying(
        body, comm, name=name, grid=(nt,),
        in_specs=[slab(0), slab(1), hslab(0), hslab(1), row, row] + [_full_spec(p) for p in pars],
        out_specs=[row, row] + [_full_spec(p) for p in pars],
        out_shape=[_sds((T, D), BF16)] * 2 + [_sds(p.shape, F32) for p in pars],
        scratch_shapes=[pltpu.VMEM((halo + tm, D), F32), pltpu.VMEM((7, halo + tm - 8, D), F32),
                        pltpu.VMEM((tm + halo, D), F32), pltpu.VMEM((7, tm + halo - 8, D), F32),
                        pltpu.VMEM((tm, D), F32)],
    )(u3, u3, u3, u3, c, dv3, *pars)
    return res if comm is None else (res, got)


def _dg(a, b, ca, cb):
    return lax.dot_general(a.astype(BF16), b.astype(BF16), (((ca,), (cb,)), ((), ())), preferred_element_type=F32)


@jax.custom_vjp
def _dot_nn(a, b):
    return _dg(a, b, 1, 0)


_dot_nn.defvjp(lambda a, b: (_dg(a, b, 1, 0), (a, b)),
               lambda res, g: (_dg(g, res[1], 1, 1), _dg(res[0], g, 0, 0)))


@jax.custom_vjp
def _dot_nt(a, b):
    return _dg(a, b, 1, 1)


_dot_nt.defvjp(lambda a, b: (_dg(a, b, 1, 1), (a, b)),
               lambda res, g: (_dg(g, res[1], 1, 0), _dg(g, res[0], 0, 0)))


@jax.custom_vjp
def _dot_tn(a, b):
    return _dg(a, b, 0, 0)


_dot_tn.defvjp(lambda a, b: (_dg(a, b, 0, 0), (a, b)),
               lambda res, g: (_dg(res[1], g, 1, 1), _dg(res[0], g, 1, 0)))


def _split3(v):
    hi = v.astype(BF16)
    r = v - hi.astype(F32)
    mid = r.astype(BF16)
    return hi, mid, (r - mid.astype(F32)).astype(BF16)


def _x01(v, m, cv, cm, m_left=False):
    acc = None
    for piece in _split3(v):
        t = _dg(m, piece, cm, cv) if m_left else _dg(piece, m, cv, cm)
        acc = t if acc is None else acc + t
    return acc


@jax.custom_vjp
def _expand01(v, m):
    return _x01(v, m, 1, 0)


_expand01.defvjp(lambda v, m: (_x01(v, m, 1, 0), m),
                 lambda m, g: (_x01(g, m, 1, 1), jnp.zeros_like(m)))


@jax.custom_vjp
def _mix01(m, v):
    return _x01(v, m, 0, 1, m_left=True)


_mix01.defvjp(lambda m, v: (_x01(v, m, 0, 1, m_left=True), m),
              lambda m, g: (jnp.zeros_like(m), _x01(g, m, 0, 0, m_left=True)))


def _causal():
    return lax.broadcasted_iota(jnp.int32, (L, L), 0) >= lax.broadcasted_iota(jnp.int32, (L, L), 1)


def _ssd_chunk_prep(dtr, alog, dtb):
    dt = _softplus(dtr + dtb)
    a_cs = _mix01(_causal().astype(F32), dt * (-jnp.exp(alog)))
    return dt, a_cs, a_cs.T


def _ssd_group(xs, dt, a_cs, a_csT, Bg, Cg, zg, sp, dsk, nwg, *, g):
    causal = _causal()
    hi = lax.broadcasted_iota(jnp.int32, (HP, RP), 0)
    ci = lax.broadcasted_iota(jnp.int32, (HP, RP), 1)
    lo = (hi - g * R) * P
    E = ((ci >= lo) & (ci < lo + P)).astype(F32)

    acs_e = _expand01(a_cs, E)
    dt_e = _expand01(dt, E)
    alast_e = acs_e[L - 1:L, :]
    xdt = xs * dt_e
    cb = _dot_nt(Cg, Bg)
    y_off = _dot_nn(Cg, sp) * jnp.exp(acs_e)
    yd = []
    for r in range(R):
        h = g * R + r
        seg = a_cs[:, h:h + 1] - a_csT[h:h + 1, :]
        dec = jnp.where(causal, jnp.exp(jnp.where(causal, seg, 0.0)), 0.0)
        yd.append(_dot_nn(cb * dec, xdt[:, r * P:(r + 1) * P]))
    y = jnp.concatenate(yd, axis=1) + y_off + xs * _expand01(jnp.broadcast_to(dsk, (8, HP)), E)[0:1, :]
    yg = y * _silu(zg)
    yn = yg * lax.rsqrt(jnp.mean(jnp.square(yg), axis=-1, keepdims=True) + RMS_EPS) * nwg
    sc = _dot_tn(Bg, xdt * jnp.exp(alast_e - acs_e))
    return yn, jnp.exp(alast_e) * sp + sc


def _group_cols(g):
    return g // 2, (g % 2) * RP


def _ssd_fwd(x0, x1, bc, dtr, u3, alog, dtb, dsk, nw, *, name, comm=None):
    T = x0.shape[0]
    nc = T // L

    def body(x0_ref, x1_ref, bc_ref, dtr_ref, z0_ref, z1_ref, alog_ref, dtb_ref, dsk_ref, nw_ref, yn_ref, sp_ref, S):
        @pl.when(pl.program_id(0) == 0)
        def _():
            S[...] = jnp.zeros_like(S)

        xr, zr = (x0_ref, x1_ref), (z0_ref, z1_ref)
        dt, a_cs, a_csT = _ssd_chunk_prep(dtr_ref[...], alog_ref[...], dtb_ref[...])
        for g in range(G):
            s, off = _group_cols(g)
            sp = S[g]
            sp_ref[0, g] = sp
            yn, s_next = _ssd_group(xr[s][:, off:off + RP], dt, a_cs, a_csT, bc_ref[:, g * N:(g + 1) * N],
                                    bc_ref[:, G * N + g * N:G * N + (g + 1) * N], zr[s][:, off:off + RP], sp,
                                    dsk_ref[...], nw_ref[:, g * RP:(g + 1) * RP], g=g)
            yn_ref[:, g * RP:(g + 1) * RP] = yn.astype(yn_ref.dtype)
            S[g] = s_next

    row = lambda C: pl.BlockSpec((L, C), lambda c: (c, 0))
    zspec = lambda s: pl.BlockSpec((None, L, D), lambda c: (s, c, 0))
    pars = [alog, dtb, dsk, nw]
    res, got = _pcall_maybe_carrying(
        body, comm, name=name, grid=(nc,),
        in_specs=[row(D), row(D), row(D), row(HP), zspec(2), zspec(3)] + [_full_spec(p) for p in pars],
        out_specs=[row(SD), pl.BlockSpec((1, G, N, RP), lambda c: (c, 0, 0, 0))],
        out_shape=[_sds((T, SD), BF16), _sds((nc, G, N, RP), F32)],
        scratch_shapes=[pltpu.VMEM((G, N, RP), F32)])(x0, x1, bc, dtr, u3, u3, *pars)
    return res if comm is None else (res, got)


def _ssd_bwd(x0, x1, bc, dtr, u3, sprev, dyn, alog, dtb, dsk, nw, *, name, comm=None):
    T = x0.shape[0]
    nc = T // L

    def body(x0_ref, x1_ref, bc_ref, dtr_ref, z0_ref, z1_ref, sp_ref, dyn_ref, alog_ref, dtb_ref, dsk_ref, nw_ref,
             dx0_ref, dx1_ref, dbc_ref, ddtr_ref, dz0_ref, dz1_ref, dalog_ref, ddtb_ref, ddsk_ref, dnw_ref, dS):
        @pl.when(pl.program_id(0) == 0)
        def _():
            dS[...] = jnp.zeros_like(dS)
            for r in (dalog_ref, ddtb_ref, ddsk_ref, dnw_ref):
                r[...] = jnp.zeros_like(r)

        xr, zr = (x0_ref, x1_ref), (z0_ref, z1_ref)
        dxr, dzr = (dx0_ref, dx1_ref), (dz0_ref, dz1_ref)
        (dt, a_cs, a_csT), prep_vjp = jax.vjp(_ssd_chunk_prep, dtr_ref[...], alog_ref[...], dtb_ref[...])
        d_dt, d_acs, d_acsT = jnp.zeros((L, HP), F32), jnp.zeros((L, HP), F32), jnp.zeros((HP, L), F32)
        for g in range(G):
            s, off = _group_cols(g)
            _, vjp = jax.vjp(functools.partial(_ssd_group, g=g), xr[s][:, off:off + RP], dt, a_cs, a_csT,
                             bc_ref[:, g * N:(g + 1) * N], bc_ref[:, G * N + g * N:G * N + (g + 1) * N],
                             zr[s][:, off:off + RP], sp_ref[0, g], dsk_ref[...], nw_ref[:, g * RP:(g + 1) * RP])
            dxs, ddt_g, dacs_g, dacsT_g, dB, dC, dz, dsp, dds, dnwg = vjp((dyn_ref[:, g * RP:(g + 1) * RP], dS[g]))
            dxr[s][:, off:off + RP] = dxs
            dzr[s][:, off:off + RP] = dz.astype(dz0_ref.dtype)
            dbc_ref[:, g * N:(g + 1) * N] = dB
            dbc_ref[:, G * N + g * N:G * N + (g + 1) * N] = dC
            dS[g] = dsp
            d_dt, d_acs, d_acsT = d_dt + ddt_g, d_acs + dacs_g, d_acsT + dacsT_g
            ddsk_ref[...] += dds
            dnw_ref[:, g * RP:(g + 1) * RP] += dnwg
        ddtr, dal, ddb = prep_vjp((d_dt, d_acs, d_acsT))
        dalog_ref[...] += dal
        ddtb_ref[...] += ddb
        ddtr_ref[...] = ddtr.astype(ddtr_ref.dtype)

    row = lambda C: pl.BlockSpec((L, C), lambda c: (nc - 1 - c, 0))
    zspec = lambda s: pl.BlockSpec((None, L, D), lambda c: (s, nc - 1 - c, 0))
    pars = [alog, dtb, dsk, nw]
    res, got = _pcall_maybe_carrying(
        body, comm, name=name, grid=(nc,),
        in_specs=[row(D), row(D), row(D), row(HP), zspec(2), zspec(3),
                  pl.BlockSpec((1, G, N, RP), lambda c: (nc - 1 - c, 0, 0, 0)), row(SD)] + [_full_spec(p) for p in pars],
        out_specs=[row(D), row(D), row(D), row(HP), row(D), row(D)] + [_full_spec(p) for p in pars],
        out_shape=[_sds((T, D), F32)] * 3 + [_sds((T, HP), BF16), _sds((T, D), BF16), _sds((T, D), BF16)]
        + [_sds(p.shape, F32) for p in pars],
        scratch_shapes=[pltpu.VMEM((G, N, RP), F32)])(x0, x1, bc, dtr, u3, u3, sprev, dyn, *pars)
    return res if comm is None else (res, got)


def _loss_head(y, target, *, name):
    T = y.shape[0]
    tm = min(TM, T)

    def body(y_ref, t_ref, loss_ref, dy_ref):
        e = y_ref[...] - t_ref[...]
        dy_ref[...] = e * (1.0 / D)

        @pl.when(pl.program_id(0) == 0)
        def _():
            loss_ref[...] = jnp.zeros_like(loss_ref)

        loss_ref[...] += 0.5 * jnp.sum(jnp.mean(jnp.square(e), axis=-1, keepdims=True), axis=0, keepdims=True)

    row = pl.BlockSpec((tm, D), lambda i: (i, 0))
    return _pcall(body, name=name, grid=(T // tm,), in_specs=[row, row],
                  out_specs=[pl.BlockSpec((1, 128), lambda i: (0, 0)), row],
                  out_shape=[_sds((1, 128), F32), _sds((T, D), F32)])(y, target)


_HBM = pl.BlockSpec(memory_space=pltpu.HBM)
_MESH = pl.DeviceIdType.MESH


def _exchange_comm(bufs, *, scatter):
    nb = len(bufs)

    def copies(in_refs, out_refs, sems, with_arrivals):
        send_sems, recv_sems, local_sems = sems
        x, y, c = lax.axis_index("x"), lax.axis_index("y"), lax.axis_index("c")
        me = 2 * x + y
        peers = [(1 - x, y), (x, 1 - y), (1 - x, 1 - y)]
        own, sends, arrivals = [], [], []
        for b in range(nb):
            src_own = in_refs[b].at[me] if scatter else in_refs[b]
            own.append(pltpu.make_async_copy(src_own, out_refs[b].at[me], local_sems.at[b]))
            for k, (px, py) in enumerate(peers):
                src = in_refs[b].at[2 * px + py] if scatter else in_refs[b]
                sends.append(pltpu.make_async_remote_copy(
                    src_ref=src, dst_ref=out_refs[b].at[me], send_sem=send_sems.at[b, k], recv_sem=recv_sems.at[b, k],
                    device_id=(px, py, c), device_id_type=_MESH))
                if with_arrivals:
                    slot = out_refs[b].at[2 * px + py]
                    arrivals.append(pltpu.make_async_remote_copy(
                        src_ref=slot, dst_ref=slot, send_sem=send_sems.at[b, k], recv_sem=recv_sems.at[b, k],
                        device_id=(px, py, c), device_id_type=_MESH))
        return own, sends, arrivals

    def start(in_refs, out_refs, sems):
        own, sends, _ = copies(in_refs, out_refs, sems, False)
        for cp in own + sends:
            cp.start()

    def wait(in_refs, out_refs, sems):
        own, sends, arrivals = copies(in_refs, out_refs, sems, True)
        for cp in arrivals:
            cp.wait_recv()
        for cp in sends:
            cp.wait_send()
        for cp in own:
            cp.wait()

    return dict(ins=list(bufs), outs=[_sds(b.shape if scatter else (NCHIP,) + b.shape, b.dtype) for b in bufs],
                sems=[pltpu.SemaphoreType.DMA((nb, 3)), pltpu.SemaphoreType.DMA((nb, 3)), pltpu.SemaphoreType.DMA((nb,))],
                start=start, wait=wait)


def _chip_exchange(bufs, *, scatter, name):
    comm = _exchange_comm(bufs, scatter=scatter)
    nb = len(bufs)

    def body(*refs):
        comm["start"](refs[:nb], refs[nb:2 * nb], refs[2 * nb:])
        comm["wait"](refs[:nb], refs[nb:2 * nb], refs[2 * nb:])

    return _ccall(body, name=name, in_specs=[_HBM] * nb, out_specs=[_HBM] * nb, out_shape=comm["outs"],
                  scratch_shapes=comm["sems"])(*bufs)


def _core_swap(bufs, *, name):
    nb = len(bufs)

    def body(*refs):
        in_refs, out_refs, send_sems, recv_sems = refs[:nb], refs[nb:2 * nb], refs[2 * nb], refs[2 * nb + 1]
        x, y, c = lax.axis_index("x"), lax.axis_index("y"), lax.axis_index("c")
        cps = [pltpu.make_async_remote_copy(src_ref=in_refs[b], dst_ref=out_refs[b], send_sem=send_sems.at[b],
                                            recv_sem=recv_sems.at[b], device_id=(x, y, 1 - c), device_id_type=_MESH)
               for b in range(nb)]
        for cp in cps:
            cp.start()
        for cp in cps:
            cp.wait()

    return _ccall(body, name=name, in_specs=[_HBM] * nb, out_specs=[_HBM] * nb,
                  out_shape=[_sds(b.shape, b.dtype) for b in bufs],
                  scratch_shapes=[pltpu.SemaphoreType.DMA((nb,)), pltpu.SemaphoreType.DMA((nb,))])(*bufs)


def _all_gather8(buf, *, name):
    def body(in_ref, out_ref, send_sems, recv_sems, local_sem):
        x, y, c = lax.axis_index("x"), lax.axis_index("y"), lax.axis_index("c")
        me = 4 * x + 2 * y + c
        own = pltpu.make_async_copy(in_ref, out_ref.at[me], local_sem)
        own.start()
        flips = [(fx, fy, fc) for fx in (0, 1) for fy in (0, 1) for fc in (0, 1)][1:]
        peers = [(x ^ fx, y ^ fy, c ^ fc) for fx, fy, fc in flips]
        sends = []
        for k, peer in enumerate(peers):
            cp = pltpu.make_async_remote_copy(src_ref=in_ref, dst_ref=out_ref.at[me], send_sem=send_sems.at[k],
                                              recv_sem=recv_sems.at[k], device_id=peer, device_id_type=_MESH)
            cp.start()
            sends.append(cp)
        for k, (px, py, pc) in enumerate(peers):
            slot = out_ref.at[4 * px + 2 * py + pc]
            pltpu.make_async_remote_copy(src_ref=slot, dst_ref=slot, send_sem=send_sems.at[k], recv_sem=recv_sems.at[k],
                                         device_id=(px, py, pc), device_id_type=_MESH).wait_recv()
        for cp in sends:
            cp.wait_send()
        own.wait()

    return _ccall(body, name=name, in_specs=[_HBM], out_specs=_HBM, out_shape=_sds((8,) + buf.shape, buf.dtype),
                  scratch_shapes=[pltpu.SemaphoreType.DMA((7,)), pltpu.SemaphoreType.DMA((7,)), pltpu.SemaphoreType.DMA])(buf)


def _row_tile(rows, cap):
    if rows <= cap:
        return rows
    return max(t for t in range(16, cap + 1, 16) if rows % t == 0)


def _sum_slots(stack, *, name, cap=256):
    S, Rr, C = stack.shape
    tr = _row_tile(Rr, cap)

    def body(s_ref, o_ref):
        acc = s_ref[0].astype(F32)
        for j in range(1, S):
            acc = acc + s_ref[j].astype(F32)
        o_ref[...] = acc

    return _pcall(body, name=name, grid=(Rr // tr,), in_specs=[pl.BlockSpec((S, tr, C), lambda i: (0, i, 0))],
                  out_specs=pl.BlockSpec((tr, C), lambda i: (i, 0)), out_shape=_sds((Rr, C), F32))(stack)


def _adamw(g_parts, w, m, v, *, name, cap=128):
    Rr, C = w.shape
    tr = _row_tile(Rr, cap)
    ng = len(g_parts)
    c1 = 1.0 / (1.0 - ADAM_B1 ** ADAM_STEP)
    c2 = 1.0 / (1.0 - ADAM_B2 ** ADAM_STEP)

    def body(*refs):
        g = refs[0][...]
        for r in refs[1:ng]:
            g = g + r[...]
        w_ref, m_ref, v_ref, g_out, d_out, m_out, v_out = refs[ng:]
        mn = ADAM_B1 * m_ref[...] + (1.0 - ADAM_B1) * g
        vn = ADAM_B2 * v_ref[...] + (1.0 - ADAM_B2) * jnp.square(g)
        g_out[...] = g
        m_out[...] = mn
        v_out[...] = vn
        d_out[...] = -ADAM_LR * ((mn * c1) / (jnp.sqrt(vn * c2) + ADAM_EPS) + ADAM_WD * w_ref[...])

    spec = pl.BlockSpec((tr, C), lambda i: (i, 0))
    return _pcall(body, name=name, grid=(Rr // tr,), in_specs=[spec] * (ng + 3), out_specs=[spec] * 4,
                  out_shape=[_sds((Rr, C), F32)] * 4)(*g_parts, w, m, v)


def _adamw_layers(mine, other, w3, m3, v3, *, name, cap=128):
    _, Rr, C = w3.shape
    tr = _row_tile(Rr, cap)
    nt = Rr // tr
    c1 = 1.0 / (1.0 - ADAM_B1 ** ADAM_STEP)
    c2 = 1.0 / (1.0 - ADAM_B2 ** ADAM_STEP)

    def body(m0, m1, o0, o1, w_ref, m_ref, v_ref, g_out, d_out, m_out, v_out):
        g = jnp.where(pl.program_id(0) == 0, m0[...] + o0[...], m1[...] + o1[...])
        mn = ADAM_B1 * m_ref[...] + (1.0 - ADAM_B1) * g
        vn = ADAM_B2 * v_ref[...] + (1.0 - ADAM_B2) * jnp.square(g)
        g_out[...] = g
        m_out[...] = mn
        v_out[...] = vn
        d_out[...] = -ADAM_LR * ((mn * c1) / (jnp.sqrt(vn * c2) + ADAM_EPS) + ADAM_WD * w_ref[...])

    g0 = pl.BlockSpec((tr, C), lambda l, i: (jnp.where(l == 0, i, nt - 1), 0))
    g1 = pl.BlockSpec((tr, C), lambda l, i: (jnp.where(l == 1, i, 0), 0))
    s3 = pl.BlockSpec((None, tr, C), lambda l, i: (l, i, 0))
    return _pcall(body, name=name, grid=(2, nt), in_specs=[g0, g1, g0, g1, s3, s3, s3], out_specs=[s3] * 4,
                  out_shape=[_sds(w3.shape, F32)] * 4)(mine[0], mine[1], other[0], other[1], w3, m3, v3)


def _pack(arrs, dtype, row_mult):
    flat = jnp.concatenate([a.reshape(-1).astype(dtype) for a in arrs])
    n = flat.shape[0]
    unit = row_mult * PACK_W
    total = unit * ((n + unit - 1) // unit)
    if total > n:
        flat = jnp.concatenate([flat, jnp.zeros((total - n,), dtype)])
    return flat.reshape(-1, PACK_W)


def _unpack(buf, shapes):
    flat = buf.reshape(-1)
    out, off = [], 0
    for s in shapes:
        n = math.prod(s)
        out.append(flat[off:off + n].reshape(s))
        off += n
    return out


def _conf_pre(a, g):
    return [a * jax.nn.sigmoid(g)]


def _conf_post(cs, cb, lg, lb):
    return (_silu(_layer_norm(cs[0] + cb, lg, lb)),)


def _xbc_post(cs):
    return tuple(_silu(c) for c in cs)


def _ffn_post(cs):
    return (_silu(cs[0]) * cs[1],)


def _mix_fn(ga, gb, ya, yb):
    return (jax.nn.sigmoid(ga) * ya + jax.nn.sigmoid(gb) * yb,)


def _res_ln_fn(h, r, g, b):
    return (_layer_norm(ALPHA * h + r, g, b),)


def _ln_fn(x, g, b):
    return (_layer_norm(x, g, b),)


def _carrying(carry, key, gr, call):
    if key not in carry:
        return call(None)
    comm, done = carry[key](gr)
    out, got = call(comm)
    done(got)
    return out


def _two_copies(fn):
    def wrapped(*args):
        return fn(*args) * 2
    return wrapped


def _copies_out(T, tm):
    return [_o2(T, D, F32, tm), _o2(T, D, BF16, tm)]


def _layer_fwd(h, hb, W, l, carry):
    T = h.shape[0]
    tm = min(TM, T)
    nt = T // tm
    tmf = min(TM_FFN, T)
    ntf = T // tmf
    nm = lambda s: f"l{l}_{s}"
    u3 = _carrying(carry, "u", None,
                   lambda comm: _mm_resident_slab_out(hb, W["w_p"], name=nm("u"), width=D, tm=TM_X, comm=comm))
    dtr = _mm_nn(hb, W["w_dt"], name=nm("dt"))
    v3, cconv = _conf_fwd(u3, W["conv_w"], W["conv_b"], W["conv_ln_g"], W["conv_ln_b"], name=nm("conf"))
    ya = _mm_nn(v3, W["w_co"], name=nm("ya"))
    tmx = min(TM_X, T)
    x0, x1, bc = _dwconv_fwd(_xbc_post, [_slab(u3, 6, tmx), _slab(u3, 7, tmx), _slab(u3, 8, tmx)], W["ssm_w"],
                             W["ssm_b"], [_o2(T, D, F32, tmx)] * 3, K=SSM_K, C=D, name=nm("xbc"), tm=tmx, nt=T // tmx)
    yn, sprev = _carrying(carry, "ssd", None, lambda comm: _ssd_fwd(
        x0, x1, bc, dtr, u3, W["a_log"], W["dt_bias"], W["d_skip"], W["norm_w"], name=nm("ssd"), comm=comm))
    yb = _mm_nn(yn, W["w_so"], name=nm("yb"), tk=min(SD, 1024))
    (m,) = _rows_fwd(_mix_fn, [_slab(u3, 4, tm), _slab(u3, 5, tm), _r2(ya, tm), _r2(yb, tm)], [],
                     [_o2(T, D, BF16, tm)], name=nm("mix"), nt=nt)
    mix = _mm_nn(m, W["w_o"], name=nm("wo"))
    h1, h1b = _rows_fwd(_two_copies(_res_ln_fn), [_r2(h, tm), _r2(mix, tm)], [W["ln1_g"], W["ln1_b"]],
                        _copies_out(T, tm), name=nm("ln1"), nt=nt)
    up3 = _mm_resident_slab_out(h1b, W["w_up"], name=nm("up"), width=FFN, tm=TM)
    (f,) = _dwconv_fwd(_ffn_post, [_slab(up3, 0, tmf), _slab(up3, 1, tmf)], W["ffn_w"], W["ffn_b"],
                       [_o2(T, FFN, BF16, tmf)], K=FFN_K, C=FFN, name=nm("ffnact"), tm=tmf, nt=ntf)
    ffn = _mm_nn(f, W["w_dn"], name=nm("dn"))
    h2, h2b = _rows_fwd(_two_copies(_res_ln_fn), [_r2(h1, tm), _r2(ffn, tm)], [W["ln2_g"], W["ln2_b"]],
                        _copies_out(T, tm), name=nm("ln2"), nt=nt)
    saved = dict(h=h, hb=hb, u3=u3, dtr=dtr, v3=v3, cconv=cconv, ya=ya, x0=x0, x1=x1, bc=bc, sprev=sprev, yn=yn, yb=yb, m=m,
                 mix=mix, h1=h1, h1b=h1b, up3=up3, f=f, ffn=ffn)
    return h2, h2b, saved


def _layer_bwd(dh2, W, sv, l, carry):
    T = dh2.shape[0]
    tm = min(TM, T)
    nt = T // tm
    tmf = min(TM_FFN, T)
    ntf = T // tmf
    nm = lambda s: f"l{l}_{s}"
    gr = {}
    (dres2, dffn), (gr["ln2_g"], gr["ln2_b"]) = _rows_bwd(
        _res_ln_fn, [_r2(sv["h1"], tm), _r2(sv["ffn"], tm)], [W["ln2_g"], W["ln2_b"]], [_r2(dh2, tm)],
        [_o2(T, D, F32, tm), _o2(T, D, BF16, tm)], name=nm("ln2_b"), nt=nt)
    df = _mm_nt(dffn, W["w_dn"], name=nm("dn_dx"))
    gr["w_dn"] = _mm_tn(sv["f"], dffn, name=nm("dn_dw"), tmo=FFN // 2)
    up3 = sv["up3"]
    (dgate, dval), gr["ffn_w"], gr["ffn_b"] = _carrying(carry, "ffnact_b", gr, lambda comm: _dwconv_bwd(
        _ffn_post, [_slab(up3, 0, tmf), _slab(up3, 1, tmf)],
        [_halo_slab(up3, 0, tmf, DW_HALO), _halo_slab(up3, 1, tmf, DW_HALO)], W["ffn_w"], W["ffn_b"], [_r2(df, tmf)],
        K=FFN_K, C=FFN, name=nm("ffnact_b"), tm=tmf, nt=ntf, comm=comm))
    dh1 = _mm_cols_nt([dgate, dval], W["w_up"], name=nm("up_dx"), add=dres2)
    gr["w_up"] = _mm_tn_cols(sv["h1b"], [dgate, dval], name=nm("up_dw"), tmo=min(512, D))
    (dres1, dmix), (gr["ln1_g"], gr["ln1_b"]) = _rows_bwd(
        _res_ln_fn, [_r2(sv["h"], tm), _r2(sv["mix"], tm)], [W["ln1_g"], W["ln1_b"]], [_r2(dh1, tm)],
        [_o2(T, D, F32, tm), _o2(T, D, BF16, tm)], name=nm("ln1_b"), nt=nt)
    dm = _mm_nt(dmix, W["w_o"], name=nm("wo_dx"))
    gr["w_o"] = _mm_tn(sv["m"], dmix, name=nm("wo_dw"), tmo=min(512, D))
    u3 = sv["u3"]
    (dga, dgb, dya, dyb), _ = _rows_bwd(
        _mix_fn, [_slab(u3, 4, tm), _slab(u3, 5, tm), _r2(sv["ya"], tm), _r2(sv["yb"], tm)], [], [_r2(dm, tm)],
        [_o2(T, D, BF16, tm)] * 4, name=nm("mix_b"), nt=nt)
    dv3 = _mm_nt(dya, W["w_co"], name=nm("ya_dx"))
    gr["w_co"] = _mm_tn(sv["v3"], dya, name=nm("ya_dw"), tmo=min(512, D))
    da, dg, gr["conv_w"], gr["conv_b"], gr["conv_ln_g"], gr["conv_ln_b"] = _carrying(carry, "conf_b", gr, lambda comm: _conf_bwd(
        u3, sv["cconv"], dv3, W["conv_w"], W["conv_b"], W["conv_ln_g"], W["conv_ln_b"], name=nm("conf_b"), comm=comm))
    dyn = _mm_nt(dyb, W["w_so"], name=nm("yb_dx"))
    gr["w_so"] = _mm_tn(sv["yn"], dyb, name=nm("yb_dw"), tmo=min(512, SD))
    (dx0, dx1, dbc, ddtr, dz0, dz1, gr["a_log"], gr["dt_bias"], gr["d_skip"], gr["norm_w"]) = _carrying(
        carry, "ssd_b", gr, lambda comm: _ssd_bwd(
            sv["x0"], sv["x1"], sv["bc"], sv["dtr"], u3, sv["sprev"], dyn, W["a_log"], W["dt_bias"], W["d_skip"],
            W["norm_w"], name=nm("ssd_b"), comm=comm))
    tmx = min(TM_X, T)
    (du6, du7, du8), gr["ssm_w"], gr["ssm_b"] = _dwconv_bwd(
        _xbc_post, [_slab(u3, 6, tmx), _slab(u3, 7, tmx), _slab(u3, 8, tmx)],
        [_halo_slab(u3, 6, tmx, DW_HALO), _halo_slab(u3, 7, tmx, DW_HALO), _halo_slab(u3, 8, tmx, DW_HALO)],
        W["ssm_w"], W["ssm_b"], [_r2(dx0, tmx), _r2(dx1, tmx), _r2(dbc, tmx)], K=SSM_K, C=D, name=nm("xbc_b"),
        tm=tmx, nt=T // tmx)
    du = [da, dg, dz0, dz1, dga, dgb, du6, du7, du8]
    gr["w_dt"] = _mm_tn(sv["hb"], ddtr, name=nm("dt_dw"), tmo=min(512, D))
    gr["w_p"] = _carrying(carry, "u_dw", gr, lambda comm: _mm_tn_cols(sv["hb"], du, name=nm("u_dw"),
                                                                      tmo=min(512, D), comm=comm))
    dh_a = _mm_nt(ddtr, W["w_dt"], name=nm("dt_dx"), add=dres1)
    dh = _carrying(carry, "u_dx", gr, lambda comm: _mm_cols_nt(du, W["w_p"], name=nm("u_dx"), add=dh_a, comm=comm))
    return dh, gr


_U_SPLIT = (2 * D + SD, 2 * D + SD + XBC, 2 * D + SD + XBC + H)


def _pad_rows(a, rows):
    return jnp.concatenate([a, jnp.zeros((rows - a.shape[0],) + a.shape[1:], a.dtype)], axis=0)


def _pad_lanes(a, lanes):
    return jnp.concatenate([a, jnp.zeros(a.shape[:-1] + (lanes - a.shape[-1],), a.dtype)], axis=-1)


def _w_in_layout(w_in):
    e0, e1, e2 = _U_SPLIT
    return dict(w_p=jnp.concatenate([w_in[:, :e0], w_in[:, e2:], w_in[:, e0:e1]], axis=1),
                w_dt=_pad_lanes(w_in[:, e1:e2], HP))


_MM_KEY = dict(w_conv_out="w_co", w_ssm_out="w_so", w_o="w_o", w_ffn_up="w_up", w_ffn_down="w_dn")


def _small_layer_weights(full, l):
    row = lambda a: a.reshape(1, -1)
    ssm_w = full["ssm_conv_w"][l]
    ffn_w = full["ffn_dw_w"][l]
    ssm_b = full["ssm_conv_b"][l]
    ffn_b = full["ffn_dw_b"][l]
    W = dict(
        conv_w=_pad_rows(full["conv_dw_w"][l], 32),
        conv_b=row(full["conv_dw_b"][l]), conv_ln_g=row(full["conv_ln_g"][l]), conv_ln_b=row(full["conv_ln_b"][l]),
        ssm_w=jnp.stack([_pad_rows(ssm_w[:, p * D:(p + 1) * D], 8) for p in range(3)]),
        ssm_b=[row(ssm_b[p * D:(p + 1) * D]) for p in range(3)],
        a_log=_pad_lanes(row(full["ssm_a_log"][l]), HP), dt_bias=_pad_lanes(row(full["ssm_dt_bias"][l]), HP),
        d_skip=_pad_lanes(row(full["ssm_d"][l]), HP), norm_w=row(full["ssm_norm_w"][l]),
        ln1_g=row(full["ln1_g"][l]), ln1_b=row(full["ln1_b"][l]),
        ffn_w=jnp.stack([_pad_rows(ffn_w[:, p * FFN:(p + 1) * FFN], 8) for p in range(2)]),
        ffn_b=[row(ffn_b[p * FFN:(p + 1) * FFN]) for p in range(2)],
        ln2_g=row(full["ln2_g"][l]), ln2_b=row(full["ln2_b"][l]),
    )
    return W


def _w_in_grad(gr):
    e0 = _U_SPLIT[0]
    wp = gr["w_p"]
    return jnp.concatenate([wp[:, :e0], wp[:, e0 + 2 * D:e0 + 2 * D + XBC], gr["w_dt"][:, :H], wp[:, e0:e0 + 2 * D]], axis=1)


def _layer_grads_to_reference_layout(gr):
    return dict(
        w_in=_w_in_grad(gr), conv_dw_w=gr["conv_w"][:CONV_K], conv_dw_b=gr["conv_b"][0], conv_ln_g=gr["conv_ln_g"][0],
        conv_ln_b=gr["conv_ln_b"][0], w_conv_out=gr["w_co"],
        ssm_conv_w=jnp.concatenate([gr["ssm_w"][p, :SSM_K] for p in range(3)], axis=1),
        ssm_conv_b=jnp.concatenate([b[0] for b in gr["ssm_b"]]),
        ssm_dt_bias=gr["dt_bias"][0, :H], ssm_a_log=gr["a_log"][0, :H], ssm_d=gr["d_skip"][0, :H],
        ssm_norm_w=gr["norm_w"][0], w_ssm_out=gr["w_so"], w_o=gr["w_o"], ln1_g=gr["ln1_g"][0], ln1_b=gr["ln1_b"][0],
        w_ffn_up=gr["w_up"], ffn_dw_w=jnp.concatenate([gr["ffn_w"][p, :FFN_K] for p in range(2)], axis=1),
        ffn_dw_b=jnp.concatenate([b[0] for b in gr["ffn_b"]]), w_ffn_down=gr["w_dn"], ln2_g=gr["ln2_g"][0],
        ln2_b=gr["ln2_b"][0],
    )


_BIG = dict(w_in=2, w_conv_out=1, w_ssm_out=1, w_o=1, w_ffn_up=2, w_ffn_down=1)
_SMALL_SHARDED = dict(conv_dw_w=2, ssm_conv_w=2, ffn_dw_w=2)
_REPLICATED = ("ln_in_g", "ln_in_b", "conv_dw_b", "conv_ln_g", "conv_ln_b", "ssm_conv_b", "ssm_dt_bias", "ssm_a_log",
               "ssm_d", "ssm_norm_w", "ln1_g", "ln1_b", "ffn_dw_b", "ln2_g", "ln2_b")
_WEIGHTS = ("ln_in_g", "ln_in_b", "w_in", "conv_dw_w", "conv_dw_b", "conv_ln_g", "conv_ln_b", "w_conv_out", "ssm_conv_w",
            "ssm_conv_b", "ssm_dt_bias", "ssm_a_log", "ssm_d", "ssm_norm_w", "w_ssm_out", "w_o", "ln1_g", "ln1_b",
            "w_ffn_up", "ffn_dw_w", "ffn_dw_b", "w_ffn_down", "ln2_g", "ln2_b")


def _split_chips(a, axis):
    rows, cols = a.shape
    if axis == 0:
        return a.reshape(NCHIP, rows // NCHIP, cols)
    return a.reshape(rows, NCHIP, cols // NCHIP).transpose(1, 0, 2)


def kernel(x, ln_in_g, ln_in_b, w_in, conv_dw_w, conv_dw_b, conv_ln_g, conv_ln_b, w_conv_out, ssm_conv_w, ssm_conv_b, ssm_dt_bias, ssm_a_log, ssm_d, ssm_norm_w, w_ssm_out, w_o, ln1_g, ln1_b, w_ffn_up, ffn_dw_w, ffn_dw_b, w_ffn_down, ln2_g, ln2_b, loss_target, m_ln_in_g, m_ln_in_b, m_w_in, m_conv_dw_w, m_conv_dw_b, m_conv_ln_g, m_conv_ln_b, m_w_conv_out, m_ssm_conv_w, m_ssm_conv_b, m_ssm_dt_bias, m_ssm_a_log, m_ssm_d, m_ssm_norm_w, m_w_ssm_out, m_w_o, m_ln1_g, m_ln1_b, m_w_ffn_up, m_ffn_dw_w, m_ffn_dw_b, m_w_ffn_down, m_ln2_g, m_ln2_b, v_ln_in_g, v_ln_in_b, v_w_in, v_conv_dw_w, v_conv_dw_b, v_conv_ln_g, v_conv_ln_b, v_w_conv_out, v_ssm_conv_w, v_ssm_conv_b, v_ssm_dt_bias, v_ssm_a_log, v_ssm_d, v_ssm_norm_w, v_w_ssm_out, v_w_o, v_ln1_g, v_ln1_b, v_w_ffn_up, v_ffn_dw_w, v_ffn_dw_b, v_w_ffn_down, v_ln2_g, v_ln2_b):
    args = locals()
    w = {n: args[n] for n in _WEIGHTS}
    mom = {n: args["m_" + n] for n in _WEIGHTS}
    vel = {n: args["v_" + n] for n in _WEIGHTS}
    T = x.shape[1]
    tm = min(TM, T)
    nt = T // tm
    chip = 2 * lax.axis_index("x") + lax.axis_index("y")

    assert DEPTH == 2
    big_names, small_names = list(_BIG), list(_SMALL_SHARDED)
    rest_big = [n for n in big_names if n != "w_in"]
    bf = {n: w[n].astype(BF16) for n in big_names}
    join = lambda got, axis: jnp.concatenate([got[j] for j in range(NCHIP)], axis=axis)
    first = _chip_exchange([bf["w_in"][0]] + [w[n] for n in small_names], scatter=False, name="gather_first")
    full = {n: join(gk, _SMALL_SHARDED[n]) for n, gk in zip(small_names, first[1:])}
    for n in _REPLICATED:
        full[n] = w[n]
    Ws = [_small_layer_weights(full, l) for l in range(DEPTH)]
    Ws[0].update(_w_in_layout(join(first[0], 1)))

    def rest_arrived(l):
        def done(got):
            for n, gk in zip(rest_big, got):
                Ws[l][_MM_KEY[n]] = join(gk, _BIG[n] - 1)
        return done

    carry_fwd = [
        {"u": lambda gr: (_exchange_comm([bf[n][0] for n in rest_big], scatter=False), rest_arrived(0)),
         "ssd": lambda gr: (_exchange_comm([bf["w_in"][1]], scatter=False),
                            lambda got: Ws[1].update(_w_in_layout(join(got[0], 1))))},
        {"u": lambda gr: (_exchange_comm([bf[n][1] for n in rest_big], scatter=False), rest_arrived(1))},
    ]

    x2 = x.reshape(T, D)
    g_in, b_in = ln_in_g.reshape(1, D), ln_in_b.reshape(1, D)
    h, hb = _rows_fwd(_two_copies(_ln_fn), [_r2(x2, tm)], [g_in, b_in], _copies_out(T, tm), name="ln_in", nt=nt)
    saved = []
    for l in range(DEPTH):
        h, hb, sv = _layer_fwd(h, hb, Ws[l], l, carry_fwd[l])
        saved.append(sv)
    loss_row, dh = _loss_head(h, loss_target.reshape(T, D), name="loss")

    arrived = {}

    def exchange(names, l, grads):
        def make(gr):
            src = grads(gr)
            def done(got):
                for n, gk in zip(names, got):
                    arrived[(n, l)] = gk
            return _exchange_comm([_split_chips(src[n], _BIG[n] - 1) for n in names], scatter=True), done
        return make

    layer_grads = [None] * DEPTH
    dh, gr = _layer_bwd(dh, Ws[1], saved[1], 1, {})
    layer_grads[1] = _layer_grads_to_reference_layout(gr)
    g1 = lambda gr: layer_grads[1]
    g0 = lambda gr: {n: gr[_MM_KEY[n]] for n in rest_big}
    dh, gr = _layer_bwd(dh, Ws[0], saved[0], 0, {
        "ffnact_b": exchange(["w_in"], 1, g1),
        "conf_b": exchange(["w_ffn_up", "w_ffn_down"], 1, g1),
        "ssd_b": exchange(["w_conv_out", "w_ssm_out", "w_o"], 1, g1),
        "u_dw": exchange(rest_big, 0, g0),
        "u_dx": exchange(["w_in"], 0, lambda gr: {"w_in": _w_in_grad(gr)})})
    layer_grads[0] = _layer_grads_to_reference_layout(gr)
    (grad_x2,), (d_g_in, d_b_in) = _rows_bwd(_ln_fn, [_r2(x2, tm)], [g_in, b_in], [_r2(dh, tm)], [_o2(T, D, F32, tm)],
                                             name="ln_in_b", nt=nt)
    local = {n: jnp.stack([layer_grads[l][n] for l in range(DEPTH)]) for n in _WEIGHTS[2:] if n not in _BIG}
    local["ln_in_g"], local["ln_in_b"] = d_g_in[0], d_b_in[0]
    res = [{}, {}, {}, {}]

    keys = [(n, l) for n in big_names for l in range(DEPTH)]
    mine = [_sum_slots(arrived[k], name=f"sum_chips_{k[0]}_{k[1]}") for k in keys]
    other = _core_swap(mine, name="swap_cores")
    for i, n in enumerate(big_names):
        outs = _adamw_layers(mine[2 * i:2 * i + 2], other[2 * i:2 * i + 2], w[n], mom[n], vel[n], name="adamw_" + n)
        for q in range(4):
            res[q][n] = outs[q]

    rest_names = list(_REPLICATED) + small_names
    part = _pack([loss_row] + [local[n] for n in rest_names], F32, 8)
    parts = _all_gather8(part, name="gather_small")
    total = _sum_slots(parts, name="sum_devices")
    tot = _unpack(total, [loss_row.shape] + [local[n].shape for n in rest_names])
    loss = tot[0][0, 0]
    g_rest = {}
    for n, t in zip(rest_names, tot[1:]):
        if n in _SMALL_SHARDED:
            ax = _SMALL_SHARDED[n]
            t = lax.dynamic_slice_in_dim(t, chip * w[n].shape[ax], w[n].shape[ax], axis=ax)
        g_rest[n] = t
    pk = lambda d: _pack([d[n] for n in rest_names], F32, 8)
    rest_out = _adamw([pk(g_rest)], pk(w), pk(mom), pk(vel), name="adamw_rest")
    rest_out = [_unpack(o, [w[n].shape for n in rest_names]) for o in rest_out]

    for q in range(4):
        for k, n in enumerate(rest_names):
            res[q][n] = rest_out[q][k]
    grad_x = grad_x2.reshape(x.shape)
    return (loss, grad_x, *[res[0][n] for n in _WEIGHTS], *[res[1][n] for n in _WEIGHTS],
            *[res[2][n] for n in _WEIGHTS], *[res[3][n] for n in _WEIGHTS])
```

```python
import functools
import math

import jax
import jax.numpy as jnp
from jax import lax
from jax.experimental import pallas as pl
from jax.experimental.pallas import tpu as pltpu

F32 = jnp.float32
BF16 = jnp.bfloat16

D = 1024
DEPTH = 2
CONV_K = 31
SD = 2 * D
P = 64
H = SD // P
G = 4
R = H // G
N = 128
RP = R * P
SSM_K = 4
L = 128
XBC = SD + 2 * G * N
FFN = 2816
FFN_K = 3
IN_DIM = 2 * D + SD + XBC + H + 2 * D
ALPHA = (2 * DEPTH) ** 0.25
LN_EPS = 1e-5
RMS_EPS = 1e-5
ADAM_LR, ADAM_B1, ADAM_B2, ADAM_EPS, ADAM_WD, ADAM_STEP = 0.001, 0.9, 0.999, 1e-08, 0.01, 10

HP = 128
NCHIP = 4
PACK_W = 1024
VMEM_LIMIT = 56 * 1024 * 1024
TM = 512
TM_X = 256
TM_FFN = 256
TK = 1024

assert D == 2 * RP and 2 * G * N == D and XBC == 3 * D and H <= HP


def _pcall(body, *, name, grid=(), in_specs, out_specs, out_shape, scratch_shapes=()):
    params = pltpu.CompilerParams(vmem_limit_bytes=VMEM_LIMIT, dimension_semantics=("arbitrary",) * len(grid))
    return pl.pallas_call(body, name=name, grid=grid, in_specs=in_specs, out_specs=out_specs, out_shape=out_shape,
                          scratch_shapes=list(scratch_shapes), compiler_params=params)


def _pcall_carrying(body, comm, *, name, grid, in_specs, out_specs, out_shape, scratch_shapes=()):
    in_specs, out_specs, out_shape = list(in_specs), list(out_specs), list(out_shape)
    scratch_shapes = list(scratch_shapes)
    n_in, n_out, n_scr = len(in_specs), len(out_specs), len(scratch_shapes)
    nci, nco = len(comm["ins"]), len(comm["outs"])

    def wrapped(*refs):
        ins, cin = refs[:n_in], refs[n_in:n_in + nci]
        outs = refs[n_in + nci:n_in + nci + n_out]
        cout = refs[n_in + nci + n_out:n_in + nci + n_out + nco]
        scr = refs[n_in + nci + n_out + nco:n_in + nci + n_out + nco + n_scr]
        csem = refs[n_in + nci + n_out + nco + n_scr:]
        ids = [pl.program_id(ax) for ax in range(len(grid))]
        first = functools.reduce(jnp.logical_and, [i == 0 for i in ids])
        last = functools.reduce(jnp.logical_and, [i == g - 1 for i, g in zip(ids, grid)])

        @pl.when(first)
        def _():
            comm["start"](cin, cout, csem)

        body(*ins, *outs, *scr)

        @pl.when(last)
        def _():
            comm["wait"](cin, cout, csem)

    call = _pcall(wrapped, name=name, grid=grid, in_specs=in_specs + [_HBM] * nci, out_specs=out_specs + [_HBM] * nco,
                  out_shape=out_shape + list(comm["outs"]), scratch_shapes=scratch_shapes + list(comm["sems"]))

    def run(*operands):
        res = call(*operands, *comm["ins"])
        return list(res[:n_out]), list(res[n_out:])

    return run


def _pcall_maybe_carrying(body, comm, **kw):
    if comm is not None:
        return _pcall_carrying(body, comm, **kw)
    call = _pcall(body, **kw)
    return lambda *operands: (list(call(*operands)), None)


def _ccall(body, *, name, in_specs, out_specs, out_shape, scratch_shapes):
    return pl.pallas_call(body, name=name, in_specs=in_specs, out_specs=out_specs, out_shape=out_shape,
                          scratch_shapes=list(scratch_shapes))


def _full_spec(a):
    nd = a.ndim
    return pl.BlockSpec(a.shape, lambda *_: (0,) * nd)


def _sds(shape, dtype):
    return jax.ShapeDtypeStruct(tuple(shape), dtype)


def _mm(a, b, *, name, grid, a_spec, b_spec, o_spec, out_shape, acc_shape, trans_a=False, trans_b=False, add=None,
        add_spec=None, comm=None):
    nk = grid[2]
    dn = (((0 if trans_a else 1,), (1 if trans_b else 0,)), ((), ()))
    has_add = add is not None

    def body(*refs):
        a_ref, b_ref = refs[0], refs[1]
        add_ref = refs[2] if has_add else None
        o_ref = refs[3] if has_add else refs[2]
        part = lax.dot_general(a_ref[...].astype(BF16), b_ref[...].astype(BF16), dn, preferred_element_type=F32)

        def finish(res):
            if has_add:
                res = res + add_ref[...]
            o_ref[...] = res.astype(o_ref.dtype)

        if nk == 1:
            finish(part)
        else:
            acc = refs[-1]
            k = pl.program_id(2)

            @pl.when(k == 0)
            def _():
                acc[...] = part

            @pl.when(k > 0)
            def _():
                acc[...] += part

            @pl.when(k == nk - 1)
            def _():
                finish(acc[...])

    ins = [a, b] + ([add] if has_add else [])
    specs = [a_spec, b_spec] + ([add_spec] if has_add else [])
    scratch = [pltpu.VMEM(acc_shape, F32)] if nk > 1 else []
    if comm is not None:
        (out,), got = _pcall_carrying(body, comm, name=name, grid=grid, in_specs=specs, out_specs=[o_spec],
                                      out_shape=[out_shape], scratch_shapes=scratch)(*ins)
        return out, got
    return _pcall(body, name=name, grid=grid, in_specs=specs, out_specs=o_spec, out_shape=out_shape,
                  scratch_shapes=scratch)(*ins)


def _mm_nn(a, b, *, name, out_dtype=F32, tn=None, tk=None, add=None):
    M, K = a.shape
    Nn = b.shape[1]
    tm = min(TM, M)
    tn = Nn if tn is None else tn
    tk = K if tk is None else tk
    grid = (M // tm, Nn // tn, K // tk)
    return _mm(a, b, name=name, grid=grid,
               a_spec=pl.BlockSpec((tm, tk), lambda i, j, k: (i, k)),
               b_spec=pl.BlockSpec((tk, tn), lambda i, j, k: (k, j)),
               o_spec=pl.BlockSpec((tm, tn), lambda i, j, k: (i, j)),
               out_shape=_sds((M, Nn), out_dtype), acc_shape=(tm, tn), add=add,
               add_spec=pl.BlockSpec((tm, tn), lambda i, j, k: (i, j)))


def _mm_nt(a, b, *, name, add=None):
    M, K = a.shape
    Nn = b.shape[0]
    tm = min(TM, M)
    return _mm(a, b, name=name, grid=(M // tm, 1, 1), trans_b=True,
               a_spec=pl.BlockSpec((tm, K), lambda i, j, k: (i, 0)),
               b_spec=pl.BlockSpec((Nn, K), lambda i, j, k: (0, 0)),
               o_spec=pl.BlockSpec((tm, Nn), lambda i, j, k: (i, 0)),
               out_shape=_sds((M, Nn), F32), acc_shape=(tm, Nn), add=add,
               add_spec=pl.BlockSpec((tm, Nn), lambda i, j, k: (i, 0)))


def _mm_resident_slab_out(a, w, *, name, width, tm, comm=None):
    M, K = a.shape
    S = w.shape[1] // width
    tm = min(tm, M)

    def body(a_ref, w_hbm, o_ref, w_vmem, sem):
        @pl.when(pl.program_id(0) == 0)
        def _():
            cp = pltpu.make_async_copy(w_hbm, w_vmem, sem)
            cp.start()
            cp.wait()

        av = a_ref[...].astype(BF16)
        for s in range(S):
            o_ref[s] = jnp.dot(av, w_vmem[:, s * width:(s + 1) * width], preferred_element_type=F32)

    kw = dict(name=name, grid=(M // tm,),
              in_specs=[pl.BlockSpec((tm, K), lambda i: (i, 0)), pl.BlockSpec(memory_space=pl.ANY)],
              scratch_shapes=[pltpu.VMEM(w.shape, w.dtype), pltpu.SemaphoreType.DMA])
    o_spec = pl.BlockSpec((S, tm, width), lambda i: (0, i, 0))
    if comm is not None:
        (out,), got = _pcall_carrying(body, comm, out_specs=[o_spec], out_shape=[_sds((S, M, width), F32)], **kw)(a, w)
        return out, got
    return _pcall(body, out_specs=o_spec, out_shape=_sds((S, M, width), F32), **kw)(a, w)


def _mm_cols_nt(a_list, w, *, name, add, comm=None):
    S = len(a_list)
    M, width = a_list[0].shape
    Nn = w.shape[0]
    tm = min(TM, M)

    def body(*refs):
        a_refs, w_hbm, add_ref, o_ref, w_vmem, sem = refs[:S], refs[S], refs[S + 1], refs[S + 2], refs[S + 3], refs[S + 4]

        @pl.when(pl.program_id(0) == 0)
        def _():
            cp = pltpu.make_async_copy(w_hbm, w_vmem, sem)
            cp.start()
            cp.wait()

        acc = add_ref[...]
        for s in range(S):
            acc = acc + lax.dot_general(a_refs[s][...].astype(BF16), w_vmem[:, s * width:(s + 1) * width],
                                        (((1,), (1,)), ((), ())), preferred_element_type=F32)
        o_ref[...] = acc

    row = lambda C: pl.BlockSpec((tm, C), lambda i: (i, 0))
    kw = dict(name=name, grid=(M // tm,), in_specs=[row(width)] * S + [pl.BlockSpec(memory_space=pl.ANY), row(Nn)],
              scratch_shapes=[pltpu.VMEM(w.shape, w.dtype), pltpu.SemaphoreType.DMA])
    if comm is not None:
        (out,), got = _pcall_carrying(body, comm, out_specs=[row(Nn)], out_shape=[_sds((M, Nn), F32)], **kw)(*a_list, w, add)
        return out, got
    return _pcall(body, out_specs=row(Nn), out_shape=_sds((M, Nn), F32), **kw)(*a_list, w, add)


def _mm_tn_cols(a, b_list, *, name, tmo, comm=None):
    T, M = a.shape
    S = len(b_list)
    width = b_list[0].shape[1]
    tk = min(TK, T)
    nk = T // tk

    def body(*refs):
        a_ref, b_refs, o_ref, acc = refs[0], refs[1:S + 1], refs[S + 1], refs[S + 2]
        j, k = pl.program_id(1), pl.program_id(2)
        for s in range(S):
            @pl.when(j == s)
            def _():
                part = lax.dot_general(a_ref[...], b_refs[s][...], (((0,), (0,)), ((), ())), preferred_element_type=F32)

                @pl.when(k == 0)
                def _():
                    acc[...] = part

                @pl.when(k > 0)
                def _():
                    acc[...] += part

        @pl.when(k == nk - 1)
        def _():
            o_ref[...] = acc[...].astype(o_ref.dtype)

    def b_spec(s):
        return pl.BlockSpec((tk, width), lambda i, j, k: (jnp.where(j == s, k, jnp.where(j < s, 0, nk - 1)), 0))

    kw = dict(name=name, grid=(M // tmo, S, nk),
              in_specs=[pl.BlockSpec((tk, tmo), lambda i, j, k: (k, i))] + [b_spec(s) for s in range(S)],
              scratch_shapes=[pltpu.VMEM((tmo, width), F32)])
    o_spec = pl.BlockSpec((tmo, width), lambda i, j, k: (i, j))
    if comm is not None:
        (out,), got = _pcall_carrying(body, comm, out_specs=[o_spec], out_shape=[_sds((M, S * width), BF16)], **kw)(a, *b_list)
        return out, got
    return _pcall(body, out_specs=o_spec, out_shape=_sds((M, S * width), BF16), **kw)(a, *b_list)


def _mm_tn(a, b, *, name, tmo, tn=None, comm=None):
    T, M = a.shape
    Nn = b.shape[1]
    tn = Nn if tn is None else tn
    tk = min(TK, T)
    return _mm(a, b, name=name, grid=(M // tmo, Nn // tn, T // tk), trans_a=True,
               a_spec=pl.BlockSpec((tk, tmo), lambda i, j, k: (k, i)),
               b_spec=pl.BlockSpec((tk, tn), lambda i, j, k: (k, j)),
               o_spec=pl.BlockSpec((tmo, tn), lambda i, j, k: (i, j)),
               out_shape=_sds((M, Nn), BF16), acc_shape=(tmo, tn), comm=comm)


def _r2(a, tm):
    return (a, (tm, a.shape[1]), lambda i: (i, 0))


def _slab(a3, s, tm):
    return (a3, (None, tm, a3.shape[2]), lambda i: (s, i, 0))


def _o2(T, C, dtype, tm):
    return ((T, C), dtype, (tm, C), lambda i: (i, 0))


def _rows_fwd(fn, row_ins, par_ins, outs, *, name, nt):
    nr, npar = len(row_ins), len(par_ins)

    def body(*refs):
        vals = [r[...] for r in refs[:nr + npar]]
        res = fn(*vals)
        for o_ref, v in zip(refs[nr + npar:], res):
            o_ref[...] = v.astype(o_ref.dtype)

    return _pcall(body, name=name, grid=(nt,),
                  in_specs=[pl.BlockSpec(bs, im) for (_, bs, im) in row_ins] + [_full_spec(p) for p in par_ins],
                  out_specs=[pl.BlockSpec(bs, im) for (_, _, bs, im) in outs],
                  out_shape=[_sds(s, d) for (s, d, _, _) in outs])(*[r[0] for r in row_ins], *par_ins)


def _rows_bwd(fn, row_ins, par_ins, cot_ins, drow_outs, *, name, nt):
    nr, npar, nc = len(row_ins), len(par_ins), len(cot_ins)
    keep = [k for k, o in enumerate(drow_outs) if o is not None]

    def body(*refs):
        vals = [r[...].astype(F32) for r in refs[:nr + npar]]
        cots = [r[...].astype(F32) for r in refs[nr + npar:nr + npar + nc]]
        orefs = refs[nr + npar + nc:]
        _, vjp = jax.vjp(fn, *vals)
        grads = vjp(tuple(cots))
        for o_ref, k in zip(orefs[:len(keep)], keep):
            o_ref[...] = grads[k].astype(o_ref.dtype)
        prefs = orefs[len(keep):]

        @pl.when(pl.program_id(0) == 0)
        def _():
            for p_ref in prefs:
                p_ref[...] = jnp.zeros_like(p_ref)

        for p_ref, g in zip(prefs, grads[nr:]):
            p_ref[...] += g

    outs = [drow_outs[k] for k in keep]
    res = _pcall(body, name=name, grid=(nt,),
                 in_specs=[pl.BlockSpec(bs, im) for (_, bs, im) in row_ins] + [_full_spec(p) for p in par_ins]
                 + [pl.BlockSpec(bs, im) for (_, bs, im) in cot_ins],
                 out_specs=[pl.BlockSpec(bs, im) for (_, _, bs, im) in outs] + [_full_spec(p) for p in par_ins],
                 out_shape=[_sds(s, d) for (s, d, _, _) in outs] + [_sds(p.shape, F32) for p in par_ins],
                 )(*[r[0] for r in row_ins], *par_ins, *[c[0] for c in cot_ins])
    return list(res[:len(keep)]), list(res[len(keep):])


def _layer_norm(v, g, b):
    mu = jnp.mean(v, axis=-1, keepdims=True)
    var = jnp.mean(jnp.square(v - mu), axis=-1, keepdims=True)
    return (v - mu) * lax.rsqrt(var + LN_EPS) * g + b


def _silu(v):
    return v * jax.nn.sigmoid(v)


def _softplus(v):
    return jnp.maximum(v, 0.0) + jnp.log1p(jnp.exp(-jnp.abs(v)))


def _halo_of(K):
    return 8 * ((K - 1 + 7) // 8)


DW_HALO = 8
DW_RB = 16
DW_LC = 256


def _dw_taps(win, w_ref, p, ls, K, shift_of):
    acc = None
    for k in range(K):
        o = shift_of(k)
        term = win[o:o + DW_RB, :] * w_ref[p, k:k + 1, ls]
        acc = term if acc is None else acc + term
    return acc


def _dwconv_fwd(post, part_ins, w, biases, outs, *, K, C, name, tm, nt):
    nparts, halo = len(part_ins), DW_HALO

    def body(*refs):
        x_refs, w_ref = refs[:nparts], refs[nparts]
        b_refs = refs[nparts + 1:2 * nparts + 1]
        orefs, buf = refs[2 * nparts + 1:-1], refs[-1]
        i = pl.program_id(0)
        for p in range(nparts):
            @pl.when(i == 0)
            def _():
                buf[p, pl.ds(0, halo), :] = jnp.zeros((halo, C), F32)

            @pl.when(i > 0)
            def _():
                buf[p, pl.ds(0, halo), :] = buf[p, pl.ds(tm, halo), :]

            buf[p, pl.ds(halo, tm), :] = x_refs[p][...]

        def group(r, carry):
            base = pl.multiple_of(r * DW_RB, DW_RB)
            for cj in range(C // DW_LC):
                ls = slice(cj * DW_LC, (cj + 1) * DW_LC)
                cs = [_dw_taps(buf[p, pl.ds(base, DW_RB + halo), ls], w_ref, p, ls, K, lambda k: halo - (K - 1) + k)
                      + b_refs[p][:, ls] for p in range(nparts)]
                for o_ref, v in zip(orefs, post(cs)):
                    o_ref[pl.ds(base, DW_RB), ls] = v.astype(o_ref.dtype)
            return carry

        lax.fori_loop(0, tm // DW_RB, group, 0)

    return _pcall(body, name=name, grid=(nt,),
                  in_specs=[pl.BlockSpec(bs, im) for (_, bs, im) in part_ins] + [_full_spec(w)] + [_full_spec(b) for b in biases],
                  out_specs=[pl.BlockSpec(bs, im) for (_, _, bs, im) in outs],
                  out_shape=[_sds(s, d) for (s, d, _, _) in outs],
                  scratch_shapes=[pltpu.VMEM((nparts, halo + tm, C), F32)])(*[r[0] for r in part_ins], w, *biases)


def _dwconv_bwd(post, part_ins, halo_ins, w, biases, cot_ins, *, K, C, name, tm, nt, comm=None):
    nparts, halo, nc, RB = len(part_ins), DW_HALO, len(cot_ins), DW_RB
    T = nt * tm

    def body(*refs):
        x_refs, h_refs, w_ref = refs[:nparts], refs[nparts:2 * nparts], refs[2 * nparts]
        b_refs = refs[2 * nparts + 1:3 * nparts + 1]
        cot_refs = refs[3 * nparts + 1:3 * nparts + 1 + nc]
        rest = refs[3 * nparts + 1 + nc:]
        dx_refs, dw_ref, db_refs = rest[:nparts], rest[nparts], rest[nparts + 1:2 * nparts + 1]
        bufx, bufd, acc = rest[-3], rest[-2], rest[-1]
        s = pl.program_id(0)
        first_tile = s == nt - 1

        @pl.when(s == 0)
        def _():
            acc[...] = jnp.zeros_like(acc)
            for p in range(nparts):
                bufd[p, pl.ds(tm, halo), :] = jnp.zeros((halo, C), F32)

        for p in range(nparts):
            bufx[p, pl.ds(0, halo), :] = jnp.where(first_tile, 0.0, h_refs[p][...])
            bufx[p, pl.ds(halo, tm), :] = x_refs[p][...]
        fold = lambda v: v[0:8, :] + v[8:16, :]

        def conv_out_grads(r, carry):
            base = pl.multiple_of(r * RB, RB)
            for cj in range(C // DW_LC):
                ls = slice(cj * DW_LC, (cj + 1) * DW_LC)
                wins = [bufx[p, pl.ds(base, RB + halo), ls] for p in range(nparts)]
                cs = [_dw_taps(wins[p], w_ref, p, ls, K, lambda k: halo - (K - 1) + k) + b_refs[p][:, ls]
                      for p in range(nparts)]
                _, vjp = jax.vjp(lambda *c: post(list(c)), *cs)
                dcs = vjp(tuple(cr[pl.ds(base, RB), ls].astype(F32) for cr in cot_refs))
                for p in range(nparts):
                    bufd[p, pl.ds(base, RB), ls] = dcs[p]
                    for k in range(K):
                        o = halo - (K - 1) + k
                        acc[p, k, :, ls] += fold(dcs[p] * wins[p][o:o + RB, :])
                    acc[p, K, :, ls] += fold(dcs[p])
            return carry

        lax.fori_loop(0, tm // RB, conv_out_grads, 0)

        def input_grads(r, carry):
            base = pl.multiple_of(r * RB, RB)
            for cj in range(C // DW_LC):
                ls = slice(cj * DW_LC, (cj + 1) * DW_LC)
                for p in range(nparts):
                    dx = _dw_taps(bufd[p, pl.ds(base, RB + halo), ls], w_ref, p, ls, K, lambda k: K - 1 - k)
                    dx_refs[p][pl.ds(base, RB), ls] = dx.astype(dx_refs[p].dtype)
            return carry

        lax.fori_loop(0, tm // RB, input_grads, 0)
        for p in range(nparts):
            bufd[p, pl.ds(tm, halo), :] = bufd[p, pl.ds(0, halo), :]

        @pl.when(s == nt - 1)
        def _():
            dw_ref[...] = jnp.zeros_like(dw_ref)
            for p in range(nparts):
                for k in range(K):
                    dw_ref[p, k:k + 1, :] = jnp.sum(acc[p, k], axis=0, keepdims=True)
                db_refs[p][...] = jnp.sum(acc[p, K], axis=0, keepdims=True)

    rev = lambda im: (lambda s: im(nt - 1 - s))
    row = pl.BlockSpec((tm, C), lambda s: (nt - 1 - s, 0))
    res, got = _pcall_maybe_carrying(
        body, comm, name=name, grid=(nt,),
        in_specs=[pl.BlockSpec(bs, rev(im)) for (_, bs, im) in part_ins]
        + [pl.BlockSpec(bs, rev(im)) for (_, bs, im) in halo_ins]
        + [_full_spec(w)] + [_full_spec(b) for b in biases]
        + [pl.BlockSpec(bs, rev(im)) for (_, bs, im) in cot_ins],
        out_specs=[row] * nparts + [_full_spec(w)] + [_full_spec(b) for b in biases],
        out_shape=[_sds((T, C), BF16)] * nparts + [_sds(w.shape, F32)] + [_sds(b.shape, F32) for b in biases],
        scratch_shapes=[pltpu.VMEM((nparts, halo + tm, C), F32), pltpu.VMEM((nparts, tm + halo, C), F32),
                        pltpu.VMEM((nparts, K + 1, 8, C), F32)],
    )(*[r[0] for r in part_ins], *[r[0] for r in halo_ins], w, *biases, *[c[0] for c in cot_ins])
    out = (list(res[:nparts]), res[nparts], list(res[nparts + 1:]))
    return out if comm is None else (out, got)


def _halo_slab(a3, s, tm, halo):
    q = tm // halo
    return (a3, (None, halo, a3.shape[2]), lambda i: (s, jnp.maximum(i * q - 1, 0), 0))


CONF_HALO = _halo_of(CONV_K)
CONF_RB = 32


def _shifted_copies(buf, shifted, rows):
    for j in range(1, 8):
        shifted[j - 1, pl.ds(0, rows), :] = buf[pl.ds(j, rows), :]


def _shifted_rows(buf, shifted, s, base, nrows):
    j, q = s % 8, s // 8
    if j == 0:
        return buf[pl.ds(base + 8 * q, nrows), :]
    return shifted[j - 1, pl.ds(base + 8 * q, nrows), :]


def _conf_fwd(u3, w, cb, lg, lb, *, name):
    T = u3.shape[1]
    tm = min(TM_X, T)
    nt = T // tm
    K, halo, RB = CONV_K, CONF_HALO, min(CONF_RB, tm)

    def body(a_ref, g_ref, w_ref, cb_ref, lg_ref, lb_ref, v3_ref, c_ref, bufx, xs):
        i = pl.program_id(0)

        @pl.when(i == 0)
        def _():
            bufx[pl.ds(0, halo), :] = jnp.zeros((halo, D), F32)

        @pl.when(i > 0)
        def _():
            bufx[pl.ds(0, halo), :] = bufx[pl.ds(tm, halo), :]

        bufx[pl.ds(halo, tm), :] = a_ref[...] * jax.nn.sigmoid(g_ref[...])
        _shifted_copies(bufx, xs, halo + tm - 8)

        def group(r, carry):
            base = pl.multiple_of(r * RB, RB)
            acc = None
            for k in range(K):
                term = _shifted_rows(bufx, xs, halo - (K - 1) + k, base, RB) * w_ref[k:k + 1, :]
                acc = term if acc is None else acc + term
            c_ref[pl.ds(base, RB), :] = acc
            return carry

        lax.fori_loop(0, tm // RB, group, 0)
        v3_ref[...] = _conf_post([c_ref[...]], cb_ref[...], lg_ref[...], lb_ref[...])[0].astype(v3_ref.dtype)

    slab = lambda s: pl.BlockSpec((None, tm, D), lambda i: (s, i, 0))
    row = pl.BlockSpec((tm, D), lambda i: (i, 0))
    pars = [w, cb, lg, lb]
    return _pcall(body, name=name, grid=(nt,), in_specs=[slab(0), slab(1)] + [_full_spec(p) for p in pars],
                  out_specs=[row, row], out_shape=[_sds((T, D), BF16), _sds((T, D), F32)],
                  scratch_shapes=[pltpu.VMEM((halo + tm, D), F32), pltpu.VMEM((7, halo + tm - 8, D), F32)],
                  )(u3, u3, *pars)


def _conf_bwd(u3, c, dv3, w, cb, lg, lb, *, name, comm=None):
    T = u3.shape[1]
    tm = min(TM_X, T)
    nt = T // tm
    K, halo, RB = CONV_K, CONF_HALO, min(CONF_RB, tm)

    def body(a_ref, g_ref, c_ref, dv3_ref, w_ref, cb_ref, lg_ref, lb_ref,
             da_ref, dg_ref, dw_ref, dcb_ref, dlg_ref, dlb_ref, xbuf, bufd, ds, dv0):
        s = pl.program_id(0)

        @pl.when(s == 0)
        def _():
            for r in (dw_ref, dcb_ref, dlg_ref, dlb_ref):
                r[...] = jnp.zeros_like(r)
            bufd[pl.ds(tm, halo), :] = jnp.zeros((halo, D), F32)

        xin, pre_vjp = jax.vjp(lambda p, q_: _conf_pre(p, q_)[0], a_ref[...], g_ref[...])
        xbuf[...] = xin

        _, post_vjp = jax.vjp(lambda cc, b_, g_, l_: _conf_post([cc], b_, g_, l_)[0],
                              c_ref[...], cb_ref[...], lg_ref[...], lb_ref[...])
        dc, dcb, dlg, dlb = post_vjp(dv3_ref[...])
        dcb_ref[...] += dcb
        dlg_ref[...] += dlg
        dlb_ref[...] += dlb
        bufd[pl.ds(0, tm), :] = dc
        _shifted_copies(bufd, ds, tm + halo - 8)

        def dx_group(r, carry):
            base = pl.multiple_of(r * RB, RB)
            acc = None
            for k in range(K):
                term = _shifted_rows(bufd, ds, K - 1 - k, base, RB) * w_ref[k:k + 1, :]
                acc = term if acc is None else acc + term
            dv0[pl.ds(base, RB), :] = acc
            return carry

        lax.fori_loop(0, tm // RB, dx_group, 0)

        for k in range(K):
            def dw_group(r, acc):
                base = pl.multiple_of(r * RB, RB)
                prod = xbuf[pl.ds(base, RB), :] * _shifted_rows(bufd, ds, K - 1 - k, base, RB)
                for v in range(RB // 8):
                    acc = acc + prod[v * 8:(v + 1) * 8, :]
                return acc

            acc = lax.fori_loop(0, tm // RB, dw_group, jnp.zeros((8, D), F32))
            dw_ref[k:k + 1, :] += jnp.sum(acc, axis=0, keepdims=True)

        bufd[pl.ds(tm, halo), :] = bufd[pl.ds(0, halo), :]
        da, dg = pre_vjp(dv0[...])
        da_ref[...] = da.astype(da_ref.dtype)
        dg_ref[...] = dg.astype(dg_ref.dtype)

    slab = lambda sl: pl.BlockSpec((None, tm, D), lambda s: (sl, nt - 1 - s, 0))
    row = pl.BlockSpec((tm, D), lambda s: (nt - 1 - s, 0))
    pars = [w, cb, lg, lb]
    res, got = _pcall_maybe_carrying(
        body, comm, name=name, grid=(nt,),
        in_specs=[slab(0), slab(1), row, row] + [_full_spec(p) for p in pars],
        out_specs=[row, row] + [_full_spec(p) for p in pars],
        out_shape=[_sds((T, D), BF16)] * 2 + [_sds(p.shape, F32) for p in pars],
        scratch_shapes=[pltpu.VMEM((tm, D), F32), pltpu.VMEM((tm + halo, D), F32),
                        pltpu.VMEM((7, tm + halo - 8, D), F32), pltpu.VMEM((tm, D), F32)],
    )(u3, u3, c, dv3, *pars)
    return res if comm is None else (res, got)


def _dg(a, b, ca, cb):
    return lax.dot_general(a.astype(BF16), b.astype(BF16), (((ca,), (cb,)), ((), ())), preferred_element_type=F32)


@jax.custom_vjp
def _dot_nn(a, b):
    return _dg(a, b, 1, 0)


_dot_nn.defvjp(lambda a, b: (_dg(a, b, 1, 0), (a, b)),
               lambda res, g: (_dg(g, res[1], 1, 1), _dg(res[0], g, 0, 0)))


@jax.custom_vjp
def _dot_nt(a, b):
    return _dg(a, b, 1, 1)


_dot_nt.defvjp(lambda a, b: (_dg(a, b, 1, 1), (a, b)),
               lambda res, g: (_dg(g, res[1], 1, 0), _dg(g, res[0], 0, 0)))


@jax.custom_vjp
def _dot_tn(a, b):
    return _dg(a, b, 0, 0)


_dot_tn.defvjp(lambda a, b: (_dg(a, b, 0, 0), (a, b)),
               lambda res, g: (_dg(res[1], g, 1, 1), _dg(res[0], g, 1, 0)))


def _split3(v):
    hi = v.astype(BF16)
    r = v - hi.astype(F32)
    mid = r.astype(BF16)
    return hi, mid, (r - mid.astype(F32)).astype(BF16)


def _x01(v, m, cv, cm, m_left=False):
    acc = None
    for piece in _split3(v):
        t = _dg(m, piece, cm, cv) if m_left else _dg(piece, m, cv, cm)
        acc = t if acc is None else acc + t
    return acc


@jax.custom_vjp
def _expand01(v, m):
    return _x01(v, m, 1, 0)


_expand01.defvjp(lambda v, m: (_x01(v, m, 1, 0), m),
                 lambda m, g: (_x01(g, m, 1, 1), jnp.zeros_like(m)))


@jax.custom_vjp
def _mix01(m, v):
    return _x01(v, m, 0, 1, m_left=True)


_mix01.defvjp(lambda m, v: (_x01(v, m, 0, 1, m_left=True), m),
              lambda m, g: (jnp.zeros_like(m), _x01(g, m, 0, 0, m_left=True)))


def _causal():
    return lax.broadcasted_iota(jnp.int32, (L, L), 0) >= lax.broadcasted_iota(jnp.int32, (L, L), 1)


def _ssd_chunk_prep(dtr, alog, dtb):
    dt = _softplus(dtr + dtb)
    a_cs = _mix01(_causal().astype(F32), dt * (-jnp.exp(alog)))
    return dt, a_cs, a_cs.T


def _ssd_group(xs, dt, a_cs, a_csT, Bg, Cg, zg, sp, dsk, nwg, *, g):
    causal = _causal()
    hi = lax.broadcasted_iota(jnp.int32, (HP, RP), 0)
    ci = lax.broadcasted_iota(jnp.int32, (HP, RP), 1)
    lo = (hi - g * R) * P
    E = ((ci >= lo) & (ci < lo + P)).astype(F32)

    acs_e = _expand01(a_cs, E)
    dt_e = _expand01(dt, E)
    alast_e = acs_e[L - 1:L, :]
    xdt = xs * dt_e
    cb = _dot_nt(Cg, Bg)
    y_off = _dot_nn(Cg, sp) * jnp.exp(acs_e)
    yd = []
    for r in range(R):
        h = g * R + r
        seg = a_cs[:, h:h + 1] - a_csT[h:h + 1, :]
        dec = jnp.exp(jnp.where(causal, seg, -1e30))
        yd.append(_dot_nn(cb * dec, xdt[:, r * P:(r + 1) * P]))
    y = jnp.concatenate(yd, axis=1) + y_off + xs * _expand01(jnp.broadcast_to(dsk, (8, HP)), E)[0:1, :]
    yg = y * _silu(zg)
    yn = yg * lax.rsqrt(jnp.mean(jnp.square(yg), axis=-1, keepdims=True) + RMS_EPS) * nwg
    sc = _dot_tn(Bg, xdt * jnp.exp(alast_e - acs_e))
    return yn, jnp.exp(alast_e) * sp + sc


def _group_cols(g):
    return g // 2, (g % 2) * RP


def _ssd_fwd(x0, x1, bc, dtr, u3, alog, dtb, dsk, nw, *, name, comm=None):
    T = x0.shape[0]
    nc = T // L

    def body(x0_ref, x1_ref, bc_ref, dtr_ref, z0_ref, z1_ref, alog_ref, dtb_ref, dsk_ref, nw_ref, yn_ref, sp_ref, S):
        @pl.when(pl.program_id(0) == 0)
        def _():
            S[...] = jnp.zeros_like(S)

        xr, zr = (x0_ref, x1_ref), (z0_ref, z1_ref)
        dt, a_cs, a_csT = _ssd_chunk_prep(dtr_ref[...], alog_ref[...], dtb_ref[...])
        for g in range(G):
            s, off = _group_cols(g)
            sp = S[g]
            sp_ref[0, g] = sp
            yn, s_next = _ssd_group(xr[s][:, off:off + RP], dt, a_cs, a_csT, bc_ref[:, g * N:(g + 1) * N],
                                    bc_ref[:, G * N + g * N:G * N + (g + 1) * N], zr[s][:, off:off + RP], sp,
                                    dsk_ref[...], nw_ref[:, g * RP:(g + 1) * RP], g=g)
            yn_ref[:, g * RP:(g + 1) * RP] = yn.astype(yn_ref.dtype)
            S[g] = s_next

    row = lambda C: pl.BlockSpec((L, C), lambda c: (c, 0))
    zspec = lambda s: pl.BlockSpec((None, L, D), lambda c: (s, c, 0))
    pars = [alog, dtb, dsk, nw]
    res, got = _pcall_maybe_carrying(
        body, comm, name=name, grid=(nc,),
        in_specs=[row(D), row(D), row(D), row(HP), zspec(2), zspec(3)] + [_full_spec(p) for p in pars],
        out_specs=[row(SD), pl.BlockSpec((1, G, N, RP), lambda c: (c, 0, 0, 0))],
        out_shape=[_sds((T, SD), BF16), _sds((nc, G, N, RP), F32)],
        scratch_shapes=[pltpu.VMEM((G, N, RP), F32)])(x0, x1, bc, dtr, u3, u3, *pars)
    return res if comm is None else (res, got)


def _ssd_bwd(x0, x1, bc, dtr, u3, sprev, dyn, alog, dtb, dsk, nw, *, name, comm=None):
    T = x0.shape[0]
    nc = T // L

    def body(x0_ref, x1_ref, bc_ref, dtr_ref, z0_ref, z1_ref, sp_ref, dyn_ref, alog_ref, dtb_ref, dsk_ref, nw_ref,
             dx0_ref, dx1_ref, dbc_ref, ddtr_ref, dz0_ref, dz1_ref, dalog_ref, ddtb_ref, ddsk_ref, dnw_ref, dS):
        @pl.when(pl.program_id(0) == 0)
        def _():
            dS[...] = jnp.zeros_like(dS)
            for r in (dalog_ref, ddtb_ref, ddsk_ref, dnw_ref):
                r[...] = jnp.zeros_like(r)

        xr, zr = (x0_ref, x1_ref), (z0_ref, z1_ref)
        dxr, dzr = (dx0_ref, dx1_ref), (dz0_ref, dz1_ref)
        (dt, a_cs, a_csT), prep_vjp = jax.vjp(_ssd_chunk_prep, dtr_ref[...], alog_ref[...], dtb_ref[...])
        d_dt, d_acs, d_acsT = jnp.zeros((L, HP), F32), jnp.zeros((L, HP), F32), jnp.zeros((HP, L), F32)
        for g in range(G):
            s, off = _group_cols(g)
            _, vjp = jax.vjp(functools.partial(_ssd_group, g=g), xr[s][:, off:off + RP], dt, a_cs, a_csT,
                             bc_ref[:, g * N:(g + 1) * N], bc_ref[:, G * N + g * N:G * N + (g + 1) * N],
                             zr[s][:, off:off + RP], sp_ref[0, g], dsk_ref[...], nw_ref[:, g * RP:(g + 1) * RP])
            dxs, ddt_g, dacs_g, dacsT_g, dB, dC, dz, dsp, dds, dnwg = vjp((dyn_ref[:, g * RP:(g + 1) * RP], dS[g]))
            dxr[s][:, off:off + RP] = dxs
            dzr[s][:, off:off + RP] = dz.astype(dz0_ref.dtype)
            dbc_ref[:, g * N:(g + 1) * N] = dB
            dbc_ref[:, G * N + g * N:G * N + (g + 1) * N] = dC
            dS[g] = dsp
            d_dt, d_acs, d_acsT = d_dt + ddt_g, d_acs + dacs_g, d_acsT + dacsT_g
            ddsk_ref[...] += dds
            dnw_ref[:, g * RP:(g + 1) * RP] += dnwg
        ddtr, dal, ddb = prep_vjp((d_dt, d_acs, d_acsT))
        dalog_ref[...] += dal
        ddtb_ref[...] += ddb
        ddtr_ref[...] = ddtr.astype(ddtr_ref.dtype)

    row = lambda C: pl.BlockSpec((L, C), lambda c: (nc - 1 - c, 0))
    zspec = lambda s: pl.BlockSpec((None, L, D), lambda c: (s, nc - 1 - c, 0))
    pars = [alog, dtb, dsk, nw]
    res, got = _pcall_maybe_carrying(
        body, comm, name=name, grid=(nc,),
        in_specs=[row(D), row(D), row(D), row(HP), zspec(2), zspec(3),
                  pl.BlockSpec((1, G, N, RP), lambda c: (nc - 1 - c, 0, 0, 0)), row(SD)] + [_full_spec(p) for p in pars],
        out_specs=[row(D), row(D), row(D), row(HP), row(D), row(D)] + [_full_spec(p) for p in pars],
        out_shape=[_sds((T, D), F32)] * 3 + [_sds((T, HP), BF16), _sds((T, D), BF16), _sds((T, D), BF16)]
        + [_sds(p.shape, F32) for p in pars],
        scratch_shapes=[pltpu.VMEM((G, N, RP), F32)])(x0, x1, bc, dtr, u3, u3, sprev, dyn, *pars)
    return res if comm is None else (res, got)


def _loss_head(y, target, *, name):
    T = y.shape[0]
    tm = min(TM, T)

    def body(y_ref, t_ref, loss_ref, dy_ref):
        e = y_ref[...] - t_ref[...]
        dy_ref[...] = e * (1.0 / D)

        @pl.when(pl.program_id(0) == 0)
        def _():
            loss_ref[...] = jnp.zeros_like(loss_ref)

        loss_ref[...] += 0.5 * jnp.sum(jnp.mean(jnp.square(e), axis=-1, keepdims=True), axis=0, keepdims=True)

    row = pl.BlockSpec((tm, D), lambda i: (i, 0))
    return _pcall(body, name=name, grid=(T // tm,), in_specs=[row, row],
                  out_specs=[pl.BlockSpec((1, 128), lambda i: (0, 0)), row],
                  out_shape=[_sds((1, 128), F32), _sds((T, D), F32)])(y, target)


_HBM = pl.BlockSpec(memory_space=pltpu.HBM)
_MESH = pl.DeviceIdType.MESH


def _exchange_comm(bufs, *, scatter):
    nb = len(bufs)

    def copies(in_refs, out_refs, sems, with_arrivals):
        send_sems, recv_sems, local_sems = sems
        x, y, c = lax.axis_index("x"), lax.axis_index("y"), lax.axis_index("c")
        me = 2 * x + y
        peers = [(1 - x, y), (x, 1 - y), (1 - x, 1 - y)]
        own, sends, arrivals = [], [], []
        for b in range(nb):
            src_own = in_refs[b].at[me] if scatter else in_refs[b]
            own.append(pltpu.make_async_copy(src_own, out_refs[b].at[me], local_sems.at[b]))
            for k, (px, py) in enumerate(peers):
                src = in_refs[b].at[2 * px + py] if scatter else in_refs[b]
                sends.append(pltpu.make_async_remote_copy(
                    src_ref=src, dst_ref=out_refs[b].at[me], send_sem=send_sems.at[b, k], recv_sem=recv_sems.at[b, k],
                    device_id=(px, py, c), device_id_type=_MESH))
                if with_arrivals:
                    slot = out_refs[b].at[2 * px + py]
                    arrivals.append(pltpu.make_async_remote_copy(
                        src_ref=slot, dst_ref=slot, send_sem=send_sems.at[b, k], recv_sem=recv_sems.at[b, k],
                        device_id=(px, py, c), device_id_type=_MESH))
        return own, sends, arrivals

    def start(in_refs, out_refs, sems):
        own, sends, _ = copies(in_refs, out_refs, sems, False)
        for cp in own + sends:
            cp.start()

    def wait(in_refs, out_refs, sems):
        own, sends, arrivals = copies(in_refs, out_refs, sems, True)
        for cp in arrivals:
            cp.wait_recv()
        for cp in sends:
            cp.wait_send()
        for cp in own:
            cp.wait()

    return dict(ins=list(bufs), outs=[_sds(b.shape if scatter else (NCHIP,) + b.shape, b.dtype) for b in bufs],
                sems=[pltpu.SemaphoreType.DMA((nb, 3)), pltpu.SemaphoreType.DMA((nb, 3)), pltpu.SemaphoreType.DMA((nb,))],
                start=start, wait=wait)


def _chip_exchange(bufs, *, scatter, name):
    comm = _exchange_comm(bufs, scatter=scatter)
    nb = len(bufs)

    def body(*refs):
        comm["start"](refs[:nb], refs[nb:2 * nb], refs[2 * nb:])
        comm["wait"](refs[:nb], refs[nb:2 * nb], refs[2 * nb:])

    return _ccall(body, name=name, in_specs=[_HBM] * nb, out_specs=[_HBM] * nb, out_shape=comm["outs"],
                  scratch_shapes=comm["sems"])(*bufs)


def _core_swap(bufs, *, name):
    nb = len(bufs)

    def body(*refs):
        in_refs, out_refs, send_sems, recv_sems = refs[:nb], refs[nb:2 * nb], refs[2 * nb], refs[2 * nb + 1]
        x, y, c = lax.axis_index("x"), lax.axis_index("y"), lax.axis_index("c")
        cps = [pltpu.make_async_remote_copy(src_ref=in_refs[b], dst_ref=out_refs[b], send_sem=send_sems.at[b],
                                            recv_sem=recv_sems.at[b], device_id=(x, y, 1 - c), device_id_type=_MESH)
               for b in range(nb)]
        for cp in cps:
            cp.start()
        for cp in cps:
            cp.wait()

    return _ccall(body, name=name, in_specs=[_HBM] * nb, out_specs=[_HBM] * nb,
                  out_shape=[_sds(b.shape, b.dtype) for b in bufs],
                  scratch_shapes=[pltpu.SemaphoreType.DMA((nb,)), pltpu.SemaphoreType.DMA((nb,))])(*bufs)


def _all_gather8(buf, *, name):
    def body(in_ref, out_ref, send_sems, recv_sems, local_sem):
        x, y, c = lax.axis_index("x"), lax.axis_index("y"), lax.axis_index("c")
        me = 4 * x + 2 * y + c
        own = pltpu.make_async_copy(in_ref, out_ref.at[me], local_sem)
        own.start()
        flips = [(fx, fy, fc) for fx in (0, 1) for fy in (0, 1) for fc in (0, 1)][1:]
        peers = [(x ^ fx, y ^ fy, c ^ fc) for fx, fy, fc in flips]
        sends = []
        for k, peer in enumerate(peers):
            cp = pltpu.make_async_remote_copy(src_ref=in_ref, dst_ref=out_ref.at[me], send_sem=send_sems.at[k],
                                              recv_sem=recv_sems.at[k], device_id=peer, device_id_type=_MESH)
            cp.start()
            sends.append(cp)
        for k, (px, py, pc) in enumerate(peers):
            slot = out_ref.at[4 * px + 2 * py + pc]
            pltpu.make_async_remote_copy(src_ref=slot, dst_ref=slot, send_sem=send_sems.at[k], recv_sem=recv_sems.at[k],
                                         device_id=(px, py, pc), device_id_type=_MESH).wait_recv()
        for cp in sends:
            cp.wait_send()
        own.wait()

    return _ccall(body, name=name, in_specs=[_HBM], out_specs=_HBM, out_shape=_sds((8,) + buf.shape, buf.dtype),
                  scratch_shapes=[pltpu.SemaphoreType.DMA((7,)), pltpu.SemaphoreType.DMA((7,)), pltpu.SemaphoreType.DMA])(buf)


def _row_tile(rows, cap):
    if rows <= cap:
        return rows
    return max(t for t in range(16, cap + 1, 16) if rows % t == 0)


def _sum_slots(stack, *, name, cap=256):
    S, Rr, C = stack.shape
    tr = _row_tile(Rr, cap)

    def body(s_ref, o_ref):
        acc = s_ref[0].astype(F32)
        for j in range(1, S):
            acc = acc + s_ref[j].astype(F32)
        o_ref[...] = acc

    return _pcall(body, name=name, grid=(Rr // tr,), in_specs=[pl.BlockSpec((S, tr, C), lambda i: (0, i, 0))],
                  out_specs=pl.BlockSpec((tr, C), lambda i: (i, 0)), out_shape=_sds((Rr, C), F32))(stack)


def _adamw(g_parts, w, m, v, *, name, cap=128):
    Rr, C = w.shape
    tr = _row_tile(Rr, cap)
    ng = len(g_parts)
    c1 = 1.0 / (1.0 - ADAM_B1 ** ADAM_STEP)
    c2 = 1.0 / (1.0 - ADAM_B2 ** ADAM_STEP)

    def body(*refs):
        g = refs[0][...]
        for r in refs[1:ng]:
            g = g + r[...]
        w_ref, m_ref, v_ref, g_out, d_out, m_out, v_out = refs[ng:]
        mn = ADAM_B1 * m_ref[...] + (1.0 - ADAM_B1) * g
        vn = ADAM_B2 * v_ref[...] + (1.0 - ADAM_B2) * jnp.square(g)
        g_out[...] = g
        m_out[...] = mn
        v_out[...] = vn
        d_out[...] = -ADAM_LR * ((mn * c1) / (jnp.sqrt(vn * c2) + ADAM_EPS) + ADAM_WD * w_ref[...])

    spec = pl.BlockSpec((tr, C), lambda i: (i, 0))
    return _pcall(body, name=name, grid=(Rr // tr,), in_specs=[spec] * (ng + 3), out_specs=[spec] * 4,
                  out_shape=[_sds((Rr, C), F32)] * 4)(*g_parts, w, m, v)


def _adamw_layers(mine, other, w3, m3, v3, *, name, cap=128):
    _, Rr, C = w3.shape
    tr = _row_tile(Rr, cap)
    nt = Rr // tr
    c1 = 1.0 / (1.0 - ADAM_B1 ** ADAM_STEP)
    c2 = 1.0 / (1.0 - ADAM_B2 ** ADAM_STEP)

    def body(m0, m1, o0, o1, w_ref, m_ref, v_ref, g_out, d_out, m_out, v_out):
        g = jnp.where(pl.program_id(0) == 0, m0[...] + o0[...], m1[...] + o1[...])
        mn = ADAM_B1 * m_ref[...] + (1.0 - ADAM_B1) * g
        vn = ADAM_B2 * v_ref[...] + (1.0 - ADAM_B2) * jnp.square(g)
        g_out[...] = g
        m_out[...] = mn
        v_out[...] = vn
        d_out[...] = -ADAM_LR * ((mn * c1) / (jnp.sqrt(vn * c2) + ADAM_EPS) + ADAM_WD * w_ref[...])

    g0 = pl.BlockSpec((tr, C), lambda l, i: (jnp.where(l == 0, i, nt - 1), 0))
    g1 = pl.BlockSpec((tr, C), lambda l, i: (jnp.where(l == 1, i, 0), 0))
    s3 = pl.BlockSpec((None, tr, C), lambda l, i: (l, i, 0))
    return _pcall(body, name=name, grid=(2, nt), in_specs=[g0, g1, g0, g1, s3, s3, s3], out_specs=[s3] * 4,
                  out_shape=[_sds(w3.shape, F32)] * 4)(mine[0], mine[1], other[0], other[1], w3, m3, v3)


def _pack(arrs, dtype, row_mult):
    flat = jnp.concatenate([a.reshape(-1).astype(dtype) for a in arrs])
    n = flat.shape[0]
    unit = row_mult * PACK_W
    total = unit * ((n + unit - 1) // unit)
    if total > n:
        flat = jnp.concatenate([flat, jnp.zeros((total - n,), dtype)])
    return flat.reshape(-1, PACK_W)


def _unpack(buf, shapes):
    flat = buf.reshape(-1)
    out, off = [], 0
    for s in shapes:
        n = math.prod(s)
        out.append(flat[off:off + n].reshape(s))
        off += n
    return out


def _conf_pre(a, g):
    return [a * jax.nn.sigmoid(g)]


def _conf_post(cs, cb, lg, lb):
    return (_silu(_layer_norm(cs[0] + cb, lg, lb)),)


def _xbc_post(cs):
    return tuple(_silu(c) for c in cs)


def _ffn_post(cs):
    return (_silu(cs[0]) * cs[1],)


def _mix_fn(ga, gb, ya, yb):
    return (jax.nn.sigmoid(ga) * ya + jax.nn.sigmoid(gb) * yb,)


def _res_ln_fn(h, r, g, b):
    return (_layer_norm(ALPHA * h + r, g, b),)


def _ln_fn(x, g, b):
    return (_layer_norm(x, g, b),)


def _carrying(carry, key, gr, call):
    if key not in carry:
        return call(None)
    comm, done = carry[key](gr)
    out, got = call(comm)
    done(got)
    return out


def _two_copies(fn):
    def wrapped(*args):
        return fn(*args) * 2
    return wrapped


def _copies_out(T, tm):
    return [_o2(T, D, F32, tm), _o2(T, D, BF16, tm)]


def _layer_fwd(h, hb, W, l, carry):
    T = h.shape[0]
    tm = min(TM, T)
    nt = T // tm
    tmf = min(TM_FFN, T)
    ntf = T // tmf
    nm = lambda s: f"l{l}_{s}"
    u3 = _carrying(carry, "u", None,
                   lambda comm: _mm_resident_slab_out(hb, W["w_p"], name=nm("u"), width=D, tm=TM_X, comm=comm))
    dtr = _mm_nn(hb, W["w_dt"], name=nm("dt"))
    v3, cconv = _conf_fwd(u3, W["conv_w"], W["conv_b"], W["conv_ln_g"], W["conv_ln_b"], name=nm("conf"))
    ya = _mm_nn(v3, W["w_co"], name=nm("ya"))
    tmx = min(TM_X, T)
    x0, x1, bc = _dwconv_fwd(_xbc_post, [_slab(u3, 6, tmx), _slab(u3, 7, tmx), _slab(u3, 8, tmx)], W["ssm_w"],
                             W["ssm_b"], [_o2(T, D, F32, tmx)] * 3, K=SSM_K, C=D, name=nm("xbc"), tm=tmx, nt=T // tmx)
    yn, sprev = _carrying(carry, "ssd", None, lambda comm: _ssd_fwd(
        x0, x1, bc, dtr, u3, W["a_log"], W["dt_bias"], W["d_skip"], W["norm_w"], name=nm("ssd"), comm=comm))
    yb = _mm_nn(yn, W["w_so"], name=nm("yb"), tk=min(SD, 1024))
    (m,) = _rows_fwd(_mix_fn, [_slab(u3, 4, tm), _slab(u3, 5, tm), _r2(ya, tm), _r2(yb, tm)], [],
                     [_o2(T, D, BF16, tm)], name=nm("mix"), nt=nt)
    mix = _mm_nn(m, W["w_o"], name=nm("wo"))
    h1, h1b = _rows_fwd(_two_copies(_res_ln_fn), [_r2(h, tm), _r2(mix, tm)], [W["ln1_g"], W["ln1_b"]],
                        _copies_out(T, tm), name=nm("ln1"), nt=nt)
    up3 = _mm_resident_slab_out(h1b, W["w_up"], name=nm("up"), width=FFN, tm=TM)
    (f,) = _dwconv_fwd(_ffn_post, [_slab(up3, 0, tmf), _slab(up3, 1, tmf)], W["ffn_w"], W["ffn_b"],
                       [_o2(T, FFN, BF16, tmf)], K=FFN_K, C=FFN, name=nm("ffnact"), tm=tmf, nt=ntf)
    ffn = _mm_nn(f, W["w_dn"], name=nm("dn"))
    h2, h2b = _rows_fwd(_two_copies(_res_ln_fn), [_r2(h1, tm), _r2(ffn, tm)], [W["ln2_g"], W["ln2_b"]],
                        _copies_out(T, tm), name=nm("ln2"), nt=nt)
    saved = dict(h=h, hb=hb, u3=u3, dtr=dtr, v3=v3, cconv=cconv, ya=ya, x0=x0, x1=x1, bc=bc, sprev=sprev, yn=yn, yb=yb, m=m,
                 mix=mix, h1=h1, h1b=h1b, up3=up3, f=f, ffn=ffn)
    return h2, h2b, saved


def _layer_bwd(dh2, W, sv, l, carry):
    T = dh2.shape[0]
    tm = min(TM, T)
    nt = T // tm
    tmf = min(TM_FFN, T)
    ntf = T // tmf
    nm = lambda s: f"l{l}_{s}"
    gr = {}
    (dres2, dffn), (gr["ln2_g"], gr["ln2_b"]) = _rows_bwd(
        _res_ln_fn, [_r2(sv["h1"], tm), _r2(sv["ffn"], tm)], [W["ln2_g"], W["ln2_b"]], [_r2(dh2, tm)],
        [_o2(T, D, F32, tm), _o2(T, D, BF16, tm)], name=nm("ln2_b"), nt=nt)
    df = _mm_nt(dffn, W["w_dn"], name=nm("dn_dx"))
    gr["w_dn"] = _mm_tn(sv["f"], dffn, name=nm("dn_dw"), tmo=FFN // 2)
    up3 = sv["up3"]
    (dgate, dval), gr["ffn_w"], gr["ffn_b"] = _carrying(carry, "ffnact_b", gr, lambda comm: _dwconv_bwd(
        _ffn_post, [_slab(up3, 0, tmf), _slab(up3, 1, tmf)],
        [_halo_slab(up3, 0, tmf, DW_HALO), _halo_slab(up3, 1, tmf, DW_HALO)], W["ffn_w"], W["ffn_b"], [_r2(df, tmf)],
        K=FFN_K, C=FFN, name=nm("ffnact_b"), tm=tmf, nt=ntf, comm=comm))
    dh1 = _mm_cols_nt([dgate, dval], W["w_up"], name=nm("up_dx"), add=dres2)
    gr["w_up"] = _mm_tn_cols(sv["h1b"], [dgate, dval], name=nm("up_dw"), tmo=min(512, D))
    (dres1, dmix), (gr["ln1_g"], gr["ln1_b"]) = _rows_bwd(
        _res_ln_fn, [_r2(sv["h"], tm), _r2(sv["mix"], tm)], [W["ln1_g"], W["ln1_b"]], [_r2(dh1, tm)],
        [_o2(T, D, F32, tm), _o2(T, D, BF16, tm)], name=nm("ln1_b"), nt=nt)
    dm = _mm_nt(dmix, W["w_o"], name=nm("wo_dx"))
    gr["w_o"] = _mm_tn(sv["m"], dmix, name=nm("wo_dw"), tmo=min(512, D))
    u3 = sv["u3"]
    (dga, dgb, dya, dyb), _ = _rows_bwd(
        _mix_fn, [_slab(u3, 4, tm), _slab(u3, 5, tm), _r2(sv["ya"], tm), _r2(sv["yb"], tm)], [], [_r2(dm, tm)],
        [_o2(T, D, BF16, tm)] * 4, name=nm("mix_b"), nt=nt)
    dv3 = _mm_nt(dya, W["w_co"], name=nm("ya_dx"))
    gr["w_co"] = _mm_tn(sv["v3"], dya, name=nm("ya_dw"), tmo=min(512, D))
    da, dg, gr["conv_w"], gr["conv_b"], gr["conv_ln_g"], gr["conv_ln_b"] = _carrying(carry, "conf_b", gr, lambda comm: _conf_bwd(
        u3, sv["cconv"], dv3, W["conv_w"], W["conv_b"], W["conv_ln_g"], W["conv_ln_b"], name=nm("conf_b"), comm=comm))
    dyn = _mm_nt(dyb, W["w_so"], name=nm("yb_dx"))
    gr["w_so"] = _mm_tn(sv["yn"], dyb, name=nm("yb_dw"), tmo=min(512, SD))
    (dx0, dx1, dbc, ddtr, dz0, dz1, gr["a_log"], gr["dt_bias"], gr["d_skip"], gr["norm_w"]) = _carrying(
        carry, "ssd_b", gr, lambda comm: _ssd_bwd(
            sv["x0"], sv["x1"], sv["bc"], sv["dtr"], u3, sv["sprev"], dyn, W["a_log"], W["dt_bias"], W["d_skip"],
            W["norm_w"], name=nm("ssd_b"), comm=comm))
    tmx = min(TM_X, T)
    (du6, du7, du8), gr["ssm_w"], gr["ssm_b"] = _dwconv_bwd(
        _xbc_post, [_slab(u3, 6, tmx), _slab(u3, 7, tmx), _slab(u3, 8, tmx)],
        [_halo_slab(u3, 6, tmx, DW_HALO), _halo_slab(u3, 7, tmx, DW_HALO), _halo_slab(u3, 8, tmx, DW_HALO)],
        W["ssm_w"], W["ssm_b"], [_r2(dx0, tmx), _r2(dx1, tmx), _r2(dbc, tmx)], K=SSM_K, C=D, name=nm("xbc_b"),
        tm=tmx, nt=T // tmx)
    du = [da, dg, dz0, dz1, dga, dgb, du6, du7, du8]
    gr["w_dt"] = _mm_tn(sv["hb"], ddtr, name=nm("dt_dw"), tmo=min(512, D))
    gr["w_p"] = _carrying(carry, "u_dw", gr, lambda comm: _mm_tn_cols(sv["hb"], du, name=nm("u_dw"),
                                                                      tmo=min(512, D), comm=comm))
    dh_a = _mm_nt(ddtr, W["w_dt"], name=nm("dt_dx"), add=dres1)
    dh = _carrying(carry, "u_dx", gr, lambda comm: _mm_cols_nt(du, W["w_p"], name=nm("u_dx"), add=dh_a, comm=comm))
    return dh, gr


_U_SPLIT = (2 * D + SD, 2 * D + SD + XBC, 2 * D + SD + XBC + H)


def _pad_rows(a, rows):
    return jnp.concatenate([a, jnp.zeros((rows - a.shape[0],) + a.shape[1:], a.dtype)], axis=0)


def _pad_lanes(a, lanes):
    return jnp.concatenate([a, jnp.zeros(a.shape[:-1] + (lanes - a.shape[-1],), a.dtype)], axis=-1)


def _w_in_layout(w_in):
    e0, e1, e2 = _U_SPLIT
    return dict(w_p=jnp.concatenate([w_in[:, :e0], w_in[:, e2:], w_in[:, e0:e1]], axis=1),
                w_dt=_pad_lanes(w_in[:, e1:e2], HP))


_MM_KEY = dict(w_conv_out="w_co", w_ssm_out="w_so", w_o="w_o", w_ffn_up="w_up", w_ffn_down="w_dn")


def _small_layer_weights(full, l):
    row = lambda a: a.reshape(1, -1)
    ssm_w = full["ssm_conv_w"][l]
    ffn_w = full["ffn_dw_w"][l]
    ssm_b = full["ssm_conv_b"][l]
    ffn_b = full["ffn_dw_b"][l]
    W = dict(
        conv_w=_pad_rows(full["conv_dw_w"][l], 32),
        conv_b=row(full["conv_dw_b"][l]), conv_ln_g=row(full["conv_ln_g"][l]), conv_ln_b=row(full["conv_ln_b"][l]),
        ssm_w=jnp.stack([_pad_rows(ssm_w[:, p * D:(p + 1) * D], 8) for p in range(3)]),
        ssm_b=[row(ssm_b[p * D:(p + 1) * D]) for p in range(3)],
        a_log=_pad_lanes(row(full["ssm_a_log"][l]), HP), dt_bias=_pad_lanes(row(full["ssm_dt_bias"][l]), HP),
        d_skip=_pad_lanes(row(full["ssm_d"][l]), HP), norm_w=row(full["ssm_norm_w"][l]),
        ln1_g=row(full["ln1_g"][l]), ln1_b=row(full["ln1_b"][l]),
        ffn_w=jnp.stack([_pad_rows(ffn_w[:, p * FFN:(p + 1) * FFN], 8) for p in range(2)]),
        ffn_b=[row(ffn_b[p * FFN:(p + 1) * FFN]) for p in range(2)],
        ln2_g=row(full["ln2_g"][l]), ln2_b=row(full["ln2_b"][l]),
    )
    return W


def _w_in_grad(gr):
    e0 = _U_SPLIT[0]
    wp = gr["w_p"]
    return jnp.concatenate([wp[:, :e0], wp[:, e0 + 2 * D:e0 + 2 * D + XBC], gr["w_dt"][:, :H], wp[:, e0:e0 + 2 * D]], axis=1)


def _layer_grads_to_reference_layout(gr):
    return dict(
        w_in=_w_in_grad(gr), conv_dw_w=gr["conv_w"][:CONV_K], conv_dw_b=gr["conv_b"][0], conv_ln_g=gr["conv_ln_g"][0],
        conv_ln_b=gr["conv_ln_b"][0], w_conv_out=gr["w_co"],
        ssm_conv_w=jnp.concatenate([gr["ssm_w"][p, :SSM_K] for p in range(3)], axis=1),
        ssm_conv_b=jnp.concatenate([b[0] for b in gr["ssm_b"]]),
        ssm_dt_bias=gr["dt_bias"][0, :H], ssm_a_log=gr["a_log"][0, :H], ssm_d=gr["d_skip"][0, :H],
        ssm_norm_w=gr["norm_w"][0], w_ssm_out=gr["w_so"], w_o=gr["w_o"], ln1_g=gr["ln1_g"][0], ln1_b=gr["ln1_b"][0],
        w_ffn_up=gr["w_up"], ffn_dw_w=jnp.concatenate([gr["ffn_w"][p, :FFN_K] for p in range(2)], axis=1),
        ffn_dw_b=jnp.concatenate([b[0] for b in gr["ffn_b"]]), w_ffn_down=gr["w_dn"], ln2_g=gr["ln2_g"][0],
        ln2_b=gr["ln2_b"][0],
    )


_BIG = dict(w_in=2, w_conv_out=1, w_ssm_out=1, w_o=1, w_ffn_up=2, w_ffn_down=1)
_SMALL_SHARDED = dict(conv_dw_w=2, ssm_conv_w=2, ffn_dw_w=2)
_REPLICATED = ("ln_in_g", "ln_in_b", "conv_dw_b", "conv_ln_g", "conv_ln_b", "ssm_conv_b", "ssm_dt_bias", "ssm_a_log",
               "ssm_d", "ssm_norm_w", "ln1_g", "ln1_b", "ffn_dw_b", "ln2_g", "ln2_b")
_WEIGHTS = ("ln_in_g", "ln_in_b", "w_in", "conv_dw_w", "conv_dw_b", "conv_ln_g", "conv_ln_b", "w_conv_out", "ssm_conv_w",
            "ssm_conv_b", "ssm_dt_bias", "ssm_a_log", "ssm_d", "ssm_norm_w", "w_ssm_out", "w_o", "ln1_g", "ln1_b",
            "w_ffn_up", "ffn_dw_w", "ffn_dw_b", "w_ffn_down", "ln2_g", "ln2_b")


def _split_chips(a, axis):
    rows, cols = a.shape
    if axis == 0:
        return a.reshape(NCHIP, rows // NCHIP, cols)
    return a.reshape(rows, NCHIP, cols // NCHIP).transpose(1, 0, 2)


def kernel(x, ln_in_g, ln_in_b, w_in, conv_dw_w, conv_dw_b, conv_ln_g, conv_ln_b, w_conv_out, ssm_conv_w, ssm_conv_b, ssm_dt_bias, ssm_a_log, ssm_d, ssm_norm_w, w_ssm_out, w_o, ln1_g, ln1_b, w_ffn_up, ffn_dw_w, ffn_dw_b, w_ffn_down, ln2_g, ln2_b, loss_target, m_ln_in_g, m_ln_in_b, m_w_in, m_conv_dw_w, m_conv_dw_b, m_conv_ln_g, m_conv_ln_b, m_w_conv_out, m_ssm_conv_w, m_ssm_conv_b, m_ssm_dt_bias, m_ssm_a_log, m_ssm_d, m_ssm_norm_w, m_w_ssm_out, m_w_o, m_ln1_g, m_ln1_b, m_w_ffn_up, m_ffn_dw_w, m_ffn_dw_b, m_w_ffn_down, m_ln2_g, m_ln2_b, v_ln_in_g, v_ln_in_b, v_w_in, v_conv_dw_w, v_conv_dw_b, v_conv_ln_g, v_conv_ln_b, v_w_conv_out, v_ssm_conv_w, v_ssm_conv_b, v_ssm_dt_bias, v_ssm_a_log, v_ssm_d, v_ssm_norm_w, v_w_ssm_out, v_w_o, v_ln1_g, v_ln1_b, v_w_ffn_up, v_ffn_dw_w, v_ffn_dw_b, v_w_ffn_down, v_ln2_g, v_ln2_b):
    args = locals()
    w = {n: args[n] for n in _WEIGHTS}
    mom = {n: args["m_" + n] for n in _WEIGHTS}
    vel = {n: args["v_" + n] for n in _WEIGHTS}
    T = x.shape[1]
    tm = min(TM, T)
    nt = T // tm
    chip = 2 * lax.axis_index("x") + lax.axis_index("y")

    assert DEPTH == 2
    big_names, small_names = list(_BIG), list(_SMALL_SHARDED)
    rest_big = [n for n in big_names if n != "w_in"]
    bf = {n: w[n].astype(BF16) for n in big_names}
    join = lambda got, axis: jnp.concatenate([got[j] for j in range(NCHIP)], axis=axis)
    first = _chip_exchange([bf["w_in"][0]] + [w[n] for n in small_names], scatter=False, name="gather_first")
    full = {n: join(gk, _SMALL_SHARDED[n]) for n, gk in zip(small_names, first[1:])}
    for n in _REPLICATED:
        full[n] = w[n]
    Ws = [_small_layer_weights(full, l) for l in range(DEPTH)]
    Ws[0].update(_w_in_layout(join(first[0], 1)))

    def rest_arrived(l):
        def done(got):
            for n, gk in zip(rest_big, got):
                Ws[l][_MM_KEY[n]] = join(gk, _BIG[n] - 1)
        return done

    carry_fwd = [
        {"u": lambda gr: (_exchange_comm([bf[n][0] for n in rest_big], scatter=False), rest_arrived(0)),
         "ssd": lambda gr: (_exchange_comm([bf["w_in"][1]], scatter=False),
                            lambda got: Ws[1].update(_w_in_layout(join(got[0], 1))))},
        {"u": lambda gr: (_exchange_comm([bf[n][1] for n in rest_big], scatter=False), rest_arrived(1))},
    ]

    x2 = x.reshape(T, D)
    g_in, b_in = ln_in_g.reshape(1, D), ln_in_b.reshape(1, D)
    h, hb = _rows_fwd(_two_copies(_ln_fn), [_r2(x2, tm)], [g_in, b_in], _copies_out(T, tm), name="ln_in", nt=nt)
    saved = []
    for l in range(DEPTH):
        h, hb, sv = _layer_fwd(h, hb, Ws[l], l, carry_fwd[l])
        saved.append(sv)
    loss_row, dh = _loss_head(h, loss_target.reshape(T, D), name="loss")

    arrived = {}

    def exchange(names, l, grads):
        def make(gr):
            src = grads(gr)
            def done(got):
                for n, gk in zip(names, got):
                    arrived[(n, l)] = gk
            return _exchange_comm([_split_chips(src[n], _BIG[n] - 1) for n in names], scatter=True), done
        return make

    layer_grads = [None] * DEPTH
    dh, gr = _layer_bwd(dh, Ws[1], saved[1], 1, {})
    layer_grads[1] = _layer_grads_to_reference_layout(gr)
    g1 = lambda gr: layer_grads[1]
    g0 = lambda gr: {n: gr[_MM_KEY[n]] for n in rest_big}
    dh, gr = _layer_bwd(dh, Ws[0], saved[0], 0, {
        "ffnact_b": exchange(["w_in"], 1, g1),
        "conf_b": exchange(["w_ffn_up", "w_ffn_down"], 1, g1),
        "ssd_b": exchange(["w_conv_out", "w_ssm_out", "w_o"], 1, g1),
        "u_dw": exchange(rest_big, 0, g0),
        "u_dx": exchange(["w_in"], 0, lambda gr: {"w_in": _w_in_grad(gr)})})
    layer_grads[0] = _layer_grads_to_reference_layout(gr)
    (grad_x2,), (d_g_in, d_b_in) = _rows_bwd(_ln_fn, [_r2(x2, tm)], [g_in, b_in], [_r2(dh, tm)], [_o2(T, D, F32, tm)],
                                             name="ln_in_b", nt=nt)
    local = {n: jnp.stack([layer_grads[l][n] for l in range(DEPTH)]) for n in _WEIGHTS[2:] if n not in _BIG}
    local["ln_in_g"], local["ln_in_b"] = d_g_in[0], d_b_in[0]
    res = [{}, {}, {}, {}]

    keys = [(n, l) for n in big_names for l in range(DEPTH)]
    mine = [_sum_slots(arrived[k], name=f"sum_chips_{k[0]}_{k[1]}") for k in keys]
    other = _core_swap(mine, name="swap_cores")
    for i, n in enumerate(big_names):
        outs = _adamw_layers(mine[2 * i:2 * i + 2], other[2 * i:2 * i + 2], w[n], mom[n], vel[n], name="adamw_" + n)
        for q in range(4):
            res[q][n] = outs[q]

    rest_names = list(_REPLICATED) + small_names
    part = _pack([loss_row] + [local[n] for n in rest_names], F32, 8)
    parts = _all_gather8(part, name="gather_small")
    total = _sum_slots(parts, name="sum_devices")
    tot = _unpack(total, [loss_row.shape] + [local[n].shape for n in rest_names])
    loss = tot[0][0, 0]
    g_rest = {}
    for n, t in zip(rest_names, tot[1:]):
        if n in _SMALL_SHARDED:
            ax = _SMALL_SHARDED[n]
            t = lax.dynamic_slice_in_dim(t, chip * w[n].shape[ax], w[n].shape[ax], axis=ax)
        g_rest[n] = t
    pk = lambda d: _pack([d[n] for n in rest_names], F32, 8)
    rest_out = _adamw([pk(g_rest)], pk(w), pk(mom), pk(vel), name="adamw_rest")
    rest_out = [_unpack(o, [w[n].shape for n in rest_names]) for o in rest_out]

    for q in range(4):
        for k, n in enumerate(rest_names):
            res[q][n] = rest_out[q][k]
    grad_x = grad_x2.reshape(x.shape)
    return (loss, grad_x, *[res[0][n] for n in _WEIGHTS], *[res[1][n] for n in _WEIGHTS],
            *[res[2][n] for n in _WEIGHTS], *[res[3][n] for n in _WEIGHTS])
```

```python
import functools
import math

import jax
import jax.numpy as jnp
from jax import lax
from jax.experimental import pallas as pl
from jax.experimental.pallas import tpu as pltpu

F32 = jnp.float32
BF16 = jnp.bfloat16

D = 1024
DEPTH = 2
CONV_K = 31
SD = 2 * D
P = 64
H = SD // P
G = 4
R = H // G
N = 128
RP = R * P
SSM_K = 4
L = 128
XBC = SD + 2 * G * N
FFN = 2816
FFN_K = 3
IN_DIM = 2 * D + SD + XBC + H + 2 * D
ALPHA = (2 * DEPTH) ** 0.25
LN_EPS = 1e-5
RMS_EPS = 1e-5
ADAM_LR, ADAM_B1, ADAM_B2, ADAM_EPS, ADAM_WD, ADAM_STEP = 0.001, 0.9, 0.999, 1e-08, 0.01, 10

HP = 128
NCHIP = 4
PACK_W = 1024
VMEM_LIMIT = 56 * 1024 * 1024
TM = 512
TM_X = 256
TM_FFN = 256
TK = 1024

assert D == 2 * RP and 2 * G * N == D and XBC == 3 * D and H <= HP


def _pcall(body, *, name, grid=(), in_specs, out_specs, out_shape, scratch_shapes=()):
    params = pltpu.CompilerParams(vmem_limit_bytes=VMEM_LIMIT, dimension_semantics=("arbitrary",) * len(grid))
    return pl.pallas_call(body, name=name, grid=grid, in_specs=in_specs, out_specs=out_specs, out_shape=out_shape,
                          scratch_shapes=list(scratch_shapes), compiler_params=params)


def _pcall_carrying(body, comm, *, name, grid, in_specs, out_specs, out_shape, scratch_shapes=()):
    in_specs, out_specs, out_shape = list(in_specs), list(out_specs), list(out_shape)
    scratch_shapes = list(scratch_shapes)
    n_in, n_out, n_scr = len(in_specs), len(out_specs), len(scratch_shapes)
    nci, nco = len(comm["ins"]), len(comm["outs"])

    def wrapped(*refs):
        ins, cin = refs[:n_in], refs[n_in:n_in + nci]
        outs = refs[n_in + nci:n_in + nci + n_out]
        cout = refs[n_in + nci + n_out:n_in + nci + n_out + nco]
        scr = refs[n_in + nci + n_out + nco:n_in + nci + n_out + nco + n_scr]
        csem = refs[n_in + nci + n_out + nco + n_scr:]
        ids = [pl.program_id(ax) for ax in range(len(grid))]
        first = functools.reduce(jnp.logical_and, [i == 0 for i in ids])
        last = functools.reduce(jnp.logical_and, [i == g - 1 for i, g in zip(ids, grid)])

        @pl.when(first)
        def _():
            comm["start"](cin, cout, csem)

        body(*ins, *outs, *scr)

        @pl.when(last)
        def _():
            comm["wait"](cin, cout, csem)

    call = _pcall(wrapped, name=name, grid=grid, in_specs=in_specs + [_HBM] * nci, out_specs=out_specs + [_HBM] * nco,
                  out_shape=out_shape + list(comm["outs"]), scratch_shapes=scratch_shapes + list(comm["sems"]))

    def run(*operands):
        res = call(*operands, *comm["ins"])
        return list(res[:n_out]), list(res[n_out:])

    return run


def _pcall_maybe_carrying(body, comm, **kw):
    if comm is not None:
        return _pcall_carrying(body, comm, **kw)
    call = _pcall(body, **kw)
    return lambda *operands: (list(call(*operands)), None)


def _ccall(body, *, name, in_specs, out_specs, out_shape, scratch_shapes):
    return pl.pallas_call(body, name=name, in_specs=in_specs, out_specs=out_specs, out_shape=out_shape,
                          scratch_shapes=list(scratch_shapes))


def _full_spec(a):
    nd = a.ndim
    return pl.BlockSpec(a.shape, lambda *_: (0,) * nd)


def _sds(shape, dtype):
    return jax.ShapeDtypeStruct(tuple(shape), dtype)


def _mm(a, b, *, name, grid, a_spec, b_spec, o_spec, out_shape, acc_shape, trans_a=False, trans_b=False, add=None,
        add_spec=None, comm=None):
    nk = grid[2]
    dn = (((0 if trans_a else 1,), (1 if trans_b else 0,)), ((), ()))
    has_add = add is not None

    def body(*refs):
        a_ref, b_ref = refs[0], refs[1]
        add_ref = refs[2] if has_add else None
        o_ref = refs[3] if has_add else refs[2]
        part = lax.dot_general(a_ref[...].astype(BF16), b_ref[...].astype(BF16), dn, preferred_element_type=F32)

        def finish(res):
            if has_add:
                res = res + add_ref[...]
            o_ref[...] = res.astype(o_ref.dtype)

        if nk == 1:
            finish(part)
        else:
            acc = refs[-1]
            k = pl.program_id(2)

            @pl.when(k == 0)
            def _():
                acc[...] = part

            @pl.when(k > 0)
            def _():
                acc[...] += part

            @pl.when(k == nk - 1)
            def _():
                finish(acc[...])

    ins = [a, b] + ([add] if has_add else [])
    specs = [a_spec, b_spec] + ([add_spec] if has_add else [])
    scratch = [pltpu.VMEM(acc_shape, F32)] if nk > 1 else []
    if comm is not None:
        (out,), got = _pcall_carrying(body, comm, name=name, grid=grid, in_specs=specs, out_specs=[o_spec],
                                      out_shape=[out_shape], scratch_shapes=scratch)(*ins)
        return out, got
    return _pcall(body, name=name, grid=grid, in_specs=specs, out_specs=o_spec, out_shape=out_shape,
                  scratch_shapes=scratch)(*ins)


def _mm_nn(a, b, *, name, out_dtype=F32, tn=None, tk=None, add=None):
    M, K = a.shape
    Nn = b.shape[1]
    tm = min(TM, M)
    tn = Nn if tn is None else tn
    tk = K if tk is None else tk
    grid = (M // tm, Nn // tn, K // tk)
    return _mm(a, b, name=name, grid=grid,
               a_spec=pl.BlockSpec((tm, tk), lambda i, j, k: (i, k)),
               b_spec=pl.BlockSpec((tk, tn), lambda i, j, k: (k, j)),
               o_spec=pl.BlockSpec((tm, tn), lambda i, j, k: (i, j)),
               out_shape=_sds((M, Nn), out_dtype), acc_shape=(tm, tn), add=add,
               add_spec=pl.BlockSpec((tm, tn), lambda i, j, k: (i, j)))


def _mm_nt(a, b, *, name, add=None):
    M, K = a.shape
    Nn = b.shape[0]
    tm = min(TM, M)
    return _mm(a, b, name=name, grid=(M // tm, 1, 1), trans_b=True,
               a_spec=pl.BlockSpec((tm, K), lambda i, j, k: (i, 0)),
               b_spec=pl.BlockSpec((Nn, K), lambda i, j, k: (0, 0)),
               o_spec=pl.BlockSpec((tm, Nn), lambda i, j, k: (i, 0)),
               out_shape=_sds((M, Nn), F32), acc_shape=(tm, Nn), add=add,
               add_spec=pl.BlockSpec((tm, Nn), lambda i, j, k: (i, 0)))


def _mm_resident_slab_out(a, w, *, name, width, tm, out_dtype=F32, comm=None):
    M, K = a.shape
    S = w.shape[1] // width
    tm = min(tm, M)

    def body(a_ref, w_hbm, o_ref, w_vmem, sem):
        @pl.when(pl.program_id(0) == 0)
        def _():
            cp = pltpu.make_async_copy(w_hbm, w_vmem, sem)
            cp.start()
            cp.wait()

        av = a_ref[...].astype(BF16)
        for s in range(S):
            o_ref[s] = jnp.dot(av, w_vmem[:, s * width:(s + 1) * width], preferred_element_type=F32).astype(o_ref.dtype)

    kw = dict(name=name, grid=(M // tm,),
              in_specs=[pl.BlockSpec((tm, K), lambda i: (i, 0)), pl.BlockSpec(memory_space=pl.ANY)],
              scratch_shapes=[pltpu.VMEM(w.shape, w.dtype), pltpu.SemaphoreType.DMA])
    o_spec = pl.BlockSpec((S, tm, width), lambda i: (0, i, 0))
    if comm is not None:
        (out,), got = _pcall_carrying(body, comm, out_specs=[o_spec], out_shape=[_sds((S, M, width), out_dtype)], **kw)(a, w)
        return out, got
    return _pcall(body, out_specs=o_spec, out_shape=_sds((S, M, width), out_dtype), **kw)(a, w)


def _mm_cols_nt(a_list, w, *, name, add, comm=None):
    S = len(a_list)
    M, width = a_list[0].shape
    Nn = w.shape[0]
    tm = min(TM, M)

    def body(*refs):
        a_refs, w_hbm, add_ref, o_ref, w_vmem, sem = refs[:S], refs[S], refs[S + 1], refs[S + 2], refs[S + 3], refs[S + 4]

        @pl.when(pl.program_id(0) == 0)
        def _():
            cp = pltpu.make_async_copy(w_hbm, w_vmem, sem)
            cp.start()
            cp.wait()

        acc = add_ref[...]
        for s in range(S):
            acc = acc + lax.dot_general(a_refs[s][...].astype(BF16), w_vmem[:, s * width:(s + 1) * width],
                                        (((1,), (1,)), ((), ())), preferred_element_type=F32)
        o_ref[...] = acc

    row = lambda C: pl.BlockSpec((tm, C), lambda i: (i, 0))
    kw = dict(name=name, grid=(M // tm,), in_specs=[row(width)] * S + [pl.BlockSpec(memory_space=pl.ANY), row(Nn)],
              scratch_shapes=[pltpu.VMEM(w.shape, w.dtype), pltpu.SemaphoreType.DMA])
    if comm is not None:
        (out,), got = _pcall_carrying(body, comm, out_specs=[row(Nn)], out_shape=[_sds((M, Nn), F32)], **kw)(*a_list, w, add)
        return out, got
    return _pcall(body, out_specs=row(Nn), out_shape=_sds((M, Nn), F32), **kw)(*a_list, w, add)


def _mm_tn_cols(a, b_list, *, name, tmo, comm=None):
    T, M = a.shape
    S = len(b_list)
    width = b_list[0].shape[1]
    tk = min(TK, T)
    nk = T // tk

    def body(*refs):
        a_ref, b_refs, o_ref, acc = refs[0], refs[1:S + 1], refs[S + 1], refs[S + 2]
        j, k = pl.program_id(1), pl.program_id(2)
        for s in range(S):
            @pl.when(j == s)
            def _():
                part = lax.dot_general(a_ref[...], b_refs[s][...], (((0,), (0,)), ((), ())), preferred_element_type=F32)

                @pl.when(k == 0)
                def _():
                    acc[...] = part

                @pl.when(k > 0)
                def _():
                    acc[...] += part

        @pl.when(k == nk - 1)
        def _():
            o_ref[...] = acc[...].astype(o_ref.dtype)

    def b_spec(s):
        return pl.BlockSpec((tk, width), lambda i, j, k: (jnp.where(j == s, k, jnp.where(j < s, 0, nk - 1)), 0))

    kw = dict(name=name, grid=(M // tmo, S, nk),
              in_specs=[pl.BlockSpec((tk, tmo), lambda i, j, k: (k, i))] + [b_spec(s) for s in range(S)],
              scratch_shapes=[pltpu.VMEM((tmo, width), F32)])
    o_spec = pl.BlockSpec((tmo, width), lambda i, j, k: (i, j))
    if comm is not None:
        (out,), got = _pcall_carrying(body, comm, out_specs=[o_spec], out_shape=[_sds((M, S * width), BF16)], **kw)(a, *b_list)
        return out, got
    return _pcall(body, out_specs=o_spec, out_shape=_sds((M, S * width), BF16), **kw)(a, *b_list)


def _mm_tn(a, b, *, name, tmo, tn=None, comm=None):
    T, M = a.shape
    Nn = b.shape[1]
    tn = Nn if tn is None else tn
    tk = min(TK, T)
    return _mm(a, b, name=name, grid=(M // tmo, Nn // tn, T // tk), trans_a=True,
               a_spec=pl.BlockSpec((tk, tmo), lambda i, j, k: (k, i)),
               b_spec=pl.BlockSpec((tk, tn), lambda i, j, k: (k, j)),
               o_spec=pl.BlockSpec((tmo, tn), lambda i, j, k: (i, j)),
               out_shape=_sds((M, Nn), BF16), acc_shape=(tmo, tn), comm=comm)


def _r2(a, tm):
    return (a, (tm, a.shape[1]), lambda i: (i, 0))


def _slab(a3, s, tm):
    return (a3, (None, tm, a3.shape[2]), lambda i: (s, i, 0))


def _o2(T, C, dtype, tm):
    return ((T, C), dtype, (tm, C), lambda i: (i, 0))


def _rows_fwd(fn, row_ins, par_ins, outs, *, name, nt):
    nr, npar = len(row_ins), len(par_ins)

    def body(*refs):
        vals = [r[...] for r in refs[:nr + npar]]
        res = fn(*vals)
        for o_ref, v in zip(refs[nr + npar:], res):
            o_ref[...] = v.astype(o_ref.dtype)

    return _pcall(body, name=name, grid=(nt,),
                  in_specs=[pl.BlockSpec(bs, im) for (_, bs, im) in row_ins] + [_full_spec(p) for p in par_ins],
                  out_specs=[pl.BlockSpec(bs, im) for (_, _, bs, im) in outs],
                  out_shape=[_sds(s, d) for (s, d, _, _) in outs])(*[r[0] for r in row_ins], *par_ins)


def _rows_bwd(fn, row_ins, par_ins, cot_ins, drow_outs, *, name, nt):
    nr, npar, nc = len(row_ins), len(par_ins), len(cot_ins)
    keep = [k for k, o in enumerate(drow_outs) if o is not None]

    def body(*refs):
        vals = [r[...].astype(F32) for r in refs[:nr + npar]]
        cots = [r[...].astype(F32) for r in refs[nr + npar:nr + npar + nc]]
        orefs = refs[nr + npar + nc:]
        _, vjp = jax.vjp(fn, *vals)
        grads = vjp(tuple(cots))
        for o_ref, k in zip(orefs[:len(keep)], keep):
            o_ref[...] = grads[k].astype(o_ref.dtype)
        prefs = orefs[len(keep):]

        @pl.when(pl.program_id(0) == 0)
        def _():
            for p_ref in prefs:
                p_ref[...] = jnp.zeros_like(p_ref)

        for p_ref, g in zip(prefs, grads[nr:]):
            p_ref[...] += g

    outs = [drow_outs[k] for k in keep]
    res = _pcall(body, name=name, grid=(nt,),
                 in_specs=[pl.BlockSpec(bs, im) for (_, bs, im) in row_ins] + [_full_spec(p) for p in par_ins]
                 + [pl.BlockSpec(bs, im) for (_, bs, im) in cot_ins],
                 out_specs=[pl.BlockSpec(bs, im) for (_, _, bs, im) in outs] + [_full_spec(p) for p in par_ins],
                 out_shape=[_sds(s, d) for (s, d, _, _) in outs] + [_sds(p.shape, F32) for p in par_ins],
                 )(*[r[0] for r in row_ins], *par_ins, *[c[0] for c in cot_ins])
    return list(res[:len(keep)]), list(res[len(keep):])


def _layer_norm(v, g, b):
    mu = jnp.mean(v, axis=-1, keepdims=True)
    var = jnp.mean(jnp.square(v - mu), axis=-1, keepdims=True)
    return (v - mu) * lax.rsqrt(var + LN_EPS) * g + b


def _silu(v):
    return v * jax.nn.sigmoid(v)


def _softplus(v):
    return jnp.maximum(v, 0.0) + jnp.log1p(jnp.exp(-jnp.abs(v)))


def _halo_of(K):
    return 8 * ((K - 1 + 7) // 8)


DW_HALO = 8
DW_RB = 16
DW_LC = 256


def _dw_taps(win, w_ref, p, ls, K, shift_of):
    acc = None
    for k in range(K):
        o = shift_of(k)
        term = win[o:o + DW_RB, :] * w_ref[p, k:k + 1, ls]
        acc = term if acc is None else acc + term
    return acc


def _dwconv_fwd(post, part_ins, w, biases, outs, *, K, C, name, tm, nt):
    nparts, halo = len(part_ins), DW_HALO

    def body(*refs):
        x_refs, w_ref = refs[:nparts], refs[nparts]
        b_refs = refs[nparts + 1:2 * nparts + 1]
        orefs, buf = refs[2 * nparts + 1:-1], refs[-1]
        i = pl.program_id(0)
        for p in range(nparts):
            @pl.when(i == 0)
            def _():
                buf[p, pl.ds(0, halo), :] = jnp.zeros((halo, C), F32)

            @pl.when(i > 0)
            def _():
                buf[p, pl.ds(0, halo), :] = buf[p, pl.ds(tm, halo), :]

            buf[p, pl.ds(halo, tm), :] = x_refs[p][...].astype(F32)

        def group(r, carry):
            base = pl.multiple_of(r * DW_RB, DW_RB)
            for cj in range(C // DW_LC):
                ls = slice(cj * DW_LC, (cj + 1) * DW_LC)
                cs = [_dw_taps(buf[p, pl.ds(base, DW_RB + halo), ls], w_ref, p, ls, K, lambda k: halo - (K - 1) + k)
                      + b_refs[p][:, ls] for p in range(nparts)]
                for o_ref, v in zip(orefs, post(cs)):
                    o_ref[pl.ds(base, DW_RB), ls] = v.astype(o_ref.dtype)
            return carry

        lax.fori_loop(0, tm // DW_RB, group, 0)

    return _pcall(body, name=name, grid=(nt,),
                  in_specs=[pl.BlockSpec(bs, im) for (_, bs, im) in part_ins] + [_full_spec(w)] + [_full_spec(b) for b in biases],
                  out_specs=[pl.BlockSpec(bs, im) for (_, _, bs, im) in outs],
                  out_shape=[_sds(s, d) for (s, d, _, _) in outs],
                  scratch_shapes=[pltpu.VMEM((nparts, halo + tm, C), F32)])(*[r[0] for r in part_ins], w, *biases)


def _dwconv_bwd(post, part_ins, halo_ins, w, biases, cot_ins, *, K, C, name, tm, nt, comm=None):
    nparts, halo, nc, RB = len(part_ins), DW_HALO, len(cot_ins), DW_RB
    T = nt * tm

    def body(*refs):
        x_refs, h_refs, w_ref = refs[:nparts], refs[nparts:2 * nparts], refs[2 * nparts]
        b_refs = refs[2 * nparts + 1:3 * nparts + 1]
        cot_refs = refs[3 * nparts + 1:3 * nparts + 1 + nc]
        rest = refs[3 * nparts + 1 + nc:]
        dx_refs, dw_ref, db_refs = rest[:nparts], rest[nparts], rest[nparts + 1:2 * nparts + 1]
        bufx, bufd, acc = rest[-3], rest[-2], rest[-1]
        s = pl.program_id(0)
        first_tile = s == nt - 1

        @pl.when(s == 0)
        def _():
            acc[...] = jnp.zeros_like(acc)
            for p in range(nparts):
                bufd[p, pl.ds(tm, halo), :] = jnp.zeros((halo, C), F32)

        for p in range(nparts):
            hrows = h_refs[p].shape[0]
            bufx[p, pl.ds(0, halo), :] = jnp.where(first_tile, 0.0, h_refs[p][...].astype(F32)[hrows - halo:hrows, :])
            bufx[p, pl.ds(halo, tm), :] = x_refs[p][...].astype(F32)
        fold = lambda v: v[0:8, :] + v[8:16, :]

        def conv_out_grads(r, carry):
            base = pl.multiple_of(r * RB, RB)
            for cj in range(C // DW_LC):
                ls = slice(cj * DW_LC, (cj + 1) * DW_LC)
                wins = [bufx[p, pl.ds(base, RB + halo), ls] for p in range(nparts)]
                cs = [_dw_taps(wins[p], w_ref, p, ls, K, lambda k: halo - (K - 1) + k) + b_refs[p][:, ls]
                      for p in range(nparts)]
                _, vjp = jax.vjp(lambda *c: post(list(c)), *cs)
                dcs = vjp(tuple(cr[pl.ds(base, RB), ls].astype(F32) for cr in cot_refs))
                for p in range(nparts):
                    bufd[p, pl.ds(base, RB), ls] = dcs[p]
                    for k in range(K):
                        o = halo - (K - 1) + k
                        acc[p, k, :, ls] += fold(dcs[p] * wins[p][o:o + RB, :])
                    acc[p, K, :, ls] += fold(dcs[p])
            return carry

        lax.fori_loop(0, tm // RB, conv_out_grads, 0)

        def input_grads(r, carry):
            base = pl.multiple_of(r * RB, RB)
            for cj in range(C // DW_LC):
                ls = slice(cj * DW_LC, (cj + 1) * DW_LC)
                for p in range(nparts):
                    dx = _dw_taps(bufd[p, pl.ds(base, RB + halo), ls], w_ref, p, ls, K, lambda k: K - 1 - k)
                    dx_refs[p][pl.ds(base, RB), ls] = dx.astype(dx_refs[p].dtype)
            return carry

        lax.fori_loop(0, tm // RB, input_grads, 0)
        for p in range(nparts):
            bufd[p, pl.ds(tm, halo), :] = bufd[p, pl.ds(0, halo), :]

        @pl.when(s == nt - 1)
        def _():
            dw_ref[...] = jnp.zeros_like(dw_ref)
            for p in range(nparts):
                for k in range(K):
                    dw_ref[p, k:k + 1, :] = jnp.sum(acc[p, k], axis=0, keepdims=True)
                db_refs[p][...] = jnp.sum(acc[p, K], axis=0, keepdims=True)

    rev = lambda im: (lambda s: im(nt - 1 - s))
    row = pl.BlockSpec((tm, C), lambda s: (nt - 1 - s, 0))
    res, got = _pcall_maybe_carrying(
        body, comm, name=name, grid=(nt,),
        in_specs=[pl.BlockSpec(bs, rev(im)) for (_, bs, im) in part_ins]
        + [pl.BlockSpec(bs, rev(im)) for (_, bs, im) in halo_ins]
        + [_full_spec(w)] + [_full_spec(b) for b in biases]
        + [pl.BlockSpec(bs, rev(im)) for (_, bs, im) in cot_ins],
        out_specs=[row] * nparts + [_full_spec(w)] + [_full_spec(b) for b in biases],
        out_shape=[_sds((T, C), BF16)] * nparts + [_sds(w.shape, F32)] + [_sds(b.shape, F32) for b in biases],
        scratch_shapes=[pltpu.VMEM((nparts, halo + tm, C), F32), pltpu.VMEM((nparts, tm + halo, C), F32),
                        pltpu.VMEM((nparts, K + 1, 8, C), F32)],
    )(*[r[0] for r in part_ins], *[r[0] for r in halo_ins], w, *biases, *[c[0] for c in cot_ins])
    out = (list(res[:nparts]), res[nparts], list(res[nparts + 1:]))
    return out if comm is None else (out, got)


def _halo_slab(a3, s, tm, halo):
    rows = max(halo, 16) if a3.dtype == BF16 else halo
    q = tm // rows
    return (a3, (None, rows, a3.shape[2]), lambda i: (s, jnp.maximum(i * q - 1, 0), 0))


CONF_HALO = _halo_of(CONV_K)
CONF_RB = 32


def _shifted_copies(buf, shifted, rows):
    for j in range(1, 8):
        shifted[j - 1, pl.ds(0, rows), :] = buf[pl.ds(j, rows), :]


def _shifted_rows(buf, shifted, s, base, nrows):
    j, q = s % 8, s // 8
    if j == 0:
        return buf[pl.ds(base + 8 * q, nrows), :]
    return shifted[j - 1, pl.ds(base + 8 * q, nrows), :]


def _conf_fwd(u3, w, cb, lg, lb, *, name):
    T = u3.shape[1]
    tm = min(TM_X, T)
    nt = T // tm
    K, halo, RB = CONV_K, CONF_HALO, min(CONF_RB, tm)

    def body(a_ref, g_ref, w_ref, cb_ref, lg_ref, lb_ref, v3_ref, c_ref, bufx, xs):
        i = pl.program_id(0)

        @pl.when(i == 0)
        def _():
            bufx[pl.ds(0, halo), :] = jnp.zeros((halo, D), F32)

        @pl.when(i > 0)
        def _():
            bufx[pl.ds(0, halo), :] = bufx[pl.ds(tm, halo), :]

        bufx[pl.ds(halo, tm), :] = _conf_pre(a_ref[...].astype(F32), g_ref[...].astype(F32))[0]
        _shifted_copies(bufx, xs, halo + tm - 8)

        def group(r, carry):
            base = pl.multiple_of(r * RB, RB)
            acc = None
            for k in range(K):
                term = _shifted_rows(bufx, xs, halo - (K - 1) + k, base, RB) * w_ref[k:k + 1, :]
                acc = term if acc is None else acc + term
            c_ref[pl.ds(base, RB), :] = acc
            return carry

        lax.fori_loop(0, tm // RB, group, 0)
        v3_ref[...] = _conf_post([c_ref[...]], cb_ref[...], lg_ref[...], lb_ref[...])[0].astype(v3_ref.dtype)

    slab = lambda s: pl.BlockSpec((None, tm, D), lambda i: (s, i, 0))
    row = pl.BlockSpec((tm, D), lambda i: (i, 0))
    pars = [w, cb, lg, lb]
    return _pcall(body, name=name, grid=(nt,), in_specs=[slab(0), slab(1)] + [_full_spec(p) for p in pars],
                  out_specs=[row, row], out_shape=[_sds((T, D), BF16), _sds((T, D), F32)],
                  scratch_shapes=[pltpu.VMEM((halo + tm, D), F32), pltpu.VMEM((7, halo + tm - 8, D), F32)],
                  )(u3, u3, *pars)


def _conf_bwd(u3, c, dv3, w, cb, lg, lb, *, name, comm=None):
    T = u3.shape[1]
    tm = min(TM_X, T)
    nt = T // tm
    K, halo, RB = CONV_K, CONF_HALO, min(CONF_RB, tm)

    def body(a_ref, g_ref, c_ref, dv3_ref, w_ref, cb_ref, lg_ref, lb_ref,
             da_ref, dg_ref, dw_ref, dcb_ref, dlg_ref, dlb_ref, xbuf, bufd, ds, dv0):
        s = pl.program_id(0)

        @pl.when(s == 0)
        def _():
            for r in (dw_ref, dcb_ref, dlg_ref, dlb_ref):
                r[...] = jnp.zeros_like(r)
            bufd[pl.ds(tm, halo), :] = jnp.zeros((halo, D), F32)

        xin, pre_vjp = jax.vjp(lambda p, q_: _conf_pre(p, q_)[0], a_ref[...].astype(F32), g_ref[...].astype(F32))
        xbuf[...] = xin

        _, post_vjp = jax.vjp(lambda cc, b_, g_, l_: _conf_post([cc], b_, g_, l_)[0],
                              c_ref[...], cb_ref[...], lg_ref[...], lb_ref[...])
        dc, dcb, dlg, dlb = post_vjp(dv3_ref[...])
        dcb_ref[...] += dcb
        dlg_ref[...] += dlg
        dlb_ref[...] += dlb
        bufd[pl.ds(0, tm), :] = dc
        _shifted_copies(bufd, ds, tm + halo - 8)

        def dx_group(r, carry):
            base = pl.multiple_of(r * RB, RB)
            acc = None
            for k in range(K):
                term = _shifted_rows(bufd, ds, K - 1 - k, base, RB) * w_ref[k:k + 1, :]
                acc = term if acc is None else acc + term
            dv0[pl.ds(base, RB), :] = acc
            return carry

        lax.fori_loop(0, tm // RB, dx_group, 0)

        for k in range(K):
            def dw_group(r, acc):
                base = pl.multiple_of(r * RB, RB)
                prod = xbuf[pl.ds(base, RB), :] * _shifted_rows(bufd, ds, K - 1 - k, base, RB)
                for v in range(RB // 8):
                    acc = acc + prod[v * 8:(v + 1) * 8, :]
                return acc

            acc = lax.fori_loop(0, tm // RB, dw_group, jnp.zeros((8, D), F32))
            dw_ref[k:k + 1, :] += jnp.sum(acc, axis=0, keepdims=True)

        bufd[pl.ds(tm, halo), :] = bufd[pl.ds(0, halo), :]
        da, dg = pre_vjp(dv0[...])
        da_ref[...] = da.astype(da_ref.dtype)
        dg_ref[...] = dg.astype(dg_ref.dtype)

    slab = lambda sl: pl.BlockSpec((None, tm, D), lambda s: (sl, nt - 1 - s, 0))
    row = pl.BlockSpec((tm, D), lambda s: (nt - 1 - s, 0))
    pars = [w, cb, lg, lb]
    res, got = _pcall_maybe_carrying(
        body, comm, name=name, grid=(nt,),
        in_specs=[slab(0), slab(1), row, row] + [_full_spec(p) for p in pars],
        out_specs=[row, row] + [_full_spec(p) for p in pars],
        out_shape=[_sds((T, D), BF16)] * 2 + [_sds(p.shape, F32) for p in pars],
        scratch_shapes=[pltpu.VMEM((tm, D), F32), pltpu.VMEM((tm + halo, D), F32),
                        pltpu.VMEM((7, tm + halo - 8, D), F32), pltpu.VMEM((tm, D), F32)],
    )(u3, u3, c, dv3, *pars)
    return res if comm is None else (res, got)


def _dg(a, b, ca, cb):
    return lax.dot_general(a.astype(BF16), b.astype(BF16), (((ca,), (cb,)), ((), ())), preferred_element_type=F32)


@jax.custom_vjp
def _dot_nn(a, b):
    return _dg(a, b, 1, 0)


_dot_nn.defvjp(lambda a, b: (_dg(a, b, 1, 0), (a, b)),
               lambda res, g: (_dg(g, res[1], 1, 1), _dg(res[0], g, 0, 0)))


@jax.custom_vjp
def _dot_nt(a, b):
    return _dg(a, b, 1, 1)


_dot_nt.defvjp(lambda a, b: (_dg(a, b, 1, 1), (a, b)),
               lambda res, g: (_dg(g, res[1], 1, 0), _dg(g, res[0], 0, 0)))


@jax.custom_vjp
def _dot_tn(a, b):
    return _dg(a, b, 0, 0)


_dot_tn.defvjp(lambda a, b: (_dg(a, b, 0, 0), (a, b)),
               lambda res, g: (_dg(res[1], g, 1, 1), _dg(res[0], g, 1, 0)))


def _split3(v):
    hi = v.astype(BF16)
    r = v - hi.astype(F32)
    mid = r.astype(BF16)
    return hi, mid, (r - mid.astype(F32)).astype(BF16)


def _x01(v, m, cv, cm, m_left=False):
    acc = None
    for piece in _split3(v):
        t = _dg(m, piece, cm, cv) if m_left else _dg(piece, m, cv, cm)
        acc = t if acc is None else acc + t
    return acc


@jax.custom_vjp
def _expand01(v, m):
    return _x01(v, m, 1, 0)


_expand01.defvjp(lambda v, m: (_x01(v, m, 1, 0), m),
                 lambda m, g: (_x01(g, m, 1, 1), jnp.zeros_like(m)))


@jax.custom_vjp
def _mix01(m, v):
    return _x01(v, m, 0, 1, m_left=True)


_mix01.defvjp(lambda m, v: (_x01(v, m, 0, 1, m_left=True), m),
              lambda m, g: (jnp.zeros_like(m), _x01(g, m, 0, 0, m_left=True)))


def _causal():
    return lax.broadcasted_iota(jnp.int32, (L, L), 0) >= lax.broadcasted_iota(jnp.int32, (L, L), 1)


def _ssd_chunk_prep(dtr, alog, dtb):
    dt = _softplus(dtr + dtb)
    a_cs = _mix01(_causal().astype(F32), dt * (-jnp.exp(alog)))
    return dt, a_cs, a_cs.T


def _ssd_group(xs, dt, a_cs, a_csT, Bg, Cg, zg, sp, dsk, nwg, *, g):
    causal = _causal()
    hi = lax.broadcasted_iota(jnp.int32, (HP, RP), 0)
    ci = lax.broadcasted_iota(jnp.int32, (HP, RP), 1)
    lo = (hi - g * R) * P
    E = ((ci >= lo) & (ci < lo + P)).astype(F32)

    acs_e = _expand01(a_cs, E)
    dt_e = _expand01(dt, E)
    alast_e = acs_e[L - 1:L, :]
    xdt = xs * dt_e
    cb = _dot_nt(Cg, Bg)
    y_off = _dot_nn(Cg, sp) * jnp.exp(acs_e)
    yd = []
    for r in range(R):
        h = g * R + r
        seg = a_cs[:, h:h + 1] - a_csT[h:h + 1, :]
        dec = jnp.exp(jnp.where(causal, seg, -1e30))
        yd.append(_dot_nn(cb * dec, xdt[:, r * P:(r + 1) * P]))
    y = jnp.concatenate(yd, axis=1) + y_off + xs * _expand01(jnp.broadcast_to(dsk, (8, HP)), E)[0:1, :]
    yg = y * _silu(zg)
    yn = yg * lax.rsqrt(jnp.mean(jnp.square(yg), axis=-1, keepdims=True) + RMS_EPS) * nwg
    sc = _dot_tn(Bg, xdt * jnp.exp(alast_e - acs_e))
    return yn, jnp.exp(alast_e) * sp + sc


def _group_cols(g):
    return g // 2, (g % 2) * RP


def _ssd_fwd(x0, x1, bc, dtr, u3, alog, dtb, dsk, nw, *, name, comm=None):
    T = x0.shape[0]
    nc = T // L

    def body(x0_ref, x1_ref, bc_ref, dtr_ref, z0_ref, z1_ref, alog_ref, dtb_ref, dsk_ref, nw_ref, yn_ref, sp_ref, S):
        @pl.when(pl.program_id(0) == 0)
        def _():
            S[...] = jnp.zeros_like(S)

        xr, zr = (x0_ref, x1_ref), (z0_ref, z1_ref)
        dt, a_cs, a_csT = _ssd_chunk_prep(dtr_ref[...], alog_ref[...], dtb_ref[...])
        for g in range(G):
            s, off = _group_cols(g)
            sp = S[g]
            sp_ref[0, g] = sp
            yn, s_next = _ssd_group(xr[s][:, off:off + RP], dt, a_cs, a_csT, bc_ref[:, g * N:(g + 1) * N],
                                    bc_ref[:, G * N + g * N:G * N + (g + 1) * N], zr[s][:, off:off + RP].astype(F32), sp,
                                    dsk_ref[...], nw_ref[:, g * RP:(g + 1) * RP], g=g)
            yn_ref[:, g * RP:(g + 1) * RP] = yn.astype(yn_ref.dtype)
            S[g] = s_next

    row = lambda C: pl.BlockSpec((L, C), lambda c: (c, 0))
    zspec = lambda s: pl.BlockSpec((None, L, D), lambda c: (s, c, 0))
    pars = [alog, dtb, dsk, nw]
    res, got = _pcall_maybe_carrying(
        body, comm, name=name, grid=(nc,),
        in_specs=[row(D), row(D), row(D), row(HP), zspec(2), zspec(3)] + [_full_spec(p) for p in pars],
        out_specs=[row(SD), pl.BlockSpec((1, G, N, RP), lambda c: (c, 0, 0, 0))],
        out_shape=[_sds((T, SD), BF16), _sds((nc, G, N, RP), F32)],
        scratch_shapes=[pltpu.VMEM((G, N, RP), F32)])(x0, x1, bc, dtr, u3, u3, *pars)
    return res if comm is None else (res, got)


def _ssd_bwd(x0, x1, bc, dtr, u3, sprev, dyn, alog, dtb, dsk, nw, *, name, comm=None):
    T = x0.shape[0]
    nc = T // L

    def body(x0_ref, x1_ref, bc_ref, dtr_ref, z0_ref, z1_ref, sp_ref, dyn_ref, alog_ref, dtb_ref, dsk_ref, nw_ref,
             dx0_ref, dx1_ref, dbc_ref, ddtr_ref, dz0_ref, dz1_ref, dalog_ref, ddtb_ref, ddsk_ref, dnw_ref, dS):
        @pl.when(pl.program_id(0) == 0)
        def _():
            dS[...] = jnp.zeros_like(dS)
            for r in (dalog_ref, ddtb_ref, ddsk_ref, dnw_ref):
                r[...] = jnp.zeros_like(r)

        xr, zr = (x0_ref, x1_ref), (z0_ref, z1_ref)
        dxr, dzr = (dx0_ref, dx1_ref), (dz0_ref, dz1_ref)
        (dt, a_cs, a_csT), prep_vjp = jax.vjp(_ssd_chunk_prep, dtr_ref[...], alog_ref[...], dtb_ref[...])
        d_dt, d_acs, d_acsT = jnp.zeros((L, HP), F32), jnp.zeros((L, HP), F32), jnp.zeros((HP, L), F32)
        for g in range(G):
            s, off = _group_cols(g)
            _, vjp = jax.vjp(functools.partial(_ssd_group, g=g), xr[s][:, off:off + RP], dt, a_cs, a_csT,
                             bc_ref[:, g * N:(g + 1) * N], bc_ref[:, G * N + g * N:G * N + (g + 1) * N],
                             zr[s][:, off:off + RP].astype(F32), sp_ref[0, g], dsk_ref[...], nw_ref[:, g * RP:(g + 1) * RP])
            dxs, ddt_g, dacs_g, dacsT_g, dB, dC, dz, dsp, dds, dnwg = vjp((dyn_ref[:, g * RP:(g + 1) * RP], dS[g]))
            dxr[s][:, off:off + RP] = dxs
            dzr[s][:, off:off + RP] = dz.astype(dz0_ref.dtype)
            dbc_ref[:, g * N:(g + 1) * N] = dB
            dbc_ref[:, G * N + g * N:G * N + (g + 1) * N] = dC
            dS[g] = dsp
            d_dt, d_acs, d_acsT = d_dt + ddt_g, d_acs + dacs_g, d_acsT + dacsT_g
            ddsk_ref[...] += dds
            dnw_ref[:, g * RP:(g + 1) * RP] += dnwg
        ddtr, dal, ddb = prep_vjp((d_dt, d_acs, d_acsT))
        dalog_ref[...] += dal
        ddtb_ref[...] += ddb
        ddtr_ref[...] = ddtr.astype(ddtr_ref.dtype)

    row = lambda C: pl.BlockSpec((L, C), lambda c: (nc - 1 - c, 0))
    zspec = lambda s: pl.BlockSpec((None, L, D), lambda c: (s, nc - 1 - c, 0))
    pars = [alog, dtb, dsk, nw]
    res, got = _pcall_maybe_carrying(
        body, comm, name=name, grid=(nc,),
        in_specs=[row(D), row(D), row(D), row(HP), zspec(2), zspec(3),
                  pl.BlockSpec((1, G, N, RP), lambda c: (nc - 1 - c, 0, 0, 0)), row(SD)] + [_full_spec(p) for p in pars],
        out_specs=[row(D), row(D), row(D), row(HP), row(D), row(D)] + [_full_spec(p) for p in pars],
        out_shape=[_sds((T, D), F32)] * 3 + [_sds((T, HP), BF16), _sds((T, D), BF16), _sds((T, D), BF16)]
        + [_sds(p.shape, F32) for p in pars],
        scratch_shapes=[pltpu.VMEM((G, N, RP), F32)])(x0, x1, bc, dtr, u3, u3, sprev, dyn, *pars)
    return res if comm is None else (res, got)


def _loss_head(y, target, *, name):
    T = y.shape[0]
    tm = min(TM, T)

    def body(y_ref, t_ref, loss_ref, dy_ref):
        e = y_ref[...] - t_ref[...]
        dy_ref[...] = e * (1.0 / D)

        @pl.when(pl.program_id(0) == 0)
        def _():
            loss_ref[...] = jnp.zeros_like(loss_ref)

        loss_ref[...] += 0.5 * jnp.sum(jnp.mean(jnp.square(e), axis=-1, keepdims=True), axis=0, keepdims=True)

    row = pl.BlockSpec((tm, D), lambda i: (i, 0))
    return _pcall(body, name=name, grid=(T // tm,), in_specs=[row, row],
                  out_specs=[pl.BlockSpec((1, 128), lambda i: (0, 0)), row],
                  out_shape=[_sds((1, 128), F32), _sds((T, D), F32)])(y, target)


_HBM = pl.BlockSpec(memory_space=pltpu.HBM)
_MESH = pl.DeviceIdType.MESH


def _exchange_comm(bufs, *, scatter):
    nb = len(bufs)

    def copies(in_refs, out_refs, sems, with_arrivals):
        send_sems, recv_sems, local_sems = sems
        x, y, c = lax.axis_index("x"), lax.axis_index("y"), lax.axis_index("c")
        me = 2 * x + y
        peers = [(1 - x, y), (x, 1 - y), (1 - x, 1 - y)]
        own, sends, arrivals = [], [], []
        for b in range(nb):
            src_own = in_refs[b].at[me] if scatter else in_refs[b]
            own.append(pltpu.make_async_copy(src_own, out_refs[b].at[me], local_sems.at[b]))
            for k, (px, py) in enumerate(peers):
                src = in_refs[b].at[2 * px + py] if scatter else in_refs[b]
                sends.append(pltpu.make_async_remote_copy(
                    src_ref=src, dst_ref=out_refs[b].at[me], send_sem=send_sems.at[b, k], recv_sem=recv_sems.at[b, k],
                    device_id=(px, py, c), device_id_type=_MESH))
                if with_arrivals:
                    slot = out_refs[b].at[2 * px + py]
                    arrivals.append(pltpu.make_async_remote_copy(
                        src_ref=slot, dst_ref=slot, send_sem=send_sems.at[b, k], recv_sem=recv_sems.at[b, k],
                        device_id=(px, py, c), device_id_type=_MESH))
        return own, sends, arrivals

    def start(in_refs, out_refs, sems):
        own, sends, _ = copies(in_refs, out_refs, sems, False)
        for cp in own + sends:
            cp.start()

    def wait(in_refs, out_refs, sems):
        own, sends, arrivals = copies(in_refs, out_refs, sems, True)
        for cp in arrivals:
            cp.wait_recv()
        for cp in sends:
            cp.wait_send()
        for cp in own:
            cp.wait()

    return dict(ins=list(bufs), outs=[_sds(b.shape if scatter else (NCHIP,) + b.shape, b.dtype) for b in bufs],
                sems=[pltpu.SemaphoreType.DMA((nb, 3)), pltpu.SemaphoreType.DMA((nb, 3)), pltpu.SemaphoreType.DMA((nb,))],
                start=start, wait=wait)


def _chip_exchange(bufs, *, scatter, name):
    comm = _exchange_comm(bufs, scatter=scatter)
    nb = len(bufs)

    def body(*refs):
        comm["start"](refs[:nb], refs[nb:2 * nb], refs[2 * nb:])
        comm["wait"](refs[:nb], refs[nb:2 * nb], refs[2 * nb:])

    return _ccall(body, name=name, in_specs=[_HBM] * nb, out_specs=[_HBM] * nb, out_shape=comm["outs"],
                  scratch_shapes=comm["sems"])(*bufs)


def _core_swap(bufs, *, name):
    nb = len(bufs)

    def body(*refs):
        in_refs, out_refs, send_sems, recv_sems = refs[:nb], refs[nb:2 * nb], refs[2 * nb], refs[2 * nb + 1]
        x, y, c = lax.axis_index("x"), lax.axis_index("y"), lax.axis_index("c")
        cps = [pltpu.make_async_remote_copy(src_ref=in_refs[b], dst_ref=out_refs[b], send_sem=send_sems.at[b],
                                            recv_sem=recv_sems.at[b], device_id=(x, y, 1 - c), device_id_type=_MESH)
               for b in range(nb)]
        for cp in cps:
            cp.start()
        for cp in cps:
            cp.wait()

    return _ccall(body, name=name, in_specs=[_HBM] * nb, out_specs=[_HBM] * nb,
                  out_shape=[_sds(b.shape, b.dtype) for b in bufs],
                  scratch_shapes=[pltpu.SemaphoreType.DMA((nb,)), pltpu.SemaphoreType.DMA((nb,))])(*bufs)


def _all_gather8(buf, *, name):
    def body(in_ref, out_ref, send_sems, recv_sems, local_sem):
        x, y, c = lax.axis_index("x"), lax.axis_index("y"), lax.axis_index("c")
        me = 4 * x + 2 * y + c
        own = pltpu.make_async_copy(in_ref, out_ref.at[me], local_sem)
        own.start()
        flips = [(fx, fy, fc) for fx in (0, 1) for fy in (0, 1) for fc in (0, 1)][1:]
        peers = [(x ^ fx, y ^ fy, c ^ fc) for fx, fy, fc in flips]
        sends = []
        for k, peer in enumerate(peers):
            cp = pltpu.make_async_remote_copy(src_ref=in_ref, dst_ref=out_ref.at[me], send_sem=send_sems.at[k],
                                              recv_sem=recv_sems.at[k], device_id=peer, device_id_type=_MESH)
            cp.start()
            sends.append(cp)
        for k, (px, py, pc) in enumerate(peers):
            slot = out_ref.at[4 * px + 2 * py + pc]
            pltpu.make_async_remote_copy(src_ref=slot, dst_ref=slot, send_sem=send_sems.at[k], recv_sem=recv_sems.at[k],
                                         device_id=(px, py, pc), device_id_type=_MESH).wait_recv()
        for cp in sends:
            cp.wait_send()
        own.wait()

    return _ccall(body, name=name, in_specs=[_HBM], out_specs=_HBM, out_shape=_sds((8,) + buf.shape, buf.dtype),
                  scratch_shapes=[pltpu.SemaphoreType.DMA((7,)), pltpu.SemaphoreType.DMA((7,)), pltpu.SemaphoreType.DMA])(buf)


def _row_tile(rows, cap):
    if rows <= cap:
        return rows
    return max(t for t in range(16, cap + 1, 16) if rows % t == 0)


def _sum_slots(stack, *, name, cap=256):
    S, Rr, C = stack.shape
    tr = _row_tile(Rr, cap)

    def body(s_ref, o_ref):
        acc = s_ref[0].astype(F32)
        for j in range(1, S):
            acc = acc + s_ref[j].astype(F32)
        o_ref[...] = acc

    return _pcall(body, name=name, grid=(Rr // tr,), in_specs=[pl.BlockSpec((S, tr, C), lambda i: (0, i, 0))],
                  out_specs=pl.BlockSpec((tr, C), lambda i: (i, 0)), out_shape=_sds((Rr, C), F32))(stack)


def _adamw(g_parts, w, m, v, *, name, cap=128):
    Rr, C = w.shape
    tr = _row_tile(Rr, cap)
    ng = len(g_parts)
    c1 = 1.0 / (1.0 - ADAM_B1 ** ADAM_STEP)
    c2 = 1.0 / (1.0 - ADAM_B2 ** ADAM_STEP)

    def body(*refs):
        g = refs[0][...]
        for r in refs[1:ng]:
            g = g + r[...]
        w_ref, m_ref, v_ref, g_out, d_out, m_out, v_out = refs[ng:]
        mn = ADAM_B1 * m_ref[...] + (1.0 - ADAM_B1) * g
        vn = ADAM_B2 * v_ref[...] + (1.0 - ADAM_B2) * jnp.square(g)
        g_out[...] = g
        m_out[...] = mn
        v_out[...] = vn
        d_out[...] = -ADAM_LR * ((mn * c1) / (jnp.sqrt(vn * c2) + ADAM_EPS) + ADAM_WD * w_ref[...])

    spec = pl.BlockSpec((tr, C), lambda i: (i, 0))
    return _pcall(body, name=name, grid=(Rr // tr,), in_specs=[spec] * (ng + 3), out_specs=[spec] * 4,
                  out_shape=[_sds((Rr, C), F32)] * 4)(*g_parts, w, m, v)


def _adamw_layers(mine, other, w3, m3, v3, *, name, cap=128):
    _, Rr, C = w3.shape
    tr = _row_tile(Rr, cap)
    nt = Rr // tr
    c1 = 1.0 / (1.0 - ADAM_B1 ** ADAM_STEP)
    c2 = 1.0 / (1.0 - ADAM_B2 ** ADAM_STEP)

    def body(m0, m1, o0, o1, w_ref, m_ref, v_ref, g_out, d_out, m_out, v_out):
        g = jnp.where(pl.program_id(0) == 0, m0[...] + o0[...], m1[...] + o1[...])
        mn = ADAM_B1 * m_ref[...] + (1.0 - ADAM_B1) * g
        vn = ADAM_B2 * v_ref[...] + (1.0 - ADAM_B2) * jnp.square(g)
        g_out[...] = g
        m_out[...] = mn
        v_out[...] = vn
        d_out[...] = -ADAM_LR * ((mn * c1) / (jnp.sqrt(vn * c2) + ADAM_EPS) + ADAM_WD * w_ref[...])

    g0 = pl.BlockSpec((tr, C), lambda l, i: (jnp.where(l == 0, i, nt - 1), 0))
    g1 = pl.BlockSpec((tr, C), lambda l, i: (jnp.where(l == 1, i, 0), 0))
    s3 = pl.BlockSpec((None, tr, C), lambda l, i: (l, i, 0))
    return _pcall(body, name=name, grid=(2, nt), in_specs=[g0, g1, g0, g1, s3, s3, s3], out_specs=[s3] * 4,
                  out_shape=[_sds(w3.shape, F32)] * 4)(mine[0], mine[1], other[0], other[1], w3, m3, v3)


def _pack(arrs, dtype, row_mult):
    flat = jnp.concatenate([a.reshape(-1).astype(dtype) for a in arrs])
    n = flat.shape[0]
    unit = row_mult * PACK_W
    total = unit * ((n + unit - 1) // unit)
    if total > n:
        flat = jnp.concatenate([flat, jnp.zeros((total - n,), dtype)])
    return flat.reshape(-1, PACK_W)


def _unpack(buf, shapes):
    flat = buf.reshape(-1)
    out, off = [], 0
    for s in shapes:
        n = math.prod(s)
        out.append(flat[off:off + n].reshape(s))
        off += n
    return out


def _conf_pre(a, g):
    return [a * jax.nn.sigmoid(g)]


def _conf_post(cs, cb, lg, lb):
    return (_silu(_layer_norm(cs[0] + cb, lg, lb)),)


def _xbc_post(cs):
    return tuple(_silu(c) for c in cs)


def _ffn_post(cs):
    return (_silu(cs[0]) * cs[1],)


def _mix_fn(ga, gb, ya, yb):
    return (jax.nn.sigmoid(ga.astype(F32)) * ya + jax.nn.sigmoid(gb.astype(F32)) * yb,)


def _res_ln_fn(h, r, g, b):
    return (_layer_norm(ALPHA * h + r, g, b),)


def _ln_fn(x, g, b):
    return (_layer_norm(x, g, b),)


def _carrying(carry, key, gr, call):
    if key not in carry:
        return call(None)
    comm, done = carry[key](gr)
    out, got = call(comm)
    done(got)
    return out


def _two_copies(fn):
    def wrapped(*args):
        return fn(*args) * 2
    return wrapped


def _copies_out(T, tm):
    return [_o2(T, D, F32, tm), _o2(T, D, BF16, tm)]


def _layer_fwd(h, hb, W, l, carry):
    T = h.shape[0]
    tm = min(TM, T)
    nt = T // tm
    tmf = min(TM_FFN, T)
    ntf = T // tmf
    nm = lambda s: f"l{l}_{s}"
    u3 = _carrying(carry, "u", None,
                   lambda comm: _mm_resident_slab_out(hb, W["w_p"], name=nm("u"), width=D, tm=TM_X, out_dtype=BF16, comm=comm))
    dtr = _mm_nn(hb, W["w_dt"], name=nm("dt"))
    v3, cconv = _conf_fwd(u3, W["conv_w"], W["conv_b"], W["conv_ln_g"], W["conv_ln_b"], name=nm("conf"))
    ya = _mm_nn(v3, W["w_co"], name=nm("ya"))
    tmx = min(TM_X, T)
    x0, x1, bc = _dwconv_fwd(_xbc_post, [_slab(u3, 6, tmx), _slab(u3, 7, tmx), _slab(u3, 8, tmx)], W["ssm_w"],
                             W["ssm_b"], [_o2(T, D, F32, tmx)] * 3, K=SSM_K, C=D, name=nm("xbc"), tm=tmx, nt=T // tmx)
    yn, sprev = _carrying(carry, "ssd", None, lambda comm: _ssd_fwd(
        x0, x1, bc, dtr, u3, W["a_log"], W["dt_bias"], W["d_skip"], W["norm_w"], name=nm("ssd"), comm=comm))
    yb = _mm_nn(yn, W["w_so"], name=nm("yb"), tk=min(SD, 1024))
    (m,) = _rows_fwd(_mix_fn, [_slab(u3, 4, tm), _slab(u3, 5, tm), _r2(ya, tm), _r2(yb, tm)], [],
                     [_o2(T, D, BF16, tm)], name=nm("mix"), nt=nt)
    mix = _mm_nn(m, W["w_o"], name=nm("wo"))
    h1, h1b = _rows_fwd(_two_copies(_res_ln_fn), [_r2(h, tm), _r2(mix, tm)], [W["ln1_g"], W["ln1_b"]],
                        _copies_out(T, tm), name=nm("ln1"), nt=nt)
    up3 = _mm_resident_slab_out(h1b, W["w_up"], name=nm("up"), width=FFN, tm=TM)
    (f,) = _dwconv_fwd(_ffn_post, [_slab(up3, 0, tmf), _slab(up3, 1, tmf)], W["ffn_w"], W["ffn_b"],
                       [_o2(T, FFN, BF16, tmf)], K=FFN_K, C=FFN, name=nm("ffnact"), tm=tmf, nt=ntf)
    ffn = _mm_nn(f, W["w_dn"], name=nm("dn"))
    h2, h2b = _rows_fwd(_two_copies(_res_ln_fn), [_r2(h1, tm), _r2(ffn, tm)], [W["ln2_g"], W["ln2_b"]],
                        _copies_out(T, tm), name=nm("ln2"), nt=nt)
    saved = dict(h=h, hb=hb, u3=u3, dtr=dtr, v3=v3, cconv=cconv, ya=ya, x0=x0, x1=x1, bc=bc, sprev=sprev, yn=yn, yb=yb, m=m,
                 mix=mix, h1=h1, h1b=h1b, up3=up3, f=f, ffn=ffn)
    return h2, h2b, saved


def _layer_bwd(dh2, W, sv, l, carry):
    T = dh2.shape[0]
    tm = min(TM, T)
    nt = T // tm
    tmf = min(TM_FFN, T)
    ntf = T // tmf
    nm = lambda s: f"l{l}_{s}"
    gr = {}
    (dres2, dffn), (gr["ln2_g"], gr["ln2_b"]) = _rows_bwd(
        _res_ln_fn, [_r2(sv["h1"], tm), _r2(sv["ffn"], tm)], [W["ln2_g"], W["ln2_b"]], [_r2(dh2, tm)],
        [_o2(T, D, F32, tm), _o2(T, D, BF16, tm)], name=nm("ln2_b"), nt=nt)
    df = _mm_nt(dffn, W["w_dn"], name=nm("dn_dx"))
    gr["w_dn"] = _mm_tn(sv["f"], dffn, name=nm("dn_dw"), tmo=FFN // 2)
    up3 = sv["up3"]
    (dgate, dval), gr["ffn_w"], gr["ffn_b"] = _carrying(carry, "ffnact_b", gr, lambda comm: _dwconv_bwd(
        _ffn_post, [_slab(up3, 0, tmf), _slab(up3, 1, tmf)],
        [_halo_slab(up3, 0, tmf, DW_HALO), _halo_slab(up3, 1, tmf, DW_HALO)], W["ffn_w"], W["ffn_b"], [_r2(df, tmf)],
        K=FFN_K, C=FFN, name=nm("ffnact_b"), tm=tmf, nt=ntf, comm=comm))
    dh1 = _mm_cols_nt([dgate, dval], W["w_up"], name=nm("up_dx"), add=dres2)
    gr["w_up"] = _mm_tn_cols(sv["h1b"], [dgate, dval], name=nm("up_dw"), tmo=min(512, D))
    (dres1, dmix), (gr["ln1_g"], gr["ln1_b"]) = _rows_bwd(
        _res_ln_fn, [_r2(sv["h"], tm), _r2(sv["mix"], tm)], [W["ln1_g"], W["ln1_b"]], [_r2(dh1, tm)],
        [_o2(T, D, F32, tm), _o2(T, D, BF16, tm)], name=nm("ln1_b"), nt=nt)
    dm = _mm_nt(dmix, W["w_o"], name=nm("wo_dx"))
    gr["w_o"] = _mm_tn(sv["m"], dmix, name=nm("wo_dw"), tmo=min(512, D))
    u3 = sv["u3"]
    (dga, dgb, dya, dyb), _ = _rows_bwd(
        _mix_fn, [_slab(u3, 4, tm), _slab(u3, 5, tm), _r2(sv["ya"], tm), _r2(sv["yb"], tm)], [], [_r2(dm, tm)],
        [_o2(T, D, BF16, tm)] * 4, name=nm("mix_b"), nt=nt)
    dv3 = _mm_nt(dya, W["w_co"], name=nm("ya_dx"))
    gr["w_co"] = _mm_tn(sv["v3"], dya, name=nm("ya_dw"), tmo=min(512, D))
    da, dg, gr["conv_w"], gr["conv_b"], gr["conv_ln_g"], gr["conv_ln_b"] = _carrying(carry, "conf_b", gr, lambda comm: _conf_bwd(
        u3, sv["cconv"], dv3, W["conv_w"], W["conv_b"], W["conv_ln_g"], W["conv_ln_b"], name=nm("conf_b"), comm=comm))
    dyn = _mm_nt(dyb, W["w_so"], name=nm("yb_dx"))
    gr["w_so"] = _mm_tn(sv["yn"], dyb, name=nm("yb_dw"), tmo=min(512, SD))
    (dx0, dx1, dbc, ddtr, dz0, dz1, gr["a_log"], gr["dt_bias"], gr["d_skip"], gr["norm_w"]) = _carrying(
        carry, "ssd_b", gr, lambda comm: _ssd_bwd(
            sv["x0"], sv["x1"], sv["bc"], sv["dtr"], u3, sv["sprev"], dyn, W["a_log"], W["dt_bias"], W["d_skip"],
            W["norm_w"], name=nm("ssd_b"), comm=comm))
    tmx = min(TM_X, T)
    (du6, du7, du8), gr["ssm_w"], gr["ssm_b"] = _dwconv_bwd(
        _xbc_post, [_slab(u3, 6, tmx), _slab(u3, 7, tmx), _slab(u3, 8, tmx)],
        [_halo_slab(u3, 6, tmx, DW_HALO), _halo_slab(u3, 7, tmx, DW_HALO), _halo_slab(u3, 8, tmx, DW_HALO)],
        W["ssm_w"], W["ssm_b"], [_r2(dx0, tmx), _r2(dx1, tmx), _r2(dbc, tmx)], K=SSM_K, C=D, name=nm("xbc_b"),
        tm=tmx, nt=T // tmx)
    du = [da, dg, dz0, dz1, dga, dgb, du6, du7, du8]
    gr["w_dt"] = _mm_tn(sv["hb"], ddtr, name=nm("dt_dw"), tmo=min(512, D))
    gr["w_p"] = _carrying(carry, "u_dw", gr, lambda comm: _mm_tn_cols(sv["hb"], du, name=nm("u_dw"),
                                                                      tmo=min(512, D), comm=comm))
    dh_a = _mm_nt(ddtr, W["w_dt"], name=nm("dt_dx"), add=dres1)
    dh = _carrying(carry, "u_dx", gr, lambda comm: _mm_cols_nt(du, W["w_p"], name=nm("u_dx"), add=dh_a, comm=comm))
    return dh, gr


_U_SPLIT = (2 * D + SD, 2 * D + SD + XBC, 2 * D + SD + XBC + H)


def _pad_rows(a, rows):
    return jnp.concatenate([a, jnp.zeros((rows - a.shape[0],) + a.shape[1:], a.dtype)], axis=0)


def _pad_lanes(a, lanes):
    return jnp.concatenate([a, jnp.zeros(a.shape[:-1] + (lanes - a.shape[-1],), a.dtype)], axis=-1)


def _w_in_layout(w_in):
    e0, e1, e2 = _U_SPLIT
    return dict(w_p=jnp.concatenate([w_in[:, :e0], w_in[:, e2:], w_in[:, e0:e1]], axis=1),
                w_dt=_pad_lanes(w_in[:, e1:e2], HP))


_MM_KEY = dict(w_conv_out="w_co", w_ssm_out="w_so", w_o="w_o", w_ffn_up="w_up", w_ffn_down="w_dn")


def _small_layer_weights(full, l):
    row = lambda a: a.reshape(1, -1)
    ssm_w = full["ssm_conv_w"][l]
    ffn_w = full["ffn_dw_w"][l]
    ssm_b = full["ssm_conv_b"][l]
    ffn_b = full["ffn_dw_b"][l]
    W = dict(
        conv_w=_pad_rows(full["conv_dw_w"][l], 32),
        conv_b=row(full["conv_dw_b"][l]), conv_ln_g=row(full["conv_ln_g"][l]), conv_ln_b=row(full["conv_ln_b"][l]),
        ssm_w=jnp.stack([_pad_rows(ssm_w[:, p * D:(p + 1) * D], 8) for p in range(3)]),
        ssm_b=[row(ssm_b[p * D:(p + 1) * D]) for p in range(3)],
        a_log=_pad_lanes(row(full["ssm_a_log"][l]), HP), dt_bias=_pad_lanes(row(full["ssm_dt_bias"][l]), HP),
        d_skip=_pad_lanes(row(full["ssm_d"][l]), HP), norm_w=row(full["ssm_norm_w"][l]),
        ln1_g=row(full["ln1_g"][l]), ln1_b=row(full["ln1_b"][l]),
        ffn_w=jnp.stack([_pad_rows(ffn_w[:, p * FFN:(p + 1) * FFN], 8) for p in range(2)]),
        ffn_b=[row(ffn_b[p * FFN:(p + 1) * FFN]) for p in range(2)],
        ln2_g=row(full["ln2_g"][l]), ln2_b=row(full["ln2_b"][l]),
    )
    return W


def _w_in_grad(gr):
    e0 = _U_SPLIT[0]
    wp = gr["w_p"]
    return jnp.concatenate([wp[:, :e0], wp[:, e0 + 2 * D:e0 + 2 * D + XBC], gr["w_dt"][:, :H], wp[:, e0:e0 + 2 * D]], axis=1)


def _layer_grads_to_reference_layout(gr):
    return dict(
        w_in=_w_in_grad(gr), conv_dw_w=gr["conv_w"][:CONV_K], conv_dw_b=gr["conv_b"][0], conv_ln_g=gr["conv_ln_g"][0],
        conv_ln_b=gr["conv_ln_b"][0], w_conv_out=gr["w_co"],
        ssm_conv_w=jnp.concatenate([gr["ssm_w"][p, :SSM_K] for p in range(3)], axis=1),
        ssm_conv_b=jnp.concatenate([b[0] for b in gr["ssm_b"]]),
        ssm_dt_bias=gr["dt_bias"][0, :H], ssm_a_log=gr["a_log"][0, :H], ssm_d=gr["d_skip"][0, :H],
        ssm_norm_w=gr["norm_w"][0], w_ssm_out=gr["w_so"], w_o=gr["w_o"], ln1_g=gr["ln1_g"][0], ln1_b=gr["ln1_b"][0],
        w_ffn_up=gr["w_up"], ffn_dw_w=jnp.concatenate([gr["ffn_w"][p, :FFN_K] for p in range(2)], axis=1),
        ffn_dw_b=jnp.concatenate([b[0] for b in gr["ffn_b"]]), w_ffn_down=gr["w_dn"], ln2_g=gr["ln2_g"][0],
        ln2_b=gr["ln2_b"][0],
    )


_BIG = dict(w_in=2, w_conv_out=1, w_ssm_out=1, w_o=1, w_ffn_up=2, w_ffn_down=1)
_SMALL_SHARDED = dict(conv_dw_w=2, ssm_conv_w=2, ffn_dw_w=2)
_REPLICATED = ("ln_in_g", "ln_in_b", "conv_dw_b", "conv_ln_g", "conv_ln_b", "ssm_conv_b", "ssm_dt_bias", "ssm_a_log",
               "ssm_d", "ssm_norm_w", "ln1_g", "ln1_b", "ffn_dw_b", "ln2_g", "ln2_b")
_WEIGHTS = ("ln_in_g", "ln_in_b", "w_in", "conv_dw_w", "conv_dw_b", "conv_ln_g", "conv_ln_b", "w_conv_out", "ssm_conv_w",
            "ssm_conv_b", "ssm_dt_bias", "ssm_a_log", "ssm_d", "ssm_norm_w", "w_ssm_out", "w_o", "ln1_g", "ln1_b",
            "w_ffn_up", "ffn_dw_w", "ffn_dw_b", "w_ffn_down", "ln2_g", "ln2_b")


def _split_chips(a, axis):
    rows, cols = a.shape
    if axis == 0:
        return a.reshape(NCHIP, rows // NCHIP, cols)
    return a.reshape(rows, NCHIP, cols // NCHIP).transpose(1, 0, 2)


def kernel(x, ln_in_g, ln_in_b, w_in, conv_dw_w, conv_dw_b, conv_ln_g, conv_ln_b, w_conv_out, ssm_conv_w, ssm_conv_b, ssm_dt_bias, ssm_a_log, ssm_d, ssm_norm_w, w_ssm_out, w_o, ln1_g, ln1_b, w_ffn_up, ffn_dw_w, ffn_dw_b, w_ffn_down, ln2_g, ln2_b, loss_target, m_ln_in_g, m_ln_in_b, m_w_in, m_conv_dw_w, m_conv_dw_b, m_conv_ln_g, m_conv_ln_b, m_w_conv_out, m_ssm_conv_w, m_ssm_conv_b, m_ssm_dt_bias, m_ssm_a_log, m_ssm_d, m_ssm_norm_w, m_w_ssm_out, m_w_o, m_ln1_g, m_ln1_b, m_w_ffn_up, m_ffn_dw_w, m_ffn_dw_b, m_w_ffn_down, m_ln2_g, m_ln2_b, v_ln_in_g, v_ln_in_b, v_w_in, v_conv_dw_w, v_conv_dw_b, v_conv_ln_g, v_conv_ln_b, v_w_conv_out, v_ssm_conv_w, v_ssm_conv_b, v_ssm_dt_bias, v_ssm_a_log, v_ssm_d, v_ssm_norm_w, v_w_ssm_out, v_w_o, v_ln1_g, v_ln1_b, v_w_ffn_up, v_ffn_dw_w, v_ffn_dw_b, v_w_ffn_down, v_ln2_g, v_ln2_b):
    args = locals()
    w = {n: args[n] for n in _WEIGHTS}
    mom = {n: args["m_" + n] for n in _WEIGHTS}
    vel = {n: args["v_" + n] for n in _WEIGHTS}
    T = x.shape[1]
    tm = min(TM, T)
    nt = T // tm
    chip = 2 * lax.axis_index("x") + lax.axis_index("y")

    assert DEPTH == 2
    big_names, small_names = list(_BIG), list(_SMALL_SHARDED)
    rest_big = [n for n in big_names if n != "w_in"]
    bf = {n: w[n].astype(BF16) for n in big_names}
    join = lambda got, axis: jnp.concatenate([got[j] for j in range(NCHIP)], axis=axis)
    first = _chip_exchange([bf["w_in"][0]] + [w[n] for n in small_names], scatter=False, name="gather_first")
    full = {n: join(gk, _SMALL_SHARDED[n]) for n, gk in zip(small_names, first[1:])}
    for n in _REPLICATED:
        full[n] = w[n]
    Ws = [_small_layer_weights(full, l) for l in range(DEPTH)]
    Ws[0].update(_w_in_layout(join(first[0], 1)))

    def rest_arrived(l):
        def done(got):
            for n, gk in zip(rest_big, got):
                Ws[l][_MM_KEY[n]] = join(gk, _BIG[n] - 1)
        return done

    carry_fwd = [
        {"u": lambda gr: (_exchange_comm([bf[n][0] for n in rest_big], scatter=False), rest_arrived(0)),
         "ssd": lambda gr: (_exchange_comm([bf["w_in"][1]], scatter=False),
                            lambda got: Ws[1].update(_w_in_layout(join(got[0], 1))))},
        {"u": lambda gr: (_exchange_comm([bf[n][1] for n in rest_big], scatter=False), rest_arrived(1))},
    ]

    x2 = x.reshape(T, D)
    g_in, b_in = ln_in_g.reshape(1, D), ln_in_b.reshape(1, D)
    h, hb = _rows_fwd(_two_copies(_ln_fn), [_r2(x2, tm)], [g_in, b_in], _copies_out(T, tm), name="ln_in", nt=nt)
    saved = []
    for l in range(DEPTH):
        h, hb, sv = _layer_fwd(h, hb, Ws[l], l, carry_fwd[l])
        saved.append(sv)
    loss_row, dh = _loss_head(h, loss_target.reshape(T, D), name="loss")

    arrived = {}

    def exchange(names, l, grads):
        def make(gr):
            src = grads(gr)
            def done(got):
                for n, gk in zip(names, got):
                    arrived[(n, l)] = gk
            return _exchange_comm([_split_chips(src[n], _BIG[n] - 1) for n in names], scatter=True), done
        return make

    layer_grads = [None] * DEPTH
    dh, gr = _layer_bwd(dh, Ws[1], saved[1], 1, {})
    layer_grads[1] = _layer_grads_to_reference_layout(gr)
    g1 = lambda gr: layer_grads[1]
    g0 = lambda gr: {n: gr[_MM_KEY[n]] for n in rest_big}
    dh, gr = _layer_bwd(dh, Ws[0], saved[0], 0, {
        "ffnact_b": exchange(["w_in"], 1, g1),
        "conf_b": exchange(["w_ffn_up", "w_ffn_down"], 1, g1),
        "ssd_b": exchange(["w_conv_out", "w_ssm_out", "w_o"], 1, g1),
        "u_dw": exchange(rest_big, 0, g0),
        "u_dx": exchange(["w_in"], 0, lambda gr: {"w_in": _w_in_grad(gr)})})
    layer_grads[0] = _layer_grads_to_reference_layout(gr)
    (grad_x2,), (d_g_in, d_b_in) = _rows_bwd(_ln_fn, [_r2(x2, tm)], [g_in, b_in], [_r2(dh, tm)], [_o2(T, D, F32, tm)],
                                             name="ln_in_b", nt=nt)
    local = {n: jnp.stack([layer_grads[l][n] for l in range(DEPTH)]) for n in _WEIGHTS[2:] if n not in _BIG}
    local["ln_in_g"], local["ln_in_b"] = d_g_in[0], d_b_in[0]
    res = [{}, {}, {}, {}]

    keys = [(n, l) for n in big_names for l in range(DEPTH)]
    mine = [_sum_slots(arrived[k], name=f"sum_chips_{k[0]}_{k[1]}") for k in keys]
    other = _core_swap(mine, name="swap_cores")
    for i, n in enumerate(big_names):
        outs = _adamw_layers(mine[2 * i:2 * i + 2], other[2 * i:2 * i + 2], w[n], mom[n], vel[n], name="adamw_" + n)
        for q in range(4):
            res[q][n] = outs[q]

    rest_names = list(_REPLICATED) + small_names
    part = _pack([loss_row] + [local[n] for n in rest_names], F32, 8)
    parts = _all_gather8(part, name="gather_small")
    total = _sum_slots(parts, name="sum_devices")
    tot = _unpack(total, [loss_row.shape] + [local[n].shape for n in rest_names])
    loss = tot[0][0, 0]
    g_rest = {}
    for n, t in zip(rest_names, tot[1:]):
        if n in _SMALL_SHARDED:
            ax = _SMALL_SHARDED[n]
            t = lax.dynamic_slice_in_dim(t, chip * w[n].shape[ax], w[n].shape[ax], axis=ax)
        g_rest[n] = t
    pk = lambda d: _pack([d[n] for n in rest_names], F32, 8)
    rest_out = _adamw([pk(g_rest)], pk(w), pk(mom), pk(vel), name="adamw_rest")
    rest_out = [_unpack(o, [w[n].shape for n in rest_names]) for o in rest_out]

    for q in range(4):
        for k, n in enumerate(rest_names):
            res[q][n] = rest_out[q][k]
    grad_x = grad_x2.reshape(x.shape)
    return (loss, grad_x, *[res[0][n] for n in _WEIGHTS], *[res[1][n] for n in _WEIGHTS],
            *[res[2][n] for n in _WEIGHTS], *[res[3][n] for n in _WEIGHTS])
```

```python
import functools
import math

import jax
import jax.numpy as jnp
from jax import lax
from jax.experimental import pallas as pl
from jax.experimental.pallas import tpu as pltpu

F32 = jnp.float32
BF16 = jnp.bfloat16

D = 1024
DEPTH = 2
CONV_K = 31
SD = 2 * D
P = 64
H = SD // P
G = 4
R = H // G
N = 128
RP = R * P
SSM_K = 4
L = 128
XBC = SD + 2 * G * N
FFN = 2816
FFN_K = 3
IN_DIM = 2 * D + SD + XBC + H + 2 * D
ALPHA = (2 * DEPTH) ** 0.25
LN_EPS = 1e-5
RMS_EPS = 1e-5
ADAM_LR, ADAM_B1, ADAM_B2, ADAM_EPS, ADAM_WD, ADAM_STEP = 0.001, 0.9, 0.999, 1e-08, 0.01, 10

HP = 128
NCHIP = 4
PACK_W = 1024
VMEM_LIMIT = 56 * 1024 * 1024
TM = 512
TM_X = 256
TM_FFN = 256
TK = 1024

assert D == 2 * RP and 2 * G * N == D and XBC == 3 * D and H <= HP


def _pcall(body, *, name, grid=(), in_specs, out_specs, out_shape, scratch_shapes=()):
    params = pltpu.CompilerParams(vmem_limit_bytes=VMEM_LIMIT, dimension_semantics=("arbitrary",) * len(grid))
    return pl.pallas_call(body, name=name, grid=grid, in_specs=in_specs, out_specs=out_specs, out_shape=out_shape,
                          scratch_shapes=list(scratch_shapes), compiler_params=params)


def _pcall_carrying(body, comm, *, name, grid, in_specs, out_specs, out_shape, scratch_shapes=()):
    in_specs, out_specs, out_shape = list(in_specs), list(out_specs), list(out_shape)
    scratch_shapes = list(scratch_shapes)
    n_in, n_out, n_scr = len(in_specs), len(out_specs), len(scratch_shapes)
    nci, nco = len(comm["ins"]), len(comm["outs"])

    def wrapped(*refs):
        ins, cin = refs[:n_in], refs[n_in:n_in + nci]
        outs = refs[n_in + nci:n_in + nci + n_out]
        cout = refs[n_in + nci + n_out:n_in + nci + n_out + nco]
        scr = refs[n_in + nci + n_out + nco:n_in + nci + n_out + nco + n_scr]
        csem = refs[n_in + nci + n_out + nco + n_scr:]
        ids = [pl.program_id(ax) for ax in range(len(grid))]
        first = functools.reduce(jnp.logical_and, [i == 0 for i in ids])
        last = functools.reduce(jnp.logical_and, [i == g - 1 for i, g in zip(ids, grid)])

        @pl.when(first)
        def _():
            comm["start"](cin, cout, csem)

        body(*ins, *outs, *scr)

        @pl.when(last)
        def _():
            comm["wait"](cin, cout, csem)

    call = _pcall(wrapped, name=name, grid=grid, in_specs=in_specs + [_HBM] * nci, out_specs=out_specs + [_HBM] * nco,
                  out_shape=out_shape + list(comm["outs"]), scratch_shapes=scratch_shapes + list(comm["sems"]))

    def run(*operands):
        res = call(*operands, *comm["ins"])
        return list(res[:n_out]), list(res[n_out:])

    return run


def _pcall_maybe_carrying(body, comm, **kw):
    if comm is not None:
        return _pcall_carrying(body, comm, **kw)
    call = _pcall(body, **kw)
    return lambda *operands: (list(call(*operands)), None)


def _ccall(body, *, name, in_specs, out_specs, out_shape, scratch_shapes):
    return pl.pallas_call(body, name=name, in_specs=in_specs, out_specs=out_specs, out_shape=out_shape,
                          scratch_shapes=list(scratch_shapes))


def _full_spec(a):
    nd = a.ndim
    return pl.BlockSpec(a.shape, lambda *_: (0,) * nd)


def _sds(shape, dtype):
    return jax.ShapeDtypeStruct(tuple(shape), dtype)


def _mm(a, b, *, name, grid, a_spec, b_spec, o_spec, out_shape, acc_shape, trans_a=False, trans_b=False, add=None,
        add_spec=None, comm=None):
    nk = grid[2]
    dn = (((0 if trans_a else 1,), (1 if trans_b else 0,)), ((), ()))
    has_add = add is not None

    def body(*refs):
        a_ref, b_ref = refs[0], refs[1]
        add_ref = refs[2] if has_add else None
        o_ref = refs[3] if has_add else refs[2]
        part = lax.dot_general(a_ref[...].astype(BF16), b_ref[...].astype(BF16), dn, preferred_element_type=F32)

        def finish(res):
            if has_add:
                res = res + add_ref[...]
            o_ref[...] = res.astype(o_ref.dtype)

        if nk == 1:
            finish(part)
        else:
            acc = refs[-1]
            k = pl.program_id(2)

            @pl.when(k == 0)
            def _():
                acc[...] = part

            @pl.when(k > 0)
            def _():
                acc[...] += part

            @pl.when(k == nk - 1)
            def _():
                finish(acc[...])

    ins = [a, b] + ([add] if has_add else [])
    specs = [a_spec, b_spec] + ([add_spec] if has_add else [])
    scratch = [pltpu.VMEM(acc_shape, F32)] if nk > 1 else []
    if comm is not None:
        (out,), got = _pcall_carrying(body, comm, name=name, grid=grid, in_specs=specs, out_specs=[o_spec],
                                      out_shape=[out_shape], scratch_shapes=scratch)(*ins)
        return out, got
    return _pcall(body, name=name, grid=grid, in_specs=specs, out_specs=o_spec, out_shape=out_shape,
                  scratch_shapes=scratch)(*ins)


def _mm_nn(a, b, *, name, out_dtype=F32, tn=None, tk=None, add=None):
    M, K = a.shape
    Nn = b.shape[1]
    tm = min(TM, M)
    tn = Nn if tn is None else tn
    tk = K if tk is None else tk
    grid = (M // tm, Nn // tn, K // tk)
    return _mm(a, b, name=name, grid=grid,
               a_spec=pl.BlockSpec((tm, tk), lambda i, j, k: (i, k)),
               b_spec=pl.BlockSpec((tk, tn), lambda i, j, k: (k, j)),
               o_spec=pl.BlockSpec((tm, tn), lambda i, j, k: (i, j)),
               out_shape=_sds((M, Nn), out_dtype), acc_shape=(tm, tn), add=add,
               add_spec=pl.BlockSpec((tm, tn), lambda i, j, k: (i, j)))


def _mm_nt(a, b, *, name, add=None):
    M, K = a.shape
    Nn = b.shape[0]
    tm = min(TM, M)
    return _mm(a, b, name=name, grid=(M // tm, 1, 1), trans_b=True,
               a_spec=pl.BlockSpec((tm, K), lambda i, j, k: (i, 0)),
               b_spec=pl.BlockSpec((Nn, K), lambda i, j, k: (0, 0)),
               o_spec=pl.BlockSpec((tm, Nn), lambda i, j, k: (i, 0)),
               out_shape=_sds((M, Nn), F32), acc_shape=(tm, Nn), add=add,
               add_spec=pl.BlockSpec((tm, Nn), lambda i, j, k: (i, 0)))


def _mm_resident_slab_out(a, w, *, name, width, tm, out_dtype=F32, comm=None):
    M, K = a.shape
    S = w.shape[1] // width
    tm = min(tm, M)

    def body(a_ref, w_hbm, o_ref, w_vmem, sem):
        @pl.when(pl.program_id(0) == 0)
        def _():
            cp = pltpu.make_async_copy(w_hbm, w_vmem, sem)
            cp.start()
            cp.wait()

        av = a_ref[...].astype(BF16)
        for s in range(S):
            o_ref[s] = jnp.dot(av, w_vmem[:, s * width:(s + 1) * width], preferred_element_type=F32).astype(o_ref.dtype)

    kw = dict(name=name, grid=(M // tm,),
              in_specs=[pl.BlockSpec((tm, K), lambda i: (i, 0)), pl.BlockSpec(memory_space=pl.ANY)],
              scratch_shapes=[pltpu.VMEM(w.shape, w.dtype), pltpu.SemaphoreType.DMA])
    o_spec = pl.BlockSpec((S, tm, width), lambda i: (0, i, 0))
    if comm is not None:
        (out,), got = _pcall_carrying(body, comm, out_specs=[o_spec], out_shape=[_sds((S, M, width), out_dtype)], **kw)(a, w)
        return out, got
    return _pcall(body, out_specs=o_spec, out_shape=_sds((S, M, width), out_dtype), **kw)(a, w)


def _mm_cols_nt(a_list, w, *, name, add, comm=None):
    S = len(a_list)
    M, width = a_list[0].shape
    Nn = w.shape[0]
    tm = min(TM, M)

    def body(*refs):
        a_refs, w_hbm, add_ref, o_ref, w_vmem, sem = refs[:S], refs[S], refs[S + 1], refs[S + 2], refs[S + 3], refs[S + 4]

        @pl.when(pl.program_id(0) == 0)
        def _():
            cp = pltpu.make_async_copy(w_hbm, w_vmem, sem)
            cp.start()
            cp.wait()

        acc = add_ref[...]
        for s in range(S):
            acc = acc + lax.dot_general(a_refs[s][...].astype(BF16), w_vmem[:, s * width:(s + 1) * width],
                                        (((1,), (1,)), ((), ())), preferred_element_type=F32)
        o_ref[...] = acc

    row = lambda C: pl.BlockSpec((tm, C), lambda i: (i, 0))
    kw = dict(name=name, grid=(M // tm,), in_specs=[row(width)] * S + [pl.BlockSpec(memory_space=pl.ANY), row(Nn)],
              scratch_shapes=[pltpu.VMEM(w.shape, w.dtype), pltpu.SemaphoreType.DMA])
    if comm is not None:
        (out,), got = _pcall_carrying(body, comm, out_specs=[row(Nn)], out_shape=[_sds((M, Nn), F32)], **kw)(*a_list, w, add)
        return out, got
    return _pcall(body, out_specs=row(Nn), out_shape=_sds((M, Nn), F32), **kw)(*a_list, w, add)


def _mm_tn_cols(a, b_list, *, name, tmo, comm=None):
    T, M = a.shape
    S = len(b_list)
    width = b_list[0].shape[1]
    tk = min(TK, T)
    nk = T // tk

    def body(*refs):
        a_ref, b_refs, o_ref, acc = refs[0], refs[1:S + 1], refs[S + 1], refs[S + 2]
        j, k = pl.program_id(1), pl.program_id(2)
        for s in range(S):
            @pl.when(j == s)
            def _():
                part = lax.dot_general(a_ref[...], b_refs[s][...], (((0,), (0,)), ((), ())), preferred_element_type=F32)

                @pl.when(k == 0)
                def _():
                    acc[...] = part

                @pl.when(k > 0)
                def _():
                    acc[...] += part

        @pl.when(k == nk - 1)
        def _():
            o_ref[...] = acc[...].astype(o_ref.dtype)

    def b_spec(s):
        return pl.BlockSpec((tk, width), lambda i, j, k: (jnp.where(j == s, k, jnp.where(j < s, 0, nk - 1)), 0))

    kw = dict(name=name, grid=(M // tmo, S, nk),
              in_specs=[pl.BlockSpec((tk, tmo), lambda i, j, k: (k, i))] + [b_spec(s) for s in range(S)],
              scratch_shapes=[pltpu.VMEM((tmo, width), F32)])
    o_spec = pl.BlockSpec((tmo, width), lambda i, j, k: (i, j))
    if comm is not None:
        (out,), got = _pcall_carrying(body, comm, out_specs=[o_spec], out_shape=[_sds((M, S * width), BF16)], **kw)(a, *b_list)
        return out, got
    return _pcall(body, out_specs=o_spec, out_shape=_sds((M, S * width), BF16), **kw)(a, *b_list)


def _mm_tn(a, b, *, name, tmo, tn=None, comm=None):
    T, M = a.shape
    Nn = b.shape[1]
    tn = Nn if tn is None else tn
    tk = min(TK, T)
    return _mm(a, b, name=name, grid=(M // tmo, Nn // tn, T // tk), trans_a=True,
               a_spec=pl.BlockSpec((tk, tmo), lambda i, j, k: (k, i)),
               b_spec=pl.BlockSpec((tk, tn), lambda i, j, k: (k, j)),
               o_spec=pl.BlockSpec((tmo, tn), lambda i, j, k: (i, j)),
               out_shape=_sds((M, Nn), BF16), acc_shape=(tmo, tn), comm=comm)


def _r2(a, tm):
    return (a, (tm, a.shape[1]), lambda i: (i, 0))


def _slab(a3, s, tm):
    return (a3, (None, tm, a3.shape[2]), lambda i: (s, i, 0))


def _o2(T, C, dtype, tm):
    return ((T, C), dtype, (tm, C), lambda i: (i, 0))


def _rows_fwd(fn, row_ins, par_ins, outs, *, name, nt):
    nr, npar = len(row_ins), len(par_ins)

    def body(*refs):
        vals = [r[...] for r in refs[:nr + npar]]
        res = fn(*vals)
        for o_ref, v in zip(refs[nr + npar:], res):
            o_ref[...] = v.astype(o_ref.dtype)

    return _pcall(body, name=name, grid=(nt,),
                  in_specs=[pl.BlockSpec(bs, im) for (_, bs, im) in row_ins] + [_full_spec(p) for p in par_ins],
                  out_specs=[pl.BlockSpec(bs, im) for (_, _, bs, im) in outs],
                  out_shape=[_sds(s, d) for (s, d, _, _) in outs])(*[r[0] for r in row_ins], *par_ins)


def _rows_bwd(fn, row_ins, par_ins, cot_ins, drow_outs, *, name, nt):
    nr, npar, nc = len(row_ins), len(par_ins), len(cot_ins)
    keep = [k for k, o in enumerate(drow_outs) if o is not None]

    def body(*refs):
        vals = [r[...].astype(F32) for r in refs[:nr + npar]]
        cots = [r[...].astype(F32) for r in refs[nr + npar:nr + npar + nc]]
        orefs = refs[nr + npar + nc:]
        _, vjp = jax.vjp(fn, *vals)
        grads = vjp(tuple(cots))
        for o_ref, k in zip(orefs[:len(keep)], keep):
            o_ref[...] = grads[k].astype(o_ref.dtype)
        prefs = orefs[len(keep):]

        @pl.when(pl.program_id(0) == 0)
        def _():
            for p_ref in prefs:
                p_ref[...] = jnp.zeros_like(p_ref)

        for p_ref, g in zip(prefs, grads[nr:]):
            p_ref[...] += g

    outs = [drow_outs[k] for k in keep]
    res = _pcall(body, name=name, grid=(nt,),
                 in_specs=[pl.BlockSpec(bs, im) for (_, bs, im) in row_ins] + [_full_spec(p) for p in par_ins]
                 + [pl.BlockSpec(bs, im) for (_, bs, im) in cot_ins],
                 out_specs=[pl.BlockSpec(bs, im) for (_, _, bs, im) in outs] + [_full_spec(p) for p in par_ins],
                 out_shape=[_sds(s, d) for (s, d, _, _) in outs] + [_sds(p.shape, F32) for p in par_ins],
                 )(*[r[0] for r in row_ins], *par_ins, *[c[0] for c in cot_ins])
    return list(res[:len(keep)]), list(res[len(keep):])


def _layer_norm(v, g, b):
    mu = jnp.mean(v, axis=-1, keepdims=True)
    var = jnp.mean(jnp.square(v - mu), axis=-1, keepdims=True)
    return (v - mu) * lax.rsqrt(var + LN_EPS) * g + b


def _silu(v):
    return v * jax.nn.sigmoid(v)


def _softplus(v):
    return jnp.maximum(v, 0.0) + jnp.log1p(jnp.exp(-jnp.abs(v)))


def _halo_of(K):
    return 8 * ((K - 1 + 7) // 8)


DW_HALO = 8
DW_RB = 16
DW_LC = 256


def _dw_taps(win, w_ref, p, ls, K, shift_of):
    acc = None
    for k in range(K):
        o = shift_of(k)
        term = win[o:o + DW_RB, :] * w_ref[p, k:k + 1, ls]
        acc = term if acc is None else acc + term
    return acc


def _dwconv_fwd(post, part_ins, w, biases, outs, *, K, C, name, tm, nt):
    nparts, halo = len(part_ins), DW_HALO

    def body(*refs):
        x_refs, w_ref = refs[:nparts], refs[nparts]
        b_refs = refs[nparts + 1:2 * nparts + 1]
        orefs, buf = refs[2 * nparts + 1:-1], refs[-1]
        i = pl.program_id(0)
        for p in range(nparts):
            @pl.when(i == 0)
            def _():
                buf[p, pl.ds(0, halo), :] = jnp.zeros((halo, C), F32)

            @pl.when(i > 0)
            def _():
                buf[p, pl.ds(0, halo), :] = buf[p, pl.ds(tm, halo), :]

            buf[p, pl.ds(halo, tm), :] = x_refs[p][...].astype(F32)

        def group(r, carry):
            base = pl.multiple_of(r * DW_RB, DW_RB)
            for cj in range(C // DW_LC):
                ls = slice(cj * DW_LC, (cj + 1) * DW_LC)
                cs = [_dw_taps(buf[p, pl.ds(base, DW_RB + halo), ls], w_ref, p, ls, K, lambda k: halo - (K - 1) + k)
                      + b_refs[p][:, ls] for p in range(nparts)]
                for o_ref, v in zip(orefs, post(cs)):
                    o_ref[pl.ds(base, DW_RB), ls] = v.astype(o_ref.dtype)
            return carry

        lax.fori_loop(0, tm // DW_RB, group, 0)

    return _pcall(body, name=name, grid=(nt,),
                  in_specs=[pl.BlockSpec(bs, im) for (_, bs, im) in part_ins] + [_full_spec(w)] + [_full_spec(b) for b in biases],
                  out_specs=[pl.BlockSpec(bs, im) for (_, _, bs, im) in outs],
                  out_shape=[_sds(s, d) for (s, d, _, _) in outs],
                  scratch_shapes=[pltpu.VMEM((nparts, halo + tm, C), F32)])(*[r[0] for r in part_ins], w, *biases)


def _dwconv_bwd(post, part_ins, halo_ins, w, biases, cot_ins, *, K, C, name, tm, nt, comm=None):
    nparts, halo, nc, RB = len(part_ins), DW_HALO, len(cot_ins), DW_RB
    T = nt * tm

    def body(*refs):
        x_refs, h_refs, w_ref = refs[:nparts], refs[nparts:2 * nparts], refs[2 * nparts]
        b_refs = refs[2 * nparts + 1:3 * nparts + 1]
        cot_refs = refs[3 * nparts + 1:3 * nparts + 1 + nc]
        rest = refs[3 * nparts + 1 + nc:]
        dx_refs, dw_ref, db_refs = rest[:nparts], rest[nparts], rest[nparts + 1:2 * nparts + 1]
        bufx, bufd, acc = rest[-3], rest[-2], rest[-1]
        s = pl.program_id(0)
        first_tile = s == nt - 1

        @pl.when(s == 0)
        def _():
            acc[...] = jnp.zeros_like(acc)
            for p in range(nparts):
                bufd[p, pl.ds(tm, halo), :] = jnp.zeros((halo, C), F32)

        for p in range(nparts):
            hrows = h_refs[p].shape[0]
            bufx[p, pl.ds(0, halo), :] = jnp.where(first_tile, 0.0, h_refs[p][...].astype(F32)[hrows - halo:hrows, :])
            bufx[p, pl.ds(halo, tm), :] = x_refs[p][...].astype(F32)
        fold = lambda v: v[0:8, :] + v[8:16, :]

        def conv_out_grads(r, carry):
            base = pl.multiple_of(r * RB, RB)
            for cj in range(C // DW_LC):
                ls = slice(cj * DW_LC, (cj + 1) * DW_LC)
                wins = [bufx[p, pl.ds(base, RB + halo), ls] for p in range(nparts)]
                cs = [_dw_taps(wins[p], w_ref, p, ls, K, lambda k: halo - (K - 1) + k) + b_refs[p][:, ls]
                      for p in range(nparts)]
                _, vjp = jax.vjp(lambda *c: post(list(c)), *cs)
                dcs = vjp(tuple(cr[pl.ds(base, RB), ls].astype(F32) for cr in cot_refs))
                for p in range(nparts):
                    bufd[p, pl.ds(base, RB), ls] = dcs[p]
                    for k in range(K):
                        o = halo - (K - 1) + k
                        acc[p, k, :, ls] += fold(dcs[p] * wins[p][o:o + RB, :])
                    acc[p, K, :, ls] += fold(dcs[p])
            return carry

        lax.fori_loop(0, tm // RB, conv_out_grads, 0)

        def input_grads(r, carry):
            base = pl.multiple_of(r * RB, RB)
            for cj in range(C // DW_LC):
                ls = slice(cj * DW_LC, (cj + 1) * DW_LC)
                for p in range(nparts):
                    dx = _dw_taps(bufd[p, pl.ds(base, RB + halo), ls], w_ref, p, ls, K, lambda k: K - 1 - k)
                    dx_refs[p][pl.ds(base, RB), ls] = dx.astype(dx_refs[p].dtype)
            return carry

        lax.fori_loop(0, tm // RB, input_grads, 0)
        for p in range(nparts):
            bufd[p, pl.ds(tm, halo), :] = bufd[p, pl.ds(0, halo), :]

        @pl.when(s == nt - 1)
        def _():
            dw_ref[...] = jnp.zeros_like(dw_ref)
            for p in range(nparts):
                for k in range(K):
                    dw_ref[p, k:k + 1, :] = jnp.sum(acc[p, k], axis=0, keepdims=True)
                db_refs[p][...] = jnp.sum(acc[p, K], axis=0, keepdims=True)

    rev = lambda im: (lambda s: im(nt - 1 - s))
    row = pl.BlockSpec((tm, C), lambda s: (nt - 1 - s, 0))
    res, got = _pcall_maybe_carrying(
        body, comm, name=name, grid=(nt,),
        in_specs=[pl.BlockSpec(bs, rev(im)) for (_, bs, im) in part_ins]
        + [pl.BlockSpec(bs, rev(im)) for (_, bs, im) in halo_ins]
        + [_full_spec(w)] + [_full_spec(b) for b in biases]
        + [pl.BlockSpec(bs, rev(im)) for (_, bs, im) in cot_ins],
        out_specs=[row] * nparts + [_full_spec(w)] + [_full_spec(b) for b in biases],
        out_shape=[_sds((T, C), BF16)] * nparts + [_sds(w.shape, F32)] + [_sds(b.shape, F32) for b in biases],
        scratch_shapes=[pltpu.VMEM((nparts, halo + tm, C), F32), pltpu.VMEM((nparts, tm + halo, C), F32),
                        pltpu.VMEM((nparts, K + 1, 8, C), F32)],
    )(*[r[0] for r in part_ins], *[r[0] for r in halo_ins], w, *biases, *[c[0] for c in cot_ins])
    out = (list(res[:nparts]), res[nparts], list(res[nparts + 1:]))
    return out if comm is None else (out, got)


def _halo_slab(a3, s, tm, halo):
    rows = max(halo, 16) if a3.dtype == BF16 else halo
    q = tm // rows
    return (a3, (None, rows, a3.shape[2]), lambda i: (s, jnp.maximum(i * q - 1, 0), 0))


CONF_HALO = _halo_of(CONV_K)
CONF_RB = 32


def _shifted_copies(buf, shifted, rows):
    for j in range(1, 8):
        shifted[j - 1, pl.ds(0, rows), :] = buf[pl.ds(j, rows), :]


def _shifted_rows(buf, shifted, s, base, nrows):
    j, q = s % 8, s // 8
    if j == 0:
        return buf[pl.ds(base + 8 * q, nrows), :]
    return shifted[j - 1, pl.ds(base + 8 * q, nrows), :]


def _conf_fwd(u3, w, cb, lg, lb, *, name):
    T = u3.shape[1]
    tm = min(TM_X, T)
    nt = T // tm
    K, halo, RB = CONV_K, CONF_HALO, min(CONF_RB, tm)

    def body(a_ref, g_ref, w_ref, cb_ref, lg_ref, lb_ref, v3_ref, c_ref, bufx, xs):
        i = pl.program_id(0)

        @pl.when(i == 0)
        def _():
            bufx[pl.ds(0, halo), :] = jnp.zeros((halo, D), F32)

        @pl.when(i > 0)
        def _():
            bufx[pl.ds(0, halo), :] = bufx[pl.ds(tm, halo), :]

        bufx[pl.ds(halo, tm), :] = _conf_pre(a_ref[...].astype(F32), g_ref[...].astype(F32))[0]
        _shifted_copies(bufx, xs, halo + tm - 8)

        def group(r, carry):
            base = pl.multiple_of(r * RB, RB)
            acc = None
            for k in range(K):
                term = _shifted_rows(bufx, xs, halo - (K - 1) + k, base, RB) * w_ref[k:k + 1, :]
                acc = term if acc is None else acc + term
            c_ref[pl.ds(base, RB), :] = acc
            return carry

        lax.fori_loop(0, tm // RB, group, 0)
        v3_ref[...] = _conf_post([c_ref[...]], cb_ref[...], lg_ref[...], lb_ref[...])[0].astype(v3_ref.dtype)

    slab = lambda s: pl.BlockSpec((None, tm, D), lambda i: (s, i, 0))
    row = pl.BlockSpec((tm, D), lambda i: (i, 0))
    pars = [w, cb, lg, lb]
    return _pcall(body, name=name, grid=(nt,), in_specs=[slab(0), slab(1)] + [_full_spec(p) for p in pars],
                  out_specs=[row, row], out_shape=[_sds((T, D), BF16), _sds((T, D), F32)],
                  scratch_shapes=[pltpu.VMEM((halo + tm, D), F32), pltpu.VMEM((7, halo + tm - 8, D), F32)],
                  )(u3, u3, *pars)


def _conf_bwd(u3, c, dv3, w, cb, lg, lb, *, name, comm=None):
    T = u3.shape[1]
    tm = min(TM_X, T)
    nt = T // tm
    K, halo, RB = CONV_K, CONF_HALO, min(CONF_RB, tm)

    def body(a_ref, g_ref, c_ref, dv3_ref, w_ref, cb_ref, lg_ref, lb_ref,
             da_ref, dg_ref, dw_ref, dcb_ref, dlg_ref, dlb_ref, xbuf, bufd, ds, dv0):
        s = pl.program_id(0)

        @pl.when(s == 0)
        def _():
            for r in (dw_ref, dcb_ref, dlg_ref, dlb_ref):
                r[...] = jnp.zeros_like(r)
            bufd[pl.ds(tm, halo), :] = jnp.zeros((halo, D), F32)

        xin, pre_vjp = jax.vjp(lambda p, q_: _conf_pre(p, q_)[0], a_ref[...].astype(F32), g_ref[...].astype(F32))
        xbuf[...] = xin

        _, post_vjp = jax.vjp(lambda cc, b_, g_, l_: _conf_post([cc], b_, g_, l_)[0],
                              c_ref[...], cb_ref[...], lg_ref[...], lb_ref[...])
        dc, dcb, dlg, dlb = post_vjp(dv3_ref[...])
        dcb_ref[...] += dcb
        dlg_ref[...] += dlg
        dlb_ref[...] += dlb
        bufd[pl.ds(0, tm), :] = dc
        _shifted_copies(bufd, ds, tm + halo - 8)

        def dx_group(r, carry):
            base = pl.multiple_of(r * RB, RB)
            acc = None
            for k in range(K):
                term = _shifted_rows(bufd, ds, K - 1 - k, base, RB) * w_ref[k:k + 1, :]
                acc = term if acc is None else acc + term
            dv0[pl.ds(base, RB), :] = acc
            return carry

        lax.fori_loop(0, tm // RB, dx_group, 0)

        for k in range(K):
            def dw_group(r, acc):
                base = pl.multiple_of(r * RB, RB)
                prod = xbuf[pl.ds(base, RB), :] * _shifted_rows(bufd, ds, K - 1 - k, base, RB)
                for v in range(RB // 8):
                    acc = acc + prod[v * 8:(v + 1) * 8, :]
                return acc

            acc = lax.fori_loop(0, tm // RB, dw_group, jnp.zeros((8, D), F32))
            dw_ref[k:k + 1, :] += jnp.sum(acc, axis=0, keepdims=True)

        bufd[pl.ds(tm, halo), :] = bufd[pl.ds(0, halo), :]
        da, dg = pre_vjp(dv0[...])
        da_ref[...] = da.astype(da_ref.dtype)
        dg_ref[...] = dg.astype(dg_ref.dtype)

    slab = lambda sl: pl.BlockSpec((None, tm, D), lambda s: (sl, nt - 1 - s, 0))
    row = pl.BlockSpec((tm, D), lambda s: (nt - 1 - s, 0))
    pars = [w, cb, lg, lb]
    res, got = _pcall_maybe_carrying(
        body, comm, name=name, grid=(nt,),
        in_specs=[slab(0), slab(1), row, row] + [_full_spec(p) for p in pars],
        out_specs=[row, row] + [_full_spec(p) for p in pars],
        out_shape=[_sds((T, D), BF16)] * 2 + [_sds(p.shape, F32) for p in pars],
        scratch_shapes=[pltpu.VMEM((tm, D), F32), pltpu.VMEM((tm + halo, D), F32),
                        pltpu.VMEM((7, tm + halo - 8, D), F32), pltpu.VMEM((tm, D), F32)],
    )(u3, u3, c, dv3, *pars)
    return res if comm is None else (res, got)


def _dg(a, b, ca, cb):
    return lax.dot_general(a.astype(BF16), b.astype(BF16), (((ca,), (cb,)), ((), ())), preferred_element_type=F32)


@jax.custom_vjp
def _dot_nn(a, b):
    return _dg(a, b, 1, 0)


_dot_nn.defvjp(lambda a, b: (_dg(a, b, 1, 0), (a, b)),
               lambda res, g: (_dg(g, res[1], 1, 1), _dg(res[0], g, 0, 0)))


@jax.custom_vjp
def _dot_nt(a, b):
    return _dg(a, b, 1, 1)


_dot_nt.defvjp(lambda a, b: (_dg(a, b, 1, 1), (a, b)),
               lambda res, g: (_dg(g, res[1], 1, 0), _dg(g, res[0], 0, 0)))


@jax.custom_vjp
def _dot_tn(a, b):
    return _dg(a, b, 0, 0)


_dot_tn.defvjp(lambda a, b: (_dg(a, b, 0, 0), (a, b)),
               lambda res, g: (_dg(res[1], g, 1, 1), _dg(res[0], g, 1, 0)))


def _split3(v):
    hi = v.astype(BF16)
    r = v - hi.astype(F32)
    mid = r.astype(BF16)
    return hi, mid, (r - mid.astype(F32)).astype(BF16)


def _x01(v, m, cv, cm, m_left=False):
    acc = None
    for piece in _split3(v):
        t = _dg(m, piece, cm, cv) if m_left else _dg(piece, m, cv, cm)
        acc = t if acc is None else acc + t
    return acc


@jax.custom_vjp
def _expand01(v, m):
    return _x01(v, m, 1, 0)


_expand01.defvjp(lambda v, m: (_x01(v, m, 1, 0), m),
                 lambda m, g: (_x01(g, m, 1, 1), jnp.zeros_like(m)))


@jax.custom_vjp
def _mix01(m, v):
    return _x01(v, m, 0, 1, m_left=True)


_mix01.defvjp(lambda m, v: (_x01(v, m, 0, 1, m_left=True), m),
              lambda m, g: (jnp.zeros_like(m), _x01(g, m, 0, 0, m_left=True)))


def _causal():
    return lax.broadcasted_iota(jnp.int32, (L, L), 0) >= lax.broadcasted_iota(jnp.int32, (L, L), 1)


def _ssd_chunk_prep(dtr, alog, dtb):
    dt = _softplus(dtr + dtb)
    a_cs = _mix01(_causal().astype(F32), dt * (-jnp.exp(alog)))
    return dt, a_cs, a_cs.T


def _ssd_group(xs, dt, a_cs, a_csT, Bg, Cg, zg, sp, dsk, nwg, *, g):
    causal = _causal()
    hi = lax.broadcasted_iota(jnp.int32, (HP, RP), 0)
    ci = lax.broadcasted_iota(jnp.int32, (HP, RP), 1)
    lo = (hi - g * R) * P
    E = ((ci >= lo) & (ci < lo + P)).astype(F32)

    acs_e = _expand01(a_cs, E)
    dt_e = _expand01(dt, E)
    alast_e = acs_e[L - 1:L, :]
    xdt = xs * dt_e
    cb = _dot_nt(Cg, Bg)
    y_off = _dot_nn(Cg, sp) * jnp.exp(acs_e)
    yd = []
    for r in range(R):
        h = g * R + r
        seg = a_cs[:, h:h + 1] - a_csT[h:h + 1, :]
        dec = jnp.exp(jnp.where(causal, seg, -1e30))
        yd.append(_dot_nn(cb * dec, xdt[:, r * P:(r + 1) * P]))
    y = jnp.concatenate(yd, axis=1) + y_off + xs * _expand01(jnp.broadcast_to(dsk, (8, HP)), E)[0:1, :]
    yg = y * _silu(zg)
    yn = yg * lax.rsqrt(jnp.mean(jnp.square(yg), axis=-1, keepdims=True) + RMS_EPS) * nwg
    sc = _dot_tn(Bg, xdt * jnp.exp(alast_e - acs_e))
    return yn, jnp.exp(alast_e) * sp + sc


def _group_cols(g):
    return g // 2, (g % 2) * RP


def _ssd_fwd(x0, x1, bc, dtr, u3, alog, dtb, dsk, nw, *, name, comm=None):
    T = x0.shape[0]
    nc = T // L

    def body(x0_ref, x1_ref, bc_ref, dtr_ref, z0_ref, z1_ref, alog_ref, dtb_ref, dsk_ref, nw_ref, yn_ref, sp_ref, S):
        @pl.when(pl.program_id(0) == 0)
        def _():
            S[...] = jnp.zeros_like(S)

        xr, zr = (x0_ref, x1_ref), (z0_ref, z1_ref)
        dt, a_cs, a_csT = _ssd_chunk_prep(dtr_ref[...], alog_ref[...], dtb_ref[...])
        for g in range(G):
            s, off = _group_cols(g)
            sp = S[g]
            sp_ref[0, g] = sp
            yn, s_next = _ssd_group(xr[s][:, off:off + RP], dt, a_cs, a_csT, bc_ref[:, g * N:(g + 1) * N],
                                    bc_ref[:, G * N + g * N:G * N + (g + 1) * N], zr[s][:, off:off + RP].astype(F32), sp,
                                    dsk_ref[...], nw_ref[:, g * RP:(g + 1) * RP], g=g)
            yn_ref[:, g * RP:(g + 1) * RP] = yn.astype(yn_ref.dtype)
            S[g] = s_next

    row = lambda C: pl.BlockSpec((L, C), lambda c: (c, 0))
    zspec = lambda s: pl.BlockSpec((None, L, D), lambda c: (s, c, 0))
    pars = [alog, dtb, dsk, nw]
    res, got = _pcall_maybe_carrying(
        body, comm, name=name, grid=(nc,),
        in_specs=[row(D), row(D), row(D), row(HP), zspec(2), zspec(3)] + [_full_spec(p) for p in pars],
        out_specs=[row(SD), pl.BlockSpec((1, G, N, RP), lambda c: (c, 0, 0, 0))],
        out_shape=[_sds((T, SD), BF16), _sds((nc, G, N, RP), F32)],
        scratch_shapes=[pltpu.VMEM((G, N, RP), F32)])(x0, x1, bc, dtr, u3, u3, *pars)
    return res if comm is None else (res, got)


def _ssd_bwd(x0, x1, bc, dtr, u3, sprev, dyn, alog, dtb, dsk, nw, *, name, comm=None):
    T = x0.shape[0]
    nc = T // L

    def body(x0_ref, x1_ref, bc_ref, dtr_ref, z0_ref, z1_ref, sp_ref, dyn_ref, alog_ref, dtb_ref, dsk_ref, nw_ref,
             dx0_ref, dx1_ref, dbc_ref, ddtr_ref, dz0_ref, dz1_ref, dalog_ref, ddtb_ref, ddsk_ref, dnw_ref, dS):
        @pl.when(pl.program_id(0) == 0)
        def _():
            dS[...] = jnp.zeros_like(dS)
            for r in (dalog_ref, ddtb_ref, ddsk_ref, dnw_ref):
                r[...] = jnp.zeros_like(r)

        xr, zr = (x0_ref, x1_ref), (z0_ref, z1_ref)
        dxr, dzr = (dx0_ref, dx1_ref), (dz0_ref, dz1_ref)
        (dt, a_cs, a_csT), prep_vjp = jax.vjp(_ssd_chunk_prep, dtr_ref[...], alog_ref[...], dtb_ref[...])
        d_dt, d_acs, d_acsT = jnp.zeros((L, HP), F32), jnp.zeros((L, HP), F32), jnp.zeros((HP, L), F32)
        for g in range(G):
            s, off = _group_cols(g)
            _, vjp = jax.vjp(functools.partial(_ssd_group, g=g), xr[s][:, off:off + RP], dt, a_cs, a_csT,
                             bc_ref[:, g * N:(g + 1) * N], bc_ref[:, G * N + g * N:G * N + (g + 1) * N],
                             zr[s][:, off:off + RP].astype(F32), sp_ref[0, g], dsk_ref[...], nw_ref[:, g * RP:(g + 1) * RP])
            dxs, ddt_g, dacs_g, dacsT_g, dB, dC, dz, dsp, dds, dnwg = vjp((dyn_ref[:, g * RP:(g + 1) * RP], dS[g]))
            dxr[s][:, off:off + RP] = dxs
            dzr[s][:, off:off + RP] = dz.astype(dz0_ref.dtype)
            dbc_ref[:, g * N:(g + 1) * N] = dB
            dbc_ref[:, G * N + g * N:G * N + (g + 1) * N] = dC
            dS[g] = dsp
            d_dt, d_acs, d_acsT = d_dt + ddt_g, d_acs + dacs_g, d_acsT + dacsT_g
            ddsk_ref[...] += dds
            dnw_ref[:, g * RP:(g + 1) * RP] += dnwg
        ddtr, dal, ddb = prep_vjp((d_dt, d_acs, d_acsT))
        dalog_ref[...] += dal
        ddtb_ref[...] += ddb
        ddtr_ref[...] = ddtr.astype(ddtr_ref.dtype)

    row = lambda C: pl.BlockSpec((L, C), lambda c: (nc - 1 - c, 0))
    zspec = lambda s: pl.BlockSpec((None, L, D), lambda c: (s, nc - 1 - c, 0))
    pars = [alog, dtb, dsk, nw]
    res, got = _pcall_maybe_carrying(
        body, comm, name=name, grid=(nc,),
        in_specs=[row(D), row(D), row(D), row(HP), zspec(2), zspec(3),
                  pl.BlockSpec((1, G, N, RP), lambda c: (nc - 1 - c, 0, 0, 0)), row(SD)] + [_full_spec(p) for p in pars],
        out_specs=[row(D), row(D), row(D), row(HP), row(D), row(D)] + [_full_spec(p) for p in pars],
        out_shape=[_sds((T, D), F32)] * 3 + [_sds((T, HP), BF16), _sds((T, D), BF16), _sds((T, D), BF16)]
        + [_sds(p.shape, F32) for p in pars],
        scratch_shapes=[pltpu.VMEM((G, N, RP), F32)])(x0, x1, bc, dtr, u3, u3, sprev, dyn, *pars)
    return res if comm is None else (res, got)


def _loss_head(y, target, *, name):
    T = y.shape[0]
    tm = min(TM, T)

    def body(y_ref, t_ref, loss_ref, dy_ref):
        e = y_ref[...] - t_ref[...]
        dy_ref[...] = e * (1.0 / D)

        @pl.when(pl.program_id(0) == 0)
        def _():
            loss_ref[...] = jnp.zeros_like(loss_ref)

        loss_ref[...] += 0.5 * jnp.sum(jnp.mean(jnp.square(e), axis=-1, keepdims=True), axis=0, keepdims=True)

    row = pl.BlockSpec((tm, D), lambda i: (i, 0))
    return _pcall(body, name=name, grid=(T // tm,), in_specs=[row, row],
                  out_specs=[pl.BlockSpec((1, 128), lambda i: (0, 0)), row],
                  out_shape=[_sds((1, 128), F32), _sds((T, D), F32)])(y, target)


_HBM = pl.BlockSpec(memory_space=pltpu.HBM)
_MESH = pl.DeviceIdType.MESH


def _exchange_comm(bufs, *, scatter):
    nb = len(bufs)

    def copies(in_refs, out_refs, sems, with_arrivals):
        send_sems, recv_sems, local_sems = sems
        x, y, c = lax.axis_index("x"), lax.axis_index("y"), lax.axis_index("c")
        me = 2 * x + y
        peers = [(1 - x, y), (x, 1 - y), (1 - x, 1 - y)]
        own, sends, arrivals = [], [], []
        for b in range(nb):
            src_own = in_refs[b].at[me] if scatter else in_refs[b]
            own.append(pltpu.make_async_copy(src_own, out_refs[b].at[me], local_sems.at[b]))
            for k, (px, py) in enumerate(peers):
                src = in_refs[b].at[2 * px + py] if scatter else in_refs[b]
                sends.append(pltpu.make_async_remote_copy(
                    src_ref=src, dst_ref=out_refs[b].at[me], send_sem=send_sems.at[b, k], recv_sem=recv_sems.at[b, k],
                    device_id=(px, py, c), device_id_type=_MESH))
                if with_arrivals:
                    slot = out_refs[b].at[2 * px + py]
                    arrivals.append(pltpu.make_async_remote_copy(
                        src_ref=slot, dst_ref=slot, send_sem=send_sems.at[b, k], recv_sem=recv_sems.at[b, k],
                        device_id=(px, py, c), device_id_type=_MESH))
        return own, sends, arrivals

    def start(in_refs, out_refs, sems):
        own, sends, _ = copies(in_refs, out_refs, sems, False)
        for cp in own + sends:
            cp.start()

    def wait(in_refs, out_refs, sems):
        own, sends, arrivals = copies(in_refs, out_refs, sems, True)
        for cp in arrivals:
            cp.wait_recv()
        for cp in sends:
            cp.wait_send()
        for cp in own:
            cp.wait()

    return dict(ins=list(bufs), outs=[_sds(b.shape if scatter else (NCHIP,) + b.shape, b.dtype) for b in bufs],
                sems=[pltpu.SemaphoreType.DMA((nb, 3)), pltpu.SemaphoreType.DMA((nb, 3)), pltpu.SemaphoreType.DMA((nb,))],
                start=start, wait=wait)


def _chip_exchange(bufs, *, scatter, name):
    comm = _exchange_comm(bufs, scatter=scatter)
    nb = len(bufs)

    def body(*refs):
        comm["start"](refs[:nb], refs[nb:2 * nb], refs[2 * nb:])
        comm["wait"](refs[:nb], refs[nb:2 * nb], refs[2 * nb:])

    return _ccall(body, name=name, in_specs=[_HBM] * nb, out_specs=[_HBM] * nb, out_shape=comm["outs"],
                  scratch_shapes=comm["sems"])(*bufs)


def _core_swap(bufs, *, name):
    nb = len(bufs)

    def body(*refs):
        in_refs, out_refs, send_sems, recv_sems = refs[:nb], refs[nb:2 * nb], refs[2 * nb], refs[2 * nb + 1]
        x, y, c = lax.axis_index("x"), lax.axis_index("y"), lax.axis_index("c")
        cps = [pltpu.make_async_remote_copy(src_ref=in_refs[b], dst_ref=out_refs[b], send_sem=send_sems.at[b],
                                            recv_sem=recv_sems.at[b], device_id=(x, y, 1 - c), device_id_type=_MESH)
               for b in range(nb)]
        for cp in cps:
            cp.start()
        for cp in cps:
            cp.wait()

    return _ccall(body, name=name, in_specs=[_HBM] * nb, out_specs=[_HBM] * nb,
                  out_shape=[_sds(b.shape, b.dtype) for b in bufs],
                  scratch_shapes=[pltpu.SemaphoreType.DMA((nb,)), pltpu.SemaphoreType.DMA((nb,))])(*bufs)


def _all_gather8(buf, *, name):
    def body(in_ref, out_ref, send_sems, recv_sems, local_sem):
        x, y, c = lax.axis_index("x"), lax.axis_index("y"), lax.axis_index("c")
        me = 4 * x + 2 * y + c
        own = pltpu.make_async_copy(in_ref, out_ref.at[me], local_sem)
        own.start()
        flips = [(fx, fy, fc) for fx in (0, 1) for fy in (0, 1) for fc in (0, 1)][1:]
        peers = [(x ^ fx, y ^ fy, c ^ fc) for fx, fy, fc in flips]
        sends = []
        for k, peer in enumerate(peers):
            cp = pltpu.make_async_remote_copy(src_ref=in_ref, dst_ref=out_ref.at[me], send_sem=send_sems.at[k],
                                              recv_sem=recv_sems.at[k], device_id=peer, device_id_type=_MESH)
            cp.start()
            sends.append(cp)
        for k, (px, py, pc) in enumerate(peers):
            slot = out_ref.at[4 * px + 2 * py + pc]
            pltpu.make_async_remote_copy(src_ref=slot, dst_ref=slot, send_sem=send_sems.at[k], recv_sem=recv_sems.at[k],
                                         device_id=(px, py, pc), device_id_type=_MESH).wait_recv()
        for cp in sends:
            cp.wait_send()
        own.wait()

    return _ccall(body, name=name, in_specs=[_HBM], out_specs=_HBM, out_shape=_sds((8,) + buf.shape, buf.dtype),
                  scratch_shapes=[pltpu.SemaphoreType.DMA((7,)), pltpu.SemaphoreType.DMA((7,)), pltpu.SemaphoreType.DMA])(buf)


def _row_tile(rows, cap):
    if rows <= cap:
        return rows
    return max(t for t in range(16, cap + 1, 16) if rows % t == 0)


def _sum_slots(stack, *, name, cap=256):
    S, Rr, C = stack.shape
    tr = _row_tile(Rr, cap)

    def body(s_ref, o_ref):
        acc = s_ref[0].astype(F32)
        for j in range(1, S):
            acc = acc + s_ref[j].astype(F32)
        o_ref[...] = acc

    return _pcall(body, name=name, grid=(Rr // tr,), in_specs=[pl.BlockSpec((S, tr, C), lambda i: (0, i, 0))],
                  out_specs=pl.BlockSpec((tr, C), lambda i: (i, 0)), out_shape=_sds((Rr, C), F32))(stack)


def _adamw(g_parts, w, m, v, *, name, cap=128):
    Rr, C = w.shape
    tr = _row_tile(Rr, cap)
    ng = len(g_parts)
    c1 = 1.0 / (1.0 - ADAM_B1 ** ADAM_STEP)
    c2 = 1.0 / (1.0 - ADAM_B2 ** ADAM_STEP)

    def body(*refs):
        g = refs[0][...]
        for r in refs[1:ng]:
            g = g + r[...]
        w_ref, m_ref, v_ref, g_out, d_out, m_out, v_out = refs[ng:]
        mn = ADAM_B1 * m_ref[...] + (1.0 - ADAM_B1) * g
        vn = ADAM_B2 * v_ref[...] + (1.0 - ADAM_B2) * jnp.square(g)
        g_out[...] = g
        m_out[...] = mn
        v_out[...] = vn
        d_out[...] = -ADAM_LR * ((mn * c1) / (jnp.sqrt(vn * c2) + ADAM_EPS) + ADAM_WD * w_ref[...])

    spec = pl.BlockSpec((tr, C), lambda i: (i, 0))
    return _pcall(body, name=name, grid=(Rr // tr,), in_specs=[spec] * (ng + 3), out_specs=[spec] * 4,
                  out_shape=[_sds((Rr, C), F32)] * 4)(*g_parts, w, m, v)


def _adamw_layers(mine, other, w3, m3, v3, *, name, cap=128):
    _, Rr, C = w3.shape
    tr = _row_tile(Rr, cap)
    nt = Rr // tr
    c1 = 1.0 / (1.0 - ADAM_B1 ** ADAM_STEP)
    c2 = 1.0 / (1.0 - ADAM_B2 ** ADAM_STEP)

    def body(m0, m1, o0, o1, w_ref, m_ref, v_ref, g_out, d_out, m_out, v_out):
        g = jnp.where(pl.program_id(0) == 0, m0[...] + o0[...], m1[...] + o1[...])
        mn = ADAM_B1 * m_ref[...] + (1.0 - ADAM_B1) * g
        vn = ADAM_B2 * v_ref[...] + (1.0 - ADAM_B2) * jnp.square(g)
        g_out[...] = g
        m_out[...] = mn
        v_out[...] = vn
        d_out[...] = -ADAM_LR * ((mn * c1) / (jnp.sqrt(vn * c2) + ADAM_EPS) + ADAM_WD * w_ref[...])

    g0 = pl.BlockSpec((tr, C), lambda l, i: (jnp.where(l == 0, i, nt - 1), 0))
    g1 = pl.BlockSpec((tr, C), lambda l, i: (jnp.where(l == 1, i, 0), 0))
    s3 = pl.BlockSpec((None, tr, C), lambda l, i: (l, i, 0))
    return _pcall(body, name=name, grid=(2, nt), in_specs=[g0, g1, g0, g1, s3, s3, s3], out_specs=[s3] * 4,
                  out_shape=[_sds(w3.shape, F32)] * 4)(mine[0], mine[1], other[0], other[1], w3, m3, v3)


def _pack(arrs, dtype, row_mult):
    flat = jnp.concatenate([a.reshape(-1).astype(dtype) for a in arrs])
    n = flat.shape[0]
    unit = row_mult * PACK_W
    total = unit * ((n + unit - 1) // unit)
    if total > n:
        flat = jnp.concatenate([flat, jnp.zeros((total - n,), dtype)])
    return flat.reshape(-1, PACK_W)


def _unpack(buf, shapes):
    flat = buf.reshape(-1)
    out, off = [], 0
    for s in shapes:
        n = math.prod(s)
        out.append(flat[off:off + n].reshape(s))
        off += n
    return out


def _conf_pre(a, g):
    return [a * jax.nn.sigmoid(g)]


def _conf_post(cs, cb, lg, lb):
    return (_silu(_layer_norm(cs[0] + cb, lg, lb)),)


def _xbc_post(cs):
    return tuple(_silu(c) for c in cs)


def _ffn_post(cs):
    return (_silu(cs[0]) * cs[1],)


def _mix_fn(ga, gb, ya, yb):
    return (jax.nn.sigmoid(ga.astype(F32)) * ya + jax.nn.sigmoid(gb.astype(F32)) * yb,)


def _res_ln_fn(h, r, g, b):
    return (_layer_norm(ALPHA * h + r, g, b),)


def _ln_fn(x, g, b):
    return (_layer_norm(x, g, b),)


def _carrying(carry, key, gr, call):
    if key not in carry:
        return call(None)
    comm, done = carry[key](gr)
    out, got = call(comm)
    done(got)
    return out


def _two_copies(fn):
    def wrapped(*args):
        return fn(*args) * 2
    return wrapped


def _copies_out(T, tm):
    return [_o2(T, D, F32, tm), _o2(T, D, BF16, tm)]


def _layer_fwd(h, hb, W, l, carry):
    T = h.shape[0]
    tm = min(TM, T)
    nt = T // tm
    tmf = min(TM_FFN, T)
    ntf = T // tmf
    nm = lambda s: f"l{l}_{s}"
    u3 = _carrying(carry, "u", None,
                   lambda comm: _mm_resident_slab_out(hb, W["w_p"], name=nm("u"), width=D, tm=TM, out_dtype=BF16, comm=comm))
    dtr = _mm_nn(hb, W["w_dt"], name=nm("dt"))
    v3, cconv = _conf_fwd(u3, W["conv_w"], W["conv_b"], W["conv_ln_g"], W["conv_ln_b"], name=nm("conf"))
    ya = _mm_nn(v3, W["w_co"], name=nm("ya"))
    tmx = min(TM_X, T)
    x0, x1, bc = _dwconv_fwd(_xbc_post, [_slab(u3, 6, tmx), _slab(u3, 7, tmx), _slab(u3, 8, tmx)], W["ssm_w"],
                             W["ssm_b"], [_o2(T, D, F32, tmx)] * 3, K=SSM_K, C=D, name=nm("xbc"), tm=tmx, nt=T // tmx)
    yn, sprev = _carrying(carry, "ssd", None, lambda comm: _ssd_fwd(
        x0, x1, bc, dtr, u3, W["a_log"], W["dt_bias"], W["d_skip"], W["norm_w"], name=nm("ssd"), comm=comm))
    yb = _mm_nn(yn, W["w_so"], name=nm("yb"), tk=min(SD, 1024))
    (m,) = _rows_fwd(_mix_fn, [_slab(u3, 4, tm), _slab(u3, 5, tm), _r2(ya, tm), _r2(yb, tm)], [],
                     [_o2(T, D, BF16, tm)], name=nm("mix"), nt=nt)
    mix = _mm_nn(m, W["w_o"], name=nm("wo"))
    h1, h1b = _rows_fwd(_two_copies(_res_ln_fn), [_r2(h, tm), _r2(mix, tm)], [W["ln1_g"], W["ln1_b"]],
                        _copies_out(T, tm), name=nm("ln1"), nt=nt)
    up3 = _mm_resident_slab_out(h1b, W["w_up"], name=nm("up"), width=FFN, tm=TM)
    (f,) = _dwconv_fwd(_ffn_post, [_slab(up3, 0, tmf), _slab(up3, 1, tmf)], W["ffn_w"], W["ffn_b"],
                       [_o2(T, FFN, BF16, tmf)], K=FFN_K, C=FFN, name=nm("ffnact"), tm=tmf, nt=ntf)
    ffn = _mm_nn(f, W["w_dn"], name=nm("dn"))
    h2, h2b = _rows_fwd(_two_copies(_res_ln_fn), [_r2(h1, tm), _r2(ffn, tm)], [W["ln2_g"], W["ln2_b"]],
                        _copies_out(T, tm), name=nm("ln2"), nt=nt)
    saved = dict(h=h, hb=hb, u3=u3, dtr=dtr, v3=v3, cconv=cconv, ya=ya, x0=x0, x1=x1, bc=bc, sprev=sprev, yn=yn, yb=yb, m=m,
                 mix=mix, h1=h1, h1b=h1b, up3=up3, f=f, ffn=ffn)
    return h2, h2b, saved


def _layer_bwd(dh2, W, sv, l, carry):
    T = dh2.shape[0]
    tm = min(TM, T)
    nt = T // tm
    tmf = min(TM_FFN, T)
    ntf = T // tmf
    nm = lambda s: f"l{l}_{s}"
    gr = {}
    (dres2, dffn), (gr["ln2_g"], gr["ln2_b"]) = _rows_bwd(
        _res_ln_fn, [_r2(sv["h1"], tm), _r2(sv["ffn"], tm)], [W["ln2_g"], W["ln2_b"]], [_r2(dh2, tm)],
        [_o2(T, D, F32, tm), _o2(T, D, BF16, tm)], name=nm("ln2_b"), nt=nt)
    df = _mm_nt(dffn, W["w_dn"], name=nm("dn_dx"))
    gr["w_dn"] = _mm_tn(sv["f"], dffn, name=nm("dn_dw"), tmo=FFN // 2)
    up3 = sv["up3"]
    (dgate, dval), gr["ffn_w"], gr["ffn_b"] = _carrying(carry, "ffnact_b", gr, lambda comm: _dwconv_bwd(
        _ffn_post, [_slab(up3, 0, tmf), _slab(up3, 1, tmf)],
        [_halo_slab(up3, 0, tmf, DW_HALO), _halo_slab(up3, 1, tmf, DW_HALO)], W["ffn_w"], W["ffn_b"], [_r2(df, tmf)],
        K=FFN_K, C=FFN, name=nm("ffnact_b"), tm=tmf, nt=ntf, comm=comm))
    dh1 = _mm_cols_nt([dgate, dval], W["w_up"], name=nm("up_dx"), add=dres2)
    gr["w_up"] = _mm_tn_cols(sv["h1b"], [dgate, dval], name=nm("up_dw"), tmo=min(512, D))
    (dres1, dmix), (gr["ln1_g"], gr["ln1_b"]) = _rows_bwd(
        _res_ln_fn, [_r2(sv["h"], tm), _r2(sv["mix"], tm)], [W["ln1_g"], W["ln1_b"]], [_r2(dh1, tm)],
        [_o2(T, D, F32, tm), _o2(T, D, BF16, tm)], name=nm("ln1_b"), nt=nt)
    dm = _mm_nt(dmix, W["w_o"], name=nm("wo_dx"))
    gr["w_o"] = _mm_tn(sv["m"], dmix, name=nm("wo_dw"), tmo=min(512, D))
    u3 = sv["u3"]
    (dga, dgb, dya, dyb), _ = _rows_bwd(
        _mix_fn, [_slab(u3, 4, tm), _slab(u3, 5, tm), _r2(sv["ya"], tm), _r2(sv["yb"], tm)], [], [_r2(dm, tm)],
        [_o2(T, D, BF16, tm)] * 4, name=nm("mix_b"), nt=nt)
    dv3 = _mm_nt(dya, W["w_co"], name=nm("ya_dx"))
    gr["w_co"] = _mm_tn(sv["v3"], dya, name=nm("ya_dw"), tmo=min(512, D))
    da, dg, gr["conv_w"], gr["conv_b"], gr["conv_ln_g"], gr["conv_ln_b"] = _carrying(carry, "conf_b", gr, lambda comm: _conf_bwd(
        u3, sv["cconv"], dv3, W["conv_w"], W["conv_b"], W["conv_ln_g"], W["conv_ln_b"], name=nm("conf_b"), comm=comm))
    dyn = _mm_nt(dyb, W["w_so"], name=nm("yb_dx"))
    gr["w_so"] = _mm_tn(sv["yn"], dyb, name=nm("yb_dw"), tmo=min(512, SD))
    (dx0, dx1, dbc, ddtr, dz0, dz1, gr["a_log"], gr["dt_bias"], gr["d_skip"], gr["norm_w"]) = _carrying(
        carry, "ssd_b", gr, lambda comm: _ssd_bwd(
            sv["x0"], sv["x1"], sv["bc"], sv["dtr"], u3, sv["sprev"], dyn, W["a_log"], W["dt_bias"], W["d_skip"],
            W["norm_w"], name=nm("ssd_b"), comm=comm))
    tmx = min(TM_X, T)
    (du6, du7, du8), gr["ssm_w"], gr["ssm_b"] = _dwconv_bwd(
        _xbc_post, [_slab(u3, 6, tmx), _slab(u3, 7, tmx), _slab(u3, 8, tmx)],
        [_halo_slab(u3, 6, tmx, DW_HALO), _halo_slab(u3, 7, tmx, DW_HALO), _halo_slab(u3, 8, tmx, DW_HALO)],
        W["ssm_w"], W["ssm_b"], [_r2(dx0, tmx), _r2(dx1, tmx), _r2(dbc, tmx)], K=SSM_K, C=D, name=nm("xbc_b"),
        tm=tmx, nt=T // tmx)
    du = [da, dg, dz0, dz1, dga, dgb, du6, du7, du8]
    gr["w_dt"] = _mm_tn(sv["hb"], ddtr, name=nm("dt_dw"), tmo=min(512, D))
    gr["w_p"] = _carrying(carry, "u_dw", gr, lambda comm: _mm_tn_cols(sv["hb"], du, name=nm("u_dw"),
                                                                      tmo=min(512, D), comm=comm))
    dh_a = _mm_nt(ddtr, W["w_dt"], name=nm("dt_dx"), add=dres1)
    dh = _carrying(carry, "u_dx", gr, lambda comm: _mm_cols_nt(du, W["w_p"], name=nm("u_dx"), add=dh_a, comm=comm))
    return dh, gr


_U_SPLIT = (2 * D + SD, 2 * D + SD + XBC, 2 * D + SD + XBC + H)


def _pad_rows(a, rows):
    return jnp.concatenate([a, jnp.zeros((rows - a.shape[0],) + a.shape[1:], a.dtype)], axis=0)


def _pad_lanes(a, lanes):
    return jnp.concatenate([a, jnp.zeros(a.shape[:-1] + (lanes - a.shape[-1],), a.dtype)], axis=-1)


def _w_in_layout(w_in):
    e0, e1, e2 = _U_SPLIT
    return dict(w_p=jnp.concatenate([w_in[:, :e0], w_in[:, e2:], w_in[:, e0:e1]], axis=1),
                w_dt=_pad_lanes(w_in[:, e1:e2], HP))


_MM_KEY = dict(w_conv_out="w_co", w_ssm_out="w_so", w_o="w_o", w_ffn_up="w_up", w_ffn_down="w_dn")


def _small_layer_weights(full, l):
    row = lambda a: a.reshape(1, -1)
    ssm_w = full["ssm_conv_w"][l]
    ffn_w = full["ffn_dw_w"][l]
    ssm_b = full["ssm_conv_b"][l]
    ffn_b = full["ffn_dw_b"][l]
    W = dict(
        conv_w=_pad_rows(full["conv_dw_w"][l], 32),
        conv_b=row(full["conv_dw_b"][l]), conv_ln_g=row(full["conv_ln_g"][l]), conv_ln_b=row(full["conv_ln_b"][l]),
        ssm_w=jnp.stack([_pad_rows(ssm_w[:, p * D:(p + 1) * D], 8) for p in range(3)]),
        ssm_b=[row(ssm_b[p * D:(p + 1) * D]) for p in range(3)],
        a_log=_pad_lanes(row(full["ssm_a_log"][l]), HP), dt_bias=_pad_lanes(row(full["ssm_dt_bias"][l]), HP),
        d_skip=_pad_lanes(row(full["ssm_d"][l]), HP), norm_w=row(full["ssm_norm_w"][l]),
        ln1_g=row(full["ln1_g"][l]), ln1_b=row(full["ln1_b"][l]),
        ffn_w=jnp.stack([_pad_rows(ffn_w[:, p * FFN:(p + 1) * FFN], 8) for p in range(2)]),
        ffn_b=[row(ffn_b[p * FFN:(p + 1) * FFN]) for p in range(2)],
        ln2_g=row(full["ln2_g"][l]), ln2_b=row(full["ln2_b"][l]),
    )
    return W


def _w_in_grad(gr):
    e0 = _U_SPLIT[0]
    wp = gr["w_p"]
    return jnp.concatenate([wp[:, :e0], wp[:, e0 + 2 * D:e0 + 2 * D + XBC], gr["w_dt"][:, :H], wp[:, e0:e0 + 2 * D]], axis=1)


def _layer_grads_to_reference_layout(gr):
    return dict(
        w_in=_w_in_grad(gr), conv_dw_w=gr["conv_w"][:CONV_K], conv_dw_b=gr["conv_b"][0], conv_ln_g=gr["conv_ln_g"][0],
        conv_ln_b=gr["conv_ln_b"][0], w_conv_out=gr["w_co"],
        ssm_conv_w=jnp.concatenate([gr["ssm_w"][p, :SSM_K] for p in range(3)], axis=1),
        ssm_conv_b=jnp.concatenate([b[0] for b in gr["ssm_b"]]),
        ssm_dt_bias=gr["dt_bias"][0, :H], ssm_a_log=gr["a_log"][0, :H], ssm_d=gr["d_skip"][0, :H],
        ssm_norm_w=gr["norm_w"][0], w_ssm_out=gr["w_so"], w_o=gr["w_o"], ln1_g=gr["ln1_g"][0], ln1_b=gr["ln1_b"][0],
        w_ffn_up=gr["w_up"], ffn_dw_w=jnp.concatenate([gr["ffn_w"][p, :FFN_K] for p in range(2)], axis=1),
        ffn_dw_b=jnp.concatenate([b[0] for b in gr["ffn_b"]]), w_ffn_down=gr["w_dn"], ln2_g=gr["ln2_g"][0],
        ln2_b=gr["ln2_b"][0],
    )


_BIG = dict(w_in=2, w_conv_out=1, w_ssm_out=1, w_o=1, w_ffn_up=2, w_ffn_down=1)
_SMALL_SHARDED = dict(conv_dw_w=2, ssm_conv_w=2, ffn_dw_w=2)
_REPLICATED = ("ln_in_g", "ln_in_b", "conv_dw_b", "conv_ln_g", "conv_ln_b", "ssm_conv_b", "ssm_dt_bias", "ssm_a_log",
               "ssm_d", "ssm_norm_w", "ln1_g", "ln1_b", "ffn_dw_b", "ln2_g", "ln2_b")
_WEIGHTS = ("ln_in_g", "ln_in_b", "w_in", "conv_dw_w", "conv_dw_b", "conv_ln_g", "conv_ln_b", "w_conv_out", "ssm_conv_w",
            "ssm_conv_b", "ssm_dt_bias", "ssm_a_log", "ssm_d", "ssm_norm_w", "w_ssm_out", "w_o", "ln1_g", "ln1_b",
            "w_ffn_up", "ffn_dw_w", "ffn_dw_b", "w_ffn_down", "ln2_g", "ln2_b")


def _split_chips(a, axis):
    rows, cols = a.shape
    if axis == 0:
        return a.reshape(NCHIP, rows // NCHIP, cols)
    return a.reshape(rows, NCHIP, cols // NCHIP).transpose(1, 0, 2)


def kernel(x, ln_in_g, ln_in_b, w_in, conv_dw_w, conv_dw_b, conv_ln_g, conv_ln_b, w_conv_out, ssm_conv_w, ssm_conv_b, ssm_dt_bias, ssm_a_log, ssm_d, ssm_norm_w, w_ssm_out, w_o, ln1_g, ln1_b, w_ffn_up, ffn_dw_w, ffn_dw_b, w_ffn_down, ln2_g, ln2_b, loss_target, m_ln_in_g, m_ln_in_b, m_w_in, m_conv_dw_w, m_conv_dw_b, m_conv_ln_g, m_conv_ln_b, m_w_conv_out, m_ssm_conv_w, m_ssm_conv_b, m_ssm_dt_bias, m_ssm_a_log, m_ssm_d, m_ssm_norm_w, m_w_ssm_out, m_w_o, m_ln1_g, m_ln1_b, m_w_ffn_up, m_ffn_dw_w, m_ffn_dw_b, m_w_ffn_down, m_ln2_g, m_ln2_b, v_ln_in_g, v_ln_in_b, v_w_in, v_conv_dw_w, v_conv_dw_b, v_conv_ln_g, v_conv_ln_b, v_w_conv_out, v_ssm_conv_w, v_ssm_conv_b, v_ssm_dt_bias, v_ssm_a_log, v_ssm_d, v_ssm_norm_w, v_w_ssm_out, v_w_o, v_ln1_g, v_ln1_b, v_w_ffn_up, v_ffn_dw_w, v_ffn_dw_b, v_w_ffn_down, v_ln2_g, v_ln2_b):
    args = locals()
    w = {n: args[n] for n in _WEIGHTS}
    mom = {n: args["m_" + n] for n in _WEIGHTS}
    vel = {n: args["v_" + n] for n in _WEIGHTS}
    T = x.shape[1]
    tm = min(TM, T)
    nt = T // tm
    chip = 2 * lax.axis_index("x") + lax.axis_index("y")

    assert DEPTH == 2
    big_names, small_names = list(_BIG), list(_SMALL_SHARDED)
    rest_big = [n for n in big_names if n != "w_in"]
    bf = {n: w[n].astype(BF16) for n in big_names}
    join = lambda got, axis: jnp.concatenate([got[j] for j in range(NCHIP)], axis=axis)
    first = _chip_exchange([bf["w_in"][0]] + [w[n] for n in small_names], scatter=False, name="gather_first")
    full = {n: join(gk, _SMALL_SHARDED[n]) for n, gk in zip(small_names, first[1:])}
    for n in _REPLICATED:
        full[n] = w[n]
    Ws = [_small_layer_weights(full, l) for l in range(DEPTH)]
    Ws[0].update(_w_in_layout(join(first[0], 1)))

    def rest_arrived(l):
        def done(got):
            for n, gk in zip(rest_big, got):
                Ws[l][_MM_KEY[n]] = join(gk, _BIG[n] - 1)
        return done

    carry_fwd = [
        {"u": lambda gr: (_exchange_comm([bf[n][0] for n in rest_big], scatter=False), rest_arrived(0)),
         "ssd": lambda gr: (_exchange_comm([bf["w_in"][1]], scatter=False),
                            lambda got: Ws[1].update(_w_in_layout(join(got[0], 1))))},
        {"u": lambda gr: (_exchange_comm([bf[n][1] for n in rest_big], scatter=False), rest_arrived(1))},
    ]

    x2 = x.reshape(T, D)
    g_in, b_in = ln_in_g.reshape(1, D), ln_in_b.reshape(1, D)
    h, hb = _rows_fwd(_two_copies(_ln_fn), [_r2(x2, tm)], [g_in, b_in], _copies_out(T, tm), name="ln_in", nt=nt)
    saved = []
    for l in range(DEPTH):
        h, hb, sv = _layer_fwd(h, hb, Ws[l], l, carry_fwd[l])
        saved.append(sv)
    loss_row, dh = _loss_head(h, loss_target.reshape(T, D), name="loss")

    arrived = {}

    def exchange(names, l, grads):
        def make(gr):
            src = grads(gr)
            def done(got):
                for n, gk in zip(names, got):
                    arrived[(n, l)] = gk
            return _exchange_comm([_split_chips(src[n], _BIG[n] - 1) for n in names], scatter=True), done
        return make

    layer_grads = [None] * DEPTH
    dh, gr = _layer_bwd(dh, Ws[1], saved[1], 1, {})
    layer_grads[1] = _layer_grads_to_reference_layout(gr)
    g1 = lambda gr: layer_grads[1]
    g0 = lambda gr: {n: gr[_MM_KEY[n]] for n in rest_big}
    dh, gr = _layer_bwd(dh, Ws[0], saved[0], 0, {
        "ffnact_b": exchange(["w_in"], 1, g1),
        "conf_b": exchange(["w_ffn_up", "w_ffn_down"], 1, g1),
        "ssd_b": exchange(["w_conv_out", "w_ssm_out", "w_o"], 1, g1),
        "u_dw": exchange(rest_big, 0, g0),
        "u_dx": exchange(["w_in"], 0, lambda gr: {"w_in": _w_in_grad(gr)})})
    layer_grads[0] = _layer_grads_to_reference_layout(gr)
    (grad_x2,), (d_g_in, d_b_in) = _rows_bwd(_ln_fn, [_r2(x2, tm)], [g_in, b_in], [_r2(dh, tm)], [_o2(T, D, F32, tm)],
                                             name="ln_in_b", nt=nt)
    local = {n: jnp.stack([layer_grads[l][n] for l in range(DEPTH)]) for n in _WEIGHTS[2:] if n not in _BIG}
    local["ln_in_g"], local["ln_in_b"] = d_g_in[0], d_b_in[0]
    res = [{}, {}, {}, {}]

    keys = [(n, l) for n in big_names for l in range(DEPTH)]
    mine = [_sum_slots(arrived[k], name=f"sum_chips_{k[0]}_{k[1]}") for k in keys]
    other = _core_swap(mine, name="swap_cores")
    for i, n in enumerate(big_names):
        outs = _adamw_layers(mine[2 * i:2 * i + 2], other[2 * i:2 * i + 2], w[n], mom[n], vel[n], name="adamw_" + n)
        for q in range(4):
            res[q][n] = outs[q]

    rest_names = list(_REPLICATED) + small_names
    part = _pack([loss_row] + [local[n] for n in rest_names], F32, 8)
    parts = _all_gather8(part, name="gather_small")
    total = _sum_slots(parts, name="sum_devices")
    tot = _unpack(total, [loss_row.shape] + [local[n].shape for n in rest_names])
    loss = tot[0][0, 0]
    g_rest = {}
    for n, t in zip(rest_names, tot[1:]):
        if n in _SMALL_SHARDED:
            ax = _SMALL_SHARDED[n]
            t = lax.dynamic_slice_in_dim(t, chip * w[n].shape[ax], w[n].shape[ax], axis=ax)
        g_rest[n] = t
    pk = lambda d: _pack([d[n] for n in rest_names], F32, 8)
    rest_out = _adamw([pk(g_rest)], pk(w), pk(mom), pk(vel), name="adamw_rest")
    rest_out = [_unpack(o, [w[n].shape for n in rest_names]) for o in rest_out]

    for q in range(4):
        for k, n in enumerate(rest_names):
            res[q][n] = rest_out[q][k]
    grad_x = grad_x2.reshape(x.shape)
    return (loss, grad_x, *[res[0][n] for n in _WEIGHTS], *[res[1][n] for n in _WEIGHTS],
            *[res[2][n] for n in _WEIGHTS], *[res[3][n] for n in _WEIGHTS])
```

```python
import functools
import math

import jax
import jax.numpy as jnp
from jax import lax
from jax.experimental import pallas as pl
from jax.experimental.pallas import tpu as pltpu

F32 = jnp.float32
BF16 = jnp.bfloat16

D = 1024
DEPTH = 2
CONV_K = 31
SD = 2 * D
P = 64
H = SD // P
G = 4
R = H // G
N = 128
RP = R * P
SSM_K = 4
L = 128
XBC = SD + 2 * G * N
FFN = 2816
FFN_K = 3
IN_DIM = 2 * D + SD + XBC + H + 2 * D
ALPHA = (2 * DEPTH) ** 0.25
LN_EPS = 1e-5
RMS_EPS = 1e-5
ADAM_LR, ADAM_B1, ADAM_B2, ADAM_EPS, ADAM_WD, ADAM_STEP = 0.001, 0.9, 0.999, 1e-08, 0.01, 10

HP = 128
NCHIP = 4
PACK_W = 1024
VMEM_LIMIT = 56 * 1024 * 1024
TM = 512
TM_X = 256
TM_FFN = 256
TK = 1024

assert D == 2 * RP and 2 * G * N == D and XBC == 3 * D and H <= HP


def _pcall(body, *, name, grid=(), in_specs, out_specs, out_shape, scratch_shapes=()):
    params = pltpu.CompilerParams(vmem_limit_bytes=VMEM_LIMIT, dimension_semantics=("arbitrary",) * len(grid))
    return pl.pallas_call(body, name=name, grid=grid, in_specs=in_specs, out_specs=out_specs, out_shape=out_shape,
                          scratch_shapes=list(scratch_shapes), compiler_params=params)


def _pcall_carrying(body, comm, *, name, grid, in_specs, out_specs, out_shape, scratch_shapes=()):
    in_specs, out_specs, out_shape = list(in_specs), list(out_specs), list(out_shape)
    scratch_shapes = list(scratch_shapes)
    n_in, n_out, n_scr = len(in_specs), len(out_specs), len(scratch_shapes)
    nci, nco = len(comm["ins"]), len(comm["outs"])

    def wrapped(*refs):
        ins, cin = refs[:n_in], refs[n_in:n_in + nci]
        outs = refs[n_in + nci:n_in + nci + n_out]
        cout = refs[n_in + nci + n_out:n_in + nci + n_out + nco]
        scr = refs[n_in + nci + n_out + nco:n_in + nci + n_out + nco + n_scr]
        csem = refs[n_in + nci + n_out + nco + n_scr:]
        ids = [pl.program_id(ax) for ax in range(len(grid))]
        first = functools.reduce(jnp.logical_and, [i == 0 for i in ids])
        last = functools.reduce(jnp.logical_and, [i == g - 1 for i, g in zip(ids, grid)])

        @pl.when(first)
        def _():
            comm["start"](cin, cout, csem)

        body(*ins, *outs, *scr)

        @pl.when(last)
        def _():
            comm["wait"](cin, cout, csem)

    call = _pcall(wrapped, name=name, grid=grid, in_specs=in_specs + [_HBM] * nci, out_specs=out_specs + [_HBM] * nco,
                  out_shape=out_shape + list(comm["outs"]), scratch_shapes=scratch_shapes + list(comm["sems"]))

    def run(*operands):
        res = call(*operands, *comm["ins"])
        return list(res[:n_out]), list(res[n_out:])

    return run


def _pcall_maybe_carrying(body, comm, **kw):
    if comm is not None:
        return _pcall_carrying(body, comm, **kw)
    call = _pcall(body, **kw)
    return lambda *operands: (list(call(*operands)), None)


def _ccall(body, *, name, in_specs, out_specs, out_shape, scratch_shapes):
    return pl.pallas_call(body, name=name, in_specs=in_specs, out_specs=out_specs, out_shape=out_shape,
                          scratch_shapes=list(scratch_shapes))


def _full_spec(a):
    nd = a.ndim
    return pl.BlockSpec(a.shape, lambda *_: (0,) * nd)


def _sds(shape, dtype):
    return jax.ShapeDtypeStruct(tuple(shape), dtype)


def _mm(a, b, *, name, grid, a_spec, b_spec, o_spec, out_shape, acc_shape, trans_a=False, trans_b=False, add=None,
        add_spec=None, comm=None):
    nk = grid[2]
    dn = (((0 if trans_a else 1,), (1 if trans_b else 0,)), ((), ()))
    has_add = add is not None

    def body(*refs):
        a_ref, b_ref = refs[0], refs[1]
        add_ref = refs[2] if has_add else None
        o_ref = refs[3] if has_add else refs[2]
        part = lax.dot_general(a_ref[...].astype(BF16), b_ref[...].astype(BF16), dn, preferred_element_type=F32)

        def finish(res):
            if has_add:
                res = res + add_ref[...]
            o_ref[...] = res.astype(o_ref.dtype)

        if nk == 1:
            finish(part)
        else:
            acc = refs[-1]
            k = pl.program_id(2)

            @pl.when(k == 0)
            def _():
                acc[...] = part

            @pl.when(k > 0)
            def _():
                acc[...] += part

            @pl.when(k == nk - 1)
            def _():
                finish(acc[...])

    ins = [a, b] + ([add] if has_add else [])
    specs = [a_spec, b_spec] + ([add_spec] if has_add else [])
    scratch = [pltpu.VMEM(acc_shape, F32)] if nk > 1 else []
    if comm is not None:
        (out,), got = _pcall_carrying(body, comm, name=name, grid=grid, in_specs=specs, out_specs=[o_spec],
                                      out_shape=[out_shape], scratch_shapes=scratch)(*ins)
        return out, got
    return _pcall(body, name=name, grid=grid, in_specs=specs, out_specs=o_spec, out_shape=out_shape,
                  scratch_shapes=scratch)(*ins)


def _mm_nn(a, b, *, name, out_dtype=F32, tn=None, tk=None, add=None):
    M, K = a.shape
    Nn = b.shape[1]
    tm = min(TM, M)
    tn = Nn if tn is None else tn
    tk = K if tk is None else tk
    grid = (M // tm, Nn // tn, K // tk)
    return _mm(a, b, name=name, grid=grid,
               a_spec=pl.BlockSpec((tm, tk), lambda i, j, k: (i, k)),
               b_spec=pl.BlockSpec((tk, tn), lambda i, j, k: (k, j)),
               o_spec=pl.BlockSpec((tm, tn), lambda i, j, k: (i, j)),
               out_shape=_sds((M, Nn), out_dtype), acc_shape=(tm, tn), add=add,
               add_spec=pl.BlockSpec((tm, tn), lambda i, j, k: (i, j)))


def _mm_nt(a, b, *, name, add=None):
    M, K = a.shape
    Nn = b.shape[0]
    tm = min(TM, M)
    return _mm(a, b, name=name, grid=(M // tm, 1, 1), trans_b=True,
               a_spec=pl.BlockSpec((tm, K), lambda i, j, k: (i, 0)),
               b_spec=pl.BlockSpec((Nn, K), lambda i, j, k: (0, 0)),
               o_spec=pl.BlockSpec((tm, Nn), lambda i, j, k: (i, 0)),
               out_shape=_sds((M, Nn), F32), acc_shape=(tm, Nn), add=add,
               add_spec=pl.BlockSpec((tm, Nn), lambda i, j, k: (i, 0)))


def _mm_resident_slab_out(a, w, *, name, width, tm, out_dtype=F32, comm=None):
    M, K = a.shape
    S = w.shape[1] // width
    tm = min(tm, M)

    def body(a_ref, w_hbm, o_ref, w_vmem, sem):
        @pl.when(pl.program_id(0) == 0)
        def _():
            cp = pltpu.make_async_copy(w_hbm, w_vmem, sem)
            cp.start()
            cp.wait()

        av = a_ref[...].astype(BF16)
        for s in range(S):
            o_ref[s] = jnp.dot(av, w_vmem[:, s * width:(s + 1) * width], preferred_element_type=F32).astype(o_ref.dtype)

    kw = dict(name=name, grid=(M // tm,),
              in_specs=[pl.BlockSpec((tm, K), lambda i: (i, 0)), pl.BlockSpec(memory_space=pl.ANY)],
              scratch_shapes=[pltpu.VMEM(w.shape, w.dtype), pltpu.SemaphoreType.DMA])
    o_spec = pl.BlockSpec((S, tm, width), lambda i: (0, i, 0))
    if comm is not None:
        (out,), got = _pcall_carrying(body, comm, out_specs=[o_spec], out_shape=[_sds((S, M, width), out_dtype)], **kw)(a, w)
        return out, got
    return _pcall(body, out_specs=o_spec, out_shape=_sds((S, M, width), out_dtype), **kw)(a, w)


def _mm_cols_nt(a_list, w, *, name, add, comm=None):
    S = len(a_list)
    M, width = a_list[0].shape
    Nn = w.shape[0]
    tm = min(TM, M)

    def body(*refs):
        a_refs, w_hbm, add_ref, o_ref, w_vmem, sem = refs[:S], refs[S], refs[S + 1], refs[S + 2], refs[S + 3], refs[S + 4]

        @pl.when(pl.program_id(0) == 0)
        def _():
            cp = pltpu.make_async_copy(w_hbm, w_vmem, sem)
            cp.start()
            cp.wait()

        acc = add_ref[...]
        for s in range(S):
            acc = acc + lax.dot_general(a_refs[s][...].astype(BF16), w_vmem[:, s * width:(s + 1) * width],
                                        (((1,), (1,)), ((), ())), preferred_element_type=F32)
        o_ref[...] = acc

    row = lambda C: pl.BlockSpec((tm, C), lambda i: (i, 0))
    kw = dict(name=name, grid=(M // tm,), in_specs=[row(width)] * S + [pl.BlockSpec(memory_space=pl.ANY), row(Nn)],
              scratch_shapes=[pltpu.VMEM(w.shape, w.dtype), pltpu.SemaphoreType.DMA])
    if comm is not None:
        (out,), got = _pcall_carrying(body, comm, out_specs=[row(Nn)], out_shape=[_sds((M, Nn), F32)], **kw)(*a_list, w, add)
        return out, got
    return _pcall(body, out_specs=row(Nn), out_shape=_sds((M, Nn), F32), **kw)(*a_list, w, add)


def _mm_tn_cols(a, b_list, *, name, tmo, comm=None):
    T, M = a.shape
    S = len(b_list)
    width = b_list[0].shape[1]
    tk = min(TK, T)
    nk = T // tk

    def body(*refs):
        a_ref, b_refs, o_ref, acc = refs[0], refs[1:S + 1], refs[S + 1], refs[S + 2]
        j, k = pl.program_id(1), pl.program_id(2)
        for s in range(S):
            @pl.when(j == s)
            def _():
                part = lax.dot_general(a_ref[...], b_refs[s][...], (((0,), (0,)), ((), ())), preferred_element_type=F32)

                @pl.when(k == 0)
                def _():
                    acc[...] = part

                @pl.when(k > 0)
                def _():
                    acc[...] += part

        @pl.when(k == nk - 1)
        def _():
            o_ref[...] = acc[...].astype(o_ref.dtype)

    def b_spec(s):
        return pl.BlockSpec((tk, width), lambda i, j, k: (jnp.where(j == s, k, jnp.where(j < s, 0, nk - 1)), 0))

    kw = dict(name=name, grid=(M // tmo, S, nk),
              in_specs=[pl.BlockSpec((tk, tmo), lambda i, j, k: (k, i))] + [b_spec(s) for s in range(S)],
              scratch_shapes=[pltpu.VMEM((tmo, width), F32)])
    o_spec = pl.BlockSpec((tmo, width), lambda i, j, k: (i, j))
    if comm is not None:
        (out,), got = _pcall_carrying(body, comm, out_specs=[o_spec], out_shape=[_sds((M, S * width), BF16)], **kw)(a, *b_list)
        return out, got
    return _pcall(body, out_specs=o_spec, out_shape=_sds((M, S * width), BF16), **kw)(a, *b_list)


def _mm_tn(a, b, *, name, tmo, tn=None, comm=None):
    T, M = a.shape
    Nn = b.shape[1]
    tn = Nn if tn is None else tn
    tk = min(TK, T)
    return _mm(a, b, name=name, grid=(M // tmo, Nn // tn, T // tk), trans_a=True,
               a_spec=pl.BlockSpec((tk, tmo), lambda i, j, k: (k, i)),
               b_spec=pl.BlockSpec((tk, tn), lambda i, j, k: (k, j)),
               o_spec=pl.BlockSpec((tmo, tn), lambda i, j, k: (i, j)),
               out_shape=_sds((M, Nn), BF16), acc_shape=(tmo, tn), comm=comm)


def _r2(a, tm):
    return (a, (tm, a.shape[1]), lambda i: (i, 0))


def _slab(a3, s, tm):
    return (a3, (None, tm, a3.shape[2]), lambda i: (s, i, 0))


def _o2(T, C, dtype, tm):
    return ((T, C), dtype, (tm, C), lambda i: (i, 0))


def _rows_fwd(fn, row_ins, par_ins, outs, *, name, nt):
    nr, npar = len(row_ins), len(par_ins)

    def body(*refs):
        vals = [r[...] for r in refs[:nr + npar]]
        res = fn(*vals)
        for o_ref, v in zip(refs[nr + npar:], res):
            o_ref[...] = v.astype(o_ref.dtype)

    return _pcall(body, name=name, grid=(nt,),
                  in_specs=[pl.BlockSpec(bs, im) for (_, bs, im) in row_ins] + [_full_spec(p) for p in par_ins],
                  out_specs=[pl.BlockSpec(bs, im) for (_, _, bs, im) in outs],
                  out_shape=[_sds(s, d) for (s, d, _, _) in outs])(*[r[0] for r in row_ins], *par_ins)


def _rows_bwd(fn, row_ins, par_ins, cot_ins, drow_outs, *, name, nt):
    nr, npar, nc = len(row_ins), len(par_ins), len(cot_ins)
    keep = [k for k, o in enumerate(drow_outs) if o is not None]

    def body(*refs):
        vals = [r[...].astype(F32) for r in refs[:nr + npar]]
        cots = [r[...].astype(F32) for r in refs[nr + npar:nr + npar + nc]]
        orefs = refs[nr + npar + nc:]
        _, vjp = jax.vjp(fn, *vals)
        grads = vjp(tuple(cots))
        for o_ref, k in zip(orefs[:len(keep)], keep):
            o_ref[...] = grads[k].astype(o_ref.dtype)
        prefs = orefs[len(keep):]

        @pl.when(pl.program_id(0) == 0)
        def _():
            for p_ref in prefs:
                p_ref[...] = jnp.zeros_like(p_ref)

        for p_ref, g in zip(prefs, grads[nr:]):
            p_ref[...] += g

    outs = [drow_outs[k] for k in keep]
    res = _pcall(body, name=name, grid=(nt,),
                 in_specs=[pl.BlockSpec(bs, im) for (_, bs, im) in row_ins] + [_full_spec(p) for p in par_ins]
                 + [pl.BlockSpec(bs, im) for (_, bs, im) in cot_ins],
                 out_specs=[pl.BlockSpec(bs, im) for (_, _, bs, im) in outs] + [_full_spec(p) for p in par_ins],
                 out_shape=[_sds(s, d) for (s, d, _, _) in outs] + [_sds(p.shape, F32) for p in par_ins],
                 )(*[r[0] for r in row_ins], *par_ins, *[c[0] for c in cot_ins])
    return list(res[:len(keep)]), list(res[len(keep):])


def _layer_norm(v, g, b):
    mu = jnp.mean(v, axis=-1, keepdims=True)
    var = jnp.mean(jnp.square(v - mu), axis=-1, keepdims=True)
    return (v - mu) * lax.rsqrt(var + LN_EPS) * g + b


def _silu(v):
    return v * jax.nn.sigmoid(v)


def _softplus(v):
    return jnp.maximum(v, 0.0) + jnp.log1p(jnp.exp(-jnp.abs(v)))


def _halo_of(K):
    return 8 * ((K - 1 + 7) // 8)


DW_HALO = 8
DW_RB = 16
DW_LC = 256


def _dw_taps(win, w_ref, p, ls, K, shift_of):
    acc = None
    for k in range(K):
        o = shift_of(k)
        term = win[o:o + DW_RB, :] * w_ref[p, k:k + 1, ls]
        acc = term if acc is None else acc + term
    return acc


def _dwconv_fwd(post, part_ins, w, biases, outs, *, K, C, name, tm, nt, comm=None):
    nparts, halo = len(part_ins), DW_HALO

    def body(*refs):
        x_refs, w_ref = refs[:nparts], refs[nparts]
        b_refs = refs[nparts + 1:2 * nparts + 1]
        orefs, buf = refs[2 * nparts + 1:-1], refs[-1]
        i = pl.program_id(0)
        for p in range(nparts):
            @pl.when(i == 0)
            def _():
                buf[p, pl.ds(0, halo), :] = jnp.zeros((halo, C), F32)

            @pl.when(i > 0)
            def _():
                buf[p, pl.ds(0, halo), :] = buf[p, pl.ds(tm, halo), :]

            buf[p, pl.ds(halo, tm), :] = x_refs[p][...].astype(F32)

        def group(r, carry):
            base = pl.multiple_of(r * DW_RB, DW_RB)
            for cj in range(C // DW_LC):
                ls = slice(cj * DW_LC, (cj + 1) * DW_LC)
                cs = [_dw_taps(buf[p, pl.ds(base, DW_RB + halo), ls], w_ref, p, ls, K, lambda k: halo - (K - 1) + k)
                      + b_refs[p][:, ls] for p in range(nparts)]
                for o_ref, v in zip(orefs, post(cs)):
                    o_ref[pl.ds(base, DW_RB), ls] = v.astype(o_ref.dtype)
            return carry

        lax.fori_loop(0, tm // DW_RB, group, 0)

    res, got = _pcall_maybe_carrying(
        body, comm, name=name, grid=(nt,),
        in_specs=[pl.BlockSpec(bs, im) for (_, bs, im) in part_ins] + [_full_spec(w)] + [_full_spec(b) for b in biases],
        out_specs=[pl.BlockSpec(bs, im) for (_, _, bs, im) in outs],
        out_shape=[_sds(s, d) for (s, d, _, _) in outs],
        scratch_shapes=[pltpu.VMEM((nparts, halo + tm, C), F32)])(*[r[0] for r in part_ins], w, *biases)
    return res if comm is None else (res, got)


def _dwconv_bwd(post, part_ins, halo_ins, w, biases, cot_ins, *, K, C, name, tm, nt, comm=None):
    nparts, halo, nc, RB = len(part_ins), DW_HALO, len(cot_ins), DW_RB
    T = nt * tm

    def body(*refs):
        x_refs, h_refs, w_ref = refs[:nparts], refs[nparts:2 * nparts], refs[2 * nparts]
        b_refs = refs[2 * nparts + 1:3 * nparts + 1]
        cot_refs = refs[3 * nparts + 1:3 * nparts + 1 + nc]
        rest = refs[3 * nparts + 1 + nc:]
        dx_refs, dw_ref, db_refs = rest[:nparts], rest[nparts], rest[nparts + 1:2 * nparts + 1]
        bufx, bufd, acc = rest[-3], rest[-2], rest[-1]
        s = pl.program_id(0)
        first_tile = s == nt - 1

        @pl.when(s == 0)
        def _():
            acc[...] = jnp.zeros_like(acc)
            for p in range(nparts):
                bufd[p, pl.ds(tm, halo), :] = jnp.zeros((halo, C), F32)

        for p in range(nparts):
            hrows = h_refs[p].shape[0]
            bufx[p, pl.ds(0, halo), :] = jnp.where(first_tile, 0.0, h_refs[p][...].astype(F32)[hrows - halo:hrows, :])
            bufx[p, pl.ds(halo, tm), :] = x_refs[p][...].astype(F32)
        fold = lambda v: v[0:8, :] + v[8:16, :]

        def conv_out_grads(r, carry):
            base = pl.multiple_of(r * RB, RB)
            for cj in range(C // DW_LC):
                ls = slice(cj * DW_LC, (cj + 1) * DW_LC)
                wins = [bufx[p, pl.ds(base, RB + halo), ls] for p in range(nparts)]
                cs = [_dw_taps(wins[p], w_ref, p, ls, K, lambda k: halo - (K - 1) + k) + b_refs[p][:, ls]
                      for p in range(nparts)]
                _, vjp = jax.vjp(lambda *c: post(list(c)), *cs)
                dcs = vjp(tuple(cr[pl.ds(base, RB), ls].astype(F32) for cr in cot_refs))
                for p in range(nparts):
                    bufd[p, pl.ds(base, RB), ls] = dcs[p]
                    for k in range(K):
                        o = halo - (K - 1) + k
                        acc[p, k, :, ls] += fold(dcs[p] * wins[p][o:o + RB, :])
                    acc[p, K, :, ls] += fold(dcs[p])
            return carry

        lax.fori_loop(0, tm // RB, conv_out_grads, 0)

        def input_grads(r, carry):
            base = pl.multiple_of(r * RB, RB)
            for cj in range(C // DW_LC):
                ls = slice(cj * DW_LC, (cj + 1) * DW_LC)
                for p in range(nparts):
                    dx = _dw_taps(bufd[p, pl.ds(base, RB + halo), ls], w_ref, p, ls, K, lambda k: K - 1 - k)
                    dx_refs[p][pl.ds(base, RB), ls] = dx.astype(dx_refs[p].dtype)
            return carry

        lax.fori_loop(0, tm // RB, input_grads, 0)
        for p in range(nparts):
            bufd[p, pl.ds(tm, halo), :] = bufd[p, pl.ds(0, halo), :]

        @pl.when(s == nt - 1)
        def _():
            dw_ref[...] = jnp.zeros_like(dw_ref)
            for p in range(nparts):
                for k in range(K):
                    dw_ref[p, k:k + 1, :] = jnp.sum(acc[p, k], axis=0, keepdims=True)
                db_refs[p][...] = jnp.sum(acc[p, K], axis=0, keepdims=True)

    rev = lambda im: (lambda s: im(nt - 1 - s))
    row = pl.BlockSpec((tm, C), lambda s: (nt - 1 - s, 0))
    res, got = _pcall_maybe_carrying(
        body, comm, name=name, grid=(nt,),
        in_specs=[pl.BlockSpec(bs, rev(im)) for (_, bs, im) in part_ins]
        + [pl.BlockSpec(bs, rev(im)) for (_, bs, im) in halo_ins]
        + [_full_spec(w)] + [_full_spec(b) for b in biases]
        + [pl.BlockSpec(bs, rev(im)) for (_, bs, im) in cot_ins],
        out_specs=[row] * nparts + [_full_spec(w)] + [_full_spec(b) for b in biases],
        out_shape=[_sds((T, C), BF16)] * nparts + [_sds(w.shape, F32)] + [_sds(b.shape, F32) for b in biases],
        scratch_shapes=[pltpu.VMEM((nparts, halo + tm, C), F32), pltpu.VMEM((nparts, tm + halo, C), F32),
                        pltpu.VMEM((nparts, K + 1, 8, C), F32)],
    )(*[r[0] for r in part_ins], *[r[0] for r in halo_ins], w, *biases, *[c[0] for c in cot_ins])
    out = (list(res[:nparts]), res[nparts], list(res[nparts + 1:]))
    return out if comm is None else (out, got)


def _halo_slab(a3, s, tm, halo):
    rows = max(halo, 16) if a3.dtype == BF16 else halo
    q = tm // rows
    return (a3, (None, rows, a3.shape[2]), lambda i: (s, jnp.maximum(i * q - 1, 0), 0))


CONF_HALO = _halo_of(CONV_K)
CONF_RB = 32


def _shifted_copies(buf, shifted, rows):
    for j in range(1, 8):
        shifted[j - 1, pl.ds(0, rows), :] = buf[pl.ds(j, rows), :]


def _shifted_rows(buf, shifted, s, base, nrows):
    j, q = s % 8, s // 8
    if j == 0:
        return buf[pl.ds(base + 8 * q, nrows), :]
    return shifted[j - 1, pl.ds(base + 8 * q, nrows), :]


def _conf_fwd(u3, w, cb, lg, lb, *, name, comm=None):
    T = u3.shape[1]
    tm = min(TM_X, T)
    nt = T // tm
    K, halo, RB = CONV_K, CONF_HALO, min(CONF_RB, tm)

    def body(a_ref, g_ref, w_ref, cb_ref, lg_ref, lb_ref, v3_ref, c_ref, bufx, xs):
        i = pl.program_id(0)

        @pl.when(i == 0)
        def _():
            bufx[pl.ds(0, halo), :] = jnp.zeros((halo, D), F32)

        @pl.when(i > 0)
        def _():
            bufx[pl.ds(0, halo), :] = bufx[pl.ds(tm, halo), :]

        bufx[pl.ds(halo, tm), :] = _conf_pre(a_ref[...].astype(F32), g_ref[...].astype(F32))[0]
        _shifted_copies(bufx, xs, halo + tm - 8)

        def group(r, carry):
            base = pl.multiple_of(r * RB, RB)
            acc = None
            for k in range(K):
                term = _shifted_rows(bufx, xs, halo - (K - 1) + k, base, RB) * w_ref[k:k + 1, :]
                acc = term if acc is None else acc + term
            c_ref[pl.ds(base, RB), :] = acc
            return carry

        lax.fori_loop(0, tm // RB, group, 0)
        v3_ref[...] = _conf_post([c_ref[...]], cb_ref[...], lg_ref[...], lb_ref[...])[0].astype(v3_ref.dtype)

    slab = lambda s: pl.BlockSpec((None, tm, D), lambda i: (s, i, 0))
    row = pl.BlockSpec((tm, D), lambda i: (i, 0))
    pars = [w, cb, lg, lb]
    res, got = _pcall_maybe_carrying(
        body, comm, name=name, grid=(nt,), in_specs=[slab(0), slab(1)] + [_full_spec(p) for p in pars],
        out_specs=[row, row], out_shape=[_sds((T, D), BF16), _sds((T, D), F32)],
        scratch_shapes=[pltpu.VMEM((halo + tm, D), F32), pltpu.VMEM((7, halo + tm - 8, D), F32)])(u3, u3, *pars)
    return res if comm is None else (res, got)


def _conf_bwd(u3, c, dv3, w, cb, lg, lb, *, name, comm=None):
    T = u3.shape[1]
    tm = min(TM_X, T)
    nt = T // tm
    K, halo, RB = CONV_K, CONF_HALO, min(CONF_RB, tm)

    def body(a_ref, g_ref, c_ref, dv3_ref, w_ref, cb_ref, lg_ref, lb_ref,
             da_ref, dg_ref, dw_ref, dcb_ref, dlg_ref, dlb_ref, xbuf, bufd, ds, dv0):
        s = pl.program_id(0)

        @pl.when(s == 0)
        def _():
            for r in (dw_ref, dcb_ref, dlg_ref, dlb_ref):
                r[...] = jnp.zeros_like(r)
            bufd[pl.ds(tm, halo), :] = jnp.zeros((halo, D), F32)

        xin, pre_vjp = jax.vjp(lambda p, q_: _conf_pre(p, q_)[0], a_ref[...].astype(F32), g_ref[...].astype(F32))
        xbuf[...] = xin

        _, post_vjp = jax.vjp(lambda cc, b_, g_, l_: _conf_post([cc], b_, g_, l_)[0],
                              c_ref[...], cb_ref[...], lg_ref[...], lb_ref[...])
        dc, dcb, dlg, dlb = post_vjp(dv3_ref[...])
        dcb_ref[...] += dcb
        dlg_ref[...] += dlg
        dlb_ref[...] += dlb
        bufd[pl.ds(0, tm), :] = dc
        _shifted_copies(bufd, ds, tm + halo - 8)

        def dx_group(r, carry):
            base = pl.multiple_of(r * RB, RB)
            acc = None
            for k in range(K):
                term = _shifted_rows(bufd, ds, K - 1 - k, base, RB) * w_ref[k:k + 1, :]
                acc = term if acc is None else acc + term
            dv0[pl.ds(base, RB), :] = acc
            return carry

        lax.fori_loop(0, tm // RB, dx_group, 0)

        for k in range(K):
            def dw_group(r, acc):
                base = pl.multiple_of(r * RB, RB)
                prod = xbuf[pl.ds(base, RB), :] * _shifted_rows(bufd, ds, K - 1 - k, base, RB)
                for v in range(RB // 8):
                    acc = acc + prod[v * 8:(v + 1) * 8, :]
                return acc

            acc = lax.fori_loop(0, tm // RB, dw_group, jnp.zeros((8, D), F32))
            dw_ref[k:k + 1, :] += jnp.sum(acc, axis=0, keepdims=True)

        bufd[pl.ds(tm, halo), :] = bufd[pl.ds(0, halo), :]
        da, dg = pre_vjp(dv0[...])
        da_ref[...] = da.astype(da_ref.dtype)
        dg_ref[...] = dg.astype(dg_ref.dtype)

    slab = lambda sl: pl.BlockSpec((None, tm, D), lambda s: (sl, nt - 1 - s, 0))
    row = pl.BlockSpec((tm, D), lambda s: (nt - 1 - s, 0))
    pars = [w, cb, lg, lb]
    res, got = _pcall_maybe_carrying(
        body, comm, name=name, grid=(nt,),
        in_specs=[slab(0), slab(1), row, row] + [_full_spec(p) for p in pars],
        out_specs=[row, row] + [_full_spec(p) for p in pars],
        out_shape=[_sds((T, D), BF16)] * 2 + [_sds(p.shape, F32) for p in pars],
        scratch_shapes=[pltpu.VMEM((tm, D), F32), pltpu.VMEM((tm + halo, D), F32),
                        pltpu.VMEM((7, tm + halo - 8, D), F32), pltpu.VMEM((tm, D), F32)],
    )(u3, u3, c, dv3, *pars)
    return res if comm is None else (res, got)


def _dg(a, b, ca, cb):
    return lax.dot_general(a.astype(BF16), b.astype(BF16), (((ca,), (cb,)), ((), ())), preferred_element_type=F32)


@jax.custom_vjp
def _dot_nn(a, b):
    return _dg(a, b, 1, 0)


_dot_nn.defvjp(lambda a, b: (_dg(a, b, 1, 0), (a, b)),
               lambda res, g: (_dg(g, res[1], 1, 1), _dg(res[0], g, 0, 0)))


@jax.custom_vjp
def _dot_nt(a, b):
    return _dg(a, b, 1, 1)


_dot_nt.defvjp(lambda a, b: (_dg(a, b, 1, 1), (a, b)),
               lambda res, g: (_dg(g, res[1], 1, 0), _dg(g, res[0], 0, 0)))


@jax.custom_vjp
def _dot_tn(a, b):
    return _dg(a, b, 0, 0)


_dot_tn.defvjp(lambda a, b: (_dg(a, b, 0, 0), (a, b)),
               lambda res, g: (_dg(res[1], g, 1, 1), _dg(res[0], g, 1, 0)))


def _split3(v):
    hi = v.astype(BF16)
    r = v - hi.astype(F32)
    mid = r.astype(BF16)
    return hi, mid, (r - mid.astype(F32)).astype(BF16)


def _x01(v, m, cv, cm, m_left=False):
    acc = None
    for piece in _split3(v):
        t = _dg(m, piece, cm, cv) if m_left else _dg(piece, m, cv, cm)
        acc = t if acc is None else acc + t
    return acc


@jax.custom_vjp
def _expand01(v, m):
    return _x01(v, m, 1, 0)


_expand01.defvjp(lambda v, m: (_x01(v, m, 1, 0), m),
                 lambda m, g: (_x01(g, m, 1, 1), jnp.zeros_like(m)))


@jax.custom_vjp
def _mix01(m, v):
    return _x01(v, m, 0, 1, m_left=True)


_mix01.defvjp(lambda m, v: (_x01(v, m, 0, 1, m_left=True), m),
              lambda m, g: (jnp.zeros_like(m), _x01(g, m, 0, 0, m_left=True)))


def _causal():
    return lax.broadcasted_iota(jnp.int32, (L, L), 0) >= lax.broadcasted_iota(jnp.int32, (L, L), 1)


def _ssd_chunk_prep(dtr, alog, dtb):
    dt = _softplus(dtr + dtb)
    a_cs = _mix01(_causal().astype(F32), dt * (-jnp.exp(alog)))
    return dt, a_cs, a_cs.T


def _ssd_group(xs, dt, a_cs, a_csT, Bg, Cg, zg, sp, dsk, nwg, *, g):
    causal = _causal()
    hi = lax.broadcasted_iota(jnp.int32, (HP, RP), 0)
    ci = lax.broadcasted_iota(jnp.int32, (HP, RP), 1)
    lo = (hi - g * R) * P
    E = ((ci >= lo) & (ci < lo + P)).astype(F32)

    acs_e = _expand01(a_cs, E)
    dt_e = _expand01(dt, E)
    alast_e = acs_e[L - 1:L, :]
    xdt = xs * dt_e
    cb = _dot_nt(Cg, Bg)
    y_off = _dot_nn(Cg, sp) * jnp.exp(acs_e)
    yd = []
    for r in range(R):
        h = g * R + r
        seg = a_cs[:, h:h + 1] - a_csT[h:h + 1, :]
        dec = jnp.exp(jnp.where(causal, seg, -1e30))
        yd.append(_dot_nn(cb * dec, xdt[:, r * P:(r + 1) * P]))
    y = jnp.concatenate(yd, axis=1) + y_off + xs * _expand01(jnp.broadcast_to(dsk, (8, HP)), E)[0:1, :]
    yg = y * _silu(zg)
    yn = yg * lax.rsqrt(jnp.mean(jnp.square(yg), axis=-1, keepdims=True) + RMS_EPS) * nwg
    sc = _dot_tn(Bg, xdt * jnp.exp(alast_e - acs_e))
    return yn, jnp.exp(alast_e) * sp + sc


def _group_cols(g):
    return g // 2, (g % 2) * RP


def _ssd_fwd(x0, x1, bc, dtr, u3, alog, dtb, dsk, nw, *, name, comm=None):
    T = x0.shape[0]
    nc = T // L

    def body(x0_ref, x1_ref, bc_ref, dtr_ref, z0_ref, z1_ref, alog_ref, dtb_ref, dsk_ref, nw_ref, yn_ref, sp_ref, S):
        @pl.when(pl.program_id(0) == 0)
        def _():
            S[...] = jnp.zeros_like(S)

        xr, zr = (x0_ref, x1_ref), (z0_ref, z1_ref)
        dt, a_cs, a_csT = _ssd_chunk_prep(dtr_ref[...], alog_ref[...], dtb_ref[...])
        for g in range(G):
            s, off = _group_cols(g)
            sp = S[g]
            sp_ref[0, g] = sp
            yn, s_next = _ssd_group(xr[s][:, off:off + RP], dt, a_cs, a_csT, bc_ref[:, g * N:(g + 1) * N],
                                    bc_ref[:, G * N + g * N:G * N + (g + 1) * N], zr[s][:, off:off + RP].astype(F32), sp,
                                    dsk_ref[...], nw_ref[:, g * RP:(g + 1) * RP], g=g)
            yn_ref[:, g * RP:(g + 1) * RP] = yn.astype(yn_ref.dtype)
            S[g] = s_next

    row = lambda C: pl.BlockSpec((L, C), lambda c: (c, 0))
    zspec = lambda s: pl.BlockSpec((None, L, D), lambda c: (s, c, 0))
    pars = [alog, dtb, dsk, nw]
    res, got = _pcall_maybe_carrying(
        body, comm, name=name, grid=(nc,),
        in_specs=[row(D), row(D), row(D), row(HP), zspec(2), zspec(3)] + [_full_spec(p) for p in pars],
        out_specs=[row(SD), pl.BlockSpec((1, G, N, RP), lambda c: (c, 0, 0, 0))],
        out_shape=[_sds((T, SD), BF16), _sds((nc, G, N, RP), F32)],
        scratch_shapes=[pltpu.VMEM((G, N, RP), F32)])(x0, x1, bc, dtr, u3, u3, *pars)
    return res if comm is None else (res, got)


def _ssd_bwd(x0, x1, bc, dtr, u3, sprev, dyn, alog, dtb, dsk, nw, *, name, comm=None):
    T = x0.shape[0]
    nc = T // L

    def body(x0_ref, x1_ref, bc_ref, dtr_ref, z0_ref, z1_ref, sp_ref, dyn_ref, alog_ref, dtb_ref, dsk_ref, nw_ref,
             dx0_ref, dx1_ref, dbc_ref, ddtr_ref, dz0_ref, dz1_ref, dalog_ref, ddtb_ref, ddsk_ref, dnw_ref, dS):
        @pl.when(pl.program_id(0) == 0)
        def _():
            dS[...] = jnp.zeros_like(dS)
            for r in (dalog_ref, ddtb_ref, ddsk_ref, dnw_ref):
                r[...] = jnp.zeros_like(r)

        xr, zr = (x0_ref, x1_ref), (z0_ref, z1_ref)
        dxr, dzr = (dx0_ref, dx1_ref), (dz0_ref, dz1_ref)
        (dt, a_cs, a_csT), prep_vjp = jax.vjp(_ssd_chunk_prep, dtr_ref[...], alog_ref[...], dtb_ref[...])
        d_dt, d_acs, d_acsT = jnp.zeros((L, HP), F32), jnp.zeros((L, HP), F32), jnp.zeros((HP, L), F32)
        for g in range(G):
            s, off = _group_cols(g)
            _, vjp = jax.vjp(functools.partial(_ssd_group, g=g), xr[s][:, off:off + RP], dt, a_cs, a_csT,
                             bc_ref[:, g * N:(g + 1) * N], bc_ref[:, G * N + g * N:G * N + (g + 1) * N],
                             zr[s][:, off:off + RP].astype(F32), sp_ref[0, g], dsk_ref[...], nw_ref[:, g * RP:(g + 1) * RP])
            dxs, ddt_g, dacs_g, dacsT_g, dB, dC, dz, dsp, dds, dnwg = vjp((dyn_ref[:, g * RP:(g + 1) * RP], dS[g]))
            dxr[s][:, off:off + RP] = dxs
            dzr[s][:, off:off + RP] = dz.astype(dz0_ref.dtype)
            dbc_ref[:, g * N:(g + 1) * N] = dB
            dbc_ref[:, G * N + g * N:G * N + (g + 1) * N] = dC
            dS[g] = dsp
            d_dt, d_acs, d_acsT = d_dt + ddt_g, d_acs + dacs_g, d_acsT + dacsT_g
            ddsk_ref[...] += dds
            dnw_ref[:, g * RP:(g + 1) * RP] += dnwg
        ddtr, dal, ddb = prep_vjp((d_dt, d_acs, d_acsT))
        dalog_ref[...] += dal
        ddtb_ref[...] += ddb
        ddtr_ref[...] = ddtr.astype(ddtr_ref.dtype)

    row = lambda C: pl.BlockSpec((L, C), lambda c: (nc - 1 - c, 0))
    zspec = lambda s: pl.BlockSpec((None, L, D), lambda c: (s, nc - 1 - c, 0))
    pars = [alog, dtb, dsk, nw]
    res, got = _pcall_maybe_carrying(
        body, comm, name=name, grid=(nc,),
        in_specs=[row(D), row(D), row(D), row(HP), zspec(2), zspec(3),
                  pl.BlockSpec((1, G, N, RP), lambda c: (nc - 1 - c, 0, 0, 0)), row(SD)] + [_full_spec(p) for p in pars],
        out_specs=[row(D), row(D), row(D), row(HP), row(D), row(D)] + [_full_spec(p) for p in pars],
        out_shape=[_sds((T, D), F32)] * 3 + [_sds((T, HP), BF16), _sds((T, D), BF16), _sds((T, D), BF16)]
        + [_sds(p.shape, F32) for p in pars],
        scratch_shapes=[pltpu.VMEM((G, N, RP), F32)])(x0, x1, bc, dtr, u3, u3, sprev, dyn, *pars)
    return res if comm is None else (res, got)


def _loss_head(y, target, *, name):
    T = y.shape[0]
    tm = min(TM, T)

    def body(y_ref, t_ref, loss_ref, dy_ref):
        e = y_ref[...] - t_ref[...]
        dy_ref[...] = e * (1.0 / D)

        @pl.when(pl.program_id(0) == 0)
        def _():
            loss_ref[...] = jnp.zeros_like(loss_ref)

        loss_ref[...] += 0.5 * jnp.sum(jnp.mean(jnp.square(e), axis=-1, keepdims=True), axis=0, keepdims=True)

    row = pl.BlockSpec((tm, D), lambda i: (i, 0))
    return _pcall(body, name=name, grid=(T // tm,), in_specs=[row, row],
                  out_specs=[pl.BlockSpec((1, 128), lambda i: (0, 0)), row],
                  out_shape=[_sds((1, 128), F32), _sds((T, D), F32)])(y, target)


_HBM = pl.BlockSpec(memory_space=pltpu.HBM)
_MESH = pl.DeviceIdType.MESH


def _exchange_comm(bufs, *, scatter):
    nb = len(bufs)

    def copies(in_refs, out_refs, sems, with_arrivals):
        send_sems, recv_sems, local_sems = sems
        x, y, c = lax.axis_index("x"), lax.axis_index("y"), lax.axis_index("c")
        me = 2 * x + y
        peers = [(1 - x, y), (x, 1 - y), (1 - x, 1 - y)]
        own, sends, arrivals = [], [], []
        for b in range(nb):
            src_own = in_refs[b].at[me] if scatter else in_refs[b]
            own.append(pltpu.make_async_copy(src_own, out_refs[b].at[me], local_sems.at[b]))
            for k, (px, py) in enumerate(peers):
                src = in_refs[b].at[2 * px + py] if scatter else in_refs[b]
                sends.append(pltpu.make_async_remote_copy(
                    src_ref=src, dst_ref=out_refs[b].at[me], send_sem=send_sems.at[b, k], recv_sem=recv_sems.at[b, k],
                    device_id=(px, py, c), device_id_type=_MESH))
                if with_arrivals:
                    slot = out_refs[b].at[2 * px + py]
                    arrivals.append(pltpu.make_async_remote_copy(
                        src_ref=slot, dst_ref=slot, send_sem=send_sems.at[b, k], recv_sem=recv_sems.at[b, k],
                        device_id=(px, py, c), device_id_type=_MESH))
        return own, sends, arrivals

    def start(in_refs, out_refs, sems):
        own, sends, _ = copies(in_refs, out_refs, sems, False)
        for cp in own + sends:
            cp.start()

    def wait(in_refs, out_refs, sems):
        own, sends, arrivals = copies(in_refs, out_refs, sems, True)
        for cp in arrivals:
            cp.wait_recv()
        for cp in sends:
            cp.wait_send()
        for cp in own:
            cp.wait()

    return dict(ins=list(bufs), outs=[_sds(b.shape if scatter else (NCHIP,) + b.shape, b.dtype) for b in bufs],
                sems=[pltpu.SemaphoreType.DMA((nb, 3)), pltpu.SemaphoreType.DMA((nb, 3)), pltpu.SemaphoreType.DMA((nb,))],
                start=start, wait=wait)


def _chip_exchange(bufs, *, scatter, name):
    comm = _exchange_comm(bufs, scatter=scatter)
    nb = len(bufs)

    def body(*refs):
        comm["start"](refs[:nb], refs[nb:2 * nb], refs[2 * nb:])
        comm["wait"](refs[:nb], refs[nb:2 * nb], refs[2 * nb:])

    return _ccall(body, name=name, in_specs=[_HBM] * nb, out_specs=[_HBM] * nb, out_shape=comm["outs"],
                  scratch_shapes=comm["sems"])(*bufs)


def _core_swap(bufs, *, name):
    nb = len(bufs)

    def body(*refs):
        in_refs, out_refs, send_sems, recv_sems = refs[:nb], refs[nb:2 * nb], refs[2 * nb], refs[2 * nb + 1]
        x, y, c = lax.axis_index("x"), lax.axis_index("y"), lax.axis_index("c")
        cps = [pltpu.make_async_remote_copy(src_ref=in_refs[b], dst_ref=out_refs[b], send_sem=send_sems.at[b],
                                            recv_sem=recv_sems.at[b], device_id=(x, y, 1 - c), device_id_type=_MESH)
               for b in range(nb)]
        for cp in cps:
            cp.start()
        for cp in cps:
            cp.wait()

    return _ccall(body, name=name, in_specs=[_HBM] * nb, out_specs=[_HBM] * nb,
                  out_shape=[_sds(b.shape, b.dtype) for b in bufs],
                  scratch_shapes=[pltpu.SemaphoreType.DMA((nb,)), pltpu.SemaphoreType.DMA((nb,))])(*bufs)


def _all_gather8(buf, *, name):
    def body(in_ref, out_ref, send_sems, recv_sems, local_sem):
        x, y, c = lax.axis_index("x"), lax.axis_index("y"), lax.axis_index("c")
        me = 4 * x + 2 * y + c
        own = pltpu.make_async_copy(in_ref, out_ref.at[me], local_sem)
        own.start()
        flips = [(fx, fy, fc) for fx in (0, 1) for fy in (0, 1) for fc in (0, 1)][1:]
        peers = [(x ^ fx, y ^ fy, c ^ fc) for fx, fy, fc in flips]
        sends = []
        for k, peer in enumerate(peers):
            cp = pltpu.make_async_remote_copy(src_ref=in_ref, dst_ref=out_ref.at[me], send_sem=send_sems.at[k],
                                              recv_sem=recv_sems.at[k], device_id=peer, device_id_type=_MESH)
            cp.start()
            sends.append(cp)
        for k, (px, py, pc) in enumerate(peers):
            slot = out_ref.at[4 * px + 2 * py + pc]
            pltpu.make_async_remote_copy(src_ref=slot, dst_ref=slot, send_sem=send_sems.at[k], recv_sem=recv_sems.at[k],
                                         device_id=(px, py, pc), device_id_type=_MESH).wait_recv()
        for cp in sends:
            cp.wait_send()
        own.wait()

    return _ccall(body, name=name, in_specs=[_HBM], out_specs=_HBM, out_shape=_sds((8,) + buf.shape, buf.dtype),
                  scratch_shapes=[pltpu.SemaphoreType.DMA((7,)), pltpu.SemaphoreType.DMA((7,)), pltpu.SemaphoreType.DMA])(buf)


def _row_tile(rows, cap):
    if rows <= cap:
        return rows
    return max(t for t in range(16, cap + 1, 16) if rows % t == 0)


def _sum_slots(stack, *, name, cap=256):
    S, Rr, C = stack.shape
    tr = _row_tile(Rr, cap)

    def body(s_ref, o_ref):
        acc = s_ref[0].astype(F32)
        for j in range(1, S):
            acc = acc + s_ref[j].astype(F32)
        o_ref[...] = acc

    return _pcall(body, name=name, grid=(Rr // tr,), in_specs=[pl.BlockSpec((S, tr, C), lambda i: (0, i, 0))],
                  out_specs=pl.BlockSpec((tr, C), lambda i: (i, 0)), out_shape=_sds((Rr, C), F32))(stack)


def _adamw(g_parts, w, m, v, *, name, cap=128):
    Rr, C = w.shape
    tr = _row_tile(Rr, cap)
    ng = len(g_parts)
    c1 = 1.0 / (1.0 - ADAM_B1 ** ADAM_STEP)
    c2 = 1.0 / (1.0 - ADAM_B2 ** ADAM_STEP)

    def body(*refs):
        g = refs[0][...]
        for r in refs[1:ng]:
            g = g + r[...]
        w_ref, m_ref, v_ref, g_out, d_out, m_out, v_out = refs[ng:]
        mn = ADAM_B1 * m_ref[...] + (1.0 - ADAM_B1) * g
        vn = ADAM_B2 * v_ref[...] + (1.0 - ADAM_B2) * jnp.square(g)
        g_out[...] = g
        m_out[...] = mn
        v_out[...] = vn
        d_out[...] = -ADAM_LR * ((mn * c1) / (jnp.sqrt(vn * c2) + ADAM_EPS) + ADAM_WD * w_ref[...])

    spec = pl.BlockSpec((tr, C), lambda i: (i, 0))
    return _pcall(body, name=name, grid=(Rr // tr,), in_specs=[spec] * (ng + 3), out_specs=[spec] * 4,
                  out_shape=[_sds((Rr, C), F32)] * 4)(*g_parts, w, m, v)


def _adamw_layers(mine, other, w3, m3, v3, *, name, cap=128):
    _, Rr, C = w3.shape
    tr = _row_tile(Rr, cap)
    nt = Rr // tr
    c1 = 1.0 / (1.0 - ADAM_B1 ** ADAM_STEP)
    c2 = 1.0 / (1.0 - ADAM_B2 ** ADAM_STEP)

    def body(m0, m1, o0, o1, w_ref, m_ref, v_ref, g_out, d_out, m_out, v_out):
        g = jnp.where(pl.program_id(0) == 0, m0[...] + o0[...], m1[...] + o1[...])
        mn = ADAM_B1 * m_ref[...] + (1.0 - ADAM_B1) * g
        vn = ADAM_B2 * v_ref[...] + (1.0 - ADAM_B2) * jnp.square(g)
        g_out[...] = g
        m_out[...] = mn
        v_out[...] = vn
        d_out[...] = -ADAM_LR * ((mn * c1) / (jnp.sqrt(vn * c2) + ADAM_EPS) + ADAM_WD * w_ref[...])

    g0 = pl.BlockSpec((tr, C), lambda l, i: (jnp.where(l == 0, i, nt - 1), 0))
    g1 = pl.BlockSpec((tr, C), lambda l, i: (jnp.where(l == 1, i, 0), 0))
    s3 = pl.BlockSpec((None, tr, C), lambda l, i: (l, i, 0))
    return _pcall(body, name=name, grid=(2, nt), in_specs=[g0, g1, g0, g1, s3, s3, s3], out_specs=[s3] * 4,
                  out_shape=[_sds(w3.shape, F32)] * 4)(mine[0], mine[1], other[0], other[1], w3, m3, v3)


def _pack(arrs, dtype, row_mult):
    flat = jnp.concatenate([a.reshape(-1).astype(dtype) for a in arrs])
    n = flat.shape[0]
    unit = row_mult * PACK_W
    total = unit * ((n + unit - 1) // unit)
    if total > n:
        flat = jnp.concatenate([flat, jnp.zeros((total - n,), dtype)])
    return flat.reshape(-1, PACK_W)


def _unpack(buf, shapes):
    flat = buf.reshape(-1)
    out, off = [], 0
    for s in shapes:
        n = math.prod(s)
        out.append(flat[off:off + n].reshape(s))
        off += n
    return out


def _conf_pre(a, g):
    return [a * jax.nn.sigmoid(g)]


def _conf_post(cs, cb, lg, lb):
    return (_silu(_layer_norm(cs[0] + cb, lg, lb)),)


def _xbc_post(cs):
    return tuple(_silu(c) for c in cs)


def _ffn_post(cs):
    return (_silu(cs[0]) * cs[1],)


def _mix_fn(ga, gb, ya, yb):
    return (jax.nn.sigmoid(ga.astype(F32)) * ya + jax.nn.sigmoid(gb.astype(F32)) * yb,)


def _res_ln_fn(h, r, g, b):
    return (_layer_norm(ALPHA * h + r, g, b),)


def _ln_fn(x, g, b):
    return (_layer_norm(x, g, b),)


def _carrying(carry, key, gr, call):
    if key not in carry:
        return call(None)
    comm, done = carry[key](gr)
    out, got = call(comm)
    done(got)
    return out


def _two_copies(fn):
    def wrapped(*args):
        return fn(*args) * 2
    return wrapped


def _copies_out(T, tm):
    return [_o2(T, D, F32, tm), _o2(T, D, BF16, tm)]


def _layer_fwd(h, hb, W, l, carry):
    T = h.shape[0]
    tm = min(TM, T)
    nt = T // tm
    tmf = min(TM_FFN, T)
    ntf = T // tmf
    nm = lambda s: f"l{l}_{s}"
    u3 = _carrying(carry, "u", None,
                   lambda comm: _mm_resident_slab_out(hb, W["w_p"], name=nm("u"), width=D, tm=TM, out_dtype=BF16, comm=comm))
    dtr = _mm_nn(hb, W["w_dt"], name=nm("dt"))
    v3, cconv = _carrying(carry, "conf", None, lambda comm: _conf_fwd(
        u3, W["conv_w"], W["conv_b"], W["conv_ln_g"], W["conv_ln_b"], name=nm("conf"), comm=comm))
    ya = _mm_nn(v3, W["w_co"], name=nm("ya"))
    tmx = min(TM_X, T)
    x0, x1, bc = _carrying(carry, "xbc", None, lambda comm: _dwconv_fwd(
        _xbc_post, [_slab(u3, 6, tmx), _slab(u3, 7, tmx), _slab(u3, 8, tmx)], W["ssm_w"], W["ssm_b"],
        [_o2(T, D, F32, tmx)] * 3, K=SSM_K, C=D, name=nm("xbc"), tm=tmx, nt=T // tmx, comm=comm))
    yn, sprev = _carrying(carry, "ssd", None, lambda comm: _ssd_fwd(
        x0, x1, bc, dtr, u3, W["a_log"], W["dt_bias"], W["d_skip"], W["norm_w"], name=nm("ssd"), comm=comm))
    yb = _mm_nn(yn, W["w_so"], name=nm("yb"), tk=min(SD, 1024))
    (m,) = _rows_fwd(_mix_fn, [_slab(u3, 4, tm), _slab(u3, 5, tm), _r2(ya, tm), _r2(yb, tm)], [],
                     [_o2(T, D, BF16, tm)], name=nm("mix"), nt=nt)
    mix = _mm_nn(m, W["w_o"], name=nm("wo"))
    h1, h1b = _rows_fwd(_two_copies(_res_ln_fn), [_r2(h, tm), _r2(mix, tm)], [W["ln1_g"], W["ln1_b"]],
                        _copies_out(T, tm), name=nm("ln1"), nt=nt)
    up3 = _carrying(carry, "up", None,
                    lambda comm: _mm_resident_slab_out(h1b, W["w_up"], name=nm("up"), width=FFN, tm=TM, comm=comm))
    (f,) = _carrying(carry, "ffnact", None, lambda comm: _dwconv_fwd(
        _ffn_post, [_slab(up3, 0, tmf), _slab(up3, 1, tmf)], W["ffn_w"], W["ffn_b"],
        [_o2(T, FFN, BF16, tmf)], K=FFN_K, C=FFN, name=nm("ffnact"), tm=tmf, nt=ntf, comm=comm))
    ffn = _mm_nn(f, W["w_dn"], name=nm("dn"))
    h2, h2b = _rows_fwd(_two_copies(_res_ln_fn), [_r2(h1, tm), _r2(ffn, tm)], [W["ln2_g"], W["ln2_b"]],
                        _copies_out(T, tm), name=nm("ln2"), nt=nt)
    saved = dict(h=h, hb=hb, u3=u3, dtr=dtr, v3=v3, cconv=cconv, ya=ya, x0=x0, x1=x1, bc=bc, sprev=sprev, yn=yn, yb=yb, m=m,
                 mix=mix, h1=h1, h1b=h1b, up3=up3, f=f, ffn=ffn)
    return h2, h2b, saved


def _layer_bwd(dh2, W, sv, l, carry):
    T = dh2.shape[0]
    tm = min(TM, T)
    nt = T // tm
    tmf = min(TM_FFN, T)
    ntf = T // tmf
    nm = lambda s: f"l{l}_{s}"
    gr = {}
    (dres2, dffn), (gr["ln2_g"], gr["ln2_b"]) = _rows_bwd(
        _res_ln_fn, [_r2(sv["h1"], tm), _r2(sv["ffn"], tm)], [W["ln2_g"], W["ln2_b"]], [_r2(dh2, tm)],
        [_o2(T, D, F32, tm), _o2(T, D, BF16, tm)], name=nm("ln2_b"), nt=nt)
    df = _mm_nt(dffn, W["w_dn"], name=nm("dn_dx"))
    gr["w_dn"] = _mm_tn(sv["f"], dffn, name=nm("dn_dw"), tmo=FFN // 2)
    up3 = sv["up3"]
    (dgate, dval), gr["ffn_w"], gr["ffn_b"] = _carrying(carry, "ffnact_b", gr, lambda comm: _dwconv_bwd(
        _ffn_post, [_slab(up3, 0, tmf), _slab(up3, 1, tmf)],
        [_halo_slab(up3, 0, tmf, DW_HALO), _halo_slab(up3, 1, tmf, DW_HALO)], W["ffn_w"], W["ffn_b"], [_r2(df, tmf)],
        K=FFN_K, C=FFN, name=nm("ffnact_b"), tm=tmf, nt=ntf, comm=comm))
    dh1 = _mm_cols_nt([dgate, dval], W["w_up"], name=nm("up_dx"), add=dres2)
    gr["w_up"] = _mm_tn_cols(sv["h1b"], [dgate, dval], name=nm("up_dw"), tmo=min(512, D))
    (dres1, dmix), (gr["ln1_g"], gr["ln1_b"]) = _rows_bwd(
        _res_ln_fn, [_r2(sv["h"], tm), _r2(sv["mix"], tm)], [W["ln1_g"], W["ln1_b"]], [_r2(dh1, tm)],
        [_o2(T, D, F32, tm), _o2(T, D, BF16, tm)], name=nm("ln1_b"), nt=nt)
    dm = _mm_nt(dmix, W["w_o"], name=nm("wo_dx"))
    gr["w_o"] = _mm_tn(sv["m"], dmix, name=nm("wo_dw"), tmo=min(512, D))
    u3 = sv["u3"]
    (dga, dgb, dya, dyb), _ = _rows_bwd(
        _mix_fn, [_slab(u3, 4, tm), _slab(u3, 5, tm), _r2(sv["ya"], tm), _r2(sv["yb"], tm)], [], [_r2(dm, tm)],
        [_o2(T, D, BF16, tm)] * 4, name=nm("mix_b"), nt=nt)
    dv3 = _mm_nt(dya, W["w_co"], name=nm("ya_dx"))
    gr["w_co"] = _mm_tn(sv["v3"], dya, name=nm("ya_dw"), tmo=min(512, D))
    da, dg, gr["conv_w"], gr["conv_b"], gr["conv_ln_g"], gr["conv_ln_b"] = _carrying(carry, "conf_b", gr, lambda comm: _conf_bwd(
        u3, sv["cconv"], dv3, W["conv_w"], W["conv_b"], W["conv_ln_g"], W["conv_ln_b"], name=nm("conf_b"), comm=comm))
    dyn = _mm_nt(dyb, W["w_so"], name=nm("yb_dx"))
    gr["w_so"] = _mm_tn(sv["yn"], dyb, name=nm("yb_dw"), tmo=min(512, SD))
    (dx0, dx1, dbc, ddtr, dz0, dz1, gr["a_log"], gr["dt_bias"], gr["d_skip"], gr["norm_w"]) = _carrying(
        carry, "ssd_b", gr, lambda comm: _ssd_bwd(
            sv["x0"], sv["x1"], sv["bc"], sv["dtr"], u3, sv["sprev"], dyn, W["a_log"], W["dt_bias"], W["d_skip"],
            W["norm_w"], name=nm("ssd_b"), comm=comm))
    tmx = min(TM_X, T)
    (du6, du7, du8), gr["ssm_w"], gr["ssm_b"] = _dwconv_bwd(
        _xbc_post, [_slab(u3, 6, tmx), _slab(u3, 7, tmx), _slab(u3, 8, tmx)],
        [_halo_slab(u3, 6, tmx, DW_HALO), _halo_slab(u3, 7, tmx, DW_HALO), _halo_slab(u3, 8, tmx, DW_HALO)],
        W["ssm_w"], W["ssm_b"], [_r2(dx0, tmx), _r2(dx1, tmx), _r2(dbc, tmx)], K=SSM_K, C=D, name=nm("xbc_b"),
        tm=tmx, nt=T // tmx)
    du = [da, dg, dz0, dz1, dga, dgb, du6, du7, du8]
    gr["w_dt"] = _mm_tn(sv["hb"], ddtr, name=nm("dt_dw"), tmo=min(512, D))
    gr["w_p"] = _carrying(carry, "u_dw", gr, lambda comm: _mm_tn_cols(sv["hb"], du, name=nm("u_dw"),
                                                                      tmo=min(512, D), comm=comm))
    dh_a = _mm_nt(ddtr, W["w_dt"], name=nm("dt_dx"), add=dres1)
    dh = _carrying(carry, "u_dx", gr, lambda comm: _mm_cols_nt(du, W["w_p"], name=nm("u_dx"), add=dh_a, comm=comm))
    return dh, gr


_U_SPLIT = (2 * D + SD, 2 * D + SD + XBC, 2 * D + SD + XBC + H)


def _pad_rows(a, rows):
    return jnp.concatenate([a, jnp.zeros((rows - a.shape[0],) + a.shape[1:], a.dtype)], axis=0)


def _pad_lanes(a, lanes):
    return jnp.concatenate([a, jnp.zeros(a.shape[:-1] + (lanes - a.shape[-1],), a.dtype)], axis=-1)


def _w_in_layout(w_in):
    e0, e1, e2 = _U_SPLIT
    return dict(w_p=jnp.concatenate([w_in[:, :e0], w_in[:, e2:], w_in[:, e0:e1]], axis=1),
                w_dt=_pad_lanes(w_in[:, e1:e2], HP))


_MM_KEY = dict(w_conv_out="w_co", w_ssm_out="w_so", w_o="w_o", w_ffn_up="w_up", w_ffn_down="w_dn")


def _small_layer_weights(full, l):
    row = lambda a: a.reshape(1, -1)
    ssm_w = full["ssm_conv_w"][l]
    ffn_w = full["ffn_dw_w"][l]
    ssm_b = full["ssm_conv_b"][l]
    ffn_b = full["ffn_dw_b"][l]
    W = dict(
        conv_w=_pad_rows(full["conv_dw_w"][l], 32),
        conv_b=row(full["conv_dw_b"][l]), conv_ln_g=row(full["conv_ln_g"][l]), conv_ln_b=row(full["conv_ln_b"][l]),
        ssm_w=jnp.stack([_pad_rows(ssm_w[:, p * D:(p + 1) * D], 8) for p in range(3)]),
        ssm_b=[row(ssm_b[p * D:(p + 1) * D]) for p in range(3)],
        a_log=_pad_lanes(row(full["ssm_a_log"][l]), HP), dt_bias=_pad_lanes(row(full["ssm_dt_bias"][l]), HP),
        d_skip=_pad_lanes(row(full["ssm_d"][l]), HP), norm_w=row(full["ssm_norm_w"][l]),
        ln1_g=row(full["ln1_g"][l]), ln1_b=row(full["ln1_b"][l]),
        ffn_w=jnp.stack([_pad_rows(ffn_w[:, p * FFN:(p + 1) * FFN], 8) for p in range(2)]),
        ffn_b=[row(ffn_b[p * FFN:(p + 1) * FFN]) for p in range(2)],
        ln2_g=row(full["ln2_g"][l]), ln2_b=row(full["ln2_b"][l]),
    )
    return W


def _w_in_grad(gr):
    e0 = _U_SPLIT[0]
    wp = gr["w_p"]
    return jnp.concatenate([wp[:, :e0], wp[:, e0 + 2 * D:e0 + 2 * D + XBC], gr["w_dt"][:, :H], wp[:, e0:e0 + 2 * D]], axis=1)


def _layer_grads_to_reference_layout(gr):
    return dict(
        w_in=_w_in_grad(gr), conv_dw_w=gr["conv_w"][:CONV_K], conv_dw_b=gr["conv_b"][0], conv_ln_g=gr["conv_ln_g"][0],
        conv_ln_b=gr["conv_ln_b"][0], w_conv_out=gr["w_co"],
        ssm_conv_w=jnp.concatenate([gr["ssm_w"][p, :SSM_K] for p in range(3)], axis=1),
        ssm_conv_b=jnp.concatenate([b[0] for b in gr["ssm_b"]]),
        ssm_dt_bias=gr["dt_bias"][0, :H], ssm_a_log=gr["a_log"][0, :H], ssm_d=gr["d_skip"][0, :H],
        ssm_norm_w=gr["norm_w"][0], w_ssm_out=gr["w_so"], w_o=gr["w_o"], ln1_g=gr["ln1_g"][0], ln1_b=gr["ln1_b"][0],
        w_ffn_up=gr["w_up"], ffn_dw_w=jnp.concatenate([gr["ffn_w"][p, :FFN_K] for p in range(2)], axis=1),
        ffn_dw_b=jnp.concatenate([b[0] for b in gr["ffn_b"]]), w_ffn_down=gr["w_dn"], ln2_g=gr["ln2_g"][0],
        ln2_b=gr["ln2_b"][0],
    )


_BIG = dict(w_in=2, w_conv_out=1, w_ssm_out=1, w_o=1, w_ffn_up=2, w_ffn_down=1)
_SMALL_SHARDED = dict(conv_dw_w=2, ssm_conv_w=2, ffn_dw_w=2)
_REPLICATED = ("ln_in_g", "ln_in_b", "conv_dw_b", "conv_ln_g", "conv_ln_b", "ssm_conv_b", "ssm_dt_bias", "ssm_a_log",
               "ssm_d", "ssm_norm_w", "ln1_g", "ln1_b", "ffn_dw_b", "ln2_g", "ln2_b")
_WEIGHTS = ("ln_in_g", "ln_in_b", "w_in", "conv_dw_w", "conv_dw_b", "conv_ln_g", "conv_ln_b", "w_conv_out", "ssm_conv_w",
            "ssm_conv_b", "ssm_dt_bias", "ssm_a_log", "ssm_d", "ssm_norm_w", "w_ssm_out", "w_o", "ln1_g", "ln1_b",
            "w_ffn_up", "ffn_dw_w", "ffn_dw_b", "w_ffn_down", "ln2_g", "ln2_b")


def _split_chips(a, axis):
    rows, cols = a.shape
    if axis == 0:
        return a.reshape(NCHIP, rows // NCHIP, cols)
    return a.reshape(rows, NCHIP, cols // NCHIP).transpose(1, 0, 2)


def kernel(x, ln_in_g, ln_in_b, w_in, conv_dw_w, conv_dw_b, conv_ln_g, conv_ln_b, w_conv_out, ssm_conv_w, ssm_conv_b, ssm_dt_bias, ssm_a_log, ssm_d, ssm_norm_w, w_ssm_out, w_o, ln1_g, ln1_b, w_ffn_up, ffn_dw_w, ffn_dw_b, w_ffn_down, ln2_g, ln2_b, loss_target, m_ln_in_g, m_ln_in_b, m_w_in, m_conv_dw_w, m_conv_dw_b, m_conv_ln_g, m_conv_ln_b, m_w_conv_out, m_ssm_conv_w, m_ssm_conv_b, m_ssm_dt_bias, m_ssm_a_log, m_ssm_d, m_ssm_norm_w, m_w_ssm_out, m_w_o, m_ln1_g, m_ln1_b, m_w_ffn_up, m_ffn_dw_w, m_ffn_dw_b, m_w_ffn_down, m_ln2_g, m_ln2_b, v_ln_in_g, v_ln_in_b, v_w_in, v_conv_dw_w, v_conv_dw_b, v_conv_ln_g, v_conv_ln_b, v_w_conv_out, v_ssm_conv_w, v_ssm_conv_b, v_ssm_dt_bias, v_ssm_a_log, v_ssm_d, v_ssm_norm_w, v_w_ssm_out, v_w_o, v_ln1_g, v_ln1_b, v_w_ffn_up, v_ffn_dw_w, v_ffn_dw_b, v_w_ffn_down, v_ln2_g, v_ln2_b):
    args = locals()
    w = {n: args[n] for n in _WEIGHTS}
    mom = {n: args["m_" + n] for n in _WEIGHTS}
    vel = {n: args["v_" + n] for n in _WEIGHTS}
    T = x.shape[1]
    tm = min(TM, T)
    nt = T // tm
    chip = 2 * lax.axis_index("x") + lax.axis_index("y")

    assert DEPTH == 2
    big_names, small_names = list(_BIG), list(_SMALL_SHARDED)
    rest_big = [n for n in big_names if n != "w_in"]
    bf = {n: w[n].astype(BF16) for n in big_names}
    join = lambda got, axis: jnp.concatenate([got[j] for j in range(NCHIP)], axis=axis)
    first = _chip_exchange([bf["w_in"][0]] + [w[n] for n in small_names], scatter=False, name="gather_first")
    full = {n: join(gk, _SMALL_SHARDED[n]) for n, gk in zip(small_names, first[1:])}
    for n in _REPLICATED:
        full[n] = w[n]
    Ws = [_small_layer_weights(full, l) for l in range(DEPTH)]
    Ws[0].update(_w_in_layout(join(first[0], 1)))

    def gather(names, l):
        def done(got):
            for n, gk in zip(names, got):
                Ws[l][_MM_KEY[n]] = join(gk, _BIG[n] - 1)
        return lambda gr: (_exchange_comm([bf[n][l] for n in names], scatter=False), done)

    mixer = ["w_conv_out", "w_ssm_out", "w_o"]
    carry_fwd = [
        {"u": gather(mixer, 0), "conf": gather(["w_ffn_up"], 0), "xbc": gather(["w_ffn_down"], 0),
         "ssd": lambda gr: (_exchange_comm([bf["w_in"][1]], scatter=False),
                            lambda got: Ws[1].update(_w_in_layout(join(got[0], 1)))),
         "ffnact": gather(mixer, 1), "up": gather(["w_ffn_down"], 1)},
        {"u": gather(["w_ffn_up"], 1)},
    ]

    x2 = x.reshape(T, D)
    g_in, b_in = ln_in_g.reshape(1, D), ln_in_b.reshape(1, D)
    h, hb = _rows_fwd(_two_copies(_ln_fn), [_r2(x2, tm)], [g_in, b_in], _copies_out(T, tm), name="ln_in", nt=nt)
    saved = []
    for l in range(DEPTH):
        h, hb, sv = _layer_fwd(h, hb, Ws[l], l, carry_fwd[l])
        saved.append(sv)
    loss_row, dh = _loss_head(h, loss_target.reshape(T, D), name="loss")

    arrived = {}

    def exchange(names, l, grads):
        def make(gr):
            src = grads(gr)
            def done(got):
                for n, gk in zip(names, got):
                    arrived[(n, l)] = gk
            return _exchange_comm([_split_chips(src[n], _BIG[n] - 1) for n in names], scatter=True), done
        return make

    layer_grads = [None] * DEPTH
    dh, gr = _layer_bwd(dh, Ws[1], saved[1], 1, {})
    layer_grads[1] = _layer_grads_to_reference_layout(gr)
    g1 = lambda gr: layer_grads[1]
    g0 = lambda gr: {n: gr[_MM_KEY[n]] for n in rest_big}
    dh, gr = _layer_bwd(dh, Ws[0], saved[0], 0, {
        "ffnact_b": exchange(["w_in"], 1, g1),
        "conf_b": exchange(["w_ffn_up", "w_ffn_down"], 1, g1),
        "ssd_b": exchange(["w_conv_out", "w_ssm_out", "w_o"], 1, g1),
        "u_dw": exchange(rest_big, 0, g0),
        "u_dx": exchange(["w_in"], 0, lambda gr: {"w_in": _w_in_grad(gr)})})
    layer_grads[0] = _layer_grads_to_reference_layout(gr)
    (grad_x2,), (d_g_in, d_b_in) = _rows_bwd(_ln_fn, [_r2(x2, tm)], [g_in, b_in], [_r2(dh, tm)], [_o2(T, D, F32, tm)],
                                             name="ln_in_b", nt=nt)
    local = {n: jnp.stack([layer_grads[l][n] for l in range(DEPTH)]) for n in _WEIGHTS[2:] if n not in _BIG}
    local["ln_in_g"], local["ln_in_b"] = d_g_in[0], d_b_in[0]
    res = [{}, {}, {}, {}]

    keys = [(n, l) for n in big_names for l in range(DEPTH)]
    mine = [_sum_slots(arrived[k], name=f"sum_chips_{k[0]}_{k[1]}") for k in keys]
    other = _core_swap(mine, name="swap_cores")
    for i, n in enumerate(big_names):
        outs = _adamw_layers(mine[2 * i:2 * i + 2], other[2 * i:2 * i + 2], w[n], mom[n], vel[n], name="adamw_" + n)
        for q in range(4):
            res[q][n] = outs[q]

    rest_names = list(_REPLICATED) + small_names
    part = _pack([loss_row] + [local[n] for n in rest_names], F32, 8)
    parts = _all_gather8(part, name="gather_small")
    total = _sum_slots(parts, name="sum_devices")
    tot = _unpack(total, [loss_row.shape] + [local[n].shape for n in rest_names])
    loss = tot[0][0, 0]
    g_rest = {}
    for n, t in zip(rest_names, tot[1:]):
        if n in _SMALL_SHARDED:
            ax = _SMALL_SHARDED[n]
            t = lax.dynamic_slice_in_dim(t, chip * w[n].shape[ax], w[n].shape[ax], axis=ax)
        g_rest[n] = t
    pk = lambda d: _pack([d[n] for n in rest_names], F32, 8)
    rest_out = _adamw([pk(g_rest)], pk(w), pk(mom), pk(vel), name="adamw_rest")
    rest_out = [_unpack(o, [w[n].shape for n in rest_names]) for o in rest_out]

    for q in range(4):
        for k, n in enumerate(rest_names):
            res[q][n] = rest_out[q][k]
    grad_x = grad_x2.reshape(x.shape)
    return (loss, grad_x, *[res[0][n] for n in _WEIGHTS], *[res[1][n] for n in _WEIGHTS],
            *[res[2][n] for n in _WEIGHTS], *[res[3][n] for n in _WEIGHTS])
```

```python
import functools
import math

import jax
import jax.numpy as jnp
from jax import lax
from jax.experimental import pallas as pl
from jax.experimental.pallas import tpu as pltpu

F32 = jnp.float32
BF16 = jnp.bfloat16

D = 1024
DEPTH = 2
CONV_K = 31
SD = 2 * D
P = 64
H = SD // P
G = 4
R = H // G
N = 128
RP = R * P
SSM_K = 4
L = 128
XBC = SD + 2 * G * N
FFN = 2816
FFN_K = 3
IN_DIM = 2 * D + SD + XBC + H + 2 * D
ALPHA = (2 * DEPTH) ** 0.25
LN_EPS = 1e-5
RMS_EPS = 1e-5
ADAM_LR, ADAM_B1, ADAM_B2, ADAM_EPS, ADAM_WD, ADAM_STEP = 0.001, 0.9, 0.999, 1e-08, 0.01, 10

HP = 128
NCHIP = 4
PACK_W = 1024
VMEM_LIMIT = 56 * 1024 * 1024
TM = 512
TM_X = 256
TM_FFN = 256
TK = 1024

assert D == 2 * RP and 2 * G * N == D and XBC == 3 * D and H <= HP


def _pcall(body, *, name, grid=(), in_specs, out_specs, out_shape, scratch_shapes=()):
    params = pltpu.CompilerParams(vmem_limit_bytes=VMEM_LIMIT, dimension_semantics=("arbitrary",) * len(grid))
    return pl.pallas_call(body, name=name, grid=grid, in_specs=in_specs, out_specs=out_specs, out_shape=out_shape,
                          scratch_shapes=list(scratch_shapes), compiler_params=params)


def _pcall_carrying(body, comm, *, name, grid, in_specs, out_specs, out_shape, scratch_shapes=()):
    in_specs, out_specs, out_shape = list(in_specs), list(out_specs), list(out_shape)
    scratch_shapes = list(scratch_shapes)
    n_in, n_out, n_scr = len(in_specs), len(out_specs), len(scratch_shapes)
    nci, nco = len(comm["ins"]), len(comm["outs"])

    def wrapped(*refs):
        ins, cin = refs[:n_in], refs[n_in:n_in + nci]
        outs = refs[n_in + nci:n_in + nci + n_out]
        cout = refs[n_in + nci + n_out:n_in + nci + n_out + nco]
        scr = refs[n_in + nci + n_out + nco:n_in + nci + n_out + nco + n_scr]
        csem = refs[n_in + nci + n_out + nco + n_scr:]
        ids = [pl.program_id(ax) for ax in range(len(grid))]
        first = functools.reduce(jnp.logical_and, [i == 0 for i in ids])
        last = functools.reduce(jnp.logical_and, [i == g - 1 for i, g in zip(ids, grid)])

        @pl.when(first)
        def _():
            comm["start"](cin, cout, csem)

        body(*ins, *outs, *scr)

        @pl.when(last)
        def _():
            comm["wait"](cin, cout, csem)

    call = _pcall(wrapped, name=name, grid=grid, in_specs=in_specs + [_HBM] * nci, out_specs=out_specs + [_HBM] * nco,
                  out_shape=out_shape + list(comm["outs"]), scratch_shapes=scratch_shapes + list(comm["sems"]))

    def run(*operands):
        res = call(*operands, *comm["ins"])
        return list(res[:n_out]), list(res[n_out:])

    return run


def _pcall_maybe_carrying(body, comm, **kw):
    if comm is not None:
        return _pcall_carrying(body, comm, **kw)
    call = _pcall(body, **kw)
    return lambda *operands: (list(call(*operands)), None)


def _ccall(body, *, name, in_specs, out_specs, out_shape, scratch_shapes):
    return pl.pallas_call(body, name=name, in_specs=in_specs, out_specs=out_specs, out_shape=out_shape,
                          scratch_shapes=list(scratch_shapes))


def _full_spec(a):
    nd = a.ndim
    return pl.BlockSpec(a.shape, lambda *_: (0,) * nd)


def _sds(shape, dtype):
    return jax.ShapeDtypeStruct(tuple(shape), dtype)


def _mm(a, b, *, name, grid, a_spec, b_spec, o_spec, out_shape, acc_shape, trans_a=False, trans_b=False, add=None,
        add_spec=None, comm=None):
    nk = grid[2]
    dn = (((0 if trans_a else 1,), (1 if trans_b else 0,)), ((), ()))
    has_add = add is not None

    def body(*refs):
        a_ref, b_ref = refs[0], refs[1]
        add_ref = refs[2] if has_add else None
        o_ref = refs[3] if has_add else refs[2]
        part = lax.dot_general(a_ref[...].astype(BF16), b_ref[...].astype(BF16), dn, preferred_element_type=F32)

        def finish(res):
            if has_add:
                res = res + add_ref[...]
            o_ref[...] = res.astype(o_ref.dtype)

        if nk == 1:
            finish(part)
        else:
            acc = refs[-1]
            k = pl.program_id(2)

            @pl.when(k == 0)
            def _():
                acc[...] = part

            @pl.when(k > 0)
            def _():
                acc[...] += part

            @pl.when(k == nk - 1)
            def _():
                finish(acc[...])

    ins = [a, b] + ([add] if has_add else [])
    specs = [a_spec, b_spec] + ([add_spec] if has_add else [])
    scratch = [pltpu.VMEM(acc_shape, F32)] if nk > 1 else []
    if comm is not None:
        (out,), got = _pcall_carrying(body, comm, name=name, grid=grid, in_specs=specs, out_specs=[o_spec],
                                      out_shape=[out_shape], scratch_shapes=scratch)(*ins)
        return out, got
    return _pcall(body, name=name, grid=grid, in_specs=specs, out_specs=o_spec, out_shape=out_shape,
                  scratch_shapes=scratch)(*ins)


def _mm_nn(a, b, *, name, out_dtype=F32, tn=None, tk=None, add=None):
    M, K = a.shape
    Nn = b.shape[1]
    tm = min(TM, M)
    tn = Nn if tn is None else tn
    tk = K if tk is None else tk
    grid = (M // tm, Nn // tn, K // tk)
    return _mm(a, b, name=name, grid=grid,
               a_spec=pl.BlockSpec((tm, tk), lambda i, j, k: (i, k)),
               b_spec=pl.BlockSpec((tk, tn), lambda i, j, k: (k, j)),
               o_spec=pl.BlockSpec((tm, tn), lambda i, j, k: (i, j)),
               out_shape=_sds((M, Nn), out_dtype), acc_shape=(tm, tn), add=add,
               add_spec=pl.BlockSpec((tm, tn), lambda i, j, k: (i, j)))


def _mm_nt(a, b, *, name, add=None):
    M, K = a.shape
    Nn = b.shape[0]
    tm = min(TM, M)
    return _mm(a, b, name=name, grid=(M // tm, 1, 1), trans_b=True,
               a_spec=pl.BlockSpec((tm, K), lambda i, j, k: (i, 0)),
               b_spec=pl.BlockSpec((Nn, K), lambda i, j, k: (0, 0)),
               o_spec=pl.BlockSpec((tm, Nn), lambda i, j, k: (i, 0)),
               out_shape=_sds((M, Nn), F32), acc_shape=(tm, Nn), add=add,
               add_spec=pl.BlockSpec((tm, Nn), lambda i, j, k: (i, 0)))


def _mm_resident_slab_out(a, w, *, name, width, tm, out_dtype=F32, comm=None):
    M, K = a.shape
    S = w.shape[1] // width
    tm = min(tm, M)

    def body(a_ref, w_hbm, o_ref, w_vmem, sem):
        @pl.when(pl.program_id(0) == 0)
        def _():
            cp = pltpu.make_async_copy(w_hbm, w_vmem, sem)
            cp.start()
            cp.wait()

        av = a_ref[...].astype(BF16)
        for s in range(S):
            o_ref[s] = jnp.dot(av, w_vmem[:, s * width:(s + 1) * width], preferred_element_type=F32).astype(o_ref.dtype)

    kw = dict(name=name, grid=(M // tm,),
              in_specs=[pl.BlockSpec((tm, K), lambda i: (i, 0)), pl.BlockSpec(memory_space=pl.ANY)],
              scratch_shapes=[pltpu.VMEM(w.shape, w.dtype), pltpu.SemaphoreType.DMA])
    o_spec = pl.BlockSpec((S, tm, width), lambda i: (0, i, 0))
    if comm is not None:
        (out,), got = _pcall_carrying(body, comm, out_specs=[o_spec], out_shape=[_sds((S, M, width), out_dtype)], **kw)(a, w)
        return out, got
    return _pcall(body, out_specs=o_spec, out_shape=_sds((S, M, width), out_dtype), **kw)(a, w)


def _mm_cols_nt(a_list, w, *, name, add, comm=None):
    S = len(a_list)
    M, width = a_list[0].shape
    Nn = w.shape[0]
    tm = min(TM, M)

    def body(*refs):
        a_refs, w_hbm, add_ref, o_ref, w_vmem, sem = refs[:S], refs[S], refs[S + 1], refs[S + 2], refs[S + 3], refs[S + 4]

        @pl.when(pl.program_id(0) == 0)
        def _():
            cp = pltpu.make_async_copy(w_hbm, w_vmem, sem)
            cp.start()
            cp.wait()

        acc = add_ref[...]
        for s in range(S):
            acc = acc + lax.dot_general(a_refs[s][...].astype(BF16), w_vmem[:, s * width:(s + 1) * width],
                                        (((1,), (1,)), ((), ())), preferred_element_type=F32)
        o_ref[...] = acc

    row = lambda C: pl.BlockSpec((tm, C), lambda i: (i, 0))
    kw = dict(name=name, grid=(M // tm,), in_specs=[row(width)] * S + [pl.BlockSpec(memory_space=pl.ANY), row(Nn)],
              scratch_shapes=[pltpu.VMEM(w.shape, w.dtype), pltpu.SemaphoreType.DMA])
    if comm is not None:
        (out,), got = _pcall_carrying(body, comm, out_specs=[row(Nn)], out_shape=[_sds((M, Nn), F32)], **kw)(*a_list, w, add)
        return out, got
    return _pcall(body, out_specs=row(Nn), out_shape=_sds((M, Nn), F32), **kw)(*a_list, w, add)


def _mm_tn_cols(a, b_list, *, name, tmo, comm=None):
    T, M = a.shape
    S = len(b_list)
    width = b_list[0].shape[1]
    tk = min(TK, T)
    nk = T // tk

    def body(*refs):
        a_ref, b_refs, o_ref, acc = refs[0], refs[1:S + 1], refs[S + 1], refs[S + 2]
        j, k = pl.program_id(1), pl.program_id(2)
        for s in range(S):
            @pl.when(j == s)
            def _():
                part = lax.dot_general(a_ref[...], b_refs[s][...], (((0,), (0,)), ((), ())), preferred_element_type=F32)

                @pl.when(k == 0)
                def _():
                    acc[...] = part

                @pl.when(k > 0)
                def _():
                    acc[...] += part

        @pl.when(k == nk - 1)
        def _():
            o_ref[...] = acc[...].astype(o_ref.dtype)

    def b_spec(s):
        return pl.BlockSpec((tk, width), lambda i, j, k: (jnp.where(j == s, k, jnp.where(j < s, 0, nk - 1)), 0))

    kw = dict(name=name, grid=(M // tmo, S, nk),
              in_specs=[pl.BlockSpec((tk, tmo), lambda i, j, k: (k, i))] + [b_spec(s) for s in range(S)],
              scratch_shapes=[pltpu.VMEM((tmo, width), F32)])
    o_spec = pl.BlockSpec((tmo, width), lambda i, j, k: (i, j))
    if comm is not None:
        (out,), got = _pcall_carrying(body, comm, out_specs=[o_spec], out_shape=[_sds((M, S * width), BF16)], **kw)(a, *b_list)
        return out, got
    return _pcall(body, out_specs=o_spec, out_shape=_sds((M, S * width), BF16), **kw)(a, *b_list)


def _mm_tn(a, b, *, name, tmo, tn=None, comm=None):
    T, M = a.shape
    Nn = b.shape[1]
    tn = Nn if tn is None else tn
    tk = min(2 * TK, T)
    return _mm(a, b, name=name, grid=(M // tmo, Nn // tn, T // tk), trans_a=True,
               a_spec=pl.BlockSpec((tk, tmo), lambda i, j, k: (k, i)),
               b_spec=pl.BlockSpec((tk, tn), lambda i, j, k: (k, j)),
               o_spec=pl.BlockSpec((tmo, tn), lambda i, j, k: (i, j)),
               out_shape=_sds((M, Nn), BF16), acc_shape=(tmo, tn), comm=comm)


def _r2(a, tm):
    return (a, (tm, a.shape[1]), lambda i: (i, 0))


def _slab(a3, s, tm):
    return (a3, (None, tm, a3.shape[2]), lambda i: (s, i, 0))


def _o2(T, C, dtype, tm):
    return ((T, C), dtype, (tm, C), lambda i: (i, 0))


def _rows_fwd(fn, row_ins, par_ins, outs, *, name, nt):
    nr, npar = len(row_ins), len(par_ins)

    def body(*refs):
        vals = [r[...] for r in refs[:nr + npar]]
        res = fn(*vals)
        for o_ref, v in zip(refs[nr + npar:], res):
            o_ref[...] = v.astype(o_ref.dtype)

    return _pcall(body, name=name, grid=(nt,),
                  in_specs=[pl.BlockSpec(bs, im) for (_, bs, im) in row_ins] + [_full_spec(p) for p in par_ins],
                  out_specs=[pl.BlockSpec(bs, im) for (_, _, bs, im) in outs],
                  out_shape=[_sds(s, d) for (s, d, _, _) in outs])(*[r[0] for r in row_ins], *par_ins)


def _rows_bwd(fn, row_ins, par_ins, cot_ins, drow_outs, *, name, nt):
    nr, npar, nc = len(row_ins), len(par_ins), len(cot_ins)
    keep = [k for k, o in enumerate(drow_outs) if o is not None]

    def body(*refs):
        vals = [r[...].astype(F32) for r in refs[:nr + npar]]
        cots = [r[...].astype(F32) for r in refs[nr + npar:nr + npar + nc]]
        orefs = refs[nr + npar + nc:]
        _, vjp = jax.vjp(fn, *vals)
        grads = vjp(tuple(cots))
        for o_ref, k in zip(orefs[:len(keep)], keep):
            o_ref[...] = grads[k].astype(o_ref.dtype)
        prefs = orefs[len(keep):]

        @pl.when(pl.program_id(0) == 0)
        def _():
            for p_ref in prefs:
                p_ref[...] = jnp.zeros_like(p_ref)

        for p_ref, g in zip(prefs, grads[nr:]):
            p_ref[...] += g

    outs = [drow_outs[k] for k in keep]
    res = _pcall(body, name=name, grid=(nt,),
                 in_specs=[pl.BlockSpec(bs, im) for (_, bs, im) in row_ins] + [_full_spec(p) for p in par_ins]
                 + [pl.BlockSpec(bs, im) for (_, bs, im) in cot_ins],
                 out_specs=[pl.BlockSpec(bs, im) for (_, _, bs, im) in outs] + [_full_spec(p) for p in par_ins],
                 out_shape=[_sds(s, d) for (s, d, _, _) in outs] + [_sds(p.shape, F32) for p in par_ins],
                 )(*[r[0] for r in row_ins], *par_ins, *[c[0] for c in cot_ins])
    return list(res[:len(keep)]), list(res[len(keep):])


def _layer_norm(v, g, b):
    mu = jnp.mean(v, axis=-1, keepdims=True)
    var = jnp.mean(jnp.square(v - mu), axis=-1, keepdims=True)
    return (v - mu) * lax.rsqrt(var + LN_EPS) * g + b


def _silu(v):
    return v * jax.nn.sigmoid(v)


def _softplus(v):
    return jnp.maximum(v, 0.0) + jnp.log1p(jnp.exp(-jnp.abs(v)))


def _halo_of(K):
    return 8 * ((K - 1 + 7) // 8)


DW_HALO = 8
DW_RB = 16
DW_LC = 256


def _dw_taps(win, w_ref, p, ls, K, shift_of):
    acc = None
    for k in range(K):
        o = shift_of(k)
        term = win[o:o + DW_RB, :] * w_ref[p, k:k + 1, ls]
        acc = term if acc is None else acc + term
    return acc


def _dwconv_fwd(post, part_ins, w, biases, outs, *, K, C, name, tm, nt, comm=None):
    nparts, halo = len(part_ins), DW_HALO

    def body(*refs):
        x_refs, w_ref = refs[:nparts], refs[nparts]
        b_refs = refs[nparts + 1:2 * nparts + 1]
        orefs, buf = refs[2 * nparts + 1:-1], refs[-1]
        i = pl.program_id(0)
        for p in range(nparts):
            @pl.when(i == 0)
            def _():
                buf[p, pl.ds(0, halo), :] = jnp.zeros((halo, C), F32)

            @pl.when(i > 0)
            def _():
                buf[p, pl.ds(0, halo), :] = buf[p, pl.ds(tm, halo), :]

            buf[p, pl.ds(halo, tm), :] = x_refs[p][...].astype(F32)

        def group(r, carry):
            base = pl.multiple_of(r * DW_RB, DW_RB)
            for cj in range(C // DW_LC):
                ls = slice(cj * DW_LC, (cj + 1) * DW_LC)
                cs = [_dw_taps(buf[p, pl.ds(base, DW_RB + halo), ls], w_ref, p, ls, K, lambda k: halo - (K - 1) + k)
                      + b_refs[p][:, ls] for p in range(nparts)]
                for o_ref, v in zip(orefs, post(cs)):
                    o_ref[pl.ds(base, DW_RB), ls] = v.astype(o_ref.dtype)
            return carry

        lax.fori_loop(0, tm // DW_RB, group, 0)

    res, got = _pcall_maybe_carrying(
        body, comm, name=name, grid=(nt,),
        in_specs=[pl.BlockSpec(bs, im) for (_, bs, im) in part_ins] + [_full_spec(w)] + [_full_spec(b) for b in biases],
        out_specs=[pl.BlockSpec(bs, im) for (_, _, bs, im) in outs],
        out_shape=[_sds(s, d) for (s, d, _, _) in outs],
        scratch_shapes=[pltpu.VMEM((nparts, halo + tm, C), F32)])(*[r[0] for r in part_ins], w, *biases)
    return res if comm is None else (res, got)


def _dwconv_bwd(post, part_ins, halo_ins, w, biases, cot_ins, *, K, C, name, tm, nt, comm=None):
    nparts, halo, nc, RB = len(part_ins), DW_HALO, len(cot_ins), DW_RB
    T = nt * tm

    def body(*refs):
        x_refs, h_refs, w_ref = refs[:nparts], refs[nparts:2 * nparts], refs[2 * nparts]
        b_refs = refs[2 * nparts + 1:3 * nparts + 1]
        cot_refs = refs[3 * nparts + 1:3 * nparts + 1 + nc]
        rest = refs[3 * nparts + 1 + nc:]
        dx_refs, dw_ref, db_refs = rest[:nparts], rest[nparts], rest[nparts + 1:2 * nparts + 1]
        bufx, bufd, acc = rest[-3], rest[-2], rest[-1]
        s = pl.program_id(0)
        first_tile = s == nt - 1

        @pl.when(s == 0)
        def _():
            acc[...] = jnp.zeros_like(acc)
            for p in range(nparts):
                bufd[p, pl.ds(tm, halo), :] = jnp.zeros((halo, C), F32)

        for p in range(nparts):
            hrows = h_refs[p].shape[0]
            bufx[p, pl.ds(0, halo), :] = jnp.where(first_tile, 0.0, h_refs[p][...].astype(F32)[hrows - halo:hrows, :])
            bufx[p, pl.ds(halo, tm), :] = x_refs[p][...].astype(F32)
        fold = lambda v: v[0:8, :] + v[8:16, :]

        def conv_out_grads(r, carry):
            base = pl.multiple_of(r * RB, RB)
            for cj in range(C // DW_LC):
                ls = slice(cj * DW_LC, (cj + 1) * DW_LC)
                wins = [bufx[p, pl.ds(base, RB + halo), ls] for p in range(nparts)]
                cs = [_dw_taps(wins[p], w_ref, p, ls, K, lambda k: halo - (K - 1) + k) + b_refs[p][:, ls]
                      for p in range(nparts)]
                _, vjp = jax.vjp(lambda *c: post(list(c)), *cs)
                dcs = vjp(tuple(cr[pl.ds(base, RB), ls].astype(F32) for cr in cot_refs))
                for p in range(nparts):
                    bufd[p, pl.ds(base, RB), ls] = dcs[p]
                    for k in range(K):
                        o = halo - (K - 1) + k
                        acc[p, k, :, ls] += fold(dcs[p] * wins[p][o:o + RB, :])
                    acc[p, K, :, ls] += fold(dcs[p])
            return carry

        lax.fori_loop(0, tm // RB, conv_out_grads, 0)

        def input_grads(r, carry):
            base = pl.multiple_of(r * RB, RB)
            for cj in range(C // DW_LC):
                ls = slice(cj * DW_LC, (cj + 1) * DW_LC)
                for p in range(nparts):
                    dx = _dw_taps(bufd[p, pl.ds(base, RB + halo), ls], w_ref, p, ls, K, lambda k: K - 1 - k)
                    dx_refs[p][pl.ds(base, RB), ls] = dx.astype(dx_refs[p].dtype)
            return carry

        lax.fori_loop(0, tm // RB, input_grads, 0)
        for p in range(nparts):
            bufd[p, pl.ds(tm, halo), :] = bufd[p, pl.ds(0, halo), :]

        @pl.when(s == nt - 1)
        def _():
            dw_ref[...] = jnp.zeros_like(dw_ref)
            for p in range(nparts):
                for k in range(K):
                    dw_ref[p, k:k + 1, :] = jnp.sum(acc[p, k], axis=0, keepdims=True)
                db_refs[p][...] = jnp.sum(acc[p, K], axis=0, keepdims=True)

    rev = lambda im: (lambda s: im(nt - 1 - s))
    row = pl.BlockSpec((tm, C), lambda s: (nt - 1 - s, 0))
    res, got = _pcall_maybe_carrying(
        body, comm, name=name, grid=(nt,),
        in_specs=[pl.BlockSpec(bs, rev(im)) for (_, bs, im) in part_ins]
        + [pl.BlockSpec(bs, rev(im)) for (_, bs, im) in halo_ins]
        + [_full_spec(w)] + [_full_spec(b) for b in biases]
        + [pl.BlockSpec(bs, rev(im)) for (_, bs, im) in cot_ins],
        out_specs=[row] * nparts + [_full_spec(w)] + [_full_spec(b) for b in biases],
        out_shape=[_sds((T, C), BF16)] * nparts + [_sds(w.shape, F32)] + [_sds(b.shape, F32) for b in biases],
        scratch_shapes=[pltpu.VMEM((nparts, halo + tm, C), F32), pltpu.VMEM((nparts, tm + halo, C), F32),
                        pltpu.VMEM((nparts, K + 1, 8, C), F32)],
    )(*[r[0] for r in part_ins], *[r[0] for r in halo_ins], w, *biases, *[c[0] for c in cot_ins])
    out = (list(res[:nparts]), res[nparts], list(res[nparts + 1:]))
    return out if comm is None else (out, got)


def _halo_slab(a3, s, tm, halo):
    rows = max(halo, 16) if a3.dtype == BF16 else halo
    q = tm // rows
    return (a3, (None, rows, a3.shape[2]), lambda i: (s, jnp.maximum(i * q - 1, 0), 0))


CONF_HALO = _halo_of(CONV_K)
CONF_RB = 32


def _shifted_copies(buf, shifted, rows):
    for j in range(1, 8):
        shifted[j - 1, pl.ds(0, rows), :] = buf[pl.ds(j, rows), :]


def _shifted_rows(buf, shifted, s, base, nrows):
    j, q = s % 8, s // 8
    if j == 0:
        return buf[pl.ds(base + 8 * q, nrows), :]
    return shifted[j - 1, pl.ds(base + 8 * q, nrows), :]


def _conf_fwd(u3, w, cb, lg, lb, *, name, comm=None):
    T = u3.shape[1]
    tm = min(TM_X, T)
    nt = T // tm
    K, halo, RB = CONV_K, CONF_HALO, min(CONF_RB, tm)

    def body(a_ref, g_ref, w_ref, cb_ref, lg_ref, lb_ref, v3_ref, c_ref, bufx, xs):
        i = pl.program_id(0)

        @pl.when(i == 0)
        def _():
            bufx[pl.ds(0, halo), :] = jnp.zeros((halo, D), F32)

        @pl.when(i > 0)
        def _():
            bufx[pl.ds(0, halo), :] = bufx[pl.ds(tm, halo), :]

        bufx[pl.ds(halo, tm), :] = _conf_pre(a_ref[...].astype(F32), g_ref[...].astype(F32))[0]
        _shifted_copies(bufx, xs, halo + tm - 8)

        def group(r, carry):
            base = pl.multiple_of(r * RB, RB)
            acc = None
            for k in range(K):
                term = _shifted_rows(bufx, xs, halo - (K - 1) + k, base, RB) * w_ref[k:k + 1, :]
                acc = term if acc is None else acc + term
            c_ref[pl.ds(base, RB), :] = acc
            return carry

        lax.fori_loop(0, tm // RB, group, 0)
        v3_ref[...] = _conf_post([c_ref[...]], cb_ref[...], lg_ref[...], lb_ref[...])[0].astype(v3_ref.dtype)

    slab = lambda s: pl.BlockSpec((None, tm, D), lambda i: (s, i, 0))
    row = pl.BlockSpec((tm, D), lambda i: (i, 0))
    pars = [w, cb, lg, lb]
    res, got = _pcall_maybe_carrying(
        body, comm, name=name, grid=(nt,), in_specs=[slab(0), slab(1)] + [_full_spec(p) for p in pars],
        out_specs=[row, row], out_shape=[_sds((T, D), BF16), _sds((T, D), F32)],
        scratch_shapes=[pltpu.VMEM((halo + tm, D), F32), pltpu.VMEM((7, halo + tm - 8, D), F32)])(u3, u3, *pars)
    return res if comm is None else (res, got)


def _conf_bwd(u3, c, dv3, w, cb, lg, lb, *, name, comm=None):
    T = u3.shape[1]
    tm = min(TM_X, T)
    nt = T // tm
    K, halo, RB = CONV_K, CONF_HALO, min(CONF_RB, tm)

    def body(a_ref, g_ref, c_ref, dv3_ref, w_ref, cb_ref, lg_ref, lb_ref,
             da_ref, dg_ref, dw_ref, dcb_ref, dlg_ref, dlb_ref, xbuf, bufd, ds, dv0):
        s = pl.program_id(0)

        @pl.when(s == 0)
        def _():
            for r in (dw_ref, dcb_ref, dlg_ref, dlb_ref):
                r[...] = jnp.zeros_like(r)
            bufd[pl.ds(tm, halo), :] = jnp.zeros((halo, D), F32)

        xin, pre_vjp = jax.vjp(lambda p, q_: _conf_pre(p, q_)[0], a_ref[...].astype(F32), g_ref[...].astype(F32))
        xbuf[...] = xin

        _, post_vjp = jax.vjp(lambda cc, b_, g_, l_: _conf_post([cc], b_, g_, l_)[0],
                              c_ref[...], cb_ref[...], lg_ref[...], lb_ref[...])
        dc, dcb, dlg, dlb = post_vjp(dv3_ref[...])
        dcb_ref[...] += dcb
        dlg_ref[...] += dlg
        dlb_ref[...] += dlb
        bufd[pl.ds(0, tm), :] = dc
        _shifted_copies(bufd, ds, tm + halo - 8)

        def dx_group(r, carry):
            base = pl.multiple_of(r * RB, RB)
            acc = None
            for k in range(K):
                term = _shifted_rows(bufd, ds, K - 1 - k, base, RB) * w_ref[k:k + 1, :]
                acc = term if acc is None else acc + term
            dv0[pl.ds(base, RB), :] = acc
            return carry

        lax.fori_loop(0, tm // RB, dx_group, 0)

        for k in range(K):
            def dw_group(r, acc):
                base = pl.multiple_of(r * RB, RB)
                prod = xbuf[pl.ds(base, RB), :] * _shifted_rows(bufd, ds, K - 1 - k, base, RB)
                for v in range(RB // 8):
                    acc = acc + prod[v * 8:(v + 1) * 8, :]
                return acc

            acc = lax.fori_loop(0, tm // RB, dw_group, jnp.zeros((8, D), F32))
            dw_ref[k:k + 1, :] += jnp.sum(acc, axis=0, keepdims=True)

        bufd[pl.ds(tm, halo), :] = bufd[pl.ds(0, halo), :]
        da, dg = pre_vjp(dv0[...])
        da_ref[...] = da.astype(da_ref.dtype)
        dg_ref[...] = dg.astype(dg_ref.dtype)

    slab = lambda sl: pl.BlockSpec((None, tm, D), lambda s: (sl, nt - 1 - s, 0))
    row = pl.BlockSpec((tm, D), lambda s: (nt - 1 - s, 0))
    pars = [w, cb, lg, lb]
    res, got = _pcall_maybe_carrying(
        body, comm, name=name, grid=(nt,),
        in_specs=[slab(0), slab(1), row, row] + [_full_spec(p) for p in pars],
        out_specs=[row, row] + [_full_spec(p) for p in pars],
        out_shape=[_sds((T, D), BF16)] * 2 + [_sds(p.shape, F32) for p in pars],
        scratch_shapes=[pltpu.VMEM((tm, D), F32), pltpu.VMEM((tm + halo, D), F32),
                        pltpu.VMEM((7, tm + halo - 8, D), F32), pltpu.VMEM((tm, D), F32)],
    )(u3, u3, c, dv3, *pars)
    return res if comm is None else (res, got)


def _dg(a, b, ca, cb):
    return lax.dot_general(a.astype(BF16), b.astype(BF16), (((ca,), (cb,)), ((), ())), preferred_element_type=F32)


@jax.custom_vjp
def _dot_nn(a, b):
    return _dg(a, b, 1, 0)


_dot_nn.defvjp(lambda a, b: (_dg(a, b, 1, 0), (a, b)),
               lambda res, g: (_dg(g, res[1], 1, 1), _dg(res[0], g, 0, 0)))


@jax.custom_vjp
def _dot_nt(a, b):
    return _dg(a, b, 1, 1)


_dot_nt.defvjp(lambda a, b: (_dg(a, b, 1, 1), (a, b)),
               lambda res, g: (_dg(g, res[1], 1, 0), _dg(g, res[0], 0, 0)))


@jax.custom_vjp
def _dot_tn(a, b):
    return _dg(a, b, 0, 0)


_dot_tn.defvjp(lambda a, b: (_dg(a, b, 0, 0), (a, b)),
               lambda res, g: (_dg(res[1], g, 1, 1), _dg(res[0], g, 1, 0)))


def _split3(v):
    hi = v.astype(BF16)
    r = v - hi.astype(F32)
    mid = r.astype(BF16)
    return hi, mid, (r - mid.astype(F32)).astype(BF16)


def _x01(v, m, cv, cm, m_left=False):
    acc = None
    for piece in _split3(v):
        t = _dg(m, piece, cm, cv) if m_left else _dg(piece, m, cv, cm)
        acc = t if acc is None else acc + t
    return acc


@jax.custom_vjp
def _expand01(v, m):
    return _x01(v, m, 1, 0)


_expand01.defvjp(lambda v, m: (_x01(v, m, 1, 0), m),
                 lambda m, g: (_x01(g, m, 1, 1), jnp.zeros_like(m)))


@jax.custom_vjp
def _mix01(m, v):
    return _x01(v, m, 0, 1, m_left=True)


_mix01.defvjp(lambda m, v: (_x01(v, m, 0, 1, m_left=True), m),
              lambda m, g: (jnp.zeros_like(m), _x01(g, m, 0, 0, m_left=True)))


def _causal():
    return lax.broadcasted_iota(jnp.int32, (L, L), 0) >= lax.broadcasted_iota(jnp.int32, (L, L), 1)


def _ssd_chunk_prep(dtr, alog, dtb):
    dt = _softplus(dtr + dtb)
    a_cs = _mix01(_causal().astype(F32), dt * (-jnp.exp(alog)))
    return dt, a_cs, a_cs.T


def _ssd_group(xs, dt, a_cs, a_csT, Bg, Cg, zg, sp, dsk, nwg, *, g):
    causal = _causal()
    hi = lax.broadcasted_iota(jnp.int32, (HP, RP), 0)
    ci = lax.broadcasted_iota(jnp.int32, (HP, RP), 1)
    lo = (hi - g * R) * P
    E = ((ci >= lo) & (ci < lo + P)).astype(F32)

    acs_e = _expand01(a_cs, E)
    dt_e = _expand01(dt, E)
    alast_e = acs_e[L - 1:L, :]
    xdt = xs * dt_e
    cb = _dot_nt(Cg, Bg)
    y_off = _dot_nn(Cg, sp) * jnp.exp(acs_e)
    yd = []
    for r in range(R):
        h = g * R + r
        seg = a_cs[:, h:h + 1] - a_csT[h:h + 1, :]
        dec = jnp.exp(jnp.where(causal, seg, -1e30))
        yd.append(_dot_nn(cb * dec, xdt[:, r * P:(r + 1) * P]))
    y = jnp.concatenate(yd, axis=1) + y_off + xs * _expand01(jnp.broadcast_to(dsk, (8, HP)), E)[0:1, :]
    yg = y * _silu(zg)
    yn = yg * lax.rsqrt(jnp.mean(jnp.square(yg), axis=-1, keepdims=True) + RMS_EPS) * nwg
    sc = _dot_tn(Bg, xdt * jnp.exp(alast_e - acs_e))
    return yn, jnp.exp(alast_e) * sp + sc


def _group_cols(g):
    return g // 2, (g % 2) * RP


def _ssd_fwd(x0, x1, bc, dtr, u3, alog, dtb, dsk, nw, *, name, comm=None):
    T = x0.shape[0]
    nc = T // L

    def body(x0_ref, x1_ref, bc_ref, dtr_ref, z0_ref, z1_ref, alog_ref, dtb_ref, dsk_ref, nw_ref, yn_ref, sp_ref, S):
        @pl.when(pl.program_id(0) == 0)
        def _():
            S[...] = jnp.zeros_like(S)

        xr, zr = (x0_ref, x1_ref), (z0_ref, z1_ref)
        dt, a_cs, a_csT = _ssd_chunk_prep(dtr_ref[...], alog_ref[...], dtb_ref[...])
        for g in range(G):
            s, off = _group_cols(g)
            sp = S[g]
            sp_ref[0, g] = sp
            yn, s_next = _ssd_group(xr[s][:, off:off + RP], dt, a_cs, a_csT, bc_ref[:, g * N:(g + 1) * N],
                                    bc_ref[:, G * N + g * N:G * N + (g + 1) * N], zr[s][:, off:off + RP].astype(F32), sp,
                                    dsk_ref[...], nw_ref[:, g * RP:(g + 1) * RP], g=g)
            yn_ref[:, g * RP:(g + 1) * RP] = yn.astype(yn_ref.dtype)
            S[g] = s_next

    row = lambda C: pl.BlockSpec((L, C), lambda c: (c, 0))
    zspec = lambda s: pl.BlockSpec((None, L, D), lambda c: (s, c, 0))
    pars = [alog, dtb, dsk, nw]
    res, got = _pcall_maybe_carrying(
        body, comm, name=name, grid=(nc,),
        in_specs=[row(D), row(D), row(D), row(HP), zspec(2), zspec(3)] + [_full_spec(p) for p in pars],
        out_specs=[row(SD), pl.BlockSpec((1, G, N, RP), lambda c: (c, 0, 0, 0))],
        out_shape=[_sds((T, SD), BF16), _sds((nc, G, N, RP), F32)],
        scratch_shapes=[pltpu.VMEM((G, N, RP), F32)])(x0, x1, bc, dtr, u3, u3, *pars)
    return res if comm is None else (res, got)


def _ssd_bwd(x0, x1, bc, dtr, u3, sprev, dyn, alog, dtb, dsk, nw, *, name, comm=None):
    T = x0.shape[0]
    nc = T // L

    def body(x0_ref, x1_ref, bc_ref, dtr_ref, z0_ref, z1_ref, sp_ref, dyn_ref, alog_ref, dtb_ref, dsk_ref, nw_ref,
             dx0_ref, dx1_ref, dbc_ref, ddtr_ref, dz0_ref, dz1_ref, dalog_ref, ddtb_ref, ddsk_ref, dnw_ref, dS):
        @pl.when(pl.program_id(0) == 0)
        def _():
            dS[...] = jnp.zeros_like(dS)
            for r in (dalog_ref, ddtb_ref, ddsk_ref, dnw_ref):
                r[...] = jnp.zeros_like(r)

        xr, zr = (x0_ref, x1_ref), (z0_ref, z1_ref)
        dxr, dzr = (dx0_ref, dx1_ref), (dz0_ref, dz1_ref)
        (dt, a_cs, a_csT), prep_vjp = jax.vjp(_ssd_chunk_prep, dtr_ref[...], alog_ref[...], dtb_ref[...])
        d_dt, d_acs, d_acsT = jnp.zeros((L, HP), F32), jnp.zeros((L, HP), F32), jnp.zeros((HP, L), F32)
        for g in range(G):
            s, off = _group_cols(g)
            _, vjp = jax.vjp(functools.partial(_ssd_group, g=g), xr[s][:, off:off + RP], dt, a_cs, a_csT,
                             bc_ref[:, g * N:(g + 1) * N], bc_ref[:, G * N + g * N:G * N + (g + 1) * N],
                             zr[s][:, off:off + RP].astype(F32), sp_ref[0, g], dsk_ref[...], nw_ref[:, g * RP:(g + 1) * RP])
            dxs, ddt_g, dacs_g, dacsT_g, dB, dC, dz, dsp, dds, dnwg = vjp((dyn_ref[:, g * RP:(g + 1) * RP], dS[g]))
            dxr[s][:, off:off + RP] = dxs
            dzr[s][:, off:off + RP] = dz.astype(dz0_ref.dtype)
            dbc_ref[:, g * N:(g + 1) * N] = dB
            dbc_ref[:, G * N + g * N:G * N + (g + 1) * N] = dC
            dS[g] = dsp
            d_dt, d_acs, d_acsT = d_dt + ddt_g, d_acs + dacs_g, d_acsT + dacsT_g
            ddsk_ref[...] += dds
            dnw_ref[:, g * RP:(g + 1) * RP] += dnwg
        ddtr, dal, ddb = prep_vjp((d_dt, d_acs, d_acsT))
        dalog_ref[...] += dal
        ddtb_ref[...] += ddb
        ddtr_ref[...] = ddtr.astype(ddtr_ref.dtype)

    row = lambda C: pl.BlockSpec((L, C), lambda c: (nc - 1 - c, 0))
    zspec = lambda s: pl.BlockSpec((None, L, D), lambda c: (s, nc - 1 - c, 0))
    pars = [alog, dtb, dsk, nw]
    res, got = _pcall_maybe_carrying(
        body, comm, name=name, grid=(nc,),
        in_specs=[row(D), row(D), row(D), row(HP), zspec(2), zspec(3),
                  pl.BlockSpec((1, G, N, RP), lambda c: (nc - 1 - c, 0, 0, 0)), row(SD)] + [_full_spec(p) for p in pars],
        out_specs=[row(D), row(D), row(D), row(HP), row(D), row(D)] + [_full_spec(p) for p in pars],
        out_shape=[_sds((T, D), F32)] * 3 + [_sds((T, HP), BF16), _sds((T, D), BF16), _sds((T, D), BF16)]
        + [_sds(p.shape, F32) for p in pars],
        scratch_shapes=[pltpu.VMEM((G, N, RP), F32)])(x0, x1, bc, dtr, u3, u3, sprev, dyn, *pars)
    return res if comm is None else (res, got)


def _loss_head(y, target, *, name):
    T = y.shape[0]
    tm = min(TM, T)

    def body(y_ref, t_ref, loss_ref, dy_ref):
        e = y_ref[...] - t_ref[...]
        dy_ref[...] = e * (1.0 / D)

        @pl.when(pl.program_id(0) == 0)
        def _():
            loss_ref[...] = jnp.zeros_like(loss_ref)

        loss_ref[...] += 0.5 * jnp.sum(jnp.mean(jnp.square(e), axis=-1, keepdims=True), axis=0, keepdims=True)

    row = pl.BlockSpec((tm, D), lambda i: (i, 0))
    return _pcall(body, name=name, grid=(T // tm,), in_specs=[row, row],
                  out_specs=[pl.BlockSpec((1, 128), lambda i: (0, 0)), row],
                  out_shape=[_sds((1, 128), F32), _sds((T, D), F32)])(y, target)


_HBM = pl.BlockSpec(memory_space=pltpu.HBM)
_MESH = pl.DeviceIdType.MESH


def _exchange_comm(bufs, *, scatter):
    nb = len(bufs)

    def copies(in_refs, out_refs, sems, with_arrivals):
        send_sems, recv_sems, local_sems = sems
        x, y, c = lax.axis_index("x"), lax.axis_index("y"), lax.axis_index("c")
        me = 2 * x + y
        peers = [(1 - x, y), (x, 1 - y), (1 - x, 1 - y)]
        own, sends, arrivals = [], [], []
        for b in range(nb):
            src_own = in_refs[b].at[me] if scatter else in_refs[b]
            own.append(pltpu.make_async_copy(src_own, out_refs[b].at[me], local_sems.at[b]))
            for k, (px, py) in enumerate(peers):
                src = in_refs[b].at[2 * px + py] if scatter else in_refs[b]
                sends.append(pltpu.make_async_remote_copy(
                    src_ref=src, dst_ref=out_refs[b].at[me], send_sem=send_sems.at[b, k], recv_sem=recv_sems.at[b, k],
                    device_id=(px, py, c), device_id_type=_MESH))
                if with_arrivals:
                    slot = out_refs[b].at[2 * px + py]
                    arrivals.append(pltpu.make_async_remote_copy(
                        src_ref=slot, dst_ref=slot, send_sem=send_sems.at[b, k], recv_sem=recv_sems.at[b, k],
                        device_id=(px, py, c), device_id_type=_MESH))
        return own, sends, arrivals

    def start(in_refs, out_refs, sems):
        own, sends, _ = copies(in_refs, out_refs, sems, False)
        for cp in own + sends:
            cp.start()

    def wait(in_refs, out_refs, sems):
        own, sends, arrivals = copies(in_refs, out_refs, sems, True)
        for cp in arrivals:
            cp.wait_recv()
        for cp in sends:
            cp.wait_send()
        for cp in own:
            cp.wait()

    return dict(ins=list(bufs), outs=[_sds(b.shape if scatter else (NCHIP,) + b.shape, b.dtype) for b in bufs],
                sems=[pltpu.SemaphoreType.DMA((nb, 3)), pltpu.SemaphoreType.DMA((nb, 3)), pltpu.SemaphoreType.DMA((nb,))],
                start=start, wait=wait)


def _chip_exchange(bufs, *, scatter, name):
    comm = _exchange_comm(bufs, scatter=scatter)
    nb = len(bufs)

    def body(*refs):
        comm["start"](refs[:nb], refs[nb:2 * nb], refs[2 * nb:])
        comm["wait"](refs[:nb], refs[nb:2 * nb], refs[2 * nb:])

    return _ccall(body, name=name, in_specs=[_HBM] * nb, out_specs=[_HBM] * nb, out_shape=comm["outs"],
                  scratch_shapes=comm["sems"])(*bufs)


def _core_swap(bufs, *, name):
    nb = len(bufs)

    def body(*refs):
        in_refs, out_refs, send_sems, recv_sems = refs[:nb], refs[nb:2 * nb], refs[2 * nb], refs[2 * nb + 1]
        x, y, c = lax.axis_index("x"), lax.axis_index("y"), lax.axis_index("c")
        cps = [pltpu.make_async_remote_copy(src_ref=in_refs[b], dst_ref=out_refs[b], send_sem=send_sems.at[b],
                                            recv_sem=recv_sems.at[b], device_id=(x, y, 1 - c), device_id_type=_MESH)
               for b in range(nb)]
        for cp in cps:
            cp.start()
        for cp in cps:
            cp.wait()

    return _ccall(body, name=name, in_specs=[_HBM] * nb, out_specs=[_HBM] * nb,
                  out_shape=[_sds(b.shape, b.dtype) for b in bufs],
                  scratch_shapes=[pltpu.SemaphoreType.DMA((nb,)), pltpu.SemaphoreType.DMA((nb,))])(*bufs)


def _all_gather8(buf, *, name):
    def body(in_ref, out_ref, send_sems, recv_sems, local_sem):
        x, y, c = lax.axis_index("x"), lax.axis_index("y"), lax.axis_index("c")
        me = 4 * x + 2 * y + c
        own = pltpu.make_async_copy(in_ref, out_ref.at[me], local_sem)
        own.start()
        flips = [(fx, fy, fc) for fx in (0, 1) for fy in (0, 1) for fc in (0, 1)][1:]
        peers = [(x ^ fx, y ^ fy, c ^ fc) for fx, fy, fc in flips]
        sends = []
        for k, peer in enumerate(peers):
            cp = pltpu.make_async_remote_copy(src_ref=in_ref, dst_ref=out_ref.at[me], send_sem=send_sems.at[k],
                                              recv_sem=recv_sems.at[k], device_id=peer, device_id_type=_MESH)
            cp.start()
            sends.append(cp)
        for k, (px, py, pc) in enumerate(peers):
            slot = out_ref.at[4 * px + 2 * py + pc]
            pltpu.make_async_remote_copy(src_ref=slot, dst_ref=slot, send_sem=send_sems.at[k], recv_sem=recv_sems.at[k],
                                         device_id=(px, py, pc), device_id_type=_MESH).wait_recv()
        for cp in sends:
            cp.wait_send()
        own.wait()

    return _ccall(body, name=name, in_specs=[_HBM], out_specs=_HBM, out_shape=_sds((8,) + buf.shape, buf.dtype),
                  scratch_shapes=[pltpu.SemaphoreType.DMA((7,)), pltpu.SemaphoreType.DMA((7,)), pltpu.SemaphoreType.DMA])(buf)


def _row_tile(rows, cap):
    if rows <= cap:
        return rows
    return max(t for t in range(16, cap + 1, 16) if rows % t == 0)


def _sum_slots(stack, *, name, cap=256):
    S, Rr, C = stack.shape
    tr = _row_tile(Rr, cap)

    def body(s_ref, o_ref):
        acc = s_ref[0].astype(F32)
        for j in range(1, S):
            acc = acc + s_ref[j].astype(F32)
        o_ref[...] = acc

    return _pcall(body, name=name, grid=(Rr // tr,), in_specs=[pl.BlockSpec((S, tr, C), lambda i: (0, i, 0))],
                  out_specs=pl.BlockSpec((tr, C), lambda i: (i, 0)), out_shape=_sds((Rr, C), F32))(stack)


def _adamw(g_parts, w, m, v, *, name, cap=128):
    Rr, C = w.shape
    tr = _row_tile(Rr, cap)
    ng = len(g_parts)
    c1 = 1.0 / (1.0 - ADAM_B1 ** ADAM_STEP)
    c2 = 1.0 / (1.0 - ADAM_B2 ** ADAM_STEP)

    def body(*refs):
        g = refs[0][...]
        for r in refs[1:ng]:
            g = g + r[...]
        w_ref, m_ref, v_ref, g_out, d_out, m_out, v_out = refs[ng:]
        mn = ADAM_B1 * m_ref[...] + (1.0 - ADAM_B1) * g
        vn = ADAM_B2 * v_ref[...] + (1.0 - ADAM_B2) * jnp.square(g)
        g_out[...] = g
        m_out[...] = mn
        v_out[...] = vn
        d_out[...] = -ADAM_LR * ((mn * c1) / (jnp.sqrt(vn * c2) + ADAM_EPS) + ADAM_WD * w_ref[...])

    spec = pl.BlockSpec((tr, C), lambda i: (i, 0))
    return _pcall(body, name=name, grid=(Rr // tr,), in_specs=[spec] * (ng + 3), out_specs=[spec] * 4,
                  out_shape=[_sds((Rr, C), F32)] * 4)(*g_parts, w, m, v)


def _adamw_layers(mine, other, w3, m3, v3, *, name, cap=128):
    _, Rr, C = w3.shape
    tr = _row_tile(Rr, cap)
    nt = Rr // tr
    c1 = 1.0 / (1.0 - ADAM_B1 ** ADAM_STEP)
    c2 = 1.0 / (1.0 - ADAM_B2 ** ADAM_STEP)

    def body(m0, m1, o0, o1, w_ref, m_ref, v_ref, g_out, d_out, m_out, v_out):
        g = jnp.where(pl.program_id(0) == 0, m0[...] + o0[...], m1[...] + o1[...])
        mn = ADAM_B1 * m_ref[...] + (1.0 - ADAM_B1) * g
        vn = ADAM_B2 * v_ref[...] + (1.0 - ADAM_B2) * jnp.square(g)
        g_out[...] = g
        m_out[...] = mn
        v_out[...] = vn
        d_out[...] = -ADAM_LR * ((mn * c1) / (jnp.sqrt(vn * c2) + ADAM_EPS) + ADAM_WD * w_ref[...])

    g0 = pl.BlockSpec((tr, C), lambda l, i: (jnp.where(l == 0, i, nt - 1), 0))
    g1 = pl.BlockSpec((tr, C), lambda l, i: (jnp.where(l == 1, i, 0), 0))
    s3 = pl.BlockSpec((None, tr, C), lambda l, i: (l, i, 0))
    return _pcall(body, name=name, grid=(2, nt), in_specs=[g0, g1, g0, g1, s3, s3, s3], out_specs=[s3] * 4,
                  out_shape=[_sds(w3.shape, F32)] * 4)(mine[0], mine[1], other[0], other[1], w3, m3, v3)


def _pack(arrs, dtype, row_mult):
    flat = jnp.concatenate([a.reshape(-1).astype(dtype) for a in arrs])
    n = flat.shape[0]
    unit = row_mult * PACK_W
    total = unit * ((n + unit - 1) // unit)
    if total > n:
        flat = jnp.concatenate([flat, jnp.zeros((total - n,), dtype)])
    return flat.reshape(-1, PACK_W)


def _unpack(buf, shapes):
    flat = buf.reshape(-1)
    out, off = [], 0
    for s in shapes:
        n = math.prod(s)
        out.append(flat[off:off + n].reshape(s))
        off += n
    return out


def _conf_pre(a, g):
    return [a * jax.nn.sigmoid(g)]


def _conf_post(cs, cb, lg, lb):
    return (_silu(_layer_norm(cs[0] + cb, lg, lb)),)


def _xbc_post(cs):
    return tuple(_silu(c) for c in cs)


def _ffn_post(cs):
    return (_silu(cs[0]) * cs[1],)


def _mix_fn(ga, gb, ya, yb):
    return (jax.nn.sigmoid(ga.astype(F32)) * ya + jax.nn.sigmoid(gb.astype(F32)) * yb,)


def _res_ln_fn(h, r, g, b):
    return (_layer_norm(ALPHA * h + r, g, b),)


def _ln_fn(x, g, b):
    return (_layer_norm(x, g, b),)


def _carrying(carry, key, gr, call):
    if key not in carry:
        return call(None)
    comm, done = carry[key](gr)
    out, got = call(comm)
    done(got)
    return out


def _two_copies(fn):
    def wrapped(*args):
        return fn(*args) * 2
    return wrapped


def _copies_out(T, tm):
    return [_o2(T, D, F32, tm), _o2(T, D, BF16, tm)]


def _layer_fwd(h, hb, W, l, carry):
    T = h.shape[0]
    tm = min(TM, T)
    nt = T // tm
    tmf = min(TM_FFN, T)
    ntf = T // tmf
    nm = lambda s: f"l{l}_{s}"
    u3 = _carrying(carry, "u", None,
                   lambda comm: _mm_resident_slab_out(hb, W["w_p"], name=nm("u"), width=D, tm=TM, out_dtype=BF16, comm=comm))
    dtr = _mm_nn(hb, W["w_dt"], name=nm("dt"))
    v3, cconv = _carrying(carry, "conf", None, lambda comm: _conf_fwd(
        u3, W["conv_w"], W["conv_b"], W["conv_ln_g"], W["conv_ln_b"], name=nm("conf"), comm=comm))
    ya = _mm_nn(v3, W["w_co"], name=nm("ya"))
    tmx = min(TM_X, T)
    x0, x1, bc = _carrying(carry, "xbc", None, lambda comm: _dwconv_fwd(
        _xbc_post, [_slab(u3, 6, tmx), _slab(u3, 7, tmx), _slab(u3, 8, tmx)], W["ssm_w"], W["ssm_b"],
        [_o2(T, D, F32, tmx)] * 3, K=SSM_K, C=D, name=nm("xbc"), tm=tmx, nt=T // tmx, comm=comm))
    yn, sprev = _carrying(carry, "ssd", None, lambda comm: _ssd_fwd(
        x0, x1, bc, dtr, u3, W["a_log"], W["dt_bias"], W["d_skip"], W["norm_w"], name=nm("ssd"), comm=comm))
    yb = _mm_nn(yn, W["w_so"], name=nm("yb"), tk=min(SD, 1024))
    (m,) = _rows_fwd(_mix_fn, [_slab(u3, 4, tm), _slab(u3, 5, tm), _r2(ya, tm), _r2(yb, tm)], [],
                     [_o2(T, D, BF16, tm)], name=nm("mix"), nt=nt)
    mix = _mm_nn(m, W["w_o"], name=nm("wo"))
    h1, h1b = _rows_fwd(_two_copies(_res_ln_fn), [_r2(h, tm), _r2(mix, tm)], [W["ln1_g"], W["ln1_b"]],
                        _copies_out(T, tm), name=nm("ln1"), nt=nt)
    up3 = _carrying(carry, "up", None,
                    lambda comm: _mm_resident_slab_out(h1b, W["w_up"], name=nm("up"), width=FFN, tm=TM, comm=comm))
    (f,) = _carrying(carry, "ffnact", None, lambda comm: _dwconv_fwd(
        _ffn_post, [_slab(up3, 0, tmf), _slab(up3, 1, tmf)], W["ffn_w"], W["ffn_b"],
        [_o2(T, FFN, BF16, tmf)], K=FFN_K, C=FFN, name=nm("ffnact"), tm=tmf, nt=ntf, comm=comm))
    ffn = _mm_nn(f, W["w_dn"], name=nm("dn"))
    h2, h2b = _rows_fwd(_two_copies(_res_ln_fn), [_r2(h1, tm), _r2(ffn, tm)], [W["ln2_g"], W["ln2_b"]],
                        _copies_out(T, tm), name=nm("ln2"), nt=nt)
    saved = dict(h=h, hb=hb, u3=u3, dtr=dtr, v3=v3, cconv=cconv, ya=ya, x0=x0, x1=x1, bc=bc, sprev=sprev, yn=yn, yb=yb, m=m,
                 mix=mix, h1=h1, h1b=h1b, up3=up3, f=f, ffn=ffn)
    return h2, h2b, saved


def _layer_bwd(dh2, W, sv, l, carry):
    T = dh2.shape[0]
    tm = min(TM, T)
    nt = T // tm
    tmf = min(TM_FFN, T)
    ntf = T // tmf
    nm = lambda s: f"l{l}_{s}"
    gr = {}
    (dres2, dffn), (gr["ln2_g"], gr["ln2_b"]) = _rows_bwd(
        _res_ln_fn, [_r2(sv["h1"], tm), _r2(sv["ffn"], tm)], [W["ln2_g"], W["ln2_b"]], [_r2(dh2, tm)],
        [_o2(T, D, F32, tm), _o2(T, D, BF16, tm)], name=nm("ln2_b"), nt=nt)
    df = _mm_nt(dffn, W["w_dn"], name=nm("dn_dx"))
    gr["w_dn"] = _mm_tn(sv["f"], dffn, name=nm("dn_dw"), tmo=FFN // 2)
    up3 = sv["up3"]
    (dgate, dval), gr["ffn_w"], gr["ffn_b"] = _carrying(carry, "ffnact_b", gr, lambda comm: _dwconv_bwd(
        _ffn_post, [_slab(up3, 0, tmf), _slab(up3, 1, tmf)],
        [_halo_slab(up3, 0, tmf, DW_HALO), _halo_slab(up3, 1, tmf, DW_HALO)], W["ffn_w"], W["ffn_b"], [_r2(df, tmf)],
        K=FFN_K, C=FFN, name=nm("ffnact_b"), tm=tmf, nt=ntf, comm=comm))
    dh1 = _mm_cols_nt([dgate, dval], W["w_up"], name=nm("up_dx"), add=dres2)
    gr["w_up"] = _mm_tn_cols(sv["h1b"], [dgate, dval], name=nm("up_dw"), tmo=min(512, D))
    (dres1, dmix), (gr["ln1_g"], gr["ln1_b"]) = _rows_bwd(
        _res_ln_fn, [_r2(sv["h"], tm), _r2(sv["mix"], tm)], [W["ln1_g"], W["ln1_b"]], [_r2(dh1, tm)],
        [_o2(T, D, F32, tm), _o2(T, D, BF16, tm)], name=nm("ln1_b"), nt=nt)
    dm = _mm_nt(dmix, W["w_o"], name=nm("wo_dx"))
    gr["w_o"] = _mm_tn(sv["m"], dmix, name=nm("wo_dw"), tmo=min(512, D))
    u3 = sv["u3"]
    (dga, dgb, dya, dyb), _ = _rows_bwd(
        _mix_fn, [_slab(u3, 4, tm), _slab(u3, 5, tm), _r2(sv["ya"], tm), _r2(sv["yb"], tm)], [], [_r2(dm, tm)],
        [_o2(T, D, BF16, tm)] * 4, name=nm("mix_b"), nt=nt)
    dv3 = _mm_nt(dya, W["w_co"], name=nm("ya_dx"))
    gr["w_co"] = _mm_tn(sv["v3"], dya, name=nm("ya_dw"), tmo=min(512, D))
    da, dg, gr["conv_w"], gr["conv_b"], gr["conv_ln_g"], gr["conv_ln_b"] = _carrying(carry, "conf_b", gr, lambda comm: _conf_bwd(
        u3, sv["cconv"], dv3, W["conv_w"], W["conv_b"], W["conv_ln_g"], W["conv_ln_b"], name=nm("conf_b"), comm=comm))
    dyn = _mm_nt(dyb, W["w_so"], name=nm("yb_dx"))
    gr["w_so"] = _mm_tn(sv["yn"], dyb, name=nm("yb_dw"), tmo=min(512, SD))
    (dx0, dx1, dbc, ddtr, dz0, dz1, gr["a_log"], gr["dt_bias"], gr["d_skip"], gr["norm_w"]) = _carrying(
        carry, "ssd_b", gr, lambda comm: _ssd_bwd(
            sv["x0"], sv["x1"], sv["bc"], sv["dtr"], u3, sv["sprev"], dyn, W["a_log"], W["dt_bias"], W["d_skip"],
            W["norm_w"], name=nm("ssd_b"), comm=comm))
    tmx = min(TM_X, T)
    (du6, du7, du8), gr["ssm_w"], gr["ssm_b"] = _dwconv_bwd(
        _xbc_post, [_slab(u3, 6, tmx), _slab(u3, 7, tmx), _slab(u3, 8, tmx)],
        [_halo_slab(u3, 6, tmx, DW_HALO), _halo_slab(u3, 7, tmx, DW_HALO), _halo_slab(u3, 8, tmx, DW_HALO)],
        W["ssm_w"], W["ssm_b"], [_r2(dx0, tmx), _r2(dx1, tmx), _r2(dbc, tmx)], K=SSM_K, C=D, name=nm("xbc_b"),
        tm=tmx, nt=T // tmx)
    du = [da, dg, dz0, dz1, dga, dgb, du6, du7, du8]
    gr["w_dt"] = _mm_tn(sv["hb"], ddtr, name=nm("dt_dw"), tmo=min(512, D))
    gr["w_p"] = _carrying(carry, "u_dw", gr, lambda comm: _mm_tn_cols(sv["hb"], du, name=nm("u_dw"),
                                                                      tmo=min(512, D), comm=comm))
    dh_a = _mm_nt(ddtr, W["w_dt"], name=nm("dt_dx"), add=dres1)
    dh = _carrying(carry, "u_dx", gr, lambda comm: _mm_cols_nt(du, W["w_p"], name=nm("u_dx"), add=dh_a, comm=comm))
    return dh, gr


_U_SPLIT = (2 * D + SD, 2 * D + SD + XBC, 2 * D + SD + XBC + H)


def _pad_rows(a, rows):
    return jnp.concatenate([a, jnp.zeros((rows - a.shape[0],) + a.shape[1:], a.dtype)], axis=0)


def _pad_lanes(a, lanes):
    return jnp.concatenate([a, jnp.zeros(a.shape[:-1] + (lanes - a.shape[-1],), a.dtype)], axis=-1)


def _w_in_layout(w_in):
    e0, e1, e2 = _U_SPLIT
    return dict(w_p=jnp.concatenate([w_in[:, :e0], w_in[:, e2:], w_in[:, e0:e1]], axis=1),
                w_dt=_pad_lanes(w_in[:, e1:e2], HP))


_MM_KEY = dict(w_conv_out="w_co", w_ssm_out="w_so", w_o="w_o", w_ffn_up="w_up", w_ffn_down="w_dn")


def _small_layer_weights(full, l):
    row = lambda a: a.reshape(1, -1)
    ssm_w = full["ssm_conv_w"][l]
    ffn_w = full["ffn_dw_w"][l]
    ssm_b = full["ssm_conv_b"][l]
    ffn_b = full["ffn_dw_b"][l]
    W = dict(
        conv_w=_pad_rows(full["conv_dw_w"][l], 32),
        conv_b=row(full["conv_dw_b"][l]), conv_ln_g=row(full["conv_ln_g"][l]), conv_ln_b=row(full["conv_ln_b"][l]),
        ssm_w=jnp.stack([_pad_rows(ssm_w[:, p * D:(p + 1) * D], 8) for p in range(3)]),
        ssm_b=[row(ssm_b[p * D:(p + 1) * D]) for p in range(3)],
        a_log=_pad_lanes(row(full["ssm_a_log"][l]), HP), dt_bias=_pad_lanes(row(full["ssm_dt_bias"][l]), HP),
        d_skip=_pad_lanes(row(full["ssm_d"][l]), HP), norm_w=row(full["ssm_norm_w"][l]),
        ln1_g=row(full["ln1_g"][l]), ln1_b=row(full["ln1_b"][l]),
        ffn_w=jnp.stack([_pad_rows(ffn_w[:, p * FFN:(p + 1) * FFN], 8) for p in range(2)]),
        ffn_b=[row(ffn_b[p * FFN:(p + 1) * FFN]) for p in range(2)],
        ln2_g=row(full["ln2_g"][l]), ln2_b=row(full["ln2_b"][l]),
    )
    return W


def _w_in_grad(gr):
    e0 = _U_SPLIT[0]
    wp = gr["w_p"]
    return jnp.concatenate([wp[:, :e0], wp[:, e0 + 2 * D:e0 + 2 * D + XBC], gr["w_dt"][:, :H], wp[:, e0:e0 + 2 * D]], axis=1)


def _layer_grads_to_reference_layout(gr):
    return dict(
        w_in=_w_in_grad(gr), conv_dw_w=gr["conv_w"][:CONV_K], conv_dw_b=gr["conv_b"][0], conv_ln_g=gr["conv_ln_g"][0],
        conv_ln_b=gr["conv_ln_b"][0], w_conv_out=gr["w_co"],
        ssm_conv_w=jnp.concatenate([gr["ssm_w"][p, :SSM_K] for p in range(3)], axis=1),
        ssm_conv_b=jnp.concatenate([b[0] for b in gr["ssm_b"]]),
        ssm_dt_bias=gr["dt_bias"][0, :H], ssm_a_log=gr["a_log"][0, :H], ssm_d=gr["d_skip"][0, :H],
        ssm_norm_w=gr["norm_w"][0], w_ssm_out=gr["w_so"], w_o=gr["w_o"], ln1_g=gr["ln1_g"][0], ln1_b=gr["ln1_b"][0],
        w_ffn_up=gr["w_up"], ffn_dw_w=jnp.concatenate([gr["ffn_w"][p, :FFN_K] for p in range(2)], axis=1),
        ffn_dw_b=jnp.concatenate([b[0] for b in gr["ffn_b"]]), w_ffn_down=gr["w_dn"], ln2_g=gr["ln2_g"][0],
        ln2_b=gr["ln2_b"][0],
    )


_BIG = dict(w_in=2, w_conv_out=1, w_ssm_out=1, w_o=1, w_ffn_up=2, w_ffn_down=1)
_SMALL_SHARDED = dict(conv_dw_w=2, ssm_conv_w=2, ffn_dw_w=2)
_REPLICATED = ("ln_in_g", "ln_in_b", "conv_dw_b", "conv_ln_g", "conv_ln_b", "ssm_conv_b", "ssm_dt_bias", "ssm_a_log",
               "ssm_d", "ssm_norm_w", "ln1_g", "ln1_b", "ffn_dw_b", "ln2_g", "ln2_b")
_WEIGHTS = ("ln_in_g", "ln_in_b", "w_in", "conv_dw_w", "conv_dw_b", "conv_ln_g", "conv_ln_b", "w_conv_out", "ssm_conv_w",
            "ssm_conv_b", "ssm_dt_bias", "ssm_a_log", "ssm_d", "ssm_norm_w", "w_ssm_out", "w_o", "ln1_g", "ln1_b",
            "w_ffn_up", "ffn_dw_w", "ffn_dw_b", "w_ffn_down", "ln2_g", "ln2_b")


def _split_chips(a, axis):
    rows, cols = a.shape
    if axis == 0:
        return a.reshape(NCHIP, rows // NCHIP, cols)
    return a.reshape(rows, NCHIP, cols // NCHIP).transpose(1, 0, 2)


def kernel(x, ln_in_g, ln_in_b, w_in, conv_dw_w, conv_dw_b, conv_ln_g, conv_ln_b, w_conv_out, ssm_conv_w, ssm_conv_b, ssm_dt_bias, ssm_a_log, ssm_d, ssm_norm_w, w_ssm_out, w_o, ln1_g, ln1_b, w_ffn_up, ffn_dw_w, ffn_dw_b, w_ffn_down, ln2_g, ln2_b, loss_target, m_ln_in_g, m_ln_in_b, m_w_in, m_conv_dw_w, m_conv_dw_b, m_conv_ln_g, m_conv_ln_b, m_w_conv_out, m_ssm_conv_w, m_ssm_conv_b, m_ssm_dt_bias, m_ssm_a_log, m_ssm_d, m_ssm_norm_w, m_w_ssm_out, m_w_o, m_ln1_g, m_ln1_b, m_w_ffn_up, m_ffn_dw_w, m_ffn_dw_b, m_w_ffn_down, m_ln2_g, m_ln2_b, v_ln_in_g, v_ln_in_b, v_w_in, v_conv_dw_w, v_conv_dw_b, v_conv_ln_g, v_conv_ln_b, v_w_conv_out, v_ssm_conv_w, v_ssm_conv_b, v_ssm_dt_bias, v_ssm_a_log, v_ssm_d, v_ssm_norm_w, v_w_ssm_out, v_w_o, v_ln1_g, v_ln1_b, v_w_ffn_up, v_ffn_dw_w, v_ffn_dw_b, v_w_ffn_down, v_ln2_g, v_ln2_b):
    args = locals()
    w = {n: args[n] for n in _WEIGHTS}
    mom = {n: args["m_" + n] for n in _WEIGHTS}
    vel = {n: args["v_" + n] for n in _WEIGHTS}
    T = x.shape[1]
    tm = min(TM, T)
    nt = T // tm
    chip = 2 * lax.axis_index("x") + lax.axis_index("y")

    assert DEPTH == 2
    big_names, small_names = list(_BIG), list(_SMALL_SHARDED)
    rest_big = [n for n in big_names if n != "w_in"]
    bf = {n: w[n].astype(BF16) for n in big_names}
    join = lambda got, axis: jnp.concatenate([got[j] for j in range(NCHIP)], axis=axis)
    first = _chip_exchange([bf["w_in"][0]] + [w[n] for n in small_names], scatter=False, name="gather_first")
    full = {n: join(gk, _SMALL_SHARDED[n]) for n, gk in zip(small_names, first[1:])}
    for n in _REPLICATED:
        full[n] = w[n]
    Ws = [_small_layer_weights(full, l) for l in range(DEPTH)]
    Ws[0].update(_w_in_layout(join(first[0], 1)))

    def gather(names, l):
        def done(got):
            for n, gk in zip(names, got):
                Ws[l][_MM_KEY[n]] = join(gk, _BIG[n] - 1)
        return lambda gr: (_exchange_comm([bf[n][l] for n in names], scatter=False), done)

    mixer = ["w_conv_out", "w_ssm_out", "w_o"]
    carry_fwd = [
        {"u": gather(mixer, 0), "conf": gather(["w_ffn_up"], 0), "xbc": gather(["w_ffn_down"], 0),
         "ssd": lambda gr: (_exchange_comm([bf["w_in"][1]], scatter=False),
                            lambda got: Ws[1].update(_w_in_layout(join(got[0], 1)))),
         "ffnact": gather(mixer, 1), "up": gather(["w_ffn_down"], 1)},
        {"u": gather(["w_ffn_up"], 1)},
    ]

    x2 = x.reshape(T, D)
    g_in, b_in = ln_in_g.reshape(1, D), ln_in_b.reshape(1, D)
    h, hb = _rows_fwd(_two_copies(_ln_fn), [_r2(x2, tm)], [g_in, b_in], _copies_out(T, tm), name="ln_in", nt=nt)
    saved = []
    for l in range(DEPTH):
        h, hb, sv = _layer_fwd(h, hb, Ws[l], l, carry_fwd[l])
        saved.append(sv)
    loss_row, dh = _loss_head(h, loss_target.reshape(T, D), name="loss")

    arrived = {}

    def exchange(names, l, grads):
        def make(gr):
            src = grads(gr)
            def done(got):
                for n, gk in zip(names, got):
                    arrived[(n, l)] = gk
            return _exchange_comm([_split_chips(src[n], _BIG[n] - 1) for n in names], scatter=True), done
        return make

    layer_grads = [None] * DEPTH
    dh, gr = _layer_bwd(dh, Ws[1], saved[1], 1, {})
    layer_grads[1] = _layer_grads_to_reference_layout(gr)
    g1 = lambda gr: layer_grads[1]
    g0 = lambda gr: {n: gr[_MM_KEY[n]] for n in rest_big}
    dh, gr = _layer_bwd(dh, Ws[0], saved[0], 0, {
        "ffnact_b": exchange(["w_in"], 1, g1),
        "conf_b": exchange(["w_ffn_up", "w_ffn_down"], 1, g1),
        "ssd_b": exchange(["w_conv_out", "w_ssm_out", "w_o"], 1, g1),
        "u_dw": exchange(rest_big, 0, g0),
        "u_dx": exchange(["w_in"], 0, lambda gr: {"w_in": _w_in_grad(gr)})})
    layer_grads[0] = _layer_grads_to_reference_layout(gr)
    (grad_x2,), (d_g_in, d_b_in) = _rows_bwd(_ln_fn, [_r2(x2, tm)], [g_in, b_in], [_r2(dh, tm)], [_o2(T, D, F32, tm)],
                                             name="ln_in_b", nt=nt)
    local = {n: jnp.stack([layer_grads[l][n] for l in range(DEPTH)]) for n in _WEIGHTS[2:] if n not in _BIG}
    local["ln_in_g"], local["ln_in_b"] = d_g_in[0], d_b_in[0]
    res = [{}, {}, {}, {}]

    keys = [(n, l) for n in big_names for l in range(DEPTH)]
    mine = [_sum_slots(arrived[k], name=f"sum_chips_{k[0]}_{k[1]}") for k in keys]
    other = _core_swap(mine, name="swap_cores")
    for i, n in enumerate(big_names):
        outs = _adamw_layers(mine[2 * i:2 * i + 2], other[2 * i:2 * i + 2], w[n], mom[n], vel[n], name="adamw_" + n)
        for q in range(4):
            res[q][n] = outs[q]

    rest_names = list(_REPLICATED) + small_names
    part = _pack([loss_row] + [local[n] for n in rest_names], F32, 8)
    parts = _all_gather8(part, name="gather_small")
    total = _sum_slots(parts, name="sum_devices")
    tot = _unpack(total, [loss_row.shape] + [local[n].shape for n in rest_names])
    loss = tot[0][0, 0]
    g_rest = {}
    for n, t in zip(rest_names, tot[1:]):
        if n in _SMALL_SHARDED:
            ax = _SMALL_SHARDED[n]
            t = lax.dynamic_slice_in_dim(t, chip * w[n].shape[ax], w[n].shape[ax], axis=ax)
        g_rest[n] = t
    pk = lambda d: _pack([d[n] for n in rest_names], F32, 8)
    rest_out = _adamw([pk(g_rest)], pk(w), pk(mom), pk(vel), name="adamw_rest")
    rest_out = [_unpack(o, [w[n].shape for n in rest_names]) for o in rest_out]

    for q in range(4):
        for k, n in enumerate(rest_names):
            res[q][n] = rest_out[q][k]
    grad_x = grad_x2.reshape(x.shape)
    return (loss, grad_x, *[res[0][n] for n in _WEIGHTS], *[res[1][n] for n in _WEIGHTS],
            *[res[2][n] for n in _WEIGHTS], *[res[3][n] for n in _WEIGHTS])
```
